```python
import jax, jax.numpy as jnp
from jax import lax
import numpy as np

D_MODEL = 2048
BATCH = 4
SEQ = 8192
DEPTH = 4

N_BRANCH = 4
MIX_W = 512
S5_GROUP = 16
S5_GROUPS = MIX_W // S5_GROUP
S5_STATE = 64
LRU_HEADS = 8
LRU_HD = MIX_W // LRU_HEADS
LRU_CONV = 4
LRU_C = 8.0
M2_HEADDIM = 64
M2_HEADS = MIX_W // M2_HEADDIM
M2_GROUPS = 2
M2_STATE = 128
M2_CONV = 4
M2_CHUNK = 128
M2_CONV_DIM = MIX_W + 2 * M2_GROUPS * M2_STATE
RW_HD = 64
RW_HEADS = MIX_W // RW_HD
RW_DECAY_R = 32
RW_A_R = 32
RW_GATE_R = 96
RW_COLS = 3 * MIX_W + RW_DECAY_R + RW_A_R + RW_GATE_R
RW_LN_EPS = 64e-5
FFN_DIM = 3 * D_MODEL
FFN_CONV = 3
EPS = 1e-6
MIXER_SPLITS = (MIX_W, MIX_W, MIX_W, MIX_W, M2_CONV_DIM, M2_HEADS, RW_COLS)
MIXER_COLS = MIX_W * 4 + M2_CONV_DIM + M2_HEADS + RW_COLS
IN_COLS = MIXER_COLS + N_BRANCH * D_MODEL

kernel_name = "hybrid_gated_s5_rglru_ssd_rwkv7"


def split_cols(t, sizes):
    offs, s = [], 0
    for n in sizes[:-1]:
        s += n
        offs.append(s)
    return jnp.split(t, offs, axis=-1)


def rms_norm(x, g):
    x32 = x.astype(jnp.float32)
    y = x32 * lax.rsqrt(jnp.mean(x32 * x32, axis=-1, keepdims=True) + EPS)
    return (y * g.astype(jnp.float32)).astype(x.dtype)


def causal_dwconv(x, w, b):
    k_w, seq = w.shape[0], x.shape[1]
    xp = jnp.pad(x, ((0, 0), (k_w - 1, 0), (0, 0)))
    out = b + xp[:, k_w - 1:k_w - 1 + seq] * w[k_w - 1]
    for j in range(k_w - 1):
        out = out + xp[:, j:j + seq] * w[j]
    return out


def token_shift(x):
    return jnp.pad(x, ((0, 0), (1, 0), (0, 0)))[:, :-1]


def real_scan_op(e1, e2):
    a1, b1 = e1
    a2, b2 = e2
    return a2 * a1, a2 * b1 + b2


def complex_scan_op(e1, e2):
    ar1, ai1, br1, bi1 = e1
    ar2, ai2, br2, bi2 = e2
    return (ar2 * ar1 - ai2 * ai1, ar2 * ai1 + ai2 * ar1,
            ar2 * br1 - ai2 * bi1 + br2, ar2 * bi1 + ai2 * br1 + bi2)


def s5_mixer(u, lam_re, lam_im, b_re, b_im, c_re, c_im, d, log_dt, w_glu):
    bsz, seq, _ = u.shape
    f32 = jnp.float32
    u32 = u.astype(f32)
    ug = u32.reshape(bsz, seq, S5_GROUPS, S5_GROUP)
    lr, li = lam_re.astype(f32), lam_im.astype(f32)
    dt = jnp.exp(log_dt.astype(f32))[:, None]
    mag = jnp.exp(lr * dt)
    ab_re, ab_im = mag * jnp.cos(li * dt), mag * jnp.sin(li * dt)
    den = lr * lr + li * li
    nr, ni = ab_re - 1.0, ab_im
    f_re, f_im = (nr * lr + ni * li) / den, (ni * lr - nr * li) / den
    bu_re = jnp.einsum("blgc,gpc->blgp", ug, b_re.astype(f32))
    bu_im = jnp.einsum("blgc,gpc->blgp", ug, b_im.astype(f32))
    x_re = f_re * bu_re - f_im * bu_im
    x_im = f_re * bu_im + f_im * bu_re
    a_re = jnp.broadcast_to(ab_re, (1, seq) + ab_re.shape)
    a_im = jnp.broadcast_to(ab_im, (1, seq) + ab_im.shape)
    _, _, s_re, s_im = lax.associative_scan(complex_scan_op, (a_re, a_im, x_re, x_im), axis=1)
    y = (jnp.einsum("blgp,gcp->blgc", s_re, c_re.astype(f32))
         - jnp.einsum("blgp,gcp->blgc", s_im, c_im.astype(f32)))
    y = y.reshape(bsz, seq, MIX_W) + d.astype(f32) * u32
    y = jax.nn.gelu(y)
    y = y * jax.nn.sigmoid(y @ w_glu.astype(f32))
    return y.astype(u.dtype)


def rglru_mixer(xb, gate, conv_w, conv_b, w_a, b_a, w_x, b_x, lam):
    bsz, seq, _ = xb.shape
    f32 = jnp.float32
    x = causal_dwconv(xb, conv_w, conv_b)
    xh = x.reshape(bsz, seq, LRU_HEADS, LRU_HD)
    r = jax.nn.sigmoid(jnp.einsum("blhi,hij->blhj", xh, w_a).reshape(bsz, seq, MIX_W) + b_a)
    i = jax.nn.sigmoid(jnp.einsum("blhi,hij->blhj", xh, w_x).reshape(bsz, seq, MIX_W) + b_x)
    log_a = (-LRU_C * r.astype(f32)) * jax.nn.softplus(-lam.astype(f32))
    a = jnp.exp(log_a)
    mult = jnp.sqrt(-jnp.expm1(2.0 * log_a))
    bx = (x * i).astype(f32) * mult
    _, h = lax.associative_scan(real_scan_op, (a, bx), axis=1)
    return (h.astype(xb.dtype) * jax.nn.gelu(gate)).astype(xb.dtype)


def ssd_chunked(x, dt, a, b, c):
    bsz, seq, nh, hp = x.shape
    ng, ns = b.shape[2], b.shape[3]
    hg, q = nh // ng, M2_CHUNK
    nc = seq // q
    xd = (x * dt[..., None]).reshape(bsz, nc, q, ng, hg, hp)
    ad = (dt * a).reshape(bsz, nc, q, ng, hg).transpose(0, 3, 4, 1, 2)
    bq = b.reshape(bsz, nc, q, ng, ns)
    cq = c.reshape(bsz, nc, q, ng, ns)
    a_cs = jnp.cumsum(ad, axis=-1)
    causal = jnp.tril(jnp.ones((q, q), dtype=bool))
    seg = a_cs[..., :, None] - a_cs[..., None, :]
    decay_in = jnp.exp(jnp.where(causal, seg, -jnp.inf))
    cb = jnp.einsum("bcign,bcjgn->bgcij", cq, bq)
    y_diag = jnp.einsum("bgcij,bghcij,bcjghp->bcighp", cb, decay_in, xd)
    decay_st = jnp.exp(a_cs[..., -1:] - a_cs)
    states = jnp.einsum("bcjgn,bghcj,bcjghp->cbghpn", bq, decay_st, xd)
    chunk_decay = jnp.exp(a_cs[..., -1]).transpose(3, 0, 1, 2)

    def step(carry, inp):
        s_c, dec = inp
        return carry * dec[..., None, None] + s_c, carry

    _, s_in = lax.scan(step, jnp.zeros(states.shape[1:], states.dtype), (states, chunk_decay))
    y_off = jnp.einsum("bcign,cbghpn,bghci->bcighp", cq, s_in, jnp.exp(a_cs))
    return (y_diag + y_off).reshape(bsz, seq, nh, hp)


def mamba2_mixer(z, xbc, dt_raw, conv_w, conv_b, dt_bias, a_log, d, norm_g):
    bsz, seq, _ = z.shape
    f32 = jnp.float32
    xbc = jax.nn.silu(causal_dwconv(xbc, conv_w, conv_b)).astype(f32)
    xs, bs, cs = split_cols(xbc, (MIX_W, M2_GROUPS * M2_STATE, M2_GROUPS * M2_STATE))
    xs = xs.reshape(bsz, seq, M2_HEADS, M2_HEADDIM)
    bs = bs.reshape(bsz, seq, M2_GROUPS, M2_STATE)
    cs = cs.reshape(bsz, seq, M2_GROUPS, M2_STATE)
    dt = jax.nn.softplus(dt_raw.astype(f32) + dt_bias.astype(f32))
    a = -jnp.exp(a_log.astype(f32))
    y = ssd_chunked(xs, dt, a, bs, cs) + d.astype(f32)[:, None] * xs
    y = y.reshape(bsz, seq, MIX_W) * jax.nn.silu(z.astype(f32))
    return rms_norm(y, norm_g).astype(z.dtype)


def rwkv7_scan(r, w, k, v, a, b):
    tm = lambda t: jnp.moveaxis(t, 1, 0)

    def step(s, inp):
        rt, wt, kt, vt, at, bt = inp
        sa = jnp.einsum("bhvk,bhk->bhv", s, at)
        s = s * wt[:, :, None, :] + sa[..., None] * bt[:, :, None, :] + vt[..., None] * kt[:, :, None, :]
        return s, jnp.einsum("bhvk,bhk->bhv", s, rt)

    bsz = r.shape[0]
    s0 = jnp.zeros((bsz, RW_HEADS, RW_HD, RW_HD), jnp.float32)
    _, y = lax.scan(step, s0, (tm(r), tm(w), tm(k), tm(v), tm(a), tm(b)))
    return jnp.moveaxis(y, 0, 1)


def rwkv7_mixer(p, mu, w0, w2, a0, a2, g2, k_k, k_a, r_k, ln_g, ln_b):
    bsz, seq, _ = p.shape
    f32 = jnp.float32
    p = p + (token_shift(p) - p) * mu
    r, k, v, xw, xa, xg = split_cols(p, (MIX_W, MIX_W, MIX_W, RW_DECAY_R, RW_A_R, RW_GATE_R))
    w = -jax.nn.softplus(-(w0 + jnp.tanh(xw) @ w2).astype(f32)) - 0.5
    decay = jnp.exp(-jnp.exp(w))
    a = jax.nn.sigmoid((a0 + xa @ a2).astype(f32))
    g = jax.nn.sigmoid(xg) @ g2
    heads = lambda t: t.reshape(bsz, seq, RW_HEADS, RW_HD)
    kk = heads((k * k_k).astype(f32))
    kk = kk / jnp.maximum(jnp.sqrt(jnp.sum(kk * kk, axis=-1, keepdims=True)), 1e-12)
    k = k.astype(f32) * (1.0 + (a - 1.0) * k_a.astype(f32))
    rh, kh, vh, wh, ah = heads(r.astype(f32)), heads(k), heads(v.astype(f32)), heads(decay), heads(a)
    y = rwkv7_scan(rh, wh, kh, vh, -kk, kk * ah)
    mean = jnp.mean(y, axis=-1, keepdims=True)
    var = jnp.mean(jnp.square(y - mean), axis=-1, keepdims=True)
    y = ((y - mean) * lax.rsqrt(var + RW_LN_EPS)).reshape(bsz, seq, MIX_W)
    y = y * ln_g.astype(f32) + ln_b.astype(f32)
    bonus = jnp.sum(rh * kh * r_k.astype(f32), axis=-1, keepdims=True) * vh
    y = y + bonus.reshape(bsz, seq, MIX_W)
    return (y * g.astype(f32)).astype(p.dtype)


def setup_inputs(seed: int = 0) -> dict:
    key = jax.random.key(seed)
    ks = iter(jax.random.split(key, 64))
    f32 = jnp.float32
    nrm = lambda shape, scale: scale * jax.random.normal(next(ks), shape, f32)
    unif = lambda shape, lo, hi: jax.random.uniform(next(ks), shape, f32, lo, hi)
    L, D, W = DEPTH, D_MODEL, MIX_W
    x = nrm((BATCH, SEQ, D), 1.0)
    norm_mix_g = 1.0 + nrm((L, D), 0.01)
    w_in = nrm((L, D, IN_COLS), D ** -0.5)
    s5_lambda_re = -0.5 + nrm((L, S5_GROUPS, S5_STATE), 0.01)
    s5_lambda_im = jnp.pi * jnp.arange(S5_STATE, dtype=f32) + nrm((L, S5_GROUPS, S5_STATE), 0.01)
    s5_b_re = nrm((L, S5_GROUPS, S5_STATE, S5_GROUP), (2 * S5_GROUP) ** -0.5)
    s5_b_im = nrm((L, S5_GROUPS, S5_STATE, S5_GROUP), (2 * S5_GROUP) ** -0.5)
    s5_c_re = nrm((L, S5_GROUPS, S5_GROUP, S5_STATE), S5_STATE ** -0.5)
    s5_c_im = nrm((L, S5_GROUPS, S5_GROUP, S5_STATE), S5_STATE ** -0.5)
    s5_d = nrm((L, W), 1.0)
    s5_log_dt = unif((L, S5_GROUPS), float(np.log(1e-3)), float(np.log(1e-1)))
    s5_w_glu = nrm((L, W, W), W ** -0.5)
    lru_conv_w = nrm((L, LRU_CONV, W), LRU_CONV ** -0.5)
    lru_conv_b = nrm((L, W), 0.01)
    lru_w_a = nrm((L, LRU_HEADS, LRU_HD, LRU_HD), LRU_HD ** -0.5)
    lru_b_a = nrm((L, W), 0.01)
    lru_w_x = nrm((L, LRU_HEADS, LRU_HD, LRU_HD), LRU_HD ** -0.5)
    lru_b_x = nrm((L, W), 0.01)
    base = unif((L, W), 0.9, 0.999) ** (1.0 / LRU_C)
    lru_lambda = jnp.log(base) - jnp.log1p(-base)
    m2_conv_w = nrm((L, M2_CONV, M2_CONV_DIM), M2_CONV ** -0.5)
    m2_conv_b = nrm((L, M2_CONV_DIM), 0.01)
    dt0 = jnp.exp(unif((L, M2_HEADS), float(np.log(1e-3)), float(np.log(1e-1))))
    m2_dt_bias = dt0 + jnp.log(-jnp.expm1(-dt0))
    m2_a_log = jnp.log(unif((L, M2_HEADS), 1.0, 16.0))
    m2_d = 1.0 + nrm((L, M2_HEADS), 0.1)
    m2_norm_g = 1.0 + nrm((L, W), 0.01)
    rw_mu = unif((L, RW_COLS), 0.0, 1.0)
    rw_w0 = unif((L, W), -6.0, -1.0)
    rw_w2 = nrm((L, RW_DECAY_R, W), 0.1)
    rw_a0 = nrm((L, W), 0.1)
    rw_a2 = nrm((L, RW_A_R, W), RW_A_R ** -0.5)
    rw_g2 = nrm((L, RW_GATE_R, W), RW_GATE_R ** -0.5)
    rw_k_k = 0.85 + nrm((L, W), 0.05)
    rw_k_a = 1.0 + nrm((L, W), 0.05)
    rw_r_k = nrm((L, RW_HEADS, RW_HD), 0.1)
    rw_ln_g = 1.0 + nrm((L, W), 0.01)
    rw_ln_b = nrm((L, W), 0.01)
    w_branch = nrm((L, N_BRANCH, W, D), W ** -0.5)
    w_out = nrm((L, D, D), D ** -0.5)
    norm_ffn_g = 1.0 + nrm((L, D), 0.01)
    w_ffn_gate = nrm((L, D, FFN_DIM), D ** -0.5)
    w_ffn_up = nrm((L, D, FFN_DIM), D ** -0.5)
    ffn_conv_w = nrm((L, FFN_CONV, FFN_DIM), FFN_CONV ** -0.5)
    ffn_conv_b = nrm((L, FFN_DIM), 0.01)
    w_ffn_down = nrm((L, FFN_DIM, D), FFN_DIM ** -0.5)
    final_norm_g = 1.0 + nrm((D,), 0.01)
    return {
        "x": x, "norm_mix_g": norm_mix_g, "w_in": w_in,
        "s5_lambda_re": s5_lambda_re, "s5_lambda_im": s5_lambda_im,
        "s5_b_re": s5_b_re, "s5_b_im": s5_b_im, "s5_c_re": s5_c_re, "s5_c_im": s5_c_im,
        "s5_d": s5_d, "s5_log_dt": s5_log_dt, "s5_w_glu": s5_w_glu,
        "lru_conv_w": lru_conv_w, "lru_conv_b": lru_conv_b, "lru_w_a": lru_w_a, "lru_b_a": lru_b_a,
        "lru_w_x": lru_w_x, "lru_b_x": lru_b_x, "lru_lambda": lru_lambda,
        "m2_conv_w": m2_conv_w, "m2_conv_b": m2_conv_b, "m2_dt_bias": m2_dt_bias,
        "m2_a_log": m2_a_log, "m2_d": m2_d, "m2_norm_g": m2_norm_g,
        "rw_mu": rw_mu, "rw_w0": rw_w0, "rw_w2": rw_w2, "rw_a0": rw_a0, "rw_a2": rw_a2,
        "rw_g2": rw_g2, "rw_k_k": rw_k_k, "rw_k_a": rw_k_a, "rw_r_k": rw_r_k,
        "rw_ln_g": rw_ln_g, "rw_ln_b": rw_ln_b,
        "w_branch": w_branch, "w_out": w_out,
        "norm_ffn_g": norm_ffn_g, "w_ffn_gate": w_ffn_gate, "w_ffn_up": w_ffn_up,
        "ffn_conv_w": ffn_conv_w, "ffn_conv_b": ffn_conv_b, "w_ffn_down": w_ffn_down,
        "final_norm_g": final_norm_g,
    }


def reference(x, norm_mix_g, w_in,
              s5_lambda_re, s5_lambda_im, s5_b_re, s5_b_im, s5_c_re, s5_c_im,
              s5_d, s5_log_dt, s5_w_glu,
              lru_conv_w, lru_conv_b, lru_w_a, lru_b_a, lru_w_x, lru_b_x, lru_lambda,
              m2_conv_w, m2_conv_b, m2_dt_bias, m2_a_log, m2_d, m2_norm_g,
              rw_mu, rw_w0, rw_w2, rw_a0, rw_a2, rw_g2, rw_k_k, rw_k_a, rw_r_k,
              rw_ln_g, rw_ln_b,
              w_branch, w_out,
              norm_ffn_g, w_ffn_gate, w_ffn_up, ffn_conv_w, ffn_conv_b, w_ffn_down,
              final_norm_g):
    for l in range(DEPTH):
        h = rms_norm(x, norm_mix_g[l])
        proj = h @ w_in[l, :, :MIXER_COLS]
        s5_u, lru_x, lru_g, m2_z, m2_xbc, m2_dt, rw_p = split_cols(proj, MIXER_SPLITS)
        y_a = s5_mixer(s5_u, s5_lambda_re[l], s5_lambda_im[l], s5_b_re[l], s5_b_im[l],
                       s5_c_re[l], s5_c_im[l], s5_d[l], s5_log_dt[l], s5_w_glu[l])
        y_b = rglru_mixer(lru_x, lru_g, lru_conv_w[l], lru_conv_b[l], lru_w_a[l], lru_b_a[l],
                          lru_w_x[l], lru_b_x[l], lru_lambda[l])
        y_c = mamba2_mixer(m2_z, m2_xbc, m2_dt, m2_conv_w[l], m2_conv_b[l], m2_dt_bias[l],
                           m2_a_log[l], m2_d[l], m2_norm_g[l])
        y_d = rwkv7_mixer(rw_p, rw_mu[l], rw_w0[l], rw_w2[l], rw_a0[l], rw_a2[l], rw_g2[l],
                          rw_k_k[l], rw_k_a[l], rw_r_k[l], rw_ln_g[l], rw_ln_b[l])
        branches = (y_a, y_b, y_c, y_d)
        merged = jnp.zeros_like(x)
        for k in range(N_BRANCH):
            c0 = MIXER_COLS + k * D_MODEL
            gate = jax.nn.sigmoid(h @ w_in[l, :, c0:c0 + D_MODEL])
            merged = merged + gate * (branches[k] @ w_branch[l, k])
        x = x + merged @ w_out[l]
        h = rms_norm(x, norm_ffn_g[l])
        u = causal_dwconv(h @ w_ffn_gate[l], ffn_conv_w[l], ffn_conv_b[l])
        x = x + (jax.nn.gelu(u) * (h @ w_ffn_up[l])) @ w_ffn_down[l]
    return rms_norm(x, final_norm_g)
```

```python
import functools

import jax
import jax.numpy as jnp
from jax import lax
from jax.experimental import pallas as pl
from jax.experimental.pallas import tpu as pltpu

F32 = jnp.float32
BF16 = jnp.bfloat16

D_MODEL = 2048
MIX_W = 512
HEAD_W = 64
N_HEADS = MIX_W // HEAD_W
S5_GROUP = 16
S5_GROUPS = MIX_W // S5_GROUP
S5_STATE = 64
S5_Q = 8
S5_NSTATE = S5_GROUPS * S5_STATE
LRU_HEADS = 8
LRU_C = 8.0
M2_GROUPS = 2
M2_STATE = 128
M2_CONV_DIM = MIX_W + 2 * M2_GROUPS * M2_STATE
M2_Q = 128
RW_Q = 64
RW_LORA_PAD = 128
RW_PW = 2048
RW_LN_EPS = 64e-5
FFN_DIM = 3 * D_MODEL
EPS = 1e-6
HALO = 8
BF16_ROWS = 16

PC_RW = 0
PC_XBC = 2048
PC_LRU_X = 3072
PC_LRU_G = 3584
PC_Z = 4096
PC_DT = 4608
PROJ_W = 5120

VMEM_LIMIT_BYTES = 50 * 1024 * 1024


def _params(*sem):
    return pltpu.CompilerParams(dimension_semantics=sem, vmem_limit_bytes=VMEM_LIMIT_BYTES)


def _dot(a, b):
    return jnp.dot(a, b, preferred_element_type=F32)


def _dot_nt(a, b):
    return lax.dot_general(a, b, (((1,), (1,)), ((), ())), preferred_element_type=F32)


def _dot_tn(a, b):
    return lax.dot_general(a, b, (((0,), (0,)), ((), ())), preferred_element_type=F32)


def _split(x, terms):
    out = []
    for _ in range(terms - 1):
        hi = x.astype(BF16)
        out.append(hi)
        x = x - hi.astype(F32)
    out.append(x.astype(BF16))
    return out


def _exact_left(m, x, terms=3):
    return sum(_dot(m, p) for p in _split(x, terms))


def _exact_right(x, m, terms=2):
    return sum(_dot(p, m) for p in _split(x, terms))


def _tril_mask(n, strict=False):
    r = lax.broadcasted_iota(jnp.int32, (n, n), 0)
    c = lax.broadcasted_iota(jnp.int32, (n, n), 1)
    return (r > c) if strict else (r >= c)


def _causal_conv(x, halo, w_ref, b_ref):
    k_w = w_ref.shape[0]
    t = x.shape[0]
    xe = jnp.concatenate([halo, x], axis=0)
    out = b_ref[...] + x * w_ref[k_w - 1:k_w, :]
    for j in range(k_w - 1):
        lag = k_w - 1 - j
        out = out + xe[HALO - lag:HALO - lag + t, :] * w_ref[j:j + 1, :]
    return out


def _mm_body(a_ref, b_ref, o_ref):
    o_ref[...] = _dot(a_ref[...], b_ref[...]).astype(o_ref.dtype)


def _mm_res_body(a_ref, b_ref, r_ref, o_ref):
    o_ref[...] = (_dot(a_ref[...], b_ref[...]) + r_ref[...]).astype(o_ref.dtype)


def _matmul(a, b, res=None, out_dtype=F32, tm=1024, tn=512, n_outer=False):
    m, k = a.shape
    n = b.shape[1]
    tm, tn = min(tm, m), min(tn, n)
    assert m % tm == 0 and n % tn == 0
    if n_outer:
        grid = (n // tn, m // tm)
        row = lambda j, i: (i, 0)
        col = lambda j, i: (0, j)
        out = lambda j, i: (i, j)
    else:
        grid = (m // tm, n // tn)
        row = lambda i, j: (i, 0)
        col = lambda i, j: (0, j)
        out = lambda i, j: (i, j)
    in_specs = [pl.BlockSpec((tm, k), row), pl.BlockSpec((k, tn), col)]
    args = [a, b]
    body = _mm_body
    if res is not None:
        in_specs.append(pl.BlockSpec((tm, tn), out))
        args.append(res)
        body = _mm_res_body
    return pl.pallas_call(
        body,
        grid=grid,
        in_specs=in_specs,
        out_specs=pl.BlockSpec((tm, tn), out),
        out_shape=jax.ShapeDtypeStruct((m, n), out_dtype),
        compiler_params=_params("parallel", "parallel"),
    )(*args)


def _norm_body(x_ref, g_ref, o_ref):
    x = x_ref[...]
    ms = jnp.mean(x * x, axis=-1, keepdims=True)
    o_ref[...] = (x * lax.rsqrt(ms + EPS) * g_ref[...]).astype(o_ref.dtype)


def _rms_norm(x, g, out_dtype, tm=512):
    m, d = x.shape
    tm = min(tm, m)
    return pl.pallas_call(
        _norm_body,
        grid=(m // tm,),
        in_specs=[pl.BlockSpec((tm, d), lambda i: (i, 0)), pl.BlockSpec((1, d), lambda i: (0, 0))],
        out_specs=pl.BlockSpec((tm, d), lambda i: (i, 0)),
        out_shape=jax.ShapeDtypeStruct((m, d), out_dtype),
        compiler_params=_params("parallel"),
    )(x, g.reshape(1, d))


def _s5_matrices(lam_re, lam_im, b_re, b_im, c_re, c_im, d, log_dt):
    g_n, p_n, c_n, q = S5_GROUPS, S5_STATE, S5_GROUP, S5_Q
    hp = lax.Precision.HIGHEST
    dt = jnp.exp(log_dt)[:, None]
    n = jnp.arange(q + 1, dtype=F32)[:, None, None]
    mag = jnp.exp(n * (lam_re * dt))
    pw_re = mag * jnp.cos(n * (lam_im * dt))
    pw_im = mag * jnp.sin(n * (lam_im * dt))
    den = lam_re * lam_re + lam_im * lam_im
    nr, ni = pw_re[1] - 1.0, pw_im[1]
    f_re = (nr * lam_re + ni * lam_im) / den
    f_im = (ni * lam_re - nr * lam_im) / den
    e_re = f_re[..., None] * b_re - f_im[..., None] * b_im
    e_im = f_re[..., None] * b_im + f_im[..., None] * b_re
    cp_re = c_re[None] * pw_re[:, :, None, :] - c_im[None] * pw_im[:, :, None, :]
    cp_im = c_re[None] * pw_im[:, :, None, :] + c_im[None] * pw_re[:, :, None, :]
    kern = (jnp.einsum("tgop,gpi->tgio", cp_re[:q], e_re, precision=hp)
            - jnp.einsum("tgop,gpi->tgio", cp_im[:q], e_im, precision=hp))
    kern = kern.at[0].add(d.reshape(g_n, c_n)[:, :, None] * jnp.eye(c_n, dtype=F32))
    eye_g = jnp.eye(g_n, dtype=F32)
    j_idx = jnp.arange(q)[:, None]
    t_idx = jnp.arange(q)[None, :]
    lag = t_idx - j_idx
    toep = jnp.where((lag >= 0)[:, :, None, None, None], kern[jnp.clip(lag, 0, q - 1)], 0.0)
    toep = toep.transpose(0, 2, 3, 1, 4)
    toep = toep[:, :, :, :, None, :] * eye_g[None, :, None, None, :, None]
    toep = toep.reshape(q * MIX_W, q * MIX_W)

    def state_cols(m):
        lead = m.shape[:-2]
        m = m.reshape(lead + (2, S5_NSTATE // 512, 512))
        m = jnp.swapaxes(m, -3, -2)
        return m.reshape(lead + (2 * S5_NSTATE,))

    rev = pw_re[q - 1 - jnp.arange(q)], pw_im[q - 1 - jnp.arange(q)]
    ws_re = rev[0][:, :, None, :] * e_re.transpose(0, 2, 1)[None] - rev[1][:, :, None, :] * e_im.transpose(0, 2, 1)[None]
    ws_im = rev[0][:, :, None, :] * e_im.transpose(0, 2, 1)[None] + rev[1][:, :, None, :] * e_re.transpose(0, 2, 1)[None]
    ws = jnp.stack([ws_re, ws_im], axis=3)
    ws = ws[:, :, :, :, None, :] * eye_g[None, :, None, None, :, None]
    ws = state_cols(ws.reshape(q * MIX_W, 2, S5_NSTATE))
    wy = jnp.stack([cp_re[1:], -cp_im[1:]], axis=0)
    wy = wy.transpose(0, 2, 4, 1, 3)
    wy = wy[:, :, :, :, None, :] * eye_g[None, :, None, None, :, None]
    wy = wy.reshape(2, S5_NSTATE, q * MIX_W)
    wy = jnp.swapaxes(wy, 0, 1).reshape(S5_NSTATE // 512, 512, 2, q * MIX_W)
    wy = jnp.swapaxes(wy, 1, 2).reshape(2 * S5_NSTATE, q * MIX_W)
    w_in = jnp.concatenate([toep, ws], axis=1).astype(BF16)
    return w_in, wy.astype(BF16), pw_re[q].reshape(1, S5_NSTATE), pw_im[q].reshape(1, S5_NSTATE)


def _s5_scan_body(ar_ref, ai_ref, s_ref, o_ref):
    rows = 16
    a_re, a_im = ar_ref[...], ai_ref[...]
    half = a_re.shape[1]

    def block(i, carry):
        s_re, s_im = carry
        base = pl.multiple_of(i * rows, rows)
        x = s_ref[pl.ds(base, rows), :]
        out = []
        for r in range(rows):
            out.append(jnp.concatenate([s_re, s_im], axis=1))
            n_re = a_re * s_re - a_im * s_im + x[r:r + 1, :half]
            n_im = a_re * s_im + a_im * s_re + x[r:r + 1, half:]
            s_re, s_im = n_re, n_im
        o_ref[pl.ds(base, rows), :] = jnp.concatenate(out, axis=0).astype(o_ref.dtype)
        return s_re, s_im

    zero = jnp.zeros((1, half), F32)
    lax.fori_loop(0, s_ref.shape[0] // rows, block, (zero, zero))


def _s5_glu_body(y_ref, w_ref, o_ref):
    y = jax.nn.gelu(y_ref[...])
    o_ref[...] = (y * jax.nn.sigmoid(_dot(y.astype(BF16), w_ref[...]))).astype(o_ref.dtype)


def _s5_mixer(u, mats, w_glu, bsz, seq):
    w_in, w_out, aq_re, aq_im = mats
    nc = seq // S5_Q
    width = S5_Q * MIX_W
    uc = u.reshape(bsz * nc, width)
    y1s = _matmul(uc, w_in, tm=1024, tn=512)
    blk = 1024
    off = width // blk
    s_in = pl.pallas_call(
        _s5_scan_body,
        grid=(bsz, 2 * S5_NSTATE // blk),
        in_specs=[
            pl.BlockSpec((1, blk // 2), lambda b, j: (0, j)),
            pl.BlockSpec((1, blk // 2), lambda b, j: (0, j)),
            pl.BlockSpec((nc, blk), lambda b, j: (b, off + j)),
        ],
        out_specs=pl.BlockSpec((nc, blk), lambda b, j: (b, j)),
        out_shape=jax.ShapeDtypeStruct((bsz * nc, 2 * S5_NSTATE), BF16),
        compiler_params=_params("parallel", "parallel"),
    )(aq_re, aq_im, y1s)
    y = _matmul(s_in, w_out, res=y1s, tm=1024, tn=512)
    y = y.reshape(bsz * seq, MIX_W)
    tm = min(1024, bsz * seq)
    return pl.pallas_call(
        _s5_glu_body,
        grid=(bsz * seq // tm,),
        in_specs=[pl.BlockSpec((tm, MIX_W), lambda i: (i, 0)), pl.BlockSpec((MIX_W, MIX_W), lambda i: (0, 0))],
        out_specs=pl.BlockSpec((tm, MIX_W), lambda i: (i, 0)),
        out_shape=jax.ShapeDtypeStruct((bsz * seq, MIX_W), BF16),
        compiler_params=_params("parallel"),
    )(y, w_glu.astype(BF16))


def _lru_body(x_ref, g_ref, cw_ref, cb_ref, wa_ref, ba_ref, wx_ref, bx_ref, lam_ref, o_ref,
              halo_ref, h_ref, a_s, b_s):
    t = x_ref.shape[0]

    @pl.when(pl.program_id(1) == 0)
    def _():
        halo_ref[...] = jnp.zeros_like(halo_ref)
        h_ref[...] = jnp.zeros_like(h_ref)

    x_in = x_ref[...]
    x = _causal_conv(x_in, halo_ref[...], cw_ref, cb_ref)
    halo_ref[...] = x_in[t - HALO:, :]
    xb = x.astype(BF16)
    r = jax.nn.sigmoid(_dot(xb, wa_ref[...]) + ba_ref[...])
    i = jax.nn.sigmoid(_dot(xb, wx_ref[...]) + bx_ref[...])
    log_a = (-LRU_C * r) * jax.nn.softplus(-lam_ref[...])
    a = jnp.exp(log_a)
    mult = jnp.sqrt(1.0 - jnp.exp(2.0 * log_a))
    a_s[...] = a
    b_s[...] = x * i * mult
    rows = 8

    def block(k, h):
        base = pl.multiple_of(k * rows, rows)
        av = a_s[pl.ds(base, rows), :]
        bv = b_s[pl.ds(base, rows), :]
        out = []
        for rr in range(rows):
            h = av[rr:rr + 1, :] * h + bv[rr:rr + 1, :]
            out.append(h)
        b_s[pl.ds(base, rows), :] = jnp.concatenate(out, axis=0)
        return h

    h_ref[...] = lax.fori_loop(0, t // rows, block, h_ref[...])
    o_ref[...] = (b_s[...] * jax.nn.gelu(g_ref[...])).astype(o_ref.dtype)


def _block_diag(w):
    h_n, n, _ = w.shape
    eye = jnp.eye(h_n, dtype=w.dtype)
    return (w[:, :, None, :] * eye[:, None, :, None]).reshape(h_n * n, h_n * n)


def _lru_mixer(proj, conv_w, conv_b, w_a, b_a, w_x, b_x, lam, bsz, seq):
    t = min(512, seq)
    nt = seq // t
    vec = lambda v: v.reshape(1, MIX_W)
    const = lambda shape: pl.BlockSpec(shape, lambda b, c: (0, 0))
    return pl.pallas_call(
        _lru_body,
        grid=(bsz, nt),
        in_specs=[
            pl.BlockSpec((t, MIX_W), lambda b, c: (b * nt + c, PC_LRU_X // MIX_W)),
            pl.BlockSpec((t, MIX_W), lambda b, c: (b * nt + c, PC_LRU_G // MIX_W)),
            const(conv_w.shape), const((1, MIX_W)),
            const((MIX_W, MIX_W)), const((1, MIX_W)),
            const((MIX_W, MIX_W)), const((1, MIX_W)), const((1, MIX_W)),
        ],
        out_specs=pl.BlockSpec((t, MIX_W), lambda b, c: (b * nt + c, 0)),
        out_shape=jax.ShapeDtypeStruct((bsz * seq, MIX_W), BF16),
        scratch_shapes=[
            pltpu.VMEM((HALO, MIX_W), F32), pltpu.VMEM((1, MIX_W), F32),
            pltpu.VMEM((t, MIX_W), F32), pltpu.VMEM((t, MIX_W), F32),
        ],
        compiler_params=_params("parallel", "arbitrary"),
    )(proj, proj, conv_w, vec(conv_b), _block_diag(w_a).astype(BF16), vec(b_a),
      _block_diag(w_x).astype(BF16), vec(b_x), vec(lam))


def _m2_body(z_ref, xbc_ref, dt_ref, cw_ref, cb_ref, dtb_ref, alog_ref, d_ref, ng_ref, o_ref,
             halo_ref, st_ref):
    q = z_ref.shape[0]
    hg = N_HEADS // M2_GROUPS
    gw = hg * HEAD_W

    @pl.when(pl.program_id(1) == 0)
    def _():
        halo_ref[...] = jnp.zeros_like(halo_ref)
        st_ref[...] = jnp.zeros_like(st_ref)

    xbc = xbc_ref[...]
    conv = _causal_conv(xbc, halo_ref[...], cw_ref, cb_ref)
    halo_ref[...] = xbc[q - HALO:, :]
    xc = conv * jax.nn.sigmoid(conv)
    xs = xc[:, :MIX_W]
    b_all = xc[:, MIX_W:MIX_W + M2_GROUPS * M2_STATE]
    c_all = xc[:, MIX_W + M2_GROUPS * M2_STATE:]
    dt = jax.nn.softplus(dt_ref[...] + dtb_ref[...])
    ad = dt * (-jnp.exp(alog_ref[...]))
    causal = _tril_mask(q)
    a_cs = _exact_left(causal.astype(BF16), ad)
    a_cs_t = a_cs.T
    a_last = a_cs[q - 1:q, :]
    xd = xs * dt
    xd_st = (xd * jnp.exp(a_last - a_cs)).astype(BF16)
    e_cs = jnp.exp(a_cs)
    xd_b = xd.astype(BF16)
    ys = []
    for g in range(M2_GROUPS):
        bg = b_all[:, g * M2_STATE:(g + 1) * M2_STATE].astype(BF16)
        cg = c_all[:, g * M2_STATE:(g + 1) * M2_STATE].astype(BF16)
        cb = _dot_nt(cg, bg)
        y_off = _dot(cg, st_ref[g].astype(BF16)) * e_cs[:, g * gw:(g + 1) * gw]
        for hh in range(hg):
            lo = (g * hg + hh) * HEAD_W
            col = a_cs[:, lo:lo + HEAD_W]
            col = jnp.concatenate([col] * (q // HEAD_W), axis=1)
            seg = col - a_cs_t[lo:lo + 1, :]
            dec = jnp.exp(jnp.where(causal, seg, -jnp.inf))
            y_d = _dot((cb * dec).astype(BF16), xd_b[:, lo:lo + HEAD_W])
            ys.append(y_d + y_off[:, hh * HEAD_W:(hh + 1) * HEAD_W])
        upd = _dot_tn(bg, xd_st[:, g * gw:(g + 1) * gw])
        st_ref[g] = st_ref[g] * jnp.exp(a_last[:, g * gw:(g + 1) * gw]) + upd
    y = jnp.concatenate(ys, axis=1) + d_ref[...] * xs
    z = z_ref[...]
    y = y * (z * jax.nn.sigmoid(z))
    ms = jnp.mean(y * y, axis=-1, keepdims=True)
    o_ref[...] = (y * lax.rsqrt(ms + EPS) * ng_ref[...]).astype(o_ref.dtype)


def _per_head(v):
    return jnp.repeat(v, HEAD_W).reshape(1, -1)


def _m2_mixer(proj, conv_w, conv_b, dt_bias, a_log, d, norm_g, bsz, seq):
    q = min(M2_Q, seq)
    nc = seq // q
    const = lambda shape: pl.BlockSpec(shape, lambda b, c: (0, 0))
    return pl.pallas_call(
        _m2_body,
        grid=(bsz, nc),
        in_specs=[
            pl.BlockSpec((q, MIX_W), lambda b, c: (b * nc + c, PC_Z // MIX_W)),
            pl.BlockSpec((q, M2_CONV_DIM), lambda b, c: (b * nc + c, PC_XBC // M2_CONV_DIM)),
            pl.BlockSpec((q, MIX_W), lambda b, c: (b * nc + c, PC_DT // MIX_W)),
            const(conv_w.shape), const((1, M2_CONV_DIM)),
            const((1, MIX_W)), const((1, MIX_W)), const((1, MIX_W)), const((1, MIX_W)),
        ],
        out_specs=pl.BlockSpec((q, MIX_W), lambda b, c: (b * nc + c, 0)),
        out_shape=jax.ShapeDtypeStruct((bsz * seq, MIX_W), BF16),
        scratch_shapes=[
            pltpu.VMEM((HALO, M2_CONV_DIM), F32),
            pltpu.VMEM((M2_GROUPS, M2_STATE, (N_HEADS // M2_GROUPS) * HEAD_W), F32),
        ],
        compiler_params=_params("parallel", "arbitrary"),
    )(proj, proj, proj, conv_w, conv_b.reshape(1, -1), _per_head(dt_bias), _per_head(a_log),
      _per_head(d), norm_g.reshape(1, -1))


def _rw_body(p_ref, mu_ref, w0_ref, w2_ref, a0_ref, a2_ref, g2_ref, kk_ref, ka_ref, rk_ref,
             lng_ref, lnb_ref, ones_ref, o_ref, prev_ref, st_ref):
    q = p_ref.shape[0]

    @pl.when(pl.program_id(1) == 0)
    def _():
        prev_ref[...] = jnp.zeros_like(prev_ref)
        st_ref[...] = jnp.zeros_like(st_ref)

    p = p_ref[...]
    row = lax.broadcasted_iota(jnp.int32, p.shape, 0)
    shifted = jnp.where(row == 0, prev_ref[0:1, :], pltpu.roll(p, 1, axis=0))
    prev_ref[0:1, :] = p[q - 1:q, :]
    p = p + (shifted - p) * mu_ref[...]
    r = p[:, 0:MIX_W]
    k = p[:, MIX_W:2 * MIX_W]
    v = p[:, 2 * MIX_W:3 * MIX_W]
    lo = 3 * MIX_W
    xw = p[:, lo:lo + RW_LORA_PAD]
    xa = p[:, lo + RW_LORA_PAD:lo + 2 * RW_LORA_PAD]
    xg = p[:, lo + 2 * RW_LORA_PAD:lo + 3 * RW_LORA_PAD]
    w = -jax.nn.softplus(-(w0_ref[...] + _dot(jnp.tanh(xw).astype(BF16), w2_ref[...]))) - 0.5
    log_w = -jnp.exp(w)
    a_sig = jax.nn.sigmoid(a0_ref[...] + _dot(xa.astype(BF16), a2_ref[...]))
    gate = _dot(jax.nn.sigmoid(xg).astype(BF16), g2_ref[...])
    ones = ones_ref[...]
    kk = k * kk_ref[...]
    kk = kk / jnp.maximum(jnp.sqrt(_exact_right(kk * kk, ones)), 1e-12)
    k = k * (1.0 + (a_sig - 1.0) * ka_ref[...])
    a_vec = -kk
    b_vec = kk * a_sig

    gam = _exact_left(_tril_mask(q).astype(BF16), log_w)
    g_last = gam[q - 1:q, :]
    e_neg = jnp.exp(-gam)
    e_rem = jnp.exp(g_last - gam)
    a_t = a_vec * jnp.exp(gam - log_w)
    r_t = r * jnp.exp(gam)
    b_t = b_vec * e_neg
    k_t = k * e_neg
    b_h = b_vec * e_rem
    k_h = k * e_rem
    e_last = jnp.exp(g_last)
    strict = _tril_mask(q, strict=True)
    incl = _tril_mask(q)
    ys = []
    for h in range(N_HEADS):
        sl = slice(h * HEAD_W, (h + 1) * HEAD_W)
        ar = jnp.concatenate([a_t[:, sl], r_t[:, sl]], axis=0).astype(BF16)
        bk = jnp.concatenate([b_t[:, sl], k_t[:, sl]], axis=0).astype(BF16)
        gm = _dot_nt(ar, bk)
        s0 = st_ref[h]
        ah = _dot_nt(ar, s0.astype(BF16))
        vh = v[:, sl]
        vb = vh.astype(BF16)
        n_mat = jnp.where(strict, gm[:q, :q], 0.0)
        x = ah[:q] + _dot(jnp.where(strict, gm[:q, q:], 0.0).astype(BF16), vb)
        steps = max(1, (q - 1).bit_length())
        for s in range(steps):
            nb = n_mat.astype(BF16)
            x = x + _dot(nb, x.astype(BF16))
            if s + 1 < steps:
                n_mat = _dot(nb, nb)
        uv = jnp.concatenate([x, vh], axis=0).astype(BF16)
        t_mat = jnp.concatenate([jnp.where(incl, gm[q:, :q], 0.0), jnp.where(incl, gm[q:, q:], 0.0)], axis=1)
        ys.append(ah[q:] + _dot(t_mat.astype(BF16), uv))
        bkh = jnp.concatenate([b_h[:, sl], k_h[:, sl]], axis=0).astype(BF16)
        st_ref[h] = s0 * e_last[:, sl] + _dot_tn(uv, bkh)
    y = jnp.concatenate(ys, axis=1)
    inv_n = 1.0 / HEAD_W
    mean = _exact_right(y, ones) * inv_n
    yc = y - mean
    var = _exact_right(yc * yc, ones) * inv_n
    y = yc * lax.rsqrt(var + RW_LN_EPS) * lng_ref[...] + lnb_ref[...]
    bonus = _exact_right(r * k * rk_ref[...], ones) * v
    o_ref[...] = ((y + bonus) * gate).astype(o_ref.dtype)


def _pad_rows(w, rows):
    return jnp.zeros((rows, w.shape[1]), w.dtype).at[:w.shape[0]].set(w)


def _rw_mixer(proj, mu, w0, w2, a0, a2, g2, k_k, k_a, r_k, ln_g, ln_b, bsz, seq):
    q = min(RW_Q, seq)
    nc = seq // q
    vec = lambda t: t.reshape(1, MIX_W)
    const = lambda shape: pl.BlockSpec(shape, lambda b, c: (0, 0))
    lora = lambda t: _pad_rows(t, RW_LORA_PAD).astype(BF16)
    ones = _block_diag(jnp.ones((N_HEADS, HEAD_W, HEAD_W), F32)).astype(BF16)
    return pl.pallas_call(
        _rw_body,
        grid=(bsz, nc),
        in_specs=[
            pl.BlockSpec((q, RW_PW), lambda b, c: (b * nc + c, PC_RW // RW_PW)),
            const((1, RW_PW)), const((1, MIX_W)), const((RW_LORA_PAD, MIX_W)),
            const((1, MIX_W)), const((RW_LORA_PAD, MIX_W)), const((RW_LORA_PAD, MIX_W)),
            const((1, MIX_W)), const((1, MIX_W)), const((1, MIX_W)),
            const((1, MIX_W)), const((1, MIX_W)), const((MIX_W, MIX_W)),
        ],
        out_specs=pl.BlockSpec((q, MIX_W), lambda b, c: (b * nc + c, 0)),
        out_shape=jax.ShapeDtypeStruct((bsz * seq, MIX_W), BF16),
        scratch_shapes=[pltpu.VMEM((HALO, RW_PW), F32), pltpu.VMEM((N_HEADS, HEAD_W, HEAD_W), F32)],
        compiler_params=_params("parallel", "arbitrary"),
    )(proj, mu, vec(w0), lora(w2), vec(a0), lora(a2), lora(g2), vec(k_k), vec(k_a), vec(r_k),
      vec(ln_g), vec(ln_b), ones)


def _merge_body(h_ref, ya_ref, yb_ref, yc_ref, yd_ref, wg_ref, wp_ref, o_ref):
    h = h_ref[...]
    acc = None
    for kk, y_ref in enumerate((ya_ref, yb_ref, yc_ref, yd_ref)):
        term = jax.nn.sigmoid(_dot(h, wg_ref[kk])) * _dot(y_ref[...], wp_ref[kk])
        acc = term if acc is None else acc + term
    o_ref[...] = acc.astype(o_ref.dtype)


def _merge(h, ys, w_gate, w_branch, tm=512, tn=512):
    m = h.shape[0]
    tm = min(tm, m)
    nb = len(ys)
    y_spec = pl.BlockSpec((tm, MIX_W), lambda j, i: (i, 0))
    return pl.pallas_call(
        _merge_body,
        grid=(D_MODEL // tn, m // tm),
        in_specs=[pl.BlockSpec((tm, D_MODEL), lambda j, i: (i, 0))] + [y_spec] * nb + [
            pl.BlockSpec((nb, D_MODEL, tn), lambda j, i: (0, 0, j)),
            pl.BlockSpec((nb, MIX_W, tn), lambda j, i: (0, 0, j)),
        ],
        out_specs=pl.BlockSpec((tm, tn), lambda j, i: (i, j)),
        out_shape=jax.ShapeDtypeStruct((m, D_MODEL), BF16),
        compiler_params=_params("parallel", "parallel"),
    )(h, *ys, w_gate, w_branch)


def _ffn_body(h_ref, hh_ref, wg_ref, wu_ref, cw_ref, cb_ref, o_ref, *, tiles_per_seq):
    t = h_ref.shape[0]
    nh = hh_ref.shape[0]
    h = h_ref[...]
    seq_start = (pl.program_id(1) % tiles_per_seq) == 0
    halo = jnp.where(seq_start, jnp.zeros_like(hh_ref[...]), hh_ref[...])
    g_all = _dot(jnp.concatenate([halo, h], axis=0), wg_ref[...])
    k_w = cw_ref.shape[0]
    u = cb_ref[...] + g_all[nh:, :] * cw_ref[k_w - 1:k_w, :]
    for j in range(k_w - 1):
        lag = k_w - 1 - j
        u = u + g_all[nh - lag:nh - lag + t, :] * cw_ref[j:j + 1, :]
    o_ref[...] = (jax.nn.gelu(u) * _dot(h, wu_ref[...])).astype(o_ref.dtype)


def _ffn_act(h, w_gate, w_up, conv_w, conv_b, seq, tm=1024, tf=512):
    m = h.shape[0]
    tm = min(tm, seq)
    f = w_gate.shape[1]
    halo_blocks = tm // BF16_ROWS
    return pl.pallas_call(
        functools.partial(_ffn_body, tiles_per_seq=seq // tm),
        grid=(f // tf, m // tm),
        in_specs=[
            pl.BlockSpec((tm, D_MODEL), lambda j, i: (i, 0)),
            pl.BlockSpec((BF16_ROWS, D_MODEL), lambda j, i: (jnp.maximum(i * halo_blocks - 1, 0), 0)),
            pl.BlockSpec((D_MODEL, tf), lambda j, i: (0, j)),
            pl.BlockSpec((D_MODEL, tf), lambda j, i: (0, j)),
            pl.BlockSpec((conv_w.shape[0], tf), lambda j, i: (0, j)),
            pl.BlockSpec((1, tf), lambda j, i: (0, j)),
        ],
        out_specs=pl.BlockSpec((tm, tf), lambda j, i: (i, j)),
        out_shape=jax.ShapeDtypeStruct((m, f), BF16),
        compiler_params=_params("parallel", "parallel"),
    )(h, h, w_gate, w_up, conv_w, conv_b.reshape(1, f))


def _mixer_weight(w_in_l, rw_cols):
    s5_u, lru_x, lru_g, m2_z, m2_xbc, m2_dt, rw_p = jnp.split(
        w_in_l, [512, 1024, 1536, 2048, 2048 + M2_CONV_DIM, 2048 + M2_CONV_DIM + N_HEADS], axis=1)
    d = w_in_l.shape[0]
    rkv, xw, xa, xg = jnp.split(rw_p, [3 * MIX_W, 3 * MIX_W + 32, 3 * MIX_W + 64], axis=1)
    padc = lambda t, n: jnp.concatenate([t, jnp.zeros((d, n - t.shape[1]), t.dtype)], axis=1)
    rw = jnp.concatenate([rkv, padc(xw, RW_LORA_PAD), padc(xa, RW_LORA_PAD), padc(xg, RW_LORA_PAD)], axis=1)
    rw = padc(rw, RW_PW)
    dt_e = jnp.repeat(m2_dt, HEAD_W, axis=1)
    proj_w = jnp.concatenate([rw, m2_xbc, lru_x, lru_g, m2_z, dt_e], axis=1)
    assert proj_w.shape[1] == PROJ_W
    return s5_u.astype(BF16), proj_w.astype(BF16)


def _rw_mu(mu):
    rkv, xw, xa, xg = jnp.split(mu, [3 * MIX_W, 3 * MIX_W + 32, 3 * MIX_W + 64])
    pad = lambda t: jnp.concatenate([t, jnp.zeros((RW_LORA_PAD - t.shape[0],), t.dtype)])
    out = jnp.concatenate([rkv, pad(xw), pad(xa), pad(xg)])
    return jnp.concatenate([out, jnp.zeros((RW_PW - out.shape[0],), out.dtype)]).reshape(1, RW_PW)


def kernel(x, norm_mix_g, w_in, s5_lambda_re, s5_lambda_im, s5_b_re, s5_b_im, s5_c_re, s5_c_im, s5_d, s5_log_dt, s5_w_glu, lru_conv_w, lru_conv_b, lru_w_a, lru_b_a, lru_w_x, lru_b_x, lru_lambda, m2_conv_w, m2_conv_b, m2_dt_bias, m2_a_log, m2_d, m2_norm_g, rw_mu, rw_w0, rw_w2, rw_a0, rw_a2, rw_g2, rw_k_k, rw_k_a, rw_r_k, rw_ln_g, rw_ln_b, w_branch, w_out, norm_ffn_g, w_ffn_gate, w_ffn_up, ffn_conv_w, ffn_conv_b, w_ffn_down, final_norm_g):
    bsz, seq, d = x.shape
    depth = w_in.shape[0]
    mixer_cols = w_in.shape[2] - w_branch.shape[1] * d
    xf = x.reshape(bsz * seq, d)
    for l in range(depth):
        h = _rms_norm(xf, norm_mix_g[l], BF16)
        w_s5, w_proj = _mixer_weight(w_in[l, :, :mixer_cols], None)
        proj = _matmul(h, w_proj)
        s5_u = _matmul(h, w_s5, out_dtype=BF16)
        mats = _s5_matrices(s5_lambda_re[l], s5_lambda_im[l], s5_b_re[l], s5_b_im[l],
                            s5_c_re[l], s5_c_im[l], s5_d[l], s5_log_dt[l])
        y_a = _s5_mixer(s5_u, mats, s5_w_glu[l], bsz, seq)
        y_b = _lru_mixer(proj, lru_conv_w[l], lru_conv_b[l], lru_w_a[l], lru_b_a[l],
                         lru_w_x[l], lru_b_x[l], lru_lambda[l], bsz, seq)
        y_c = _m2_mixer(proj, m2_conv_w[l], m2_conv_b[l], m2_dt_bias[l], m2_a_log[l],
                        m2_d[l], m2_norm_g[l], bsz, seq)
        y_d = _rw_mixer(proj, _rw_mu(rw_mu[l]), rw_w0[l], rw_w2[l], rw_a0[l], rw_a2[l], rw_g2[l],
                        rw_k_k[l], rw_k_a[l], rw_r_k[l].reshape(-1), rw_ln_g[l], rw_ln_b[l], bsz, seq)
        w_gate = w_in[l, :, mixer_cols:].reshape(d, w_branch.shape[1], d).transpose(1, 0, 2).astype(BF16)
        merged = _merge(h, (y_a, y_b, y_c, y_d), w_gate, w_branch[l].astype(BF16))
        xf = _matmul(merged, w_out[l].astype(BF16), res=xf)
        h = _rms_norm(xf, norm_ffn_g[l], BF16)
        act = _ffn_act(h, w_ffn_gate[l].astype(BF16), w_ffn_up[l].astype(BF16),
                       ffn_conv_w[l], ffn_conv_b[l], seq)
        xf = _matmul(act, w_ffn_down[l].astype(BF16), res=xf, tm=512, n_outer=True)
    return _rms_norm(xf, final_norm_g, F32).reshape(bsz, seq, d)
```

```python
import functools

import jax
import jax.numpy as jnp
from jax import lax
from jax.experimental import pallas as pl
from jax.experimental.pallas import tpu as pltpu

F32 = jnp.float32
BF16 = jnp.bfloat16

D_MODEL = 2048
MIX_W = 512
HEAD_W = 64
N_HEADS = MIX_W // HEAD_W
S5_GROUP = 16
S5_GROUPS = MIX_W // S5_GROUP
S5_STATE = 64
S5_Q = 8
S5_NSTATE = S5_GROUPS * S5_STATE
S5_SB = 256
LRU_HEADS = 8
LRU_C = 8.0
M2_GROUPS = 2
M2_STATE = 128
M2_CONV_DIM = MIX_W + 2 * M2_GROUPS * M2_STATE
M2_Q = 128
RW_Q = 64
RW_BATCH = 2
RW_LORA_PAD = 128
RW_PW = 2048
RW_LN_EPS = 64e-5
FFN_DIM = 3 * D_MODEL
EPS = 1e-6
HALO = 8
BF16_ROWS = 16
LANES = 128

PC_RW = 0
PC_XBC = 2048
PC_LRU_X = 3072
PC_LRU_G = 3584
PC_Z = 4096
PC_DT = 4608
PROJ_W = 5120

VMEM_LIMIT_BYTES = 50 * 1024 * 1024


def _params(*sem):
    return pltpu.CompilerParams(dimension_semantics=sem, vmem_limit_bytes=VMEM_LIMIT_BYTES)


def _dot(a, b):
    return jnp.dot(a, b, preferred_element_type=F32)


def _dot_nt(a, b):
    return lax.dot_general(a, b, (((1,), (1,)), ((), ())), preferred_element_type=F32)


def _dot_tn(a, b):
    return lax.dot_general(a, b, (((0,), (0,)), ((), ())), preferred_element_type=F32)


def _split(x, terms):
    out = []
    for _ in range(terms - 1):
        hi = x.astype(BF16)
        out.append(hi)
        x = x - hi.astype(F32)
    out.append(x.astype(BF16))
    return out


def _exact_left(m, x, terms=3):
    return sum(_dot(m, p) for p in _split(x, terms))


def _exact_right(xs, m, terms=2):
    n = xs[0].shape[0]
    stacked = jnp.concatenate([p for x in xs for p in _split(x, terms)], axis=0)
    out = _dot(stacked, m)
    return [sum(out[(i * terms + t) * n:(i * terms + t + 1) * n] for t in range(terms))
            for i in range(len(xs))]


def _tril_mask(n, strict=False):
    r = lax.broadcasted_iota(jnp.int32, (n, n), 0)
    c = lax.broadcasted_iota(jnp.int32, (n, n), 1)
    return (r > c) if strict else (r >= c)


def _causal_conv(x, halo, w_ref, b_ref):
    k_w = w_ref.shape[0]
    t = x.shape[0]
    xe = jnp.concatenate([halo, x], axis=0)
    out = b_ref[...] + x * w_ref[k_w - 1:k_w, :]
    for j in range(k_w - 1):
        lag = k_w - 1 - j
        out = out + xe[HALO - lag:HALO - lag + t, :] * w_ref[j:j + 1, :]
    return out


def _mm_body(a_ref, b_ref, o_ref):
    o_ref[...] = _dot(a_ref[...], b_ref[...]).astype(o_ref.dtype)


def _mm_res_body(a_ref, b_ref, r_ref, o_ref):
    o_ref[...] = (_dot(a_ref[...], b_ref[...]) + r_ref[...]).astype(o_ref.dtype)


def _matmul(a, b, res=None, out_dtype=F32, tm=1024, tn=512, n_outer=False):
    m, k = a.shape
    n = b.shape[1]
    tm, tn = min(tm, m), min(tn, n)
    assert m % tm == 0 and n % tn == 0
    if n_outer:
        grid = (n // tn, m // tm)
        row = lambda j, i: (i, 0)
        col = lambda j, i: (0, j)
        out = lambda j, i: (i, j)
    else:
        grid = (m // tm, n // tn)
        row = lambda i, j: (i, 0)
        col = lambda i, j: (0, j)
        out = lambda i, j: (i, j)
    in_specs = [pl.BlockSpec((tm, k), row), pl.BlockSpec((k, tn), col)]
    args = [a, b]
    body = _mm_body
    if res is not None:
        in_specs.append(pl.BlockSpec((tm, tn), out))
        args.append(res)
        body = _mm_res_body
    return pl.pallas_call(
        body,
        name="matmul",
        grid=grid,
        in_specs=in_specs,
        out_specs=pl.BlockSpec((tm, tn), out),
        out_shape=jax.ShapeDtypeStruct((m, n), out_dtype),
        compiler_params=_params("parallel", "parallel"),
    )(*args)


def _norm_body(x_ref, g_ref, o_ref):
    x = x_ref[...]
    ms = jnp.mean(x * x, axis=-1, keepdims=True)
    o_ref[...] = (x * lax.rsqrt(ms + EPS) * g_ref[...]).astype(o_ref.dtype)


def _rms_norm(x, g, out_dtype, tm=512):
    m, d = x.shape
    tm = min(tm, m)
    return pl.pallas_call(
        _norm_body,
        name="rms_norm",
        grid=(m // tm,),
        in_specs=[pl.BlockSpec((tm, d), lambda i: (i, 0)), pl.BlockSpec((1, d), lambda i: (0, 0))],
        out_specs=pl.BlockSpec((tm, d), lambda i: (i, 0)),
        out_shape=jax.ShapeDtypeStruct((m, d), out_dtype),
        compiler_params=_params("parallel"),
    )(x, g.reshape(1, d))


def _s5_matrices(lam_re, lam_im, b_re, b_im, c_re, c_im, d, log_dt):
    g_n, p_n, c_n, q = S5_GROUPS, S5_STATE, S5_GROUP, S5_Q
    hp = lax.Precision.HIGHEST
    dt = jnp.exp(log_dt)[:, None]
    n = jnp.arange(q + 1, dtype=F32)[:, None, None]
    mag = jnp.exp(n * (lam_re * dt))
    pw_re = mag * jnp.cos(n * (lam_im * dt))
    pw_im = mag * jnp.sin(n * (lam_im * dt))
    den = lam_re * lam_re + lam_im * lam_im
    nr, ni = pw_re[1] - 1.0, pw_im[1]
    f_re = (nr * lam_re + ni * lam_im) / den
    f_im = (ni * lam_re - nr * lam_im) / den
    e_re = f_re[..., None] * b_re - f_im[..., None] * b_im
    e_im = f_re[..., None] * b_im + f_im[..., None] * b_re
    cp_re = c_re[None] * pw_re[:, :, None, :] - c_im[None] * pw_im[:, :, None, :]
    cp_im = c_re[None] * pw_im[:, :, None, :] + c_im[None] * pw_re[:, :, None, :]
    kern = (jnp.einsum("tgop,gpi->tgio", cp_re[:q], e_re, precision=hp)
            - jnp.einsum("tgop,gpi->tgio", cp_im[:q], e_im, precision=hp))
    kern = kern.at[0].add(d.reshape(g_n, c_n)[:, :, None] * jnp.eye(c_n, dtype=F32))
    width = q * MIX_W
    lag = jnp.arange(q)[None, :] - jnp.arange(q)[:, None]
    kt = jnp.where((lag >= 0)[:, :, None, None, None], kern[jnp.clip(lag, 0, q - 1)], 0.0)
    kt = kt.transpose(0, 2, 3, 1, 4).reshape(width, q * c_n)
    rev_re, rev_im = pw_re[q - 1 - jnp.arange(q)], pw_im[q - 1 - jnp.arange(q)]
    et_re, et_im = e_re.transpose(0, 2, 1)[None], e_im.transpose(0, 2, 1)[None]
    ws_re = rev_re[:, :, None, :] * et_re - rev_im[:, :, None, :] * et_im
    ws_im = rev_re[:, :, None, :] * et_im + rev_im[:, :, None, :] * et_re
    ws = jnp.stack([ws_re, ws_im], axis=3).reshape(width, 2 * p_n)
    wy = jnp.stack([cp_re[1:], -cp_im[1:]], axis=0).transpose(0, 2, 4, 1, 3)
    wy = wy.reshape(2, S5_NSTATE // S5_SB, S5_SB, q * c_n)
    wy = jnp.swapaxes(wy, 0, 1).reshape(2 * S5_NSTATE, q * c_n)

    col = jnp.arange(width)
    small = jnp.arange(q * c_n)[:, None]
    tok_rep = ((small // c_n == col[None, :] // MIX_W) & (small % c_n == col[None, :] % c_n)).astype(BF16)
    scol = jnp.arange(2 * S5_NSTATE)
    scol_ri, scol_state = (scol // S5_SB) % 2, (scol // (2 * S5_SB)) * S5_SB + scol % S5_SB
    small = jnp.arange(2 * p_n)[:, None]
    st_rep = ((small // p_n == scol_ri[None, :]) & (small % p_n == scol_state[None, :] % p_n)).astype(BF16)
    g_tok = (col % MIX_W) // c_n
    g_state = scol_state // p_n
    expand = lambda table, rep: jnp.dot(table.astype(BF16), rep, preferred_element_type=BF16)
    zero = jnp.zeros((), BF16)
    toep = jnp.where(g_tok[:, None] == g_tok[None, :], expand(kt, tok_rep), zero)
    w_state = jnp.where(g_tok[:, None] == g_state[None, :], expand(ws, st_rep), zero)
    w_out = jnp.where(g_state[:, None] == g_tok[None, :], expand(wy, tok_rep), zero)
    return toep, w_state, w_out, pw_re[q].reshape(1, S5_NSTATE), pw_im[q].reshape(1, S5_NSTATE)


def _s5_scan_body(ar_ref, ai_ref, s_ref, o_ref):
    rows = BF16_ROWS
    nb = s_ref.shape[0]
    a_re, a_im = ar_ref[...], ai_ref[...]
    half = a_re.shape[1]

    def block(i, carry):
        base = pl.multiple_of(i * rows, rows)
        xs = [s_ref[b, pl.ds(base, rows), :] for b in range(nb)]
        outs = [[] for _ in range(nb)]
        carry = list(carry)
        for r in range(rows):
            for b in range(nb):
                s_re, s_im = carry[b]
                outs[b].append(jnp.concatenate([s_re, s_im], axis=1))
                carry[b] = (a_re * s_re - a_im * s_im + xs[b][r:r + 1, :half],
                            a_re * s_im + a_im * s_re + xs[b][r:r + 1, half:])
        for b in range(nb):
            o_ref[b, pl.ds(base, rows), :] = jnp.concatenate(outs[b], axis=0).astype(o_ref.dtype)
        return tuple(carry)

    zero = jnp.zeros((1, half), F32)
    lax.fori_loop(0, s_ref.shape[1] // rows, block, tuple((zero, zero) for _ in range(nb)))


def _s5_fold_body(h_ref, w_ref, o_ref, tok_ref):
    u = _dot(h_ref[...], w_ref[...])
    tc = o_ref.shape[0]
    for c in range(MIX_W // LANES):
        tok_ref[c] = u[:, c * LANES:(c + 1) * LANES]
        for j in range(S5_Q):
            lo = j * MIX_W + c * LANES
            o_ref[:, lo:lo + LANES] = tok_ref[c, pl.ds(j, tc, stride=S5_Q), :].astype(o_ref.dtype)


def _s5_glu_body(y_ref, w_ref, o_ref, tok_ref):
    tc = y_ref.shape[0]
    for c in range(MIX_W // LANES):
        for j in range(S5_Q):
            lo = j * MIX_W + c * LANES
            tok_ref[c, pl.ds(j, tc, stride=S5_Q), :] = y_ref[:, lo:lo + LANES]
    y = jax.nn.gelu(jnp.concatenate([tok_ref[c] for c in range(MIX_W // LANES)], axis=1))
    o_ref[...] = (y * jax.nn.sigmoid(_dot(y.astype(BF16), w_ref[...]))).astype(o_ref.dtype)


def _s5_mixer(h, w_u, mats, w_glu, bsz, seq):
    toep, w_state, w_out, aq_re, aq_im = mats
    m = bsz * seq
    nc = seq // S5_Q
    width = S5_Q * MIX_W
    tm = min(1024, m)
    tc = tm // S5_Q
    uc = pl.pallas_call(
        _s5_fold_body,
        name="s5_input_proj",
        grid=(m // tm,),
        in_specs=[pl.BlockSpec((tm, D_MODEL), lambda i: (i, 0)), pl.BlockSpec((D_MODEL, MIX_W), lambda i: (0, 0))],
        out_specs=pl.BlockSpec((tc, width), lambda i: (i, 0)),
        out_shape=jax.ShapeDtypeStruct((m // S5_Q, width), BF16),
        scratch_shapes=[pltpu.VMEM((MIX_W // LANES, tm, LANES), F32)],
        compiler_params=_params("parallel"),
    )(h, w_u)
    y1 = _matmul(uc, toep)
    s_end = _matmul(uc, w_state)
    blk = 2 * S5_SB
    s_in = pl.pallas_call(
        _s5_scan_body,
        name="s5_chunk_scan",
        grid=(2 * S5_NSTATE // blk,),
        in_specs=[
            pl.BlockSpec((1, S5_SB), lambda j: (0, j)),
            pl.BlockSpec((1, S5_SB), lambda j: (0, j)),
            pl.BlockSpec((bsz, nc, blk), lambda j: (0, 0, j)),
        ],
        out_specs=pl.BlockSpec((bsz, nc, blk), lambda j: (0, 0, j)),
        out_shape=jax.ShapeDtypeStruct((bsz, nc, 2 * S5_NSTATE), BF16),
        compiler_params=_params("parallel"),
    )(aq_re, aq_im, s_end.reshape(bsz, nc, 2 * S5_NSTATE))
    y = _matmul(s_in.reshape(bsz * nc, 2 * S5_NSTATE), w_out, res=y1)
    return pl.pallas_call(
        _s5_glu_body,
        name="s5_glu",
        grid=(m // tm,),
        in_specs=[pl.BlockSpec((tc, width), lambda i: (i, 0)), pl.BlockSpec((MIX_W, MIX_W), lambda i: (0, 0))],
        out_specs=pl.BlockSpec((tm, MIX_W), lambda i: (i, 0)),
        out_shape=jax.ShapeDtypeStruct((m, MIX_W), BF16),
        scratch_shapes=[pltpu.VMEM((MIX_W // LANES, tm, LANES), F32)],
        compiler_params=_params("parallel"),
    )(y, w_glu.astype(BF16))


def _lru_body(x_ref, g_ref, cw_ref, cb_ref, wa_ref, ba_ref, wx_ref, bx_ref, lam_ref, o_ref,
              halo_ref, h_ref, a_s, b_s):
    nb, t, _ = x_ref.shape

    @pl.when(pl.program_id(0) == 0)
    def _():
        halo_ref[...] = jnp.zeros_like(halo_ref)
        h_ref[...] = jnp.zeros_like(h_ref)

    soft = jax.nn.softplus(-lam_ref[...])
    for b in range(nb):
        x_in = x_ref[b]
        x = _causal_conv(x_in, halo_ref[b], cw_ref, cb_ref)
        halo_ref[b] = x_in[t - HALO:, :]
        xb = x.astype(BF16)
        r = jax.nn.sigmoid(_dot(xb, wa_ref[...]) + ba_ref[...])
        i = jax.nn.sigmoid(_dot(xb, wx_ref[...]) + bx_ref[...])
        log_a = (-LRU_C * r) * soft
        a_s[b] = jnp.exp(log_a)
        b_s[b] = x * i * jnp.sqrt(1.0 - jnp.exp(2.0 * log_a))
    rows = 8

    def block(k, hs):
        base = pl.multiple_of(k * rows, rows)
        av = [a_s[b, pl.ds(base, rows), :] for b in range(nb)]
        bv = [b_s[b, pl.ds(base, rows), :] for b in range(nb)]
        hs = list(hs)
        outs = [[] for _ in range(nb)]
        for rr in range(rows):
            for b in range(nb):
                hs[b] = av[b][rr:rr + 1, :] * hs[b] + bv[b][rr:rr + 1, :]
                outs[b].append(hs[b])
        for b in range(nb):
            b_s[b, pl.ds(base, rows), :] = jnp.concatenate(outs[b], axis=0)
        return tuple(hs)

    hs = lax.fori_loop(0, t // rows, block, tuple(h_ref[b] for b in range(nb)))
    for b in range(nb):
        h_ref[b] = hs[b]
        o_ref[b] = (b_s[b] * jax.nn.gelu(g_ref[b])).astype(o_ref.dtype)


def _block_diag(w):
    h_n, n, _ = w.shape
    eye = jnp.eye(h_n, dtype=w.dtype)
    return (w[:, :, None, :] * eye[:, None, :, None]).reshape(h_n * n, h_n * n)


def _lru_mixer(proj, conv_w, conv_b, w_a, b_a, w_x, b_x, lam, bsz, seq):
    t = min(256, seq)
    vec = lambda v: v.reshape(1, MIX_W)
    const = lambda shape: pl.BlockSpec(shape, lambda c: (0, 0))
    proj3 = proj.reshape(bsz, seq, proj.shape[1])
    out = pl.pallas_call(
        _lru_body,
        name="rglru_mixer",
        grid=(seq // t,),
        in_specs=[
            pl.BlockSpec((bsz, t, MIX_W), lambda c: (0, c, PC_LRU_X // MIX_W)),
            pl.BlockSpec((bsz, t, MIX_W), lambda c: (0, c, PC_LRU_G // MIX_W)),
            const(conv_w.shape), const((1, MIX_W)),
            const((MIX_W, MIX_W)), const((1, MIX_W)),
            const((MIX_W, MIX_W)), const((1, MIX_W)), const((1, MIX_W)),
        ],
        out_specs=pl.BlockSpec((bsz, t, MIX_W), lambda c: (0, c, 0)),
        out_shape=jax.ShapeDtypeStruct((bsz, seq, MIX_W), BF16),
        scratch_shapes=[
            pltpu.VMEM((bsz, HALO, MIX_W), F32), pltpu.VMEM((bsz, 1, MIX_W), F32),
            pltpu.VMEM((bsz, t, MIX_W), F32), pltpu.VMEM((bsz, t, MIX_W), F32),
        ],
        compiler_params=_params("arbitrary"),
    )(proj3, proj3, conv_w, vec(conv_b), _block_diag(w_a).astype(BF16), vec(b_a),
      _block_diag(w_x).astype(BF16), vec(b_x), vec(lam))
    return out.reshape(bsz * seq, MIX_W)


def _m2_body(z_ref, xbc_ref, dt_ref, cw_ref, cb_ref, dtb_ref, alog_ref, d_ref, ng_ref, o_ref,
             halo_ref, st_ref):
    q = z_ref.shape[0]
    hg = N_HEADS // M2_GROUPS
    gw = hg * HEAD_W

    @pl.when(pl.program_id(1) == 0)
    def _():
        halo_ref[...] = jnp.zeros_like(halo_ref)
        st_ref[...] = jnp.zeros_like(st_ref)

    xbc = xbc_ref[...]
    conv = _causal_conv(xbc, halo_ref[...], cw_ref, cb_ref)
    halo_ref[...] = xbc[q - HALO:, :]
    xc = conv * jax.nn.sigmoid(conv)
    xs = xc[:, :MIX_W]
    b_all = xc[:, MIX_W:MIX_W + M2_GROUPS * M2_STATE]
    c_all = xc[:, MIX_W + M2_GROUPS * M2_STATE:]
    dt = jax.nn.softplus(dt_ref[...] + dtb_ref[...])
    ad = dt * (-jnp.exp(alog_ref[...]))
    causal = _tril_mask(q)
    a_cs = _exact_left(causal.astype(BF16), ad)
    a_cs_t = a_cs.T
    a_last = a_cs[q - 1:q, :]
    xd = xs * dt
    xd_st = (xd * jnp.exp(a_last - a_cs)).astype(BF16)
    e_cs = jnp.exp(a_cs)
    xd_b = xd.astype(BF16)
    ys = []
    for g in range(M2_GROUPS):
        bg = b_all[:, g * M2_STATE:(g + 1) * M2_STATE].astype(BF16)
        cg = c_all[:, g * M2_STATE:(g + 1) * M2_STATE].astype(BF16)
        cb = _dot_nt(cg, bg)
        y_off = _dot(cg, st_ref[g].astype(BF16)) * e_cs[:, g * gw:(g + 1) * gw]
        for hh in range(hg):
            lo = (g * hg + hh) * HEAD_W
            col = a_cs[:, lo:lo + HEAD_W]
            col = jnp.concatenate([col] * (q // HEAD_W), axis=1)
            seg = col - a_cs_t[lo:lo + 1, :]
            dec = jnp.exp(jnp.where(causal, seg, -jnp.inf))
            y_d = _dot((cb * dec).astype(BF16), xd_b[:, lo:lo + HEAD_W])
            ys.append(y_d + y_off[:, hh * HEAD_W:(hh + 1) * HEAD_W])
        upd = _dot_tn(bg, xd_st[:, g * gw:(g + 1) * gw])
        st_ref[g] = st_ref[g] * jnp.exp(a_last[:, g * gw:(g + 1) * gw]) + upd
    y = jnp.concatenate(ys, axis=1) + d_ref[...] * xs
    z = z_ref[...]
    y = y * (z * jax.nn.sigmoid(z))
    ms = jnp.mean(y * y, axis=-1, keepdims=True)
    o_ref[...] = (y * lax.rsqrt(ms + EPS) * ng_ref[...]).astype(o_ref.dtype)


def _per_head(v):
    return jnp.repeat(v, HEAD_W).reshape(1, -1)


def _m2_mixer(proj, conv_w, conv_b, dt_bias, a_log, d, norm_g, bsz, seq):
    q = min(M2_Q, seq)
    nc = seq // q
    const = lambda shape: pl.BlockSpec(shape, lambda b, c: (0, 0))
    return pl.pallas_call(
        _m2_body,
        name="mamba2_mixer",
        grid=(bsz, nc),
        in_specs=[
            pl.BlockSpec((q, MIX_W), lambda b, c: (b * nc + c, PC_Z // MIX_W)),
            pl.BlockSpec((q, M2_CONV_DIM), lambda b, c: (b * nc + c, PC_XBC // M2_CONV_DIM)),
            pl.BlockSpec((q, MIX_W), lambda b, c: (b * nc + c, PC_DT // MIX_W)),
            const(conv_w.shape), const((1, M2_CONV_DIM)),
            const((1, MIX_W)), const((1, MIX_W)), const((1, MIX_W)), const((1, MIX_W)),
        ],
        out_specs=pl.BlockSpec((q, MIX_W), lambda b, c: (b * nc + c, 0)),
        out_shape=jax.ShapeDtypeStruct((bsz * seq, MIX_W), BF16),
        scratch_shapes=[
            pltpu.VMEM((HALO, M2_CONV_DIM), F32),
            pltpu.VMEM((M2_GROUPS, M2_STATE, (N_HEADS // M2_GROUPS) * HEAD_W), F32),
        ],
        compiler_params=_params("parallel", "arbitrary"),
    )(proj, proj, proj, conv_w, conv_b.reshape(1, -1), _per_head(dt_bias), _per_head(a_log),
      _per_head(d), norm_g.reshape(1, -1))


def _rw_body(p_ref, mu_ref, w0_ref, w2_ref, a0_ref, a2_ref, g2_ref, kk_ref, ka_ref, rk_ref,
             lng_ref, lnb_ref, ones_ref, o_ref, prev_ref, st_ref):
    nb, q, _ = p_ref.shape
    rows = nb * q

    @pl.when(pl.program_id(1) == 0)
    def _():
        prev_ref[...] = jnp.zeros_like(prev_ref)
        st_ref[...] = jnp.zeros_like(st_ref)

    p = p_ref[...].reshape(rows, p_ref.shape[2])
    row = lax.broadcasted_iota(jnp.int32, p.shape, 0)
    shifted = pltpu.roll(p, 1, axis=0)
    for b in range(nb):
        shifted = jnp.where(row == b * q, prev_ref[b, 0:1, :], shifted)
        prev_ref[b, 0:1, :] = p[(b + 1) * q - 1:(b + 1) * q, :]
    p = p + (shifted - p) * mu_ref[...]
    r = p[:, 0:MIX_W]
    k = p[:, MIX_W:2 * MIX_W]
    v = p[:, 2 * MIX_W:3 * MIX_W]
    lo = 3 * MIX_W
    xw = p[:, lo:lo + RW_LORA_PAD]
    xa = p[:, lo + RW_LORA_PAD:lo + 2 * RW_LORA_PAD]
    xg = p[:, lo + 2 * RW_LORA_PAD:lo + 3 * RW_LORA_PAD]
    w = -jax.nn.softplus(-(w0_ref[...] + _dot(jnp.tanh(xw).astype(BF16), w2_ref[...]))) - 0.5
    log_w = -jnp.exp(w)
    a_sig = jax.nn.sigmoid(a0_ref[...] + _dot(xa.astype(BF16), a2_ref[...]))
    gate = _dot(jax.nn.sigmoid(xg).astype(BF16), g2_ref[...])
    ones = ones_ref[...]
    kk = k * kk_ref[...]
    k = k * (1.0 + (a_sig - 1.0) * ka_ref[...])
    kk_sq, rk_sum = _exact_right([kk * kk, r * k * rk_ref[...]], ones)
    kk = kk / jnp.maximum(jnp.sqrt(kk_sq), 1e-12)
    a_vec = -kk
    b_vec = kk * a_sig

    rr = lax.broadcasted_iota(jnp.int32, (rows, rows), 0)
    cc = lax.broadcasted_iota(jnp.int32, (rows, rows), 1)
    cum_mask = jnp.where((rr >= cc) & ((rr // q) == (cc // q)), 1.0, 0.0).astype(BF16)
    gam = _exact_left(cum_mask, log_w)
    g_last = jnp.concatenate(
        [jnp.broadcast_to(gam[(b + 1) * q - 1:(b + 1) * q, :], (q, MIX_W)) for b in range(nb)], axis=0)
    e_neg = jnp.exp(-gam)
    e_rem = jnp.exp(g_last - gam)
    a_t = a_vec * jnp.exp(gam - log_w)
    r_t = r * jnp.exp(gam)
    b_t = b_vec * e_neg
    k_t = k * e_neg
    b_h = b_vec * e_rem
    k_h = k * e_rem
    e_last = jnp.exp(g_last)
    assert q == HEAD_W
    row2 = lax.broadcasted_iota(jnp.int32, (q, 2 * HEAD_W), 0)
    lane2 = lax.broadcasted_iota(jnp.int32, (q, 2 * HEAD_W), 1)
    low = lane2 < HEAD_W
    col2 = jnp.where(low, lane2, lane2 - HEAD_W)
    keep_z0 = low | (col2 < row2)
    keep_t = col2 <= row2
    strict = _tril_mask(q, strict=True)
    pairs = [(b, h) for b in range(nb) for h in range(N_HEADS)]
    zeros_w = jnp.zeros((HEAD_W, 2 * HEAD_W), BF16)
    zeros_x = jnp.zeros((q, HEAD_W), F32)

    def blk(t, key):
        b, h = key
        return t[b * q:(b + 1) * q, h * HEAD_W:(h + 1) * HEAD_W]

    p0, p1, s0, z = {}, {}, {}, {}
    for key in pairs:
        ar = jnp.concatenate([blk(a_t, key), blk(r_t, key)], axis=0).astype(BF16)
        s0[key] = st_ref[key[0], key[1]]
        rhs = jnp.concatenate([s0[key], blk(b_t, key), blk(k_t, key), zeros_x], axis=0).astype(BF16)
        prod = _dot_nt(ar, rhs)
        p0[key] = prod[:, :2 * HEAD_W]
        p1[key] = prod[:, 2 * HEAD_W:]
    for key in pairs:
        a_k = jnp.where(strict, p1[key][:q, :HEAD_W], 0.0).astype(BF16)
        x0 = _dot(a_k, blk(v, key).astype(BF16))
        z[key] = jnp.where(keep_z0, p0[key][:q], 0.0) + jnp.concatenate([x0, zeros_x], axis=1)
    steps = max(1, (q - 1).bit_length())
    for _ in range(steps):
        for key in pairs:
            zb = z[key].astype(BF16)
            z[key] = jnp.where(low, z[key], 0.0) + _dot(zb, jnp.concatenate([zeros_w, zb], axis=0))
    ys = {}
    for key in pairs:
        vu = jnp.concatenate([blk(v, key), z[key][:, :HEAD_W]], axis=0).astype(BF16)
        t_mat = jnp.where(keep_t, jnp.where(low, p1[key][q:], p0[key][q:]), 0.0)
        ys[key] = p0[key][q:, :HEAD_W] + _dot(t_mat.astype(BF16), vu)
        khb = jnp.concatenate([blk(k_h, key), blk(b_h, key)], axis=0).astype(BF16)
        st_ref[key[0], key[1]] = s0[key] * blk(e_last, key)[0:1, :] + _dot_tn(vu, khb)
    y = jnp.concatenate(
        [jnp.concatenate([ys[(b, h)] for h in range(N_HEADS)], axis=1) for b in range(nb)], axis=0)
    inv_n = 1.0 / HEAD_W
    yc = y - _exact_right([y], ones)[0] * inv_n
    var = _exact_right([yc * yc], ones)[0] * inv_n
    y = yc * lax.rsqrt(var + RW_LN_EPS) * lng_ref[...] + lnb_ref[...]
    bonus = rk_sum * v
    o_ref[...] = ((y + bonus) * gate).astype(o_ref.dtype).reshape(o_ref.shape)


def _pad_rows(w, rows):
    return jnp.zeros((rows, w.shape[1]), w.dtype).at[:w.shape[0]].set(w)


def _rw_mixer(proj, mu, w0, w2, a0, a2, g2, k_k, k_a, r_k, ln_g, ln_b, bsz, seq):
    q = min(RW_Q, seq)
    nc = seq // q
    nb = RW_BATCH if bsz % RW_BATCH == 0 else 1
    vec = lambda t: t.reshape(1, MIX_W)
    const = lambda shape: pl.BlockSpec(shape, lambda b, c: (0, 0))
    lora = lambda t: _pad_rows(t, RW_LORA_PAD).astype(BF16)
    ones = _block_diag(jnp.ones((N_HEADS, HEAD_W, HEAD_W), F32)).astype(BF16)
    out = pl.pallas_call(
        _rw_body,
        name="rwkv7_mixer",
        grid=(bsz // nb, nc),
        in_specs=[
            pl.BlockSpec((nb, q, RW_PW), lambda b, c: (b, c, PC_RW // RW_PW)),
            const((1, RW_PW)), const((1, MIX_W)), const((RW_LORA_PAD, MIX_W)),
            const((1, MIX_W)), const((RW_LORA_PAD, MIX_W)), const((RW_LORA_PAD, MIX_W)),
            const((1, MIX_W)), const((1, MIX_W)), const((1, MIX_W)),
            const((1, MIX_W)), const((1, MIX_W)), const((MIX_W, MIX_W)),
        ],
        out_specs=pl.BlockSpec((nb, q, MIX_W), lambda b, c: (b, c, 0)),
        out_shape=jax.ShapeDtypeStruct((bsz, seq, MIX_W), BF16),
        scratch_shapes=[pltpu.VMEM((nb, HALO, RW_PW), F32), pltpu.VMEM((nb, N_HEADS, HEAD_W, HEAD_W), F32)],
        compiler_params=_params("parallel", "arbitrary"),
    )(proj.reshape(bsz, seq, proj.shape[1]), mu, vec(w0), lora(w2), vec(a0), lora(a2), lora(g2), vec(k_k),
      vec(k_a), vec(r_k), vec(ln_g), vec(ln_b), ones)
    return out.reshape(bsz * seq, MIX_W)


def _merge_body(h_ref, *refs):
    nb = (len(refs) - 2) // 2
    y_refs, wg_refs, wp_ref, o_ref = refs[:nb], refs[nb:2 * nb], refs[2 * nb], refs[2 * nb + 1]
    h = h_ref[...]
    acc = None
    for kk in range(nb):
        term = jax.nn.sigmoid(_dot(h, wg_refs[kk][...])) * _dot(y_refs[kk][...], wp_ref[kk])
        acc = term if acc is None else acc + term
    o_ref[...] = acc.astype(o_ref.dtype)


def _merge(h, ys, w_gate, w_branch, tm=512, tn=512):
    m = h.shape[0]
    tm = min(tm, m)
    nb = len(ys)
    y_spec = pl.BlockSpec((tm, MIX_W), lambda j, i: (i, 0))
    gate_specs = [pl.BlockSpec((D_MODEL, tn), lambda j, i, k=k: (0, k * (D_MODEL // tn) + j)) for k in range(nb)]
    return pl.pallas_call(
        _merge_body,
        name="gated_merge",
        grid=(D_MODEL // tn, m // tm),
        in_specs=[pl.BlockSpec((tm, D_MODEL), lambda j, i: (i, 0))] + [y_spec] * nb + gate_specs + [
            pl.BlockSpec((nb, MIX_W, tn), lambda j, i: (0, 0, j)),
        ],
        out_specs=pl.BlockSpec((tm, tn), lambda j, i: (i, j)),
        out_shape=jax.ShapeDtypeStruct((m, D_MODEL), BF16),
        compiler_params=_params("parallel", "parallel"),
    )(h, *ys, *([w_gate] * nb), w_branch)


def _ffn_body(h_ref, hh_ref, wg_ref, wu_ref, cw_ref, cb_ref, o_ref, *, tiles_per_seq):
    t = h_ref.shape[0]
    nh = hh_ref.shape[0]
    h = h_ref[...]
    seq_start = (pl.program_id(1) % tiles_per_seq) == 0
    halo = jnp.where(seq_start, jnp.zeros_like(hh_ref[...]), hh_ref[...])
    g_all = _dot(jnp.concatenate([halo, h], axis=0), wg_ref[...])
    k_w = cw_ref.shape[0]
    u = cb_ref[...] + g_all[nh:, :] * cw_ref[k_w - 1:k_w, :]
    for j in range(k_w - 1):
        lag = k_w - 1 - j
        u = u + g_all[nh - lag:nh - lag + t, :] * cw_ref[j:j + 1, :]
    o_ref[...] = (jax.nn.gelu(u) * _dot(h, wu_ref[...])).astype(o_ref.dtype)


def _ffn_act(h, w_gate, w_up, conv_w, conv_b, seq, tm=1024, tf=512):
    m = h.shape[0]
    tm = min(tm, seq)
    f = w_gate.shape[1]
    halo_blocks = tm // BF16_ROWS
    return pl.pallas_call(
        functools.partial(_ffn_body, tiles_per_seq=seq // tm),
        name="ffn_gate_up",
        grid=(f // tf, m // tm),
        in_specs=[
            pl.BlockSpec((tm, D_MODEL), lambda j, i: (i, 0)),
            pl.BlockSpec((BF16_ROWS, D_MODEL), lambda j, i: (jnp.maximum(i * halo_blocks - 1, 0), 0)),
            pl.BlockSpec((D_MODEL, tf), lambda j, i: (0, j)),
            pl.BlockSpec((D_MODEL, tf), lambda j, i: (0, j)),
            pl.BlockSpec((conv_w.shape[0], tf), lambda j, i: (0, j)),
            pl.BlockSpec((1, tf), lambda j, i: (0, j)),
        ],
        out_specs=pl.BlockSpec((tm, tf), lambda j, i: (i, j)),
        out_shape=jax.ShapeDtypeStruct((m, f), BF16),
        compiler_params=_params("parallel", "parallel"),
    )(h, h, w_gate, w_up, conv_w, conv_b.reshape(1, f))


def _mixer_weight(w_in_l, rw_cols):
    s5_u, lru_x, lru_g, m2_z, m2_xbc, m2_dt, rw_p = jnp.split(
        w_in_l, [512, 1024, 1536, 2048, 2048 + M2_CONV_DIM, 2048 + M2_CONV_DIM + N_HEADS], axis=1)
    d = w_in_l.shape[0]
    rkv, xw, xa, xg = jnp.split(rw_p, [3 * MIX_W, 3 * MIX_W + 32, 3 * MIX_W + 64], axis=1)
    padc = lambda t, n: jnp.concatenate([t, jnp.zeros((d, n - t.shape[1]), t.dtype)], axis=1)
    rw = jnp.concatenate([rkv, padc(xw, RW_LORA_PAD), padc(xa, RW_LORA_PAD), padc(xg, RW_LORA_PAD)], axis=1)
    rw = padc(rw, RW_PW)
    dt_e = jnp.repeat(m2_dt, HEAD_W, axis=1)
    proj_w = jnp.concatenate([rw, m2_xbc, lru_x, lru_g, m2_z, dt_e], axis=1)
    assert proj_w.shape[1] == PROJ_W
    return s5_u.astype(BF16), proj_w.astype(BF16)


def _rw_mu(mu):
    rkv, xw, xa, xg = jnp.split(mu, [3 * MIX_W, 3 * MIX_W + 32, 3 * MIX_W + 64])
    pad = lambda t: jnp.concatenate([t, jnp.zeros((RW_LORA_PAD - t.shape[0],), t.dtype)])
    out = jnp.concatenate([rkv, pad(xw), pad(xa), pad(xg)])
    return jnp.concatenate([out, jnp.zeros((RW_PW - out.shape[0],), out.dtype)]).reshape(1, RW_PW)


def kernel(x, norm_mix_g, w_in, s5_lambda_re, s5_lambda_im, s5_b_re, s5_b_im, s5_c_re, s5_c_im, s5_d, s5_log_dt, s5_w_glu, lru_conv_w, lru_conv_b, lru_w_a, lru_b_a, lru_w_x, lru_b_x, lru_lambda, m2_conv_w, m2_conv_b, m2_dt_bias, m2_a_log, m2_d, m2_norm_g, rw_mu, rw_w0, rw_w2, rw_a0, rw_a2, rw_g2, rw_k_k, rw_k_a, rw_r_k, rw_ln_g, rw_ln_b, w_branch, w_out, norm_ffn_g, w_ffn_gate, w_ffn_up, ffn_conv_w, ffn_conv_b, w_ffn_down, final_norm_g):
    bsz, seq, d = x.shape
    depth = w_in.shape[0]
    mixer_cols = w_in.shape[2] - w_branch.shape[1] * d
    xf = x.reshape(bsz * seq, d)
    for l in range(depth):
        h = _rms_norm(xf, norm_mix_g[l], BF16)
        w_s5, w_proj = _mixer_weight(w_in[l, :, :mixer_cols], None)
        proj = _matmul(h, w_proj)
        mats = _s5_matrices(s5_lambda_re[l], s5_lambda_im[l], s5_b_re[l], s5_b_im[l],
                            s5_c_re[l], s5_c_im[l], s5_d[l], s5_log_dt[l])
        y_a = _s5_mixer(h, w_s5, mats, s5_w_glu[l], bsz, seq)
        y_b = _lru_mixer(proj, lru_conv_w[l], lru_conv_b[l], lru_w_a[l], lru_b_a[l],
                         lru_w_x[l], lru_b_x[l], lru_lambda[l], bsz, seq)
        y_c = _m2_mixer(proj, m2_conv_w[l], m2_conv_b[l], m2_dt_bias[l], m2_a_log[l],
                        m2_d[l], m2_norm_g[l], bsz, seq)
        y_d = _rw_mixer(proj, _rw_mu(rw_mu[l]), rw_w0[l], rw_w2[l], rw_a0[l], rw_a2[l], rw_g2[l],
                        rw_k_k[l], rw_k_a[l], rw_r_k[l].reshape(-1), rw_ln_g[l], rw_ln_b[l], bsz, seq)
        w_gate = w_in[l, :, mixer_cols:].astype(BF16)
        merged = _merge(h, (y_a, y_b, y_c, y_d), w_gate, w_branch[l].astype(BF16))
        xf = _matmul(merged, w_out[l].astype(BF16), res=xf)
        h = _rms_norm(xf, norm_ffn_g[l], BF16)
        act = _ffn_act(h, w_ffn_gate[l].astype(BF16), w_ffn_up[l].astype(BF16),
                       ffn_conv_w[l], ffn_conv_b[l], seq)
        xf = _matmul(act, w_ffn_down[l].astype(BF16), res=xf, tm=512, n_outer=True)
    return _rms_norm(xf, final_norm_g, F32).reshape(bsz, seq, d)
```

```python
import functools

import jax
import jax.numpy as jnp
from jax import lax
from jax.experimental import pallas as pl
from jax.experimental.pallas import tpu as pltpu

F32 = jnp.float32
BF16 = jnp.bfloat16

D_MODEL = 2048
MIX_W = 512
HEAD_W = 64
N_HEADS = MIX_W // HEAD_W
S5_GROUP = 16
S5_GROUPS = MIX_W // S5_GROUP
S5_STATE = 64
S5_Q = 8
S5_NSTATE = S5_GROUPS * S5_STATE
LANES = 128
S5_SG = MIX_W // LANES
S5_SGW = S5_Q * LANES
S5_SB = S5_NSTATE // S5_SG
S5_SCAN_ROWS = 512
LRU_HEADS = 8
LRU_C = 8.0
M2_GROUPS = 2
M2_STATE = 128
M2_CONV_DIM = MIX_W + 2 * M2_GROUPS * M2_STATE
M2_Q = 128
RW_Q = 64
RW_BATCH = 2
RW_LORA_PAD = 128
RW_PW = 2048
RW_LN_EPS = 64e-5
FFN_DIM = 3 * D_MODEL
EPS = 1e-6
HALO = 8
BF16_ROWS = 16

PC_RW = 0
PC_XBC = 2048
PC_LRU_X = 3072
PC_LRU_G = 3584
PC_Z = 4096
PC_DT = 4608
PROJ_W = 5120

VMEM_LIMIT_BYTES = 50 * 1024 * 1024


def _params(*sem):
    return pltpu.CompilerParams(dimension_semantics=sem, vmem_limit_bytes=VMEM_LIMIT_BYTES)


def _dot(a, b):
    return jnp.dot(a, b, preferred_element_type=F32)


def _dot_nt(a, b):
    return lax.dot_general(a, b, (((1,), (1,)), ((), ())), preferred_element_type=F32)


def _dot_tn(a, b):
    return lax.dot_general(a, b, (((0,), (0,)), ((), ())), preferred_element_type=F32)


def _split(x, terms):
    out = []
    for _ in range(terms - 1):
        hi = x.astype(BF16)
        out.append(hi)
        x = x - hi.astype(F32)
    out.append(x.astype(BF16))
    return out


def _exact_left(m, x, terms=3):
    return sum(_dot(m, p) for p in _split(x, terms))


def _exact_right(xs, m, terms=2):
    n = xs[0].shape[0]
    stacked = jnp.concatenate([p for x in xs for p in _split(x, terms)], axis=0)
    out = _dot(stacked, m)
    return [sum(out[(i * terms + t) * n:(i * terms + t + 1) * n] for t in range(terms))
            for i in range(len(xs))]


def _tril_mask(n, strict=False):
    r = lax.broadcasted_iota(jnp.int32, (n, n), 0)
    c = lax.broadcasted_iota(jnp.int32, (n, n), 1)
    return (r > c) if strict else (r >= c)


def _causal_conv(x, halo, w_ref, b_ref):
    k_w = w_ref.shape[0]
    t = x.shape[0]
    xe = jnp.concatenate([halo, x], axis=0)
    out = b_ref[...] + x * w_ref[k_w - 1:k_w, :]
    for j in range(k_w - 1):
        lag = k_w - 1 - j
        out = out + xe[HALO - lag:HALO - lag + t, :] * w_ref[j:j + 1, :]
    return out


def _mm_body(a_ref, b_ref, o_ref):
    o_ref[...] = _dot(a_ref[...], b_ref[...]).astype(o_ref.dtype)


def _mm_res_body(a_ref, b_ref, r_ref, o_ref):
    o_ref[...] = (_dot(a_ref[...], b_ref[...]) + r_ref[...]).astype(o_ref.dtype)


def _matmul(a, b, res=None, out_dtype=F32, tm=1024, tn=512, n_outer=False):
    m, k = a.shape
    n = b.shape[1]
    tm, tn = min(tm, m), min(tn, n)
    assert m % tm == 0 and n % tn == 0
    if n_outer:
        grid = (n // tn, m // tm)
        row = lambda j, i: (i, 0)
        col = lambda j, i: (0, j)
        out = lambda j, i: (i, j)
    else:
        grid = (m // tm, n // tn)
        row = lambda i, j: (i, 0)
        col = lambda i, j: (0, j)
        out = lambda i, j: (i, j)
    in_specs = [pl.BlockSpec((tm, k), row), pl.BlockSpec((k, tn), col)]
    args = [a, b]
    body = _mm_body
    if res is not None:
        in_specs.append(pl.BlockSpec((tm, tn), out))
        args.append(res)
        body = _mm_res_body
    return pl.pallas_call(
        body,
        name="matmul",
        grid=grid,
        in_specs=in_specs,
        out_specs=pl.BlockSpec((tm, tn), out),
        out_shape=jax.ShapeDtypeStruct((m, n), out_dtype),
        compiler_params=_params("parallel", "parallel"),
    )(*args)


def _matmul_blockdiag(a, w, res=None, tm=1024):
    m = a.shape[0]
    n = w.shape[1]
    tm = min(tm, m)
    blk = pl.BlockSpec((tm, n), lambda i, s: (i, s))
    in_specs = [blk, pl.BlockSpec((n, n), lambda i, s: (s, 0))]
    args = [a, w]
    if res is not None:
        in_specs.append(blk)
        args.append(res)
    return pl.pallas_call(
        _mm_body if res is None else _mm_res_body,
        name="matmul_blockdiag",
        grid=(m // tm, a.shape[1] // n),
        in_specs=in_specs,
        out_specs=blk,
        out_shape=jax.ShapeDtypeStruct(a.shape, F32),
        compiler_params=_params("parallel", "parallel"),
    )(*args)


def _norm_body(x_ref, g_ref, o_ref):
    x = x_ref[...]
    ms = jnp.mean(x * x, axis=-1, keepdims=True)
    o_ref[...] = (x * lax.rsqrt(ms + EPS) * g_ref[...]).astype(o_ref.dtype)


def _mm_res_norm_body(a_ref, w_ref, r_ref, g_ref, x_ref, h_ref):
    x = _dot(a_ref[...], w_ref[...]) + r_ref[...]
    x_ref[...] = x
    ms = jnp.mean(x * x, axis=-1, keepdims=True)
    h_ref[...] = (x * lax.rsqrt(ms + EPS) * g_ref[...]).astype(h_ref.dtype)


def _matmul_res_norm(a, w, res, g, tm=512):
    m, k = a.shape
    n = w.shape[1]
    tm = min(tm, m)
    rows = lambda width: pl.BlockSpec((tm, width), lambda i: (i, 0))
    return pl.pallas_call(
        _mm_res_norm_body,
        name="matmul_residual_norm",
        grid=(m // tm,),
        in_specs=[rows(k), pl.BlockSpec((k, n), lambda i: (0, 0)), rows(n), pl.BlockSpec((1, n), lambda i: (0, 0))],
        out_specs=[rows(n), rows(n)],
        out_shape=[jax.ShapeDtypeStruct((m, n), F32), jax.ShapeDtypeStruct((m, n), BF16)],
        compiler_params=_params("parallel"),
    )(a, w, res, g.reshape(1, n))


def _rms_norm(x, g, out_dtype, tm=512):
    m, d = x.shape
    tm = min(tm, m)
    return pl.pallas_call(
        _norm_body,
        name="rms_norm",
        grid=(m // tm,),
        in_specs=[pl.BlockSpec((tm, d), lambda i: (i, 0)), pl.BlockSpec((1, d), lambda i: (0, 0))],
        out_specs=pl.BlockSpec((tm, d), lambda i: (i, 0)),
        out_shape=jax.ShapeDtypeStruct((m, d), out_dtype),
        compiler_params=_params("parallel"),
    )(x, g.reshape(1, d))


def _s5_matrices(lam_re, lam_im, b_re, b_im, c_re, c_im, d, log_dt):
    g_n, p_n, c_n, q = S5_GROUPS, S5_STATE, S5_GROUP, S5_Q
    hp = lax.Precision.HIGHEST
    dt = jnp.exp(log_dt)[:, None]
    n = jnp.arange(q + 1, dtype=F32)[:, None, None]
    mag = jnp.exp(n * (lam_re * dt))
    pw_re = mag * jnp.cos(n * (lam_im * dt))
    pw_im = mag * jnp.sin(n * (lam_im * dt))
    den = lam_re * lam_re + lam_im * lam_im
    nr, ni = pw_re[1] - 1.0, pw_im[1]
    f_re = (nr * lam_re + ni * lam_im) / den
    f_im = (ni * lam_re - nr * lam_im) / den
    e_re = f_re[..., None] * b_re - f_im[..., None] * b_im
    e_im = f_re[..., None] * b_im + f_im[..., None] * b_re
    cp_re = c_re[None] * pw_re[:, :, None, :] - c_im[None] * pw_im[:, :, None, :]
    cp_im = c_re[None] * pw_im[:, :, None, :] + c_im[None] * pw_re[:, :, None, :]
    kern = (jnp.einsum("tgop,gpi->tgio", cp_re[:q], e_re, precision=hp)
            - jnp.einsum("tgop,gpi->tgio", cp_im[:q], e_im, precision=hp))
    kern = kern.at[0].add(d.reshape(g_n, c_n)[:, :, None] * jnp.eye(c_n, dtype=F32))
    sg_n, gl_n = S5_SG, g_n // S5_SG
    rows = sg_n * S5_SGW
    lag = jnp.arange(q)[None, :] - jnp.arange(q)[:, None]
    kt = jnp.where((lag >= 0)[:, :, None, None, None], kern[jnp.clip(lag, 0, q - 1)], 0.0)
    kt = kt.reshape(q, q, sg_n, gl_n, c_n, c_n).transpose(2, 0, 3, 4, 1, 5)
    kt = kt.reshape(rows, q * c_n)
    rev_re, rev_im = pw_re[q - 1 - jnp.arange(q)], pw_im[q - 1 - jnp.arange(q)]
    et_re, et_im = e_re.transpose(0, 2, 1)[None], e_im.transpose(0, 2, 1)[None]
    ws_re = rev_re[:, :, None, :] * et_re - rev_im[:, :, None, :] * et_im
    ws_im = rev_re[:, :, None, :] * et_im + rev_im[:, :, None, :] * et_re
    ws = jnp.stack([ws_re, ws_im], axis=3)
    ws = ws.reshape(q, sg_n, gl_n, c_n, 2, p_n).transpose(1, 0, 2, 3, 4, 5).reshape(rows, 2 * p_n)
    wy = jnp.stack([cp_re[1:], -cp_im[1:]], axis=0)
    wy = wy.reshape(2, q, sg_n, gl_n, c_n, p_n).transpose(2, 0, 3, 5, 1, 4)
    wy = wy.reshape(rows, q * c_n)

    col = jnp.arange(S5_SGW)
    small = jnp.arange(q * c_n)[:, None]
    tok_rep = ((small // c_n == col[None, :] // LANES) & (small % c_n == col[None, :] % c_n)).astype(BF16)
    small = jnp.arange(2 * p_n)[:, None]
    st_rep = ((small // p_n == col[None, :] // S5_SB) & (small % p_n == col[None, :] % p_n)).astype(BF16)
    row = jnp.arange(rows)
    g_tok_r, g_tok_c = (row % LANES) // c_n, (col % LANES) // c_n
    g_st_r, g_st_c = (row % S5_SB) // p_n, (col % S5_SB) // p_n
    expand = lambda table, rep: jnp.dot(table.astype(BF16), rep, preferred_element_type=BF16)
    zero = jnp.zeros((), BF16)
    toep = jnp.where(g_tok_r[:, None] == g_tok_c[None, :], expand(kt, tok_rep), zero)
    w_state = jnp.where(g_tok_r[:, None] == g_st_c[None, :], expand(ws, st_rep), zero)
    w_out = jnp.where(g_st_r[:, None] == g_tok_c[None, :], expand(wy, tok_rep), zero)
    return toep, w_state, w_out, pw_re[q].reshape(1, S5_NSTATE), pw_im[q].reshape(1, S5_NSTATE)


def _s5_scan_body(ar_ref, ai_ref, s_ref, o_ref, c_ref):
    rows = BF16_ROWS
    nb = s_ref.shape[0]
    a_re, a_im = ar_ref[...], ai_ref[...]
    half = a_re.shape[1]

    @pl.when(pl.program_id(1) == 0)
    def _():
        c_ref[...] = jnp.zeros_like(c_ref)

    def block(i, carry):
        base = pl.multiple_of(i * rows, rows)
        xs = [s_ref[b, pl.ds(base, rows), :] for b in range(nb)]
        outs = [[] for _ in range(nb)]
        carry = list(carry)
        for r in range(rows):
            for b in range(nb):
                s_re, s_im = carry[b]
                outs[b].append(jnp.concatenate([s_re, s_im], axis=1))
                carry[b] = (a_re * s_re - a_im * s_im + xs[b][r:r + 1, :half],
                            a_re * s_im + a_im * s_re + xs[b][r:r + 1, half:])
        for b in range(nb):
            o_ref[b, pl.ds(base, rows), :] = jnp.concatenate(outs[b], axis=0).astype(o_ref.dtype)
        return tuple(carry)

    init = tuple((c_ref[b, :, :half], c_ref[b, :, half:]) for b in range(nb))
    last = lax.fori_loop(0, s_ref.shape[1] // rows, block, init)
    for b in range(nb):
        c_ref[b] = jnp.concatenate(last[b], axis=1)


def _s5_fold_body(h_ref, w_ref, o_ref, tok_ref):
    u = _dot(h_ref[...], w_ref[...])
    tc = o_ref.shape[0]
    for c in range(S5_SG):
        tok_ref[c] = u[:, c * LANES:(c + 1) * LANES]
        for j in range(S5_Q):
            lo = c * S5_SGW + j * LANES
            o_ref[:, lo:lo + LANES] = tok_ref[c, pl.ds(j, tc, stride=S5_Q), :].astype(o_ref.dtype)


def _s5_glu_body(y_ref, w_ref, o_ref, tok_ref):
    tc = y_ref.shape[0]
    for c in range(S5_SG):
        for j in range(S5_Q):
            lo = c * S5_SGW + j * LANES
            tok_ref[c, pl.ds(j, tc, stride=S5_Q), :] = y_ref[:, lo:lo + LANES]
    y = jax.nn.gelu(jnp.concatenate([tok_ref[c] for c in range(S5_SG)], axis=1))
    o_ref[...] = (y * jax.nn.sigmoid(_dot(y.astype(BF16), w_ref[...]))).astype(o_ref.dtype)


def _s5_mixer(h, w_u, mats, w_glu, bsz, seq):
    toep, w_state, w_out, aq_re, aq_im = mats
    m = bsz * seq
    nc = seq // S5_Q
    width = S5_Q * MIX_W
    tm = min(1024, m)
    tc = tm // S5_Q
    uc = pl.pallas_call(
        _s5_fold_body,
        name="s5_input_proj",
        grid=(m // tm,),
        in_specs=[pl.BlockSpec((tm, D_MODEL), lambda i: (i, 0)), pl.BlockSpec((D_MODEL, MIX_W), lambda i: (0, 0))],
        out_specs=pl.BlockSpec((tc, width), lambda i: (i, 0)),
        out_shape=jax.ShapeDtypeStruct((m // S5_Q, width), BF16),
        scratch_shapes=[pltpu.VMEM((MIX_W // LANES, tm, LANES), F32)],
        compiler_params=_params("parallel"),
    )(h, w_u)
    y1 = _matmul_blockdiag(uc, toep)
    s_end = _matmul_blockdiag(uc, w_state)
    rb = min(S5_SCAN_ROWS, nc)
    s_in = pl.pallas_call(
        _s5_scan_body,
        name="s5_chunk_scan",
        grid=(S5_SG, nc // rb),
        in_specs=[
            pl.BlockSpec((1, S5_SB), lambda j, r: (0, j)),
            pl.BlockSpec((1, S5_SB), lambda j, r: (0, j)),
            pl.BlockSpec((bsz, rb, S5_SGW), lambda j, r: (0, r, j)),
        ],
        out_specs=pl.BlockSpec((bsz, rb, S5_SGW), lambda j, r: (0, r, j)),
        out_shape=jax.ShapeDtypeStruct((bsz, nc, width), BF16),
        scratch_shapes=[pltpu.VMEM((bsz, 1, S5_SGW), F32)],
        compiler_params=_params("parallel", "arbitrary"),
    )(aq_re, aq_im, s_end.reshape(bsz, nc, width))
    y = _matmul_blockdiag(s_in.reshape(bsz * nc, width), w_out, res=y1)
    return pl.pallas_call(
        _s5_glu_body,
        name="s5_glu",
        grid=(m // tm,),
        in_specs=[pl.BlockSpec((tc, width), lambda i: (i, 0)), pl.BlockSpec((MIX_W, MIX_W), lambda i: (0, 0))],
        out_specs=pl.BlockSpec((tm, MIX_W), lambda i: (i, 0)),
        out_shape=jax.ShapeDtypeStruct((m, MIX_W), BF16),
        scratch_shapes=[pltpu.VMEM((MIX_W // LANES, tm, LANES), F32)],
        compiler_params=_params("parallel"),
    )(y, w_glu.astype(BF16))


def _lru_body(x_ref, g_ref, cw_ref, cb_ref, wa_ref, ba_ref, wx_ref, bx_ref, lam_ref, o_ref,
              halo_ref, h_ref, a_s, b_s):
    nb, t, _ = x_ref.shape

    @pl.when(pl.program_id(0) == 0)
    def _():
        halo_ref[...] = jnp.zeros_like(halo_ref)
        h_ref[...] = jnp.zeros_like(h_ref)

    soft = jax.nn.softplus(-lam_ref[...])
    for b in range(nb):
        x_in = x_ref[b]
        x = _causal_conv(x_in, halo_ref[b], cw_ref, cb_ref)
        halo_ref[b] = x_in[t - HALO:, :]
        xb = x.astype(BF16)
        r = jax.nn.sigmoid(_dot(xb, wa_ref[...]) + ba_ref[...])
        i = jax.nn.sigmoid(_dot(xb, wx_ref[...]) + bx_ref[...])
        log_a = (-LRU_C * r) * soft
        a_s[b] = jnp.exp(log_a)
        b_s[b] = x * i * jnp.sqrt(1.0 - jnp.exp(2.0 * log_a))
    rows = 8

    def block(k, hs):
        base = pl.multiple_of(k * rows, rows)
        av = [a_s[b, pl.ds(base, rows), :] for b in range(nb)]
        bv = [b_s[b, pl.ds(base, rows), :] for b in range(nb)]
        hs = list(hs)
        outs = [[] for _ in range(nb)]
        for rr in range(rows):
            for b in range(nb):
                hs[b] = av[b][rr:rr + 1, :] * hs[b] + bv[b][rr:rr + 1, :]
                outs[b].append(hs[b])
        for b in range(nb):
            b_s[b, pl.ds(base, rows), :] = jnp.concatenate(outs[b], axis=0)
        return tuple(hs)

    hs = lax.fori_loop(0, t // rows, block, tuple(h_ref[b] for b in range(nb)))
    for b in range(nb):
        h_ref[b] = hs[b]
        o_ref[b] = (b_s[b] * jax.nn.gelu(g_ref[b])).astype(o_ref.dtype)


def _block_diag(w):
    h_n, n, _ = w.shape
    eye = jnp.eye(h_n, dtype=w.dtype)
    return (w[:, :, None, :] * eye[:, None, :, None]).reshape(h_n * n, h_n * n)


def _lru_mixer(proj, conv_w, conv_b, w_a, b_a, w_x, b_x, lam, bsz, seq):
    t = min(256, seq)
    vec = lambda v: v.reshape(1, MIX_W)
    const = lambda shape: pl.BlockSpec(shape, lambda c: (0, 0))
    proj3 = proj.reshape(bsz, seq, proj.shape[1])
    out = pl.pallas_call(
        _lru_body,
        name="rglru_mixer",
        grid=(seq // t,),
        in_specs=[
            pl.BlockSpec((bsz, t, MIX_W), lambda c: (0, c, PC_LRU_X // MIX_W)),
            pl.BlockSpec((bsz, t, MIX_W), lambda c: (0, c, PC_LRU_G // MIX_W)),
            const(conv_w.shape), const((1, MIX_W)),
            const((MIX_W, MIX_W)), const((1, MIX_W)),
            const((MIX_W, MIX_W)), const((1, MIX_W)), const((1, MIX_W)),
        ],
        out_specs=pl.BlockSpec((bsz, t, MIX_W), lambda c: (0, c, 0)),
        out_shape=jax.ShapeDtypeStruct((bsz, seq, MIX_W), BF16),
        scratch_shapes=[
            pltpu.VMEM((bsz, HALO, MIX_W), F32), pltpu.VMEM((bsz, 1, MIX_W), F32),
            pltpu.VMEM((bsz, t, MIX_W), F32), pltpu.VMEM((bsz, t, MIX_W), F32),
        ],
        compiler_params=_params("arbitrary"),
    )(proj3, proj3, conv_w, vec(conv_b), _block_diag(w_a).astype(BF16), vec(b_a),
      _block_diag(w_x).astype(BF16), vec(b_x), vec(lam))
    return out.reshape(bsz * seq, MIX_W)


def _m2_body(z_ref, xbc_ref, dt_ref, cw_ref, cb_ref, dtb_ref, alog_ref, d_ref, ng_ref, o_ref,
             halo_ref, st_ref):
    q = z_ref.shape[0]
    hg = N_HEADS // M2_GROUPS
    gw = hg * HEAD_W

    @pl.when(pl.program_id(1) == 0)
    def _():
        halo_ref[...] = jnp.zeros_like(halo_ref)
        st_ref[...] = jnp.zeros_like(st_ref)

    xbc = xbc_ref[...]
    conv = _causal_conv(xbc, halo_ref[...], cw_ref, cb_ref)
    halo_ref[...] = xbc[q - HALO:, :]
    xc = conv * jax.nn.sigmoid(conv)
    xs = xc[:, :MIX_W]
    b_all = xc[:, MIX_W:MIX_W + M2_GROUPS * M2_STATE]
    c_all = xc[:, MIX_W + M2_GROUPS * M2_STATE:]
    dt = jax.nn.softplus(dt_ref[...] + dtb_ref[...])
    ad = dt * (-jnp.exp(alog_ref[...]))
    causal = _tril_mask(q)
    a_cs = _exact_left(causal.astype(BF16), ad)
    a_cs_t = a_cs.T
    a_last = a_cs[q - 1:q, :]
    xd = xs * dt
    xd_st = (xd * jnp.exp(a_last - a_cs)).astype(BF16)
    e_cs = jnp.exp(a_cs)
    xd_b = xd.astype(BF16)
    ys = []
    for g in range(M2_GROUPS):
        bg = b_all[:, g * M2_STATE:(g + 1) * M2_STATE].astype(BF16)
        cg = c_all[:, g * M2_STATE:(g + 1) * M2_STATE].astype(BF16)
        cb = _dot_nt(cg, bg)
        y_off = _dot(cg, st_ref[g].astype(BF16)) * e_cs[:, g * gw:(g + 1) * gw]
        for hh in range(hg):
            lo = (g * hg + hh) * HEAD_W
            col = a_cs[:, lo:lo + HEAD_W]
            col = jnp.concatenate([col] * (q // HEAD_W), axis=1)
            seg = col - a_cs_t[lo:lo + 1, :]
            dec = jnp.exp(jnp.where(causal, seg, -jnp.inf))
            y_d = _dot((cb * dec).astype(BF16), xd_b[:, lo:lo + HEAD_W])
            ys.append(y_d + y_off[:, hh * HEAD_W:(hh + 1) * HEAD_W])
        upd = _dot_tn(bg, xd_st[:, g * gw:(g + 1) * gw])
        st_ref[g] = st_ref[g] * jnp.exp(a_last[:, g * gw:(g + 1) * gw]) + upd
    y = jnp.concatenate(ys, axis=1) + d_ref[...] * xs
    z = z_ref[...]
    y = y * (z * jax.nn.sigmoid(z))
    ms = jnp.mean(y * y, axis=-1, keepdims=True)
    o_ref[...] = (y * lax.rsqrt(ms + EPS) * ng_ref[...]).astype(o_ref.dtype)


def _per_head(v):
    return jnp.repeat(v, HEAD_W).reshape(1, -1)


def _m2_mixer(proj, conv_w, conv_b, dt_bias, a_log, d, norm_g, bsz, seq):
    q = min(M2_Q, seq)
    nc = seq // q
    const = lambda shape: pl.BlockSpec(shape, lambda b, c: (0, 0))
    return pl.pallas_call(
        _m2_body,
        name="mamba2_mixer",
        grid=(bsz, nc),
        in_specs=[
            pl.BlockSpec((q, MIX_W), lambda b, c: (b * nc + c, PC_Z // MIX_W)),
            pl.BlockSpec((q, M2_CONV_DIM), lambda b, c: (b * nc + c, PC_XBC // M2_CONV_DIM)),
            pl.BlockSpec((q, MIX_W), lambda b, c: (b * nc + c, PC_DT // MIX_W)),
            const(conv_w.shape), const((1, M2_CONV_DIM)),
            const((1, MIX_W)), const((1, MIX_W)), const((1, MIX_W)), const((1, MIX_W)),
        ],
        out_specs=pl.BlockSpec((q, MIX_W), lambda b, c: (b * nc + c, 0)),
        out_shape=jax.ShapeDtypeStruct((bsz * seq, MIX_W), BF16),
        scratch_shapes=[
            pltpu.VMEM((HALO, M2_CONV_DIM), F32),
            pltpu.VMEM((M2_GROUPS, M2_STATE, (N_HEADS // M2_GROUPS) * HEAD_W), F32),
        ],
        compiler_params=_params("parallel", "arbitrary"),
    )(proj, proj, proj, conv_w, conv_b.reshape(1, -1), _per_head(dt_bias), _per_head(a_log),
      _per_head(d), norm_g.reshape(1, -1))


def _rw_body(p_ref, mu_ref, w0_ref, w2_ref, a0_ref, a2_ref, g2_ref, kk_ref, ka_ref, rk_ref,
             lng_ref, lnb_ref, ones_ref, o_ref, prev_ref, st_ref):
    nb, q, _ = p_ref.shape
    rows = nb * q

    @pl.when(pl.program_id(1) == 0)
    def _():
        prev_ref[...] = jnp.zeros_like(prev_ref)
        st_ref[...] = jnp.zeros_like(st_ref)

    p = p_ref[...].reshape(rows, p_ref.shape[2])
    row = lax.broadcasted_iota(jnp.int32, p.shape, 0)
    shifted = pltpu.roll(p, 1, axis=0)
    for b in range(nb):
        shifted = jnp.where(row == b * q, prev_ref[b, 0:1, :], shifted)
        prev_ref[b, 0:1, :] = p[(b + 1) * q - 1:(b + 1) * q, :]
    p = p + (shifted - p) * mu_ref[...]
    r = p[:, 0:MIX_W]
    k = p[:, MIX_W:2 * MIX_W]
    v = p[:, 2 * MIX_W:3 * MIX_W]
    lo = 3 * MIX_W
    xw = p[:, lo:lo + RW_LORA_PAD]
    xa = p[:, lo + RW_LORA_PAD:lo + 2 * RW_LORA_PAD]
    xg = p[:, lo + 2 * RW_LORA_PAD:lo + 3 * RW_LORA_PAD]
    w = -jax.nn.softplus(-(w0_ref[...] + _dot(jnp.tanh(xw).astype(BF16), w2_ref[...]))) - 0.5
    log_w = -jnp.exp(w)
    a_sig = jax.nn.sigmoid(a0_ref[...] + _dot(xa.astype(BF16), a2_ref[...]))
    gate = _dot(jax.nn.sigmoid(xg).astype(BF16), g2_ref[...])
    ones = ones_ref[...]
    kk = k * kk_ref[...]
    k = k * (1.0 + (a_sig - 1.0) * ka_ref[...])
    kk_sq, rk_sum = _exact_right([kk * kk, r * k * rk_ref[...]], ones)
    kk = kk / jnp.maximum(jnp.sqrt(kk_sq), 1e-12)
    a_vec = -kk
    b_vec = kk * a_sig

    rr = lax.broadcasted_iota(jnp.int32, (rows, rows), 0)
    cc = lax.broadcasted_iota(jnp.int32, (rows, rows), 1)
    cum_mask = jnp.where((rr >= cc) & ((rr // q) == (cc // q)), 1.0, 0.0).astype(BF16)
    gam = _exact_left(cum_mask, log_w)
    g_last = jnp.concatenate(
        [jnp.broadcast_to(gam[(b + 1) * q - 1:(b + 1) * q, :], (q, MIX_W)) for b in range(nb)], axis=0)
    e_neg = jnp.exp(-gam)
    e_rem = jnp.exp(g_last - gam)
    a_t = a_vec * jnp.exp(gam - log_w)
    r_t = r * jnp.exp(gam)
    b_t = b_vec * e_neg
    k_t = k * e_neg
    b_h = b_vec * e_rem
    k_h = k * e_rem
    e_last = jnp.exp(g_last)
    assert q == HEAD_W
    row2 = lax.broadcasted_iota(jnp.int32, (q, 2 * HEAD_W), 0)
    lane2 = lax.broadcasted_iota(jnp.int32, (q, 2 * HEAD_W), 1)
    low = lane2 < HEAD_W
    col2 = jnp.where(low, lane2, lane2 - HEAD_W)
    keep_z0 = low | (col2 < row2)
    keep_t = col2 <= row2
    strict = _tril_mask(q, strict=True)
    pairs = [(b, h) for b in range(nb) for h in range(N_HEADS)]
    zeros_w = jnp.zeros((HEAD_W, 2 * HEAD_W), BF16)
    zeros_x = jnp.zeros((q, HEAD_W), F32)

    def blk(t, key):
        b, h = key
        return t[b * q:(b + 1) * q, h * HEAD_W:(h + 1) * HEAD_W]

    p0, p1, s0, z = {}, {}, {}, {}
    for key in pairs:
        ar = jnp.concatenate([blk(a_t, key), blk(r_t, key)], axis=0).astype(BF16)
        s0[key] = st_ref[key[0], key[1]]
        rhs = jnp.concatenate([s0[key], blk(b_t, key), blk(k_t, key), zeros_x], axis=0).astype(BF16)
        prod = _dot_nt(ar, rhs)
        p0[key] = prod[:, :2 * HEAD_W]
        p1[key] = prod[:, 2 * HEAD_W:]
    for key in pairs:
        a_k = jnp.where(strict, p1[key][:q, :HEAD_W], 0.0).astype(BF16)
        x0 = _dot(a_k, blk(v, key).astype(BF16))
        z[key] = jnp.where(keep_z0, p0[key][:q], 0.0) + jnp.concatenate([x0, zeros_x], axis=1)
    steps = max(1, (q - 1).bit_length())
    for _ in range(steps):
        for key in pairs:
            zb = z[key].astype(BF16)
            z[key] = jnp.where(low, z[key], 0.0) + _dot(zb, jnp.concatenate([zeros_w, zb], axis=0))
    ys = {}
    for key in pairs:
        vu = jnp.concatenate([blk(v, key), z[key][:, :HEAD_W]], axis=0).astype(BF16)
        t_mat = jnp.where(keep_t, jnp.where(low, p1[key][q:], p0[key][q:]), 0.0)
        ys[key] = p0[key][q:, :HEAD_W] + _dot(t_mat.astype(BF16), vu)
        khb = jnp.concatenate([blk(k_h, key), blk(b_h, key)], axis=0).astype(BF16)
        st_ref[key[0], key[1]] = s0[key] * blk(e_last, key)[0:1, :] + _dot_tn(vu, khb)
    y = jnp.concatenate(
        [jnp.concatenate([ys[(b, h)] for h in range(N_HEADS)], axis=1) for b in range(nb)], axis=0)
    inv_n = 1.0 / HEAD_W
    yc = y - _exact_right([y], ones)[0] * inv_n
    var = _exact_right([yc * yc], ones)[0] * inv_n
    y = yc * lax.rsqrt(var + RW_LN_EPS) * lng_ref[...] + lnb_ref[...]
    bonus = rk_sum * v
    o_ref[...] = ((y + bonus) * gate).astype(o_ref.dtype).reshape(o_ref.shape)


def _pad_rows(w, rows):
    return jnp.zeros((rows, w.shape[1]), w.dtype).at[:w.shape[0]].set(w)


def _rw_mixer(proj, mu, w0, w2, a0, a2, g2, k_k, k_a, r_k, ln_g, ln_b, bsz, seq):
    q = min(RW_Q, seq)
    nc = seq // q
    nb = RW_BATCH if bsz % RW_BATCH == 0 else 1
    vec = lambda t: t.reshape(1, MIX_W)
    const = lambda shape: pl.BlockSpec(shape, lambda b, c: (0, 0))
    lora = lambda t: _pad_rows(t, RW_LORA_PAD).astype(BF16)
    ones = _block_diag(jnp.ones((N_HEADS, HEAD_W, HEAD_W), F32)).astype(BF16)
    out = pl.pallas_call(
        _rw_body,
        name="rwkv7_mixer",
        grid=(bsz // nb, nc),
        in_specs=[
            pl.BlockSpec((nb, q, RW_PW), lambda b, c: (b, c, PC_RW // RW_PW)),
            const((1, RW_PW)), const((1, MIX_W)), const((RW_LORA_PAD, MIX_W)),
            const((1, MIX_W)), const((RW_LORA_PAD, MIX_W)), const((RW_LORA_PAD, MIX_W)),
            const((1, MIX_W)), const((1, MIX_W)), const((1, MIX_W)),
            const((1, MIX_W)), const((1, MIX_W)), const((MIX_W, MIX_W)),
        ],
        out_specs=pl.BlockSpec((nb, q, MIX_W), lambda b, c: (b, c, 0)),
        out_shape=jax.ShapeDtypeStruct((bsz, seq, MIX_W), BF16),
        scratch_shapes=[pltpu.VMEM((nb, HALO, RW_PW), F32), pltpu.VMEM((nb, N_HEADS, HEAD_W, HEAD_W), F32)],
        compiler_params=_params("parallel", "arbitrary"),
    )(proj.reshape(bsz, seq, proj.shape[1]), mu, vec(w0), lora(w2), vec(a0), lora(a2), lora(g2), vec(k_k),
      vec(k_a), vec(r_k), vec(ln_g), vec(ln_b), ones)
    return out.reshape(bsz * seq, MIX_W)


def _merge_body(h_ref, *refs):
    nb = (len(refs) - 2) // 2
    y_refs, wg_refs, wp_ref, o_ref = refs[:nb], refs[nb:2 * nb], refs[2 * nb], refs[2 * nb + 1]
    h = h_ref[...]
    acc = None
    for kk in range(nb):
        term = jax.nn.sigmoid(_dot(h, wg_refs[kk][...])) * _dot(y_refs[kk][...], wp_ref[kk])
        acc = term if acc is None else acc + term
    o_ref[...] = acc.astype(o_ref.dtype)


def _merge(h, ys, w_gate, w_branch, tm=512, tn=512):
    m = h.shape[0]
    tm = min(tm, m)
    nb = len(ys)
    y_spec = pl.BlockSpec((tm, MIX_W), lambda j, i: (i, 0))
    gate_specs = [pl.BlockSpec((D_MODEL, tn), lambda j, i, k=k: (0, k * (D_MODEL // tn) + j)) for k in range(nb)]
    return pl.pallas_call(
        _merge_body,
        name="gated_merge",
        grid=(D_MODEL // tn, m // tm),
        in_specs=[pl.BlockSpec((tm, D_MODEL), lambda j, i: (i, 0))] + [y_spec] * nb + gate_specs + [
            pl.BlockSpec((nb, MIX_W, tn), lambda j, i: (0, 0, j)),
        ],
        out_specs=pl.BlockSpec((tm, tn), lambda j, i: (i, j)),
        out_shape=jax.ShapeDtypeStruct((m, D_MODEL), BF16),
        compiler_params=_params("parallel", "parallel"),
    )(h, *ys, *([w_gate] * nb), w_branch)


def _ffn_body(h_ref, hh_ref, wg_ref, wu_ref, cw_ref, cb_ref, o_ref, *, tiles_per_seq):
    t = h_ref.shape[0]
    nh = hh_ref.shape[0]
    h = h_ref[...]
    seq_start = (pl.program_id(1) % tiles_per_seq) == 0
    halo = jnp.where(seq_start, jnp.zeros_like(hh_ref[...]), hh_ref[...])
    g_all = _dot(jnp.concatenate([halo, h], axis=0), wg_ref[...])
    k_w = cw_ref.shape[0]
    u = cb_ref[...] + g_all[nh:, :] * cw_ref[k_w - 1:k_w, :]
    for j in range(k_w - 1):
        lag = k_w - 1 - j
        u = u + g_all[nh - lag:nh - lag + t, :] * cw_ref[j:j + 1, :]
    o_ref[...] = (jax.nn.gelu(u) * _dot(h, wu_ref[...])).astype(o_ref.dtype)


def _ffn_act(h, w_gate, w_up, conv_w, conv_b, seq, tm=1024, tf=512):
    m = h.shape[0]
    tm = min(tm, seq)
    f = w_gate.shape[1]
    halo_blocks = tm // BF16_ROWS
    return pl.pallas_call(
        functools.partial(_ffn_body, tiles_per_seq=seq // tm),
        name="ffn_gate_up",
        grid=(f // tf, m // tm),
        in_specs=[
            pl.BlockSpec((tm, D_MODEL), lambda j, i: (i, 0)),
            pl.BlockSpec((BF16_ROWS, D_MODEL), lambda j, i: (jnp.maximum(i * halo_blocks - 1, 0), 0)),
            pl.BlockSpec((D_MODEL, tf), lambda j, i: (0, j)),
            pl.BlockSpec((D_MODEL, tf), lambda j, i: (0, j)),
            pl.BlockSpec((conv_w.shape[0], tf), lambda j, i: (0, j)),
            pl.BlockSpec((1, tf), lambda j, i: (0, j)),
        ],
        out_specs=pl.BlockSpec((tm, tf), lambda j, i: (i, j)),
        out_shape=jax.ShapeDtypeStruct((m, f), BF16),
        compiler_params=_params("parallel", "parallel"),
    )(h, h, w_gate, w_up, conv_w, conv_b.reshape(1, f))


def _mixer_weight(w_in_l, rw_cols):
    s5_u, lru_x, lru_g, m2_z, m2_xbc, m2_dt, rw_p = jnp.split(
        w_in_l, [512, 1024, 1536, 2048, 2048 + M2_CONV_DIM, 2048 + M2_CONV_DIM + N_HEADS], axis=1)
    d = w_in_l.shape[0]
    rkv, xw, xa, xg = jnp.split(rw_p, [3 * MIX_W, 3 * MIX_W + 32, 3 * MIX_W + 64], axis=1)
    padc = lambda t, n: jnp.concatenate([t, jnp.zeros((d, n - t.shape[1]), t.dtype)], axis=1)
    rw = jnp.concatenate([rkv, padc(xw, RW_LORA_PAD), padc(xa, RW_LORA_PAD), padc(xg, RW_LORA_PAD)], axis=1)
    rw = padc(rw, RW_PW)
    dt_e = jnp.repeat(m2_dt, HEAD_W, axis=1)
    proj_w = jnp.concatenate([rw, m2_xbc, lru_x, lru_g, m2_z, dt_e], axis=1)
    assert proj_w.shape[1] == PROJ_W
    return s5_u.astype(BF16), proj_w.astype(BF16)


def _rw_mu(mu):
    rkv, xw, xa, xg = jnp.split(mu, [3 * MIX_W, 3 * MIX_W + 32, 3 * MIX_W + 64])
    pad = lambda t: jnp.concatenate([t, jnp.zeros((RW_LORA_PAD - t.shape[0],), t.dtype)])
    out = jnp.concatenate([rkv, pad(xw), pad(xa), pad(xg)])
    return jnp.concatenate([out, jnp.zeros((RW_PW - out.shape[0],), out.dtype)]).reshape(1, RW_PW)


def kernel(x, norm_mix_g, w_in, s5_lambda_re, s5_lambda_im, s5_b_re, s5_b_im, s5_c_re, s5_c_im, s5_d, s5_log_dt, s5_w_glu, lru_conv_w, lru_conv_b, lru_w_a, lru_b_a, lru_w_x, lru_b_x, lru_lambda, m2_conv_w, m2_conv_b, m2_dt_bias, m2_a_log, m2_d, m2_norm_g, rw_mu, rw_w0, rw_w2, rw_a0, rw_a2, rw_g2, rw_k_k, rw_k_a, rw_r_k, rw_ln_g, rw_ln_b, w_branch, w_out, norm_ffn_g, w_ffn_gate, w_ffn_up, ffn_conv_w, ffn_conv_b, w_ffn_down, final_norm_g):
    bsz, seq, d = x.shape
    depth = w_in.shape[0]
    mixer_cols = w_in.shape[2] - w_branch.shape[1] * d
    xf = x.reshape(bsz * seq, d)
    for l in range(depth):
        h = _rms_norm(xf, norm_mix_g[l], BF16)
        w_s5, w_proj = _mixer_weight(w_in[l, :, :mixer_cols], None)
        proj = _matmul(h, w_proj)
        mats = _s5_matrices(s5_lambda_re[l], s5_lambda_im[l], s5_b_re[l], s5_b_im[l],
                            s5_c_re[l], s5_c_im[l], s5_d[l], s5_log_dt[l])
        y_a = _s5_mixer(h, w_s5, mats, s5_w_glu[l], bsz, seq)
        y_b = _lru_mixer(proj, lru_conv_w[l], lru_conv_b[l], lru_w_a[l], lru_b_a[l],
                         lru_w_x[l], lru_b_x[l], lru_lambda[l], bsz, seq)
        y_c = _m2_mixer(proj, m2_conv_w[l], m2_conv_b[l], m2_dt_bias[l], m2_a_log[l],
                        m2_d[l], m2_norm_g[l], bsz, seq)
        y_d = _rw_mixer(proj, _rw_mu(rw_mu[l]), rw_w0[l], rw_w2[l], rw_a0[l], rw_a2[l], rw_g2[l],
                        rw_k_k[l], rw_k_a[l], rw_r_k[l].reshape(-1), rw_ln_g[l], rw_ln_b[l], bsz, seq)
        w_gate = w_in[l, :, mixer_cols:].astype(BF16)
        merged = _merge(h, (y_a, y_b, y_c, y_d), w_gate, w_branch[l].astype(BF16))
        xf, h = _matmul_res_norm(merged, w_out[l].astype(BF16), xf, norm_ffn_g[l])
        act = _ffn_act(h, w_ffn_gate[l].astype(BF16), w_ffn_up[l].astype(BF16),
                       ffn_conv_w[l], ffn_conv_b[l], seq)
        xf = _matmul(act, w_ffn_down[l].astype(BF16), res=xf, tm=512, n_outer=True)
    return _rms_norm(xf, final_norm_g, F32).reshape(bsz, seq, d)
```

```python
import functools

import jax
import jax.numpy as jnp
from jax import lax
from jax.experimental import pallas as pl
from jax.experimental.pallas import tpu as pltpu

F32 = jnp.float32
BF16 = jnp.bfloat16

D_MODEL = 2048
MIX_W = 512
HEAD_W = 64
N_HEADS = MIX_W // HEAD_W
S5_GROUP = 16
S5_GROUPS = MIX_W // S5_GROUP
S5_STATE = 64
S5_Q = 8
S5_NSTATE = S5_GROUPS * S5_STATE
LANES = 128
S5_SG = MIX_W // LANES
S5_SGW = S5_Q * LANES
S5_SB = S5_NSTATE // S5_SG
S5_SCAN_ROWS = 512
LRU_HEADS = 8
LRU_C = 8.0
M2_GROUPS = 2
M2_STATE = 128
M2_CONV_DIM = MIX_W + 2 * M2_GROUPS * M2_STATE
M2_Q = 128
RW_Q = 64
RW_BATCH = 4
RW_RKV_W = 3 * MIX_W
RW_LORA_RANKS = (32, 32, 96)
RW_LORA_W = 256
RW_LN_EPS = 64e-5
FFN_DIM = 3 * D_MODEL
EPS = 1e-6
HALO = 8
BF16_ROWS = 16

PC_XBC = 0
PC_Z = 1024
PC_RKV = 1536
PC_LRU_X = 3072
PC_LRU_G = 3584
PC_LORA = 4096
PC_DT = 4352
PROJ_W = 4480
PROJ_TN = 640

VMEM_LIMIT_BYTES = 50 * 1024 * 1024


def _params(*sem):
    return pltpu.CompilerParams(dimension_semantics=sem, vmem_limit_bytes=VMEM_LIMIT_BYTES)


def _dot(a, b):
    return jnp.dot(a, b, preferred_element_type=F32)


def _dot_nt(a, b):
    return lax.dot_general(a, b, (((1,), (1,)), ((), ())), preferred_element_type=F32)


def _dot_tn(a, b):
    return lax.dot_general(a, b, (((0,), (0,)), ((), ())), preferred_element_type=F32)


def _split(x, terms):
    out = []
    for _ in range(terms - 1):
        hi = x.astype(BF16)
        out.append(hi)
        x = x - hi.astype(F32)
    out.append(x.astype(BF16))
    return out


def _exact_left(m, x, terms=3):
    return sum(_dot(m, p) for p in _split(x, terms))


def _exact_right(xs, m, terms=2):
    n = xs[0].shape[0]
    stacked = jnp.concatenate([p for x in xs for p in _split(x, terms)], axis=0)
    out = _dot(stacked, m)
    return [sum(out[(i * terms + t) * n:(i * terms + t + 1) * n] for t in range(terms))
            for i in range(len(xs))]


def _tril_mask(n, strict=False):
    r = lax.broadcasted_iota(jnp.int32, (n, n), 0)
    c = lax.broadcasted_iota(jnp.int32, (n, n), 1)
    return (r > c) if strict else (r >= c)


def _causal_conv(x, halo, w_ref, b_ref):
    k_w = w_ref.shape[0]
    t = x.shape[0]
    xe = jnp.concatenate([halo, x], axis=0)
    out = b_ref[...] + x * w_ref[k_w - 1:k_w, :]
    for j in range(k_w - 1):
        lag = k_w - 1 - j
        out = out + xe[HALO - lag:HALO - lag + t, :] * w_ref[j:j + 1, :]
    return out


def _mm_body(a_ref, b_ref, o_ref):
    o_ref[...] = _dot(a_ref[...], b_ref[...]).astype(o_ref.dtype)


def _mm_res_body(a_ref, b_ref, r_ref, o_ref):
    o_ref[...] = (_dot(a_ref[...], b_ref[...]) + r_ref[...]).astype(o_ref.dtype)


def _matmul(a, b, res=None, out_dtype=F32, tm=1024, tn=512, n_outer=False):
    m, k = a.shape
    n = b.shape[1]
    tm, tn = min(tm, m), min(tn, n)
    assert m % tm == 0 and n % tn == 0
    if n_outer:
        grid = (n // tn, m // tm)
        row = lambda j, i: (i, 0)
        col = lambda j, i: (0, j)
        out = lambda j, i: (i, j)
    else:
        grid = (m // tm, n // tn)
        row = lambda i, j: (i, 0)
        col = lambda i, j: (0, j)
        out = lambda i, j: (i, j)
    in_specs = [pl.BlockSpec((tm, k), row), pl.BlockSpec((k, tn), col)]
    args = [a, b]
    body = _mm_body
    if res is not None:
        in_specs.append(pl.BlockSpec((tm, tn), out))
        args.append(res)
        body = _mm_res_body
    return pl.pallas_call(
        body,
        name="matmul",
        grid=grid,
        in_specs=in_specs,
        out_specs=pl.BlockSpec((tm, tn), out),
        out_shape=jax.ShapeDtypeStruct((m, n), out_dtype),
        compiler_params=_params("parallel", "parallel"),
    )(*args)


def _matmul_blockdiag(a, w, res=None, tm=1024):
    m = a.shape[0]
    n = w.shape[1]
    tm = min(tm, m)
    blk = pl.BlockSpec((tm, n), lambda i, s: (i, s))
    in_specs = [blk, pl.BlockSpec((n, n), lambda i, s: (s, 0))]
    args = [a, w]
    if res is not None:
        in_specs.append(blk)
        args.append(res)
    return pl.pallas_call(
        _mm_body if res is None else _mm_res_body,
        name="matmul_blockdiag",
        grid=(m // tm, a.shape[1] // n),
        in_specs=in_specs,
        out_specs=blk,
        out_shape=jax.ShapeDtypeStruct(a.shape, F32),
        compiler_params=_params("parallel", "parallel"),
    )(*args)


def _norm_body(x_ref, g_ref, o_ref):
    x = x_ref[...]
    ms = jnp.mean(x * x, axis=-1, keepdims=True)
    o_ref[...] = (x * lax.rsqrt(ms + EPS) * g_ref[...]).astype(o_ref.dtype)


def _mm_res_norm_body(a_ref, w_ref, r_ref, g_ref, x_ref, h_ref):
    x = _dot(a_ref[...], w_ref[...]) + r_ref[...]
    x_ref[...] = x
    ms = jnp.mean(x * x, axis=-1, keepdims=True)
    h_ref[...] = (x * lax.rsqrt(ms + EPS) * g_ref[...]).astype(h_ref.dtype)


def _matmul_res_norm(a, w, res, g, tm=512):
    m, k = a.shape
    n = w.shape[1]
    tm = min(tm, m)
    rows = lambda width: pl.BlockSpec((tm, width), lambda i: (i, 0))
    return pl.pallas_call(
        _mm_res_norm_body,
        name="matmul_residual_norm",
        grid=(m // tm,),
        in_specs=[rows(k), pl.BlockSpec((k, n), lambda i: (0, 0)), rows(n), pl.BlockSpec((1, n), lambda i: (0, 0))],
        out_specs=[rows(n), rows(n)],
        out_shape=[jax.ShapeDtypeStruct((m, n), F32), jax.ShapeDtypeStruct((m, n), BF16)],
        compiler_params=_params("parallel"),
    )(a, w, res, g.reshape(1, n))


def _mm_ksplit_res_norm_body(a_ref, w_ref, r_ref, g_ref, x_ref, h_ref):
    kk = pl.program_id(1)

    @pl.when(kk == 0)
    def _():
        x_ref[...] = r_ref[...]

    x_ref[...] += _dot(a_ref[...], w_ref[...])

    @pl.when(kk == pl.num_programs(1) - 1)
    def _():
        x = x_ref[...]
        ms = jnp.mean(x * x, axis=-1, keepdims=True)
        h_ref[...] = (x * lax.rsqrt(ms + EPS) * g_ref[...]).astype(h_ref.dtype)


def _matmul_ksplit_res_norm(a, w, res, g, h_dtype, tm=512, tk=1024):
    m, k = a.shape
    n = w.shape[1]
    tm = min(tm, m)
    rows = pl.BlockSpec((tm, n), lambda i, kk: (i, 0))
    return pl.pallas_call(
        _mm_ksplit_res_norm_body,
        name="matmul_ksplit_residual_norm",
        grid=(m // tm, k // tk),
        in_specs=[pl.BlockSpec((tm, tk), lambda i, kk: (i, kk)), pl.BlockSpec((tk, n), lambda i, kk: (kk, 0)),
                  rows, pl.BlockSpec((1, n), lambda i, kk: (0, 0))],
        out_specs=[rows, rows],
        out_shape=[jax.ShapeDtypeStruct((m, n), F32), jax.ShapeDtypeStruct((m, n), h_dtype)],
        compiler_params=_params("parallel", "arbitrary"),
    )(a, w, res, g.reshape(1, n))


def _rms_norm(x, g, out_dtype, tm=512):
    m, d = x.shape
    tm = min(tm, m)
    return pl.pallas_call(
        _norm_body,
        name="rms_norm",
        grid=(m // tm,),
        in_specs=[pl.BlockSpec((tm, d), lambda i: (i, 0)), pl.BlockSpec((1, d), lambda i: (0, 0))],
        out_specs=pl.BlockSpec((tm, d), lambda i: (i, 0)),
        out_shape=jax.ShapeDtypeStruct((m, d), out_dtype),
        compiler_params=_params("parallel"),
    )(x, g.reshape(1, d))


def _s5_matrices(lam_re, lam_im, b_re, b_im, c_re, c_im, d, log_dt):
    g_n, p_n, c_n, q = S5_GROUPS, S5_STATE, S5_GROUP, S5_Q
    hp = lax.Precision.HIGHEST
    dt = jnp.exp(log_dt)[:, None]
    n = jnp.arange(q + 1, dtype=F32)[:, None, None]
    mag = jnp.exp(n * (lam_re * dt))
    pw_re = mag * jnp.cos(n * (lam_im * dt))
    pw_im = mag * jnp.sin(n * (lam_im * dt))
    den = lam_re * lam_re + lam_im * lam_im
    nr, ni = pw_re[1] - 1.0, pw_im[1]
    f_re = (nr * lam_re + ni * lam_im) / den
    f_im = (ni * lam_re - nr * lam_im) / den
    e_re = f_re[..., None] * b_re - f_im[..., None] * b_im
    e_im = f_re[..., None] * b_im + f_im[..., None] * b_re
    cp_re = c_re[None] * pw_re[:, :, None, :] - c_im[None] * pw_im[:, :, None, :]
    cp_im = c_re[None] * pw_im[:, :, None, :] + c_im[None] * pw_re[:, :, None, :]
    kern = (jnp.einsum("tgop,gpi->tgio", cp_re[:q], e_re, precision=hp)
            - jnp.einsum("tgop,gpi->tgio", cp_im[:q], e_im, precision=hp))
    kern = kern.at[0].add(d.reshape(g_n, c_n)[:, :, None] * jnp.eye(c_n, dtype=F32))
    sg_n, gl_n = S5_SG, g_n // S5_SG
    rows = sg_n * S5_SGW
    lag = jnp.arange(q)[None, :] - jnp.arange(q)[:, None]
    kt = jnp.where((lag >= 0)[:, :, None, None, None], kern[jnp.clip(lag, 0, q - 1)], 0.0)
    kt = kt.reshape(q, q, sg_n, gl_n, c_n, c_n).transpose(2, 0, 3, 4, 1, 5)
    kt = kt.reshape(rows, q * c_n)
    rev_re, rev_im = pw_re[q - 1 - jnp.arange(q)], pw_im[q - 1 - jnp.arange(q)]
    et_re, et_im = e_re.transpose(0, 2, 1)[None], e_im.transpose(0, 2, 1)[None]
    ws_re = rev_re[:, :, None, :] * et_re - rev_im[:, :, None, :] * et_im
    ws_im = rev_re[:, :, None, :] * et_im + rev_im[:, :, None, :] * et_re
    ws = jnp.stack([ws_re, ws_im], axis=3)
    ws = ws.reshape(q, sg_n, gl_n, c_n, 2, p_n).transpose(1, 0, 2, 3, 4, 5).reshape(rows, 2 * p_n)
    wy = jnp.stack([cp_re[1:], -cp_im[1:]], axis=0)
    wy = wy.reshape(2, q, sg_n, gl_n, c_n, p_n).transpose(2, 0, 3, 5, 1, 4)
    wy = wy.reshape(rows, q * c_n)

    col = jnp.arange(S5_SGW)
    small = jnp.arange(q * c_n)[:, None]
    tok_rep = ((small // c_n == col[None, :] // LANES) & (small % c_n == col[None, :] % c_n)).astype(BF16)
    small = jnp.arange(2 * p_n)[:, None]
    st_rep = ((small // p_n == col[None, :] // S5_SB) & (small % p_n == col[None, :] % p_n)).astype(BF16)
    row = jnp.arange(rows)
    g_tok_r, g_tok_c = (row % LANES) // c_n, (col % LANES) // c_n
    g_st_r, g_st_c = (row % S5_SB) // p_n, (col % S5_SB) // p_n
    expand = lambda table, rep: jnp.dot(table.astype(BF16), rep, preferred_element_type=BF16)
    zero = jnp.zeros((), BF16)
    toep = jnp.where(g_tok_r[:, None] == g_tok_c[None, :], expand(kt, tok_rep), zero)
    w_state = jnp.where(g_tok_r[:, None] == g_st_c[None, :], expand(ws, st_rep), zero)
    w_out = jnp.where(g_st_r[:, None] == g_tok_c[None, :], expand(wy, tok_rep), zero)
    return toep, w_state, w_out, pw_re[q].reshape(1, S5_NSTATE), pw_im[q].reshape(1, S5_NSTATE)


def _s5_scan_body(ar_ref, ai_ref, s_ref, o_ref, c_ref):
    rows = BF16_ROWS
    nb = s_ref.shape[0]
    a_re, a_im = ar_ref[...], ai_ref[...]
    half = a_re.shape[1]

    @pl.when(pl.program_id(1) == 0)
    def _():
        c_ref[...] = jnp.zeros_like(c_ref)

    def block(i, carry):
        base = pl.multiple_of(i * rows, rows)
        xs = [s_ref[b, pl.ds(base, rows), :] for b in range(nb)]
        outs = [[] for _ in range(nb)]
        carry = list(carry)
        for r in range(rows):
            for b in range(nb):
                s_re, s_im = carry[b]
                outs[b].append(jnp.concatenate([s_re, s_im], axis=1))
                carry[b] = (a_re * s_re - a_im * s_im + xs[b][r:r + 1, :half],
                            a_re * s_im + a_im * s_re + xs[b][r:r + 1, half:])
        for b in range(nb):
            o_ref[b, pl.ds(base, rows), :] = jnp.concatenate(outs[b], axis=0).astype(o_ref.dtype)
        return tuple(carry)

    init = tuple((c_ref[b, :, :half], c_ref[b, :, half:]) for b in range(nb))
    last = lax.fori_loop(0, s_ref.shape[1] // rows, block, init)
    for b in range(nb):
        c_ref[b] = jnp.concatenate(last[b], axis=1)


def _s5_fold_body(h_ref, w_ref, o_ref, tok_ref):
    u = _dot(h_ref[...], w_ref[...])
    tc = o_ref.shape[0]
    for c in range(S5_SG):
        tok_ref[c] = u[:, c * LANES:(c + 1) * LANES]
        for j in range(S5_Q):
            lo = c * S5_SGW + j * LANES
            o_ref[:, lo:lo + LANES] = tok_ref[c, pl.ds(j, tc, stride=S5_Q), :].astype(o_ref.dtype)


def _s5_glu_body(y_ref, w_ref, o_ref, tok_ref):
    tc = y_ref.shape[0]
    for c in range(S5_SG):
        for j in range(S5_Q):
            lo = c * S5_SGW + j * LANES
            tok_ref[c, pl.ds(j, tc, stride=S5_Q), :] = y_ref[:, lo:lo + LANES]
    y = jax.nn.gelu(jnp.concatenate([tok_ref[c] for c in range(S5_SG)], axis=1))
    o_ref[...] = (y * jax.nn.sigmoid(_dot(y.astype(BF16), w_ref[...]))).astype(o_ref.dtype)


def _s5_mixer(h, w_u, mats, w_glu, bsz, seq):
    toep, w_state, w_out, aq_re, aq_im = mats
    m = bsz * seq
    nc = seq // S5_Q
    width = S5_Q * MIX_W
    tm = min(1024, m)
    tc = tm // S5_Q
    uc = pl.pallas_call(
        _s5_fold_body,
        name="s5_input_proj",
        grid=(m // tm,),
        in_specs=[pl.BlockSpec((tm, D_MODEL), lambda i: (i, 0)), pl.BlockSpec((D_MODEL, MIX_W), lambda i: (0, 0))],
        out_specs=pl.BlockSpec((tc, width), lambda i: (i, 0)),
        out_shape=jax.ShapeDtypeStruct((m // S5_Q, width), BF16),
        scratch_shapes=[pltpu.VMEM((MIX_W // LANES, tm, LANES), F32)],
        compiler_params=_params("parallel"),
    )(h, w_u)
    y1 = _matmul_blockdiag(uc, toep)
    s_end = _matmul_blockdiag(uc, w_state)
    rb = min(S5_SCAN_ROWS, nc)
    s_in = pl.pallas_call(
        _s5_scan_body,
        name="s5_chunk_scan",
        grid=(S5_SG, nc // rb),
        in_specs=[
            pl.BlockSpec((1, S5_SB), lambda j, r: (0, j)),
            pl.BlockSpec((1, S5_SB), lambda j, r: (0, j)),
            pl.BlockSpec((bsz, rb, S5_SGW), lambda j, r: (0, r, j)),
        ],
        out_specs=pl.BlockSpec((bsz, rb, S5_SGW), lambda j, r: (0, r, j)),
        out_shape=jax.ShapeDtypeStruct((bsz, nc, width), BF16),
        scratch_shapes=[pltpu.VMEM((bsz, 1, S5_SGW), F32)],
        compiler_params=_params("parallel", "arbitrary"),
    )(aq_re, aq_im, s_end.reshape(bsz, nc, width))
    y = _matmul_blockdiag(s_in.reshape(bsz * nc, width), w_out, res=y1)
    return pl.pallas_call(
        _s5_glu_body,
        name="s5_glu",
        grid=(m // tm,),
        in_specs=[pl.BlockSpec((tc, width), lambda i: (i, 0)), pl.BlockSpec((MIX_W, MIX_W), lambda i: (0, 0))],
        out_specs=pl.BlockSpec((tm, MIX_W), lambda i: (i, 0)),
        out_shape=jax.ShapeDtypeStruct((m, MIX_W), BF16),
        scratch_shapes=[pltpu.VMEM((MIX_W // LANES, tm, LANES), F32)],
        compiler_params=_params("parallel"),
    )(y, w_glu.astype(BF16))


def _lru_body(x_ref, g_ref, cw_ref, cb_ref, wa_ref, ba_ref, wx_ref, bx_ref, lam_ref, o_ref,
              halo_ref, h_ref, a_s, b_s):
    nb, t, _ = x_ref.shape

    @pl.when(pl.program_id(0) == 0)
    def _():
        halo_ref[...] = jnp.zeros_like(halo_ref)
        h_ref[...] = jnp.zeros_like(h_ref)

    soft = jax.nn.softplus(-lam_ref[...])
    for b in range(nb):
        x_in = x_ref[b]
        x = _causal_conv(x_in, halo_ref[b], cw_ref, cb_ref)
        halo_ref[b] = x_in[t - HALO:, :]
        xb = x.astype(BF16)
        r = jax.nn.sigmoid(_dot(xb, wa_ref[...]) + ba_ref[...])
        i = jax.nn.sigmoid(_dot(xb, wx_ref[...]) + bx_ref[...])
        log_a = (-LRU_C * r) * soft
        a_s[b] = jnp.exp(log_a)
        b_s[b] = x * i * jnp.sqrt(1.0 - jnp.exp(2.0 * log_a))
    rows = 8

    def block(k, hs):
        base = pl.multiple_of(k * rows, rows)
        av = [a_s[b, pl.ds(base, rows), :] for b in range(nb)]
        bv = [b_s[b, pl.ds(base, rows), :] for b in range(nb)]
        hs = list(hs)
        outs = [[] for _ in range(nb)]
        for rr in range(rows):
            for b in range(nb):
                hs[b] = av[b][rr:rr + 1, :] * hs[b] + bv[b][rr:rr + 1, :]
                outs[b].append(hs[b])
        for b in range(nb):
            b_s[b, pl.ds(base, rows), :] = jnp.concatenate(outs[b], axis=0)
        return tuple(hs)

    hs = lax.fori_loop(0, t // rows, block, tuple(h_ref[b] for b in range(nb)))
    for b in range(nb):
        h_ref[b] = hs[b]
        o_ref[b] = (b_s[b] * jax.nn.gelu(g_ref[b])).astype(o_ref.dtype)


def _block_diag(w):
    h_n, n, _ = w.shape
    eye = jnp.eye(h_n, dtype=w.dtype)
    return (w[:, :, None, :] * eye[:, None, :, None]).reshape(h_n * n, h_n * n)


def _lru_mixer(proj, conv_w, conv_b, w_a, b_a, w_x, b_x, lam, bsz, seq):
    t = min(256, seq)
    vec = lambda v: v.reshape(1, MIX_W)
    const = lambda shape: pl.BlockSpec(shape, lambda c: (0, 0))
    proj3 = proj.reshape(bsz, seq, proj.shape[1])
    out = pl.pallas_call(
        _lru_body,
        name="rglru_mixer",
        grid=(seq // t,),
        in_specs=[
            pl.BlockSpec((bsz, t, MIX_W), lambda c: (0, c, PC_LRU_X // MIX_W)),
            pl.BlockSpec((bsz, t, MIX_W), lambda c: (0, c, PC_LRU_G // MIX_W)),
            const(conv_w.shape), const((1, MIX_W)),
            const((MIX_W, MIX_W)), const((1, MIX_W)),
            const((MIX_W, MIX_W)), const((1, MIX_W)), const((1, MIX_W)),
        ],
        out_specs=pl.BlockSpec((bsz, t, MIX_W), lambda c: (0, c, 0)),
        out_shape=jax.ShapeDtypeStruct((bsz, seq, MIX_W), BF16),
        scratch_shapes=[
            pltpu.VMEM((bsz, HALO, MIX_W), F32), pltpu.VMEM((bsz, 1, MIX_W), F32),
            pltpu.VMEM((bsz, t, MIX_W), F32), pltpu.VMEM((bsz, t, MIX_W), F32),
        ],
        compiler_params=_params("arbitrary"),
    )(proj3, proj3, conv_w, vec(conv_b), _block_diag(w_a).astype(BF16), vec(b_a),
      _block_diag(w_x).astype(BF16), vec(b_x), vec(lam))
    return out.reshape(bsz * seq, MIX_W)


def _m2_body(z_ref, xbc_ref, dt_ref, rep_ref, cw_ref, cb_ref, dtb_ref, alog_ref, d_ref, ng_ref, o_ref,
             halo_ref, st_ref):
    q = z_ref.shape[0]
    hg = N_HEADS // M2_GROUPS
    gw = hg * HEAD_W

    @pl.when(pl.program_id(1) == 0)
    def _():
        halo_ref[...] = jnp.zeros_like(halo_ref)
        st_ref[...] = jnp.zeros_like(st_ref)

    xbc = xbc_ref[...]
    conv = _causal_conv(xbc, halo_ref[...], cw_ref, cb_ref)
    halo_ref[...] = xbc[q - HALO:, :]
    xc = conv * jax.nn.sigmoid(conv)
    xs = xc[:, :MIX_W]
    b_all = xc[:, MIX_W:MIX_W + M2_GROUPS * M2_STATE]
    c_all = xc[:, MIX_W + M2_GROUPS * M2_STATE:]
    dt_raw = _exact_right([dt_ref[...]], rep_ref[...])[0]
    dt = jax.nn.softplus(dt_raw + dtb_ref[...])
    ad = dt * (-jnp.exp(alog_ref[...]))
    causal = _tril_mask(q)
    a_cs = _exact_left(causal.astype(BF16), ad)
    a_cs_t = a_cs.T
    a_last = a_cs[q - 1:q, :]
    xd = xs * dt
    xd_st = (xd * jnp.exp(a_last - a_cs)).astype(BF16)
    e_cs = jnp.exp(a_cs)
    xd_b = xd.astype(BF16)
    ys = []
    for g in range(M2_GROUPS):
        bg = b_all[:, g * M2_STATE:(g + 1) * M2_STATE].astype(BF16)
        cg = c_all[:, g * M2_STATE:(g + 1) * M2_STATE].astype(BF16)
        cb = _dot_nt(cg, bg)
        y_off = _dot(cg, st_ref[g].astype(BF16)) * e_cs[:, g * gw:(g + 1) * gw]
        for hh in range(hg):
            lo = (g * hg + hh) * HEAD_W
            col = a_cs[:, lo:lo + HEAD_W]
            col = jnp.concatenate([col] * (q // HEAD_W), axis=1)
            seg = col - a_cs_t[lo:lo + 1, :]
            dec = jnp.exp(jnp.where(causal, seg, -jnp.inf))
            y_d = _dot((cb * dec).astype(BF16), xd_b[:, lo:lo + HEAD_W])
            ys.append(y_d + y_off[:, hh * HEAD_W:(hh + 1) * HEAD_W])
        upd = _dot_tn(bg, xd_st[:, g * gw:(g + 1) * gw])
        st_ref[g] = st_ref[g] * jnp.exp(a_last[:, g * gw:(g + 1) * gw]) + upd
    y = jnp.concatenate(ys, axis=1) + d_ref[...] * xs
    z = z_ref[...]
    y = y * (z * jax.nn.sigmoid(z))
    ms = jnp.mean(y * y, axis=-1, keepdims=True)
    o_ref[...] = (y * lax.rsqrt(ms + EPS) * ng_ref[...]).astype(o_ref.dtype)


def _per_head(v):
    return jnp.repeat(v, HEAD_W).reshape(1, -1)


def _m2_mixer(proj, conv_w, conv_b, dt_bias, a_log, d, norm_g, bsz, seq):
    q = min(M2_Q, seq)
    nc = seq // q
    const = lambda shape: pl.BlockSpec(shape, lambda b, c: (0, 0))
    head_of_lane = jnp.arange(MIX_W)[None, :] // HEAD_W
    rep = (jnp.arange(LANES)[:, None] == head_of_lane).astype(BF16)
    return pl.pallas_call(
        _m2_body,
        name="mamba2_mixer",
        grid=(bsz, nc),
        in_specs=[
            pl.BlockSpec((q, MIX_W), lambda b, c: (b * nc + c, PC_Z // MIX_W)),
            pl.BlockSpec((q, M2_CONV_DIM), lambda b, c: (b * nc + c, PC_XBC // M2_CONV_DIM)),
            pl.BlockSpec((q, LANES), lambda b, c: (b * nc + c, PC_DT // LANES)),
            const((LANES, MIX_W)), const(conv_w.shape), const((1, M2_CONV_DIM)),
            const((1, MIX_W)), const((1, MIX_W)), const((1, MIX_W)), const((1, MIX_W)),
        ],
        out_specs=pl.BlockSpec((q, MIX_W), lambda b, c: (b * nc + c, 0)),
        out_shape=jax.ShapeDtypeStruct((bsz * seq, MIX_W), BF16),
        scratch_shapes=[
            pltpu.VMEM((HALO, M2_CONV_DIM), F32),
            pltpu.VMEM((M2_GROUPS, M2_STATE, (N_HEADS // M2_GROUPS) * HEAD_W), F32),
        ],
        compiler_params=_params("parallel", "arbitrary"),
    )(proj, proj, proj, rep, conv_w, conv_b.reshape(1, -1), _per_head(dt_bias), _per_head(a_log),
      _per_head(d), norm_g.reshape(1, -1))


def _rw_body(p_ref, pl_ref, mu_ref, mul_ref, w0_ref, w2_ref, a0_ref, a2_ref, g2_ref, kk_ref, ka_ref, rk_ref,
             lng_ref, lnb_ref, ones_ref, o_ref, prev_ref, prevl_ref, st_ref):
    nb, q, _ = p_ref.shape
    rows = nb * q

    @pl.when(pl.program_id(1) == 0)
    def _():
        prev_ref[...] = jnp.zeros_like(prev_ref)
        prevl_ref[...] = jnp.zeros_like(prevl_ref)
        st_ref[...] = jnp.zeros_like(st_ref)

    def token_mix(x_ref, last_ref, m_ref):
        x = x_ref[...].reshape(rows, x_ref.shape[2])
        row = lax.broadcasted_iota(jnp.int32, x.shape, 0)
        shifted = pltpu.roll(x, 1, axis=0)
        for b in range(nb):
            shifted = jnp.where(row == b * q, last_ref[b, 0:1, :], shifted)
            last_ref[b, 0:1, :] = x[(b + 1) * q - 1:(b + 1) * q, :]
        return x + (shifted - x) * m_ref[...]

    p = token_mix(p_ref, prev_ref, mu_ref)
    lora = token_mix(pl_ref, prevl_ref, mul_ref)
    r = p[:, 0:MIX_W]
    k = p[:, MIX_W:2 * MIX_W]
    v = p[:, 2 * MIX_W:3 * MIX_W]
    w = -jax.nn.softplus(-(w0_ref[...] + _dot(jnp.tanh(lora).astype(BF16), w2_ref[...]))) - 0.5
    log_w = -jnp.exp(w)
    a_sig = jax.nn.sigmoid(a0_ref[...] + _dot(lora.astype(BF16), a2_ref[...]))
    gate = _dot(jax.nn.sigmoid(lora).astype(BF16), g2_ref[...])
    ones = ones_ref[...]
    kk = k * kk_ref[...]
    k = k * (1.0 + (a_sig - 1.0) * ka_ref[...])
    kk_sq, rk_sum = _exact_right([kk * kk, r * k * rk_ref[...]], ones)
    kk = kk / jnp.maximum(jnp.sqrt(kk_sq), 1e-12)
    a_vec = -kk
    b_vec = kk * a_sig

    rr = lax.broadcasted_iota(jnp.int32, (rows, rows), 0)
    cc = lax.broadcasted_iota(jnp.int32, (rows, rows), 1)
    cum_mask = jnp.where((rr >= cc) & ((rr // q) == (cc // q)), 1.0, 0.0).astype(BF16)
    gam = _exact_left(cum_mask, log_w)
    g_last = jnp.concatenate(
        [jnp.broadcast_to(gam[(b + 1) * q - 1:(b + 1) * q, :], (q, MIX_W)) for b in range(nb)], axis=0)
    e_neg = jnp.exp(-gam)
    e_rem = jnp.exp(g_last - gam)
    a_t = a_vec * jnp.exp(gam - log_w)
    r_t = r * jnp.exp(gam)
    b_t = b_vec * e_neg
    k_t = k * e_neg
    b_h = b_vec * e_rem
    k_h = k * e_rem
    e_last = jnp.exp(g_last)
    assert q == HEAD_W
    row2 = lax.broadcasted_iota(jnp.int32, (q, 2 * HEAD_W), 0)
    lane2 = lax.broadcasted_iota(jnp.int32, (q, 2 * HEAD_W), 1)
    low = lane2 < HEAD_W
    col2 = jnp.where(low, lane2, lane2 - HEAD_W)
    keep_z0 = low | (col2 < row2)
    keep_t = col2 <= row2
    strict = _tril_mask(q, strict=True)
    pairs = [(b, h) for b in range(nb) for h in range(N_HEADS)]
    zeros_w = jnp.zeros((HEAD_W, 2 * HEAD_W), BF16)
    zeros_x = jnp.zeros((q, HEAD_W), F32)

    def blk(t, key):
        b, h = key
        return t[b * q:(b + 1) * q, h * HEAD_W:(h + 1) * HEAD_W]

    p0, p1, s0, z = {}, {}, {}, {}
    for key in pairs:
        ar = jnp.concatenate([blk(a_t, key), blk(r_t, key)], axis=0).astype(BF16)
        s0[key] = st_ref[key[0], key[1]]
        rhs = jnp.concatenate([s0[key], blk(b_t, key), blk(k_t, key), zeros_x], axis=0).astype(BF16)
        prod = _dot_nt(ar, rhs)
        p0[key] = prod[:, :2 * HEAD_W]
        p1[key] = prod[:, 2 * HEAD_W:]
    for key in pairs:
        a_k = jnp.where(strict, p1[key][:q, :HEAD_W], 0.0).astype(BF16)
        x0 = _dot(a_k, blk(v, key).astype(BF16))
        z[key] = jnp.where(keep_z0, p0[key][:q], 0.0) + jnp.concatenate([x0, zeros_x], axis=1)
    steps = max(1, (q - 1).bit_length())
    for _ in range(steps):
        for key in pairs:
            zb = z[key].astype(BF16)
            z[key] = jnp.where(low, z[key], 0.0) + _dot(zb, jnp.concatenate([zeros_w, zb], axis=0))
    ys = {}
    for key in pairs:
        vu = jnp.concatenate([blk(v, key), z[key][:, :HEAD_W]], axis=0).astype(BF16)
        t_mat = jnp.where(keep_t, jnp.where(low, p1[key][q:], p0[key][q:]), 0.0)
        ys[key] = p0[key][q:, :HEAD_W] + _dot(t_mat.astype(BF16), vu)
        khb = jnp.concatenate([blk(k_h, key), blk(b_h, key)], axis=0).astype(BF16)
        st_ref[key[0], key[1]] = s0[key] * blk(e_last, key)[0:1, :] + _dot_tn(vu, khb)
    y = jnp.concatenate(
        [jnp.concatenate([ys[(b, h)] for h in range(N_HEADS)], axis=1) for b in range(nb)], axis=0)
    inv_n = 1.0 / HEAD_W
    yc = y - _exact_right([y], ones)[0] * inv_n
    var = _exact_right([yc * yc], ones)[0] * inv_n
    y = yc * lax.rsqrt(var + RW_LN_EPS) * lng_ref[...] + lnb_ref[...]
    bonus = rk_sum * v
    o_ref[...] = ((y + bonus) * gate).astype(o_ref.dtype).reshape(o_ref.shape)


def _pad_rows(w, rows, off):
    return jnp.zeros((rows, w.shape[1]), w.dtype).at[off:off + w.shape[0]].set(w)


def _rw_mixer(proj, mu, w0, w2, a0, a2, g2, k_k, k_a, r_k, ln_g, ln_b, bsz, seq):
    q = min(RW_Q, seq)
    nc = seq // q
    nb = RW_BATCH if bsz % RW_BATCH == 0 else 1
    vec = lambda t: t.reshape(1, MIX_W)
    const = lambda shape: pl.BlockSpec(shape, lambda b, c: (0, 0))
    offs = (0, RW_LORA_RANKS[0], RW_LORA_RANKS[0] + RW_LORA_RANKS[1])
    lora = lambda t, off: _pad_rows(t, RW_LORA_W, off).astype(BF16)
    mu_rkv, mu_lora = mu[:RW_RKV_W].reshape(1, RW_RKV_W), _pad_rows(mu[RW_RKV_W:, None], RW_LORA_W, 0).reshape(1, RW_LORA_W)
    ones = _block_diag(jnp.ones((N_HEADS, HEAD_W, HEAD_W), F32)).astype(BF16)
    proj3 = proj.reshape(bsz, seq, proj.shape[1])
    out = pl.pallas_call(
        _rw_body,
        name="rwkv7_mixer",
        grid=(bsz // nb, nc),
        in_specs=[
            pl.BlockSpec((nb, q, RW_RKV_W), lambda b, c: (b, c, PC_RKV // RW_RKV_W)),
            pl.BlockSpec((nb, q, RW_LORA_W), lambda b, c: (b, c, PC_LORA // RW_LORA_W)),
            const((1, RW_RKV_W)), const((1, RW_LORA_W)), const((1, MIX_W)), const((RW_LORA_W, MIX_W)),
            const((1, MIX_W)), const((RW_LORA_W, MIX_W)), const((RW_LORA_W, MIX_W)),
            const((1, MIX_W)), const((1, MIX_W)), const((1, MIX_W)),
            const((1, MIX_W)), const((1, MIX_W)), const((MIX_W, MIX_W)),
        ],
        out_specs=pl.BlockSpec((nb, q, MIX_W), lambda b, c: (b, c, 0)),
        out_shape=jax.ShapeDtypeStruct((bsz, seq, MIX_W), BF16),
        scratch_shapes=[pltpu.VMEM((nb, HALO, RW_RKV_W), F32), pltpu.VMEM((nb, HALO, RW_LORA_W), F32),
                        pltpu.VMEM((nb, N_HEADS, HEAD_W, HEAD_W), F32)],
        compiler_params=_params("parallel", "arbitrary"),
    )(proj3, proj3, mu_rkv, mu_lora, vec(w0), lora(w2, offs[0]), vec(a0), lora(a2, offs[1]), lora(g2, offs[2]),
      vec(k_k), vec(k_a), vec(r_k), vec(ln_g), vec(ln_b), ones)
    return out.reshape(bsz * seq, MIX_W)


def _merge_body(h_ref, *refs):
    nb = (len(refs) - 2) // 2
    y_refs, wg_refs, wp_ref, o_ref = refs[:nb], refs[nb:2 * nb], refs[2 * nb], refs[2 * nb + 1]
    h = h_ref[...]
    acc = None
    for kk in range(nb):
        term = jax.nn.sigmoid(_dot(h, wg_refs[kk][...])) * _dot(y_refs[kk][...], wp_ref[kk])
        acc = term if acc is None else acc + term
    o_ref[...] = acc.astype(o_ref.dtype)


def _merge(h, ys, w_gate, w_branch, tm=512, tn=512):
    m = h.shape[0]
    tm = min(tm, m)
    nb = len(ys)
    y_spec = pl.BlockSpec((tm, MIX_W), lambda j, i: (i, 0))
    gate_specs = [pl.BlockSpec((D_MODEL, tn), lambda j, i, k=k: (0, k * (D_MODEL // tn) + j)) for k in range(nb)]
    return pl.pallas_call(
        _merge_body,
        name="gated_merge",
        grid=(D_MODEL // tn, m // tm),
        in_specs=[pl.BlockSpec((tm, D_MODEL), lambda j, i: (i, 0))] + [y_spec] * nb + gate_specs + [
            pl.BlockSpec((nb, MIX_W, tn), lambda j, i: (0, 0, j)),
        ],
        out_specs=pl.BlockSpec((tm, tn), lambda j, i: (i, j)),
        out_shape=jax.ShapeDtypeStruct((m, D_MODEL), BF16),
        compiler_params=_params("parallel", "parallel"),
    )(h, *ys, *([w_gate] * nb), w_branch)


def _ffn_body(h_ref, hh_ref, wg_ref, wu_ref, cw_ref, cb_ref, o_ref, *, tiles_per_seq):
    t = h_ref.shape[0]
    nh = hh_ref.shape[0]
    h = h_ref[...]
    seq_start = (pl.program_id(1) % tiles_per_seq) == 0
    halo = jnp.where(seq_start, jnp.zeros_like(hh_ref[...]), hh_ref[...])
    g_all = _dot(jnp.concatenate([halo, h], axis=0), wg_ref[...])
    k_w = cw_ref.shape[0]
    u = cb_ref[...] + g_all[nh:, :] * cw_ref[k_w - 1:k_w, :]
    for j in range(k_w - 1):
        lag = k_w - 1 - j
        u = u + g_all[nh - lag:nh - lag + t, :] * cw_ref[j:j + 1, :]
    o_ref[...] = (jax.nn.gelu(u) * _dot(h, wu_ref[...])).astype(o_ref.dtype)


def _ffn_act(h, w_gate, w_up, conv_w, conv_b, seq, tm=1024, tf=512):
    m = h.shape[0]
    tm = min(tm, seq)
    f = w_gate.shape[1]
    halo_blocks = tm // BF16_ROWS
    return pl.pallas_call(
        functools.partial(_ffn_body, tiles_per_seq=seq // tm),
        name="ffn_gate_up",
        grid=(f // tf, m // tm),
        in_specs=[
            pl.BlockSpec((tm, D_MODEL), lambda j, i: (i, 0)),
            pl.BlockSpec((BF16_ROWS, D_MODEL), lambda j, i: (jnp.maximum(i * halo_blocks - 1, 0), 0)),
            pl.BlockSpec((D_MODEL, tf), lambda j, i: (0, j)),
            pl.BlockSpec((D_MODEL, tf), lambda j, i: (0, j)),
            pl.BlockSpec((conv_w.shape[0], tf), lambda j, i: (0, j)),
            pl.BlockSpec((1, tf), lambda j, i: (0, j)),
        ],
        out_specs=pl.BlockSpec((tm, tf), lambda j, i: (i, j)),
        out_shape=jax.ShapeDtypeStruct((m, f), BF16),
        compiler_params=_params("parallel", "parallel"),
    )(h, h, w_gate, w_up, conv_w, conv_b.reshape(1, f))


def _mixer_weight(w_in_l):
    s5_u, lru_x, lru_g, m2_z, m2_xbc, m2_dt, rw_p = jnp.split(
        w_in_l, [512, 1024, 1536, 2048, 2048 + M2_CONV_DIM, 2048 + M2_CONV_DIM + N_HEADS], axis=1)
    d = w_in_l.shape[0]
    padc = lambda t, n: jnp.concatenate([t, jnp.zeros((d, n - t.shape[1]), t.dtype)], axis=1)
    proj_w = jnp.concatenate([m2_xbc, m2_z, rw_p[:, :RW_RKV_W], lru_x, lru_g,
                              padc(rw_p[:, RW_RKV_W:], RW_LORA_W), padc(m2_dt, LANES)], axis=1)
    assert proj_w.shape[1] == PROJ_W
    return s5_u.astype(BF16), proj_w.astype(BF16)


def kernel(x, norm_mix_g, w_in, s5_lambda_re, s5_lambda_im, s5_b_re, s5_b_im, s5_c_re, s5_c_im, s5_d, s5_log_dt, s5_w_glu, lru_conv_w, lru_conv_b, lru_w_a, lru_b_a, lru_w_x, lru_b_x, lru_lambda, m2_conv_w, m2_conv_b, m2_dt_bias, m2_a_log, m2_d, m2_norm_g, rw_mu, rw_w0, rw_w2, rw_a0, rw_a2, rw_g2, rw_k_k, rw_k_a, rw_r_k, rw_ln_g, rw_ln_b, w_branch, w_out, norm_ffn_g, w_ffn_gate, w_ffn_up, ffn_conv_w, ffn_conv_b, w_ffn_down, final_norm_g):
    bsz, seq, d = x.shape
    depth = w_in.shape[0]
    mixer_cols = w_in.shape[2] - w_branch.shape[1] * d
    xf = x.reshape(bsz * seq, d)
    h = _rms_norm(xf, norm_mix_g[0], BF16)
    for l in range(depth):
        w_s5, w_proj = _mixer_weight(w_in[l, :, :mixer_cols])
        proj = _matmul(h, w_proj, tn=PROJ_TN)
        mats = _s5_matrices(s5_lambda_re[l], s5_lambda_im[l], s5_b_re[l], s5_b_im[l],
                            s5_c_re[l], s5_c_im[l], s5_d[l], s5_log_dt[l])
        y_a = _s5_mixer(h, w_s5, mats, s5_w_glu[l], bsz, seq)
        y_b = _lru_mixer(proj, lru_conv_w[l], lru_conv_b[l], lru_w_a[l], lru_b_a[l],
                         lru_w_x[l], lru_b_x[l], lru_lambda[l], bsz, seq)
        y_c = _m2_mixer(proj, m2_conv_w[l], m2_conv_b[l], m2_dt_bias[l], m2_a_log[l],
                        m2_d[l], m2_norm_g[l], bsz, seq)
        y_d = _rw_mixer(proj, rw_mu[l], rw_w0[l], rw_w2[l], rw_a0[l], rw_a2[l], rw_g2[l],
                        rw_k_k[l], rw_k_a[l], rw_r_k[l].reshape(-1), rw_ln_g[l], rw_ln_b[l], bsz, seq)
        w_gate = w_in[l, :, mixer_cols:].astype(BF16)
        merged = _merge(h, (y_a, y_b, y_c, y_d), w_gate, w_branch[l].astype(BF16))
        xf, h = _matmul_res_norm(merged, w_out[l].astype(BF16), xf, norm_ffn_g[l])
        act = _ffn_act(h, w_ffn_gate[l].astype(BF16), w_ffn_up[l].astype(BF16),
                       ffn_conv_w[l], ffn_conv_b[l], seq)
        last = l + 1 == depth
        xf, h = _matmul_ksplit_res_norm(act, w_ffn_down[l].astype(BF16), xf,
                                        final_norm_g if last else norm_mix_g[l + 1], F32 if last else BF16)
    return h.reshape(bsz, seq, d)
```

```python
import functools

import jax
import jax.numpy as jnp
from jax import lax
from jax.experimental import pallas as pl
from jax.experimental.pallas import tpu as pltpu

F32 = jnp.float32
BF16 = jnp.bfloat16

D_MODEL = 2048
MIX_W = 512
HEAD_W = 64
N_HEADS = MIX_W // HEAD_W
S5_GROUP = 16
S5_GROUPS = MIX_W // S5_GROUP
S5_STATE = 64
S5_Q = 8
S5_NSTATE = S5_GROUPS * S5_STATE
LANES = 128
S5_SG = MIX_W // LANES
S5_SGW = S5_Q * LANES
S5_SB = S5_NSTATE // S5_SG
S5_SCAN_ROWS = 512
LRU_HEADS = 8
LRU_C = 8.0
M2_GROUPS = 2
M2_STATE = 128
M2_CONV_DIM = MIX_W + 2 * M2_GROUPS * M2_STATE
M2_Q = 128
RW_Q = 64
RW_BATCH = 4
RW_RKV_W = 3 * MIX_W
RW_LORA_RANKS = (32, 32, 96)
RW_LORA_W = 256
RW_LN_EPS = 64e-5
FFN_DIM = 3 * D_MODEL
EPS = 1e-6
HALO = 8
BF16_ROWS = 16

PC_XBC = 0
PC_Z = 1024
PC_RKV = 1536
PC_LRU_X = 3072
PC_LRU_G = 3584
PC_LORA = 4096
PC_DT = 4352
PROJ_W = 4608

VMEM_LIMIT_BYTES = 50 * 1024 * 1024


def _params(*sem):
    return pltpu.CompilerParams(dimension_semantics=sem, vmem_limit_bytes=VMEM_LIMIT_BYTES)


def _dot(a, b):
    return jnp.dot(a, b, preferred_element_type=F32)


def _dot_nt(a, b):
    return lax.dot_general(a, b, (((1,), (1,)), ((), ())), preferred_element_type=F32)


def _dot_tn(a, b):
    return lax.dot_general(a, b, (((0,), (0,)), ((), ())), preferred_element_type=F32)


def _split(x, terms):
    out = []
    for _ in range(terms - 1):
        hi = x.astype(BF16)
        out.append(hi)
        x = x - hi.astype(F32)
    out.append(x.astype(BF16))
    return out


def _exact_left(m, x, terms=3):
    return sum(_dot(m, p) for p in _split(x, terms))


def _exact_right(xs, m, terms=2):
    n = xs[0].shape[0]
    stacked = jnp.concatenate([p for x in xs for p in _split(x, terms)], axis=0)
    out = _dot(stacked, m)
    return [sum(out[(i * terms + t) * n:(i * terms + t + 1) * n] for t in range(terms))
            for i in range(len(xs))]


def _tril_mask(n, strict=False):
    r = lax.broadcasted_iota(jnp.int32, (n, n), 0)
    c = lax.broadcasted_iota(jnp.int32, (n, n), 1)
    return (r > c) if strict else (r >= c)


def _causal_conv(x, halo, w_ref, b_ref):
    k_w = w_ref.shape[0]
    t = x.shape[0]
    xe = jnp.concatenate([halo, x], axis=0)
    out = b_ref[...] + x * w_ref[k_w - 1:k_w, :]
    for j in range(k_w - 1):
        lag = k_w - 1 - j
        out = out + xe[HALO - lag:HALO - lag + t, :] * w_ref[j:j + 1, :]
    return out


def _mm_body(a_ref, b_ref, o_ref):
    o_ref[...] = _dot(a_ref[...], b_ref[...]).astype(o_ref.dtype)


def _mm_res_body(a_ref, b_ref, r_ref, o_ref):
    o_ref[...] = (_dot(a_ref[...], b_ref[...]) + r_ref[...]).astype(o_ref.dtype)


def _weight_spec(w, layer, tn, col_of):
    k = w.shape[-2]
    if w.ndim == 2:
        return pl.BlockSpec((k, tn), lambda *g: (0, col_of(*g)))
    return pl.BlockSpec((None, k, tn), lambda *g: (layer, 0, col_of(*g)))


def _matmul(a, b, res=None, out_dtype=F32, tm=1024, tn=512, n_outer=False, layer=None):
    m, k = a.shape
    n = b.shape[-1]
    tm, tn = min(tm, m), min(tn, n)
    assert m % tm == 0 and n % tn == 0
    if n_outer:
        grid = (n // tn, m // tm)
        row = lambda j, i: (i, 0)
        col = _weight_spec(b, layer, tn, lambda j, i: j)
        out = lambda j, i: (i, j)
    else:
        grid = (m // tm, n // tn)
        row = lambda i, j: (i, 0)
        col = _weight_spec(b, layer, tn, lambda i, j: j)
        out = lambda i, j: (i, j)
    in_specs = [pl.BlockSpec((tm, k), row), col]
    args = [a, b]
    body = _mm_body
    if res is not None:
        in_specs.append(pl.BlockSpec((tm, tn), out))
        args.append(res)
        body = _mm_res_body
    return pl.pallas_call(
        body,
        name="matmul",
        grid=grid,
        in_specs=in_specs,
        out_specs=pl.BlockSpec((tm, tn), out),
        out_shape=jax.ShapeDtypeStruct((m, n), out_dtype),
        compiler_params=_params("parallel", "parallel"),
    )(*args)


def _matmul_blockdiag(a, w, res=None, tm=1024):
    m = a.shape[0]
    n = w.shape[1]
    tm = min(tm, m)
    blk = pl.BlockSpec((tm, n), lambda i, s: (i, s))
    in_specs = [blk, pl.BlockSpec((n, n), lambda i, s: (s, 0))]
    args = [a, w]
    if res is not None:
        in_specs.append(blk)
        args.append(res)
    return pl.pallas_call(
        _mm_body if res is None else _mm_res_body,
        name="matmul_blockdiag",
        grid=(m // tm, a.shape[1] // n),
        in_specs=in_specs,
        out_specs=blk,
        out_shape=jax.ShapeDtypeStruct(a.shape, F32),
        compiler_params=_params("parallel", "parallel"),
    )(*args)


def _norm_body(x_ref, g_ref, o_ref):
    x = x_ref[...]
    ms = jnp.mean(x * x, axis=-1, keepdims=True)
    o_ref[...] = (x * lax.rsqrt(ms + EPS) * g_ref[...]).astype(o_ref.dtype)


def _mm_res_norm_body(a_ref, w_ref, r_ref, g_ref, x_ref, h_ref):
    x = _dot(a_ref[...], w_ref[...]) + r_ref[...]
    x_ref[...] = x
    ms = jnp.mean(x * x, axis=-1, keepdims=True)
    h_ref[...] = (x * lax.rsqrt(ms + EPS) * g_ref[...]).astype(h_ref.dtype)


def _matmul_res_norm(a, w, res, g, layer, tm=512):
    m, k = a.shape
    n = w.shape[-1]
    tm = min(tm, m)
    rows = lambda width: pl.BlockSpec((tm, width), lambda i: (i, 0))
    return pl.pallas_call(
        _mm_res_norm_body,
        name="matmul_residual_norm",
        grid=(m // tm,),
        in_specs=[rows(k), _weight_spec(w, layer, n, lambda i: 0), rows(n), pl.BlockSpec((1, n), lambda i: (0, 0))],
        out_specs=[rows(n), rows(n)],
        out_shape=[jax.ShapeDtypeStruct((m, n), F32), jax.ShapeDtypeStruct((m, n), BF16)],
        compiler_params=_params("parallel"),
    )(a, w, res, g.reshape(1, n))


def _rms_norm(x, g, out_dtype, tm=512):
    m, d = x.shape
    tm = min(tm, m)
    return pl.pallas_call(
        _norm_body,
        name="rms_norm",
        grid=(m // tm,),
        in_specs=[pl.BlockSpec((tm, d), lambda i: (i, 0)), pl.BlockSpec((1, d), lambda i: (0, 0))],
        out_specs=pl.BlockSpec((tm, d), lambda i: (i, 0)),
        out_shape=jax.ShapeDtypeStruct((m, d), out_dtype),
        compiler_params=_params("parallel"),
    )(x, g.reshape(1, d))


def _s5_matrices(lam_re, lam_im, b_re, b_im, c_re, c_im, d, log_dt):
    g_n, p_n, c_n, q = S5_GROUPS, S5_STATE, S5_GROUP, S5_Q
    hp = lax.Precision.HIGHEST
    dt = jnp.exp(log_dt)[:, None]
    n = jnp.arange(q + 1, dtype=F32)[:, None, None]
    mag = jnp.exp(n * (lam_re * dt))
    pw_re = mag * jnp.cos(n * (lam_im * dt))
    pw_im = mag * jnp.sin(n * (lam_im * dt))
    den = lam_re * lam_re + lam_im * lam_im
    nr, ni = pw_re[1] - 1.0, pw_im[1]
    f_re = (nr * lam_re + ni * lam_im) / den
    f_im = (ni * lam_re - nr * lam_im) / den
    e_re = f_re[..., None] * b_re - f_im[..., None] * b_im
    e_im = f_re[..., None] * b_im + f_im[..., None] * b_re
    cp_re = c_re[None] * pw_re[:, :, None, :] - c_im[None] * pw_im[:, :, None, :]
    cp_im = c_re[None] * pw_im[:, :, None, :] + c_im[None] * pw_re[:, :, None, :]
    kern = (jnp.einsum("tgop,gpi->tgio", cp_re[:q], e_re, precision=hp)
            - jnp.einsum("tgop,gpi->tgio", cp_im[:q], e_im, precision=hp))
    kern = kern.at[0].add(d.reshape(g_n, c_n)[:, :, None] * jnp.eye(c_n, dtype=F32))
    sg_n, gl_n = S5_SG, g_n // S5_SG
    rows = sg_n * S5_SGW
    lag = jnp.arange(q)[None, :] - jnp.arange(q)[:, None]
    kt = jnp.where((lag >= 0)[:, :, None, None, None], kern[jnp.clip(lag, 0, q - 1)], 0.0)
    kt = kt.reshape(q, q, sg_n, gl_n, c_n, c_n).transpose(2, 0, 3, 4, 1, 5)
    kt = kt.reshape(rows, q * c_n)
    rev_re, rev_im = pw_re[q - 1 - jnp.arange(q)], pw_im[q - 1 - jnp.arange(q)]
    et_re, et_im = e_re.transpose(0, 2, 1)[None], e_im.transpose(0, 2, 1)[None]
    ws_re = rev_re[:, :, None, :] * et_re - rev_im[:, :, None, :] * et_im
    ws_im = rev_re[:, :, None, :] * et_im + rev_im[:, :, None, :] * et_re
    ws = jnp.stack([ws_re, ws_im], axis=3)
    ws = ws.reshape(q, sg_n, gl_n, c_n, 2, p_n).transpose(1, 0, 2, 3, 4, 5).reshape(rows, 2 * p_n)
    wy = jnp.stack([cp_re[1:], -cp_im[1:]], axis=0)
    wy = wy.reshape(2, q, sg_n, gl_n, c_n, p_n).transpose(2, 0, 3, 5, 1, 4)
    wy = wy.reshape(rows, q * c_n)

    col = jnp.arange(S5_SGW)
    small = jnp.arange(q * c_n)[:, None]
    tok_rep = ((small // c_n == col[None, :] // LANES) & (small % c_n == col[None, :] % c_n)).astype(BF16)
    small = jnp.arange(2 * p_n)[:, None]
    st_rep = ((small // p_n == col[None, :] // S5_SB) & (small % p_n == col[None, :] % p_n)).astype(BF16)
    row = jnp.arange(rows)
    g_tok_r, g_tok_c = (row % LANES) // c_n, (col % LANES) // c_n
    g_st_r, g_st_c = (row % S5_SB) // p_n, (col % S5_SB) // p_n
    expand = lambda table, rep: jnp.dot(table.astype(BF16), rep, preferred_element_type=BF16)
    zero = jnp.zeros((), BF16)
    toep = jnp.where(g_tok_r[:, None] == g_tok_c[None, :], expand(kt, tok_rep), zero)
    w_state = jnp.where(g_tok_r[:, None] == g_st_c[None, :], expand(ws, st_rep), zero)
    w_out = jnp.where(g_st_r[:, None] == g_tok_c[None, :], expand(wy, tok_rep), zero)
    return toep, w_state, w_out, pw_re[q].reshape(1, S5_NSTATE), pw_im[q].reshape(1, S5_NSTATE)


def _s5_scan_body(ar_ref, ai_ref, s_ref, o_ref, c_ref):
    rows = BF16_ROWS
    nb = s_ref.shape[0]
    a_re, a_im = ar_ref[...], ai_ref[...]
    half = a_re.shape[1]

    @pl.when(pl.program_id(1) == 0)
    def _():
        c_ref[...] = jnp.zeros_like(c_ref)

    def block(i, carry):
        base = pl.multiple_of(i * rows, rows)
        xs = [s_ref[b, pl.ds(base, rows), :] for b in range(nb)]
        outs = [[] for _ in range(nb)]
        carry = list(carry)
        for r in range(rows):
            for b in range(nb):
                s_re, s_im = carry[b]
                outs[b].append(jnp.concatenate([s_re, s_im], axis=1))
                carry[b] = (a_re * s_re - a_im * s_im + xs[b][r:r + 1, :half],
                            a_re * s_im + a_im * s_re + xs[b][r:r + 1, half:])
        for b in range(nb):
            o_ref[b, pl.ds(base, rows), :] = jnp.concatenate(outs[b], axis=0).astype(o_ref.dtype)
        return tuple(carry)

    init = tuple((c_ref[b, :, :half], c_ref[b, :, half:]) for b in range(nb))
    last = lax.fori_loop(0, s_ref.shape[1] // rows, block, init)
    for b in range(nb):
        c_ref[b] = jnp.concatenate(last[b], axis=1)


def _s5_fold_body(h_ref, w_ref, o_ref, tok_ref):
    u = _dot(h_ref[...], w_ref[...])
    tc = o_ref.shape[0]
    for c in range(S5_SG):
        tok_ref[c] = u[:, c * LANES:(c + 1) * LANES]
        for j in range(S5_Q):
            lo = c * S5_SGW + j * LANES
            o_ref[:, lo:lo + LANES] = tok_ref[c, pl.ds(j, tc, stride=S5_Q), :].astype(o_ref.dtype)


def _s5_glu_body(y_ref, w_ref, o_ref, tok_ref):
    tc = y_ref.shape[0]
    for c in range(S5_SG):
        for j in range(S5_Q):
            lo = c * S5_SGW + j * LANES
            tok_ref[c, pl.ds(j, tc, stride=S5_Q), :] = y_ref[:, lo:lo + LANES]
    y = jax.nn.gelu(jnp.concatenate([tok_ref[c] for c in range(S5_SG)], axis=1))
    o_ref[...] = (y * jax.nn.sigmoid(_dot(y.astype(BF16), w_ref[...]))).astype(o_ref.dtype)


def _s5_mixer(h, w_u, mats, w_glu, bsz, seq):
    toep, w_state, w_out, aq_re, aq_im = mats
    m = bsz * seq
    nc = seq // S5_Q
    width = S5_Q * MIX_W
    tm = min(1024, m)
    tc = tm // S5_Q
    uc = pl.pallas_call(
        _s5_fold_body,
        name="s5_input_proj",
        grid=(m // tm,),
        in_specs=[pl.BlockSpec((tm, D_MODEL), lambda i: (i, 0)), pl.BlockSpec((D_MODEL, MIX_W), lambda i: (0, 0))],
        out_specs=pl.BlockSpec((tc, width), lambda i: (i, 0)),
        out_shape=jax.ShapeDtypeStruct((m // S5_Q, width), BF16),
        scratch_shapes=[pltpu.VMEM((MIX_W // LANES, tm, LANES), F32)],
        compiler_params=_params("parallel"),
    )(h, w_u)
    y1 = _matmul_blockdiag(uc, toep)
    s_end = _matmul_blockdiag(uc, w_state)
    rb = min(S5_SCAN_ROWS, nc)
    s_in = pl.pallas_call(
        _s5_scan_body,
        name="s5_chunk_scan",
        grid=(S5_SG, nc // rb),
        in_specs=[
            pl.BlockSpec((1, S5_SB), lambda j, r: (0, j)),
            pl.BlockSpec((1, S5_SB), lambda j, r: (0, j)),
            pl.BlockSpec((bsz, rb, S5_SGW), lambda j, r: (0, r, j)),
        ],
        out_specs=pl.BlockSpec((bsz, rb, S5_SGW), lambda j, r: (0, r, j)),
        out_shape=jax.ShapeDtypeStruct((bsz, nc, width), BF16),
        scratch_shapes=[pltpu.VMEM((bsz, 1, S5_SGW), F32)],
        compiler_params=_params("parallel", "arbitrary"),
    )(aq_re, aq_im, s_end.reshape(bsz, nc, width))
    y = _matmul_blockdiag(s_in.reshape(bsz * nc, width), w_out, res=y1)
    return pl.pallas_call(
        _s5_glu_body,
        name="s5_glu",
        grid=(m // tm,),
        in_specs=[pl.BlockSpec((tc, width), lambda i: (i, 0)), pl.BlockSpec((MIX_W, MIX_W), lambda i: (0, 0))],
        out_specs=pl.BlockSpec((tm, MIX_W), lambda i: (i, 0)),
        out_shape=jax.ShapeDtypeStruct((m, MIX_W), BF16),
        scratch_shapes=[pltpu.VMEM((MIX_W // LANES, tm, LANES), F32)],
        compiler_params=_params("parallel"),
    )(y, w_glu.astype(BF16))


def _lru_body(x_ref, g_ref, cw_ref, cb_ref, wa_ref, ba_ref, wx_ref, bx_ref, lam_ref, o_ref,
              halo_ref, h_ref, a_s, b_s):
    nb, t, _ = x_ref.shape

    @pl.when(pl.program_id(0) == 0)
    def _():
        halo_ref[...] = jnp.zeros_like(halo_ref)
        h_ref[...] = jnp.zeros_like(h_ref)

    soft = jax.nn.softplus(-lam_ref[...])
    for b in range(nb):
        x_in = x_ref[b]
        x = _causal_conv(x_in, halo_ref[b], cw_ref, cb_ref)
        halo_ref[b] = x_in[t - HALO:, :]
        xb = x.astype(BF16)
        r = jax.nn.sigmoid(_dot(xb, wa_ref[...]) + ba_ref[...])
        i = jax.nn.sigmoid(_dot(xb, wx_ref[...]) + bx_ref[...])
        log_a = (-LRU_C * r) * soft
        a_s[b] = jnp.exp(log_a)
        b_s[b] = x * i * jnp.sqrt(1.0 - jnp.exp(2.0 * log_a))
    rows = 8

    def block(k, hs):
        base = pl.multiple_of(k * rows, rows)
        av = [a_s[b, pl.ds(base, rows), :] for b in range(nb)]
        bv = [b_s[b, pl.ds(base, rows), :] for b in range(nb)]
        hs = list(hs)
        outs = [[] for _ in range(nb)]
        for rr in range(rows):
            for b in range(nb):
                hs[b] = av[b][rr:rr + 1, :] * hs[b] + bv[b][rr:rr + 1, :]
                outs[b].append(hs[b])
        for b in range(nb):
            b_s[b, pl.ds(base, rows), :] = jnp.concatenate(outs[b], axis=0)
        return tuple(hs)

    hs = lax.fori_loop(0, t // rows, block, tuple(h_ref[b] for b in range(nb)))
    for b in range(nb):
        h_ref[b] = hs[b]
        o_ref[b] = (b_s[b] * jax.nn.gelu(g_ref[b])).astype(o_ref.dtype)


def _block_diag(w):
    h_n, n, _ = w.shape
    eye = jnp.eye(h_n, dtype=w.dtype)
    return (w[:, :, None, :] * eye[:, None, :, None]).reshape(h_n * n, h_n * n)


def _lru_mixer(proj, conv_w, conv_b, w_a, b_a, w_x, b_x, lam, bsz, seq):
    t = min(256, seq)
    vec = lambda v: v.reshape(1, MIX_W)
    const = lambda shape: pl.BlockSpec(shape, lambda c: (0, 0))
    proj3 = proj.reshape(bsz, seq, proj.shape[1])
    out = pl.pallas_call(
        _lru_body,
        name="rglru_mixer",
        grid=(seq // t,),
        in_specs=[
            pl.BlockSpec((bsz, t, MIX_W), lambda c: (0, c, PC_LRU_X // MIX_W)),
            pl.BlockSpec((bsz, t, MIX_W), lambda c: (0, c, PC_LRU_G // MIX_W)),
            const(conv_w.shape), const((1, MIX_W)),
            const((MIX_W, MIX_W)), const((1, MIX_W)),
            const((MIX_W, MIX_W)), const((1, MIX_W)), const((1, MIX_W)),
        ],
        out_specs=pl.BlockSpec((bsz, t, MIX_W), lambda c: (0, c, 0)),
        out_shape=jax.ShapeDtypeStruct((bsz, seq, MIX_W), BF16),
        scratch_shapes=[
            pltpu.VMEM((bsz, HALO, MIX_W), F32), pltpu.VMEM((bsz, 1, MIX_W), F32),
            pltpu.VMEM((bsz, t, MIX_W), F32), pltpu.VMEM((bsz, t, MIX_W), F32),
        ],
        compiler_params=_params("arbitrary"),
    )(proj3, proj3, conv_w, vec(conv_b), _block_diag(w_a).astype(BF16), vec(b_a),
      _block_diag(w_x).astype(BF16), vec(b_x), vec(lam))
    return out.reshape(bsz * seq, MIX_W)


def _m2_body(z_ref, xbc_ref, dt_ref, rep_ref, cw_ref, cb_ref, dtb_ref, alog_ref, d_ref, ng_ref, o_ref,
             halo_ref, st_ref):
    q = z_ref.shape[0]
    hg = N_HEADS // M2_GROUPS
    gw = hg * HEAD_W

    @pl.when(pl.program_id(1) == 0)
    def _():
        halo_ref[...] = jnp.zeros_like(halo_ref)
        st_ref[...] = jnp.zeros_like(st_ref)

    xbc = xbc_ref[...]
    conv = _causal_conv(xbc, halo_ref[...], cw_ref, cb_ref)
    halo_ref[...] = xbc[q - HALO:, :]
    xc = conv * jax.nn.sigmoid(conv)
    xs = xc[:, :MIX_W]
    b_all = xc[:, MIX_W:MIX_W + M2_GROUPS * M2_STATE]
    c_all = xc[:, MIX_W + M2_GROUPS * M2_STATE:]
    dt_raw = _exact_right([dt_ref[...]], rep_ref[...])[0]
    dt = jax.nn.softplus(dt_raw + dtb_ref[...])
    ad = dt * (-jnp.exp(alog_ref[...]))
    causal = _tril_mask(q)
    a_cs = _exact_left(causal.astype(BF16), ad)
    a_cs_t = a_cs.T
    a_last = a_cs[q - 1:q, :]
    xd = xs * dt
    xd_st = (xd * jnp.exp(a_last - a_cs)).astype(BF16)
    e_cs = jnp.exp(a_cs)
    xd_b = xd.astype(BF16)
    ys = []
    for g in range(M2_GROUPS):
        bg = b_all[:, g * M2_STATE:(g + 1) * M2_STATE].astype(BF16)
        cg = c_all[:, g * M2_STATE:(g + 1) * M2_STATE].astype(BF16)
        cb = _dot_nt(cg, bg)
        y_off = _dot(cg, st_ref[g].astype(BF16)) * e_cs[:, g * gw:(g + 1) * gw]
        for hh in range(hg):
            lo = (g * hg + hh) * HEAD_W
            col = a_cs[:, lo:lo + HEAD_W]
            col = jnp.concatenate([col] * (q // HEAD_W), axis=1)
            seg = col - a_cs_t[lo:lo + 1, :]
            dec = jnp.exp(jnp.where(causal, seg, -jnp.inf))
            y_d = _dot((cb * dec).astype(BF16), xd_b[:, lo:lo + HEAD_W])
            ys.append(y_d + y_off[:, hh * HEAD_W:(hh + 1) * HEAD_W])
        upd = _dot_tn(bg, xd_st[:, g * gw:(g + 1) * gw])
        st_ref[g] = st_ref[g] * jnp.exp(a_last[:, g * gw:(g + 1) * gw]) + upd
    y = jnp.concatenate(ys, axis=1) + d_ref[...] * xs
    z = z_ref[...]
    y = y * (z * jax.nn.sigmoid(z))
    ms = jnp.mean(y * y, axis=-1, keepdims=True)
    o_ref[...] = (y * lax.rsqrt(ms + EPS) * ng_ref[...]).astype(o_ref.dtype)


def _per_head(v):
    return jnp.repeat(v, HEAD_W).reshape(1, -1)


def _m2_mixer(proj, conv_w, conv_b, dt_bias, a_log, d, norm_g, bsz, seq):
    q = min(M2_Q, seq)
    nc = seq // q
    const = lambda shape: pl.BlockSpec(shape, lambda b, c: (0, 0))
    head_of_lane = jnp.arange(MIX_W)[None, :] // HEAD_W
    rep = (jnp.arange(LANES)[:, None] == head_of_lane).astype(BF16)
    return pl.pallas_call(
        _m2_body,
        name="mamba2_mixer",
        grid=(bsz, nc),
        in_specs=[
            pl.BlockSpec((q, MIX_W), lambda b, c: (b * nc + c, PC_Z // MIX_W)),
            pl.BlockSpec((q, M2_CONV_DIM), lambda b, c: (b * nc + c, PC_XBC // M2_CONV_DIM)),
            pl.BlockSpec((q, LANES), lambda b, c: (b * nc + c, PC_DT // LANES)),
            const((LANES, MIX_W)), const(conv_w.shape), const((1, M2_CONV_DIM)),
            const((1, MIX_W)), const((1, MIX_W)), const((1, MIX_W)), const((1, MIX_W)),
        ],
        out_specs=pl.BlockSpec((q, MIX_W), lambda b, c: (b * nc + c, 0)),
        out_shape=jax.ShapeDtypeStruct((bsz * seq, MIX_W), BF16),
        scratch_shapes=[
            pltpu.VMEM((HALO, M2_CONV_DIM), F32),
            pltpu.VMEM((M2_GROUPS, M2_STATE, (N_HEADS // M2_GROUPS) * HEAD_W), F32),
        ],
        compiler_params=_params("parallel", "arbitrary"),
    )(proj, proj, proj, rep, conv_w, conv_b.reshape(1, -1), _per_head(dt_bias), _per_head(a_log),
      _per_head(d), norm_g.reshape(1, -1))


def _rw_body(p_ref, pl_ref, mu_ref, mul_ref, w0_ref, w2_ref, a0_ref, a2_ref, g2_ref, kk_ref, ka_ref, rk_ref,
             lng_ref, lnb_ref, ones_ref, o_ref, prev_ref, prevl_ref, st_ref):
    nb, q, _ = p_ref.shape
    rows = nb * q

    @pl.when(pl.program_id(1) == 0)
    def _():
        prev_ref[...] = jnp.zeros_like(prev_ref)
        prevl_ref[...] = jnp.zeros_like(prevl_ref)
        st_ref[...] = jnp.zeros_like(st_ref)

    def token_mix(x_ref, last_ref, m_ref):
        x = x_ref[...].reshape(rows, x_ref.shape[2])
        row = lax.broadcasted_iota(jnp.int32, x.shape, 0)
        shifted = pltpu.roll(x, 1, axis=0)
        for b in range(nb):
            shifted = jnp.where(row == b * q, last_ref[b, 0:1, :], shifted)
            last_ref[b, 0:1, :] = x[(b + 1) * q - 1:(b + 1) * q, :]
        return x + (shifted - x) * m_ref[...]

    p = token_mix(p_ref, prev_ref, mu_ref)
    lora = token_mix(pl_ref, prevl_ref, mul_ref)
    r = p[:, 0:MIX_W]
    k = p[:, MIX_W:2 * MIX_W]
    v = p[:, 2 * MIX_W:3 * MIX_W]
    w = -jax.nn.softplus(-(w0_ref[...] + _dot(jnp.tanh(lora).astype(BF16), w2_ref[...]))) - 0.5
    log_w = -jnp.exp(w)
    a_sig = jax.nn.sigmoid(a0_ref[...] + _dot(lora.astype(BF16), a2_ref[...]))
    gate = _dot(jax.nn.sigmoid(lora).astype(BF16), g2_ref[...])
    ones = ones_ref[...]
    kk = k * kk_ref[...]
    k = k * (1.0 + (a_sig - 1.0) * ka_ref[...])
    kk_sq, rk_sum = _exact_right([kk * kk, r * k * rk_ref[...]], ones)
    kk = kk / jnp.maximum(jnp.sqrt(kk_sq), 1e-12)
    a_vec = -kk
    b_vec = kk * a_sig

    rr = lax.broadcasted_iota(jnp.int32, (rows, rows), 0)
    cc = lax.broadcasted_iota(jnp.int32, (rows, rows), 1)
    cum_mask = jnp.where((rr >= cc) & ((rr // q) == (cc // q)), 1.0, 0.0).astype(BF16)
    gam = _exact_left(cum_mask, log_w)
    g_last = jnp.concatenate(
        [jnp.broadcast_to(gam[(b + 1) * q - 1:(b + 1) * q, :], (q, MIX_W)) for b in range(nb)], axis=0)
    e_neg = jnp.exp(-gam)
    e_rem = jnp.exp(g_last - gam)
    a_t = a_vec * jnp.exp(gam - log_w)
    r_t = r * jnp.exp(gam)
    b_t = b_vec * e_neg
    k_t = k * e_neg
    b_h = b_vec * e_rem
    k_h = k * e_rem
    e_last = jnp.exp(g_last)
    assert q == HEAD_W
    row2 = lax.broadcasted_iota(jnp.int32, (q, 2 * HEAD_W), 0)
    lane2 = lax.broadcasted_iota(jnp.int32, (q, 2 * HEAD_W), 1)
    low = lane2 < HEAD_W
    col2 = jnp.where(low, lane2, lane2 - HEAD_W)
    keep_z0 = low | (col2 < row2)
    keep_t = col2 <= row2
    strict = _tril_mask(q, strict=True)
    pairs = [(b, h) for b in range(nb) for h in range(N_HEADS)]
    zeros_w = jnp.zeros((HEAD_W, 2 * HEAD_W), BF16)
    zeros_x = jnp.zeros((q, HEAD_W), F32)

    def blk(t, key):
        b, h = key
        return t[b * q:(b + 1) * q, h * HEAD_W:(h + 1) * HEAD_W]

    p0, p1, s0, z = {}, {}, {}, {}
    for key in pairs:
        ar = jnp.concatenate([blk(a_t, key), blk(r_t, key)], axis=0).astype(BF16)
        s0[key] = st_ref[key[0], key[1]]
        rhs = jnp.concatenate([s0[key], blk(b_t, key), blk(k_t, key), zeros_x], axis=0).astype(BF16)
        prod = _dot_nt(ar, rhs)
        p0[key] = prod[:, :2 * HEAD_W]
        p1[key] = prod[:, 2 * HEAD_W:]
    for key in pairs:
        a_k = jnp.where(strict, p1[key][:q, :HEAD_W], 0.0).astype(BF16)
        x0 = _dot(a_k, blk(v, key).astype(BF16))
        z[key] = jnp.where(keep_z0, p0[key][:q], 0.0) + jnp.concatenate([x0, zeros_x], axis=1)
    steps = max(1, (q - 1).bit_length())
    for _ in range(steps):
        for key in pairs:
            zb = z[key].astype(BF16)
            z[key] = jnp.where(low, z[key], 0.0) + _dot(zb, jnp.concatenate([zeros_w, zb], axis=0))
    ys = {}
    for key in pairs:
        vu = jnp.concatenate([blk(v, key), z[key][:, :HEAD_W]], axis=0).astype(BF16)
        t_mat = jnp.where(keep_t, jnp.where(low, p1[key][q:], p0[key][q:]), 0.0)
        ys[key] = p0[key][q:, :HEAD_W] + _dot(t_mat.astype(BF16), vu)
        khb = jnp.concatenate([blk(k_h, key), blk(b_h, key)], axis=0).astype(BF16)
        st_ref[key[0], key[1]] = s0[key] * blk(e_last, key)[0:1, :] + _dot_tn(vu, khb)
    y = jnp.concatenate(
        [jnp.concatenate([ys[(b, h)] for h in range(N_HEADS)], axis=1) for b in range(nb)], axis=0)
    inv_n = 1.0 / HEAD_W
    yc = y - _exact_right([y], ones)[0] * inv_n
    var = _exact_right([yc * yc], ones)[0] * inv_n
    y = yc * lax.rsqrt(var + RW_LN_EPS) * lng_ref[...] + lnb_ref[...]
    bonus = rk_sum * v
    o_ref[...] = ((y + bonus) * gate).astype(o_ref.dtype).reshape(o_ref.shape)


def _pad_rows(w, rows, off):
    return jnp.zeros((rows, w.shape[1]), w.dtype).at[off:off + w.shape[0]].set(w)


def _rw_mixer(proj, mu, w0, w2, a0, a2, g2, k_k, k_a, r_k, ln_g, ln_b, bsz, seq):
    q = min(RW_Q, seq)
    nc = seq // q
    nb = RW_BATCH if bsz % RW_BATCH == 0 else 1
    vec = lambda t: t.reshape(1, MIX_W)
    const = lambda shape: pl.BlockSpec(shape, lambda b, c: (0, 0))
    offs = (0, RW_LORA_RANKS[0], RW_LORA_RANKS[0] + RW_LORA_RANKS[1])
    lora = lambda t, off: _pad_rows(t, RW_LORA_W, off).astype(BF16)
    mu_rkv, mu_lora = mu[:RW_RKV_W].reshape(1, RW_RKV_W), _pad_rows(mu[RW_RKV_W:, None], RW_LORA_W, 0).reshape(1, RW_LORA_W)
    ones = _block_diag(jnp.ones((N_HEADS, HEAD_W, HEAD_W), F32)).astype(BF16)
    proj3 = proj.reshape(bsz, seq, proj.shape[1])
    out = pl.pallas_call(
        _rw_body,
        name="rwkv7_mixer",
        grid=(bsz // nb, nc),
        in_specs=[
            pl.BlockSpec((nb, q, RW_RKV_W), lambda b, c: (b, c, PC_RKV // RW_RKV_W)),
            pl.BlockSpec((nb, q, RW_LORA_W), lambda b, c: (b, c, PC_LORA // RW_LORA_W)),
            const((1, RW_RKV_W)), const((1, RW_LORA_W)), const((1, MIX_W)), const((RW_LORA_W, MIX_W)),
            const((1, MIX_W)), const((RW_LORA_W, MIX_W)), const((RW_LORA_W, MIX_W)),
            const((1, MIX_W)), const((1, MIX_W)), const((1, MIX_W)),
            const((1, MIX_W)), const((1, MIX_W)), const((MIX_W, MIX_W)),
        ],
        out_specs=pl.BlockSpec((nb, q, MIX_W), lambda b, c: (b, c, 0)),
        out_shape=jax.ShapeDtypeStruct((bsz, seq, MIX_W), BF16),
        scratch_shapes=[pltpu.VMEM((nb, HALO, RW_RKV_W), F32), pltpu.VMEM((nb, HALO, RW_LORA_W), F32),
                        pltpu.VMEM((nb, N_HEADS, HEAD_W, HEAD_W), F32)],
        compiler_params=_params("parallel", "arbitrary"),
    )(proj3, proj3, mu_rkv, mu_lora, vec(w0), lora(w2, offs[0]), vec(a0), lora(a2, offs[1]), lora(g2, offs[2]),
      vec(k_k), vec(k_a), vec(r_k), vec(ln_g), vec(ln_b), ones)
    return out.reshape(bsz * seq, MIX_W)


def _merge_body(h_ref, *refs):
    nb = (len(refs) - 2) // 2
    y_refs, wg_refs, wp_ref, o_ref = refs[:nb], refs[nb:2 * nb], refs[2 * nb], refs[2 * nb + 1]
    h = h_ref[...]
    acc = None
    for kk in range(nb):
        term = jax.nn.sigmoid(_dot(h, wg_refs[kk][...])) * _dot(y_refs[kk][...], wp_ref[kk])
        acc = term if acc is None else acc + term
    o_ref[...] = acc.astype(o_ref.dtype)


def _merge(h, ys, w_gate, w_branch, layer=None, tm=512, tn=512):
    m = h.shape[0]
    tm = min(tm, m)
    nb = len(ys)
    y_spec = pl.BlockSpec((tm, MIX_W), lambda j, i: (i, 0))
    gate_specs = [_weight_spec(w_gate, layer, tn, lambda j, i, k=k: k * (D_MODEL // tn) + j) for k in range(nb)]
    if w_branch.ndim == 3:
        branch_spec = pl.BlockSpec((nb, MIX_W, tn), lambda j, i: (0, 0, j))
    else:
        branch_spec = pl.BlockSpec((None, nb, MIX_W, tn), lambda j, i: (layer, 0, 0, j))
    return pl.pallas_call(
        _merge_body,
        name="gated_merge",
        grid=(D_MODEL // tn, m // tm),
        in_specs=[pl.BlockSpec((tm, D_MODEL), lambda j, i: (i, 0))] + [y_spec] * nb + gate_specs + [branch_spec],
        out_specs=pl.BlockSpec((tm, tn), lambda j, i: (i, j)),
        out_shape=jax.ShapeDtypeStruct((m, D_MODEL), BF16),
        compiler_params=_params("parallel", "parallel"),
    )(h, *ys, *([w_gate] * nb), w_branch)


def _ffn_body(h_ref, hh_ref, wg_ref, wu_ref, cw_ref, cb_ref, o_ref, *, tiles_per_seq):
    t = h_ref.shape[0]
    nh = hh_ref.shape[0]
    h = h_ref[...]
    seq_start = (pl.program_id(1) % tiles_per_seq) == 0
    halo = jnp.where(seq_start, jnp.zeros_like(hh_ref[...]), hh_ref[...])
    g_all = _dot(jnp.concatenate([halo, h], axis=0), wg_ref[...])
    k_w = cw_ref.shape[0]
    u = cb_ref[...] + g_all[nh:, :] * cw_ref[k_w - 1:k_w, :]
    for j in range(k_w - 1):
        lag = k_w - 1 - j
        u = u + g_all[nh - lag:nh - lag + t, :] * cw_ref[j:j + 1, :]
    o_ref[...] = (jax.nn.gelu(u) * _dot(h, wu_ref[...])).astype(o_ref.dtype)


def _ffn_act(h, w_gate, w_up, conv_w, conv_b, seq, layer=None, tm=1024, tf=512):
    m = h.shape[0]
    tm = min(tm, seq)
    f = w_gate.shape[-1]
    halo_blocks = tm // BF16_ROWS
    return pl.pallas_call(
        functools.partial(_ffn_body, tiles_per_seq=seq // tm),
        name="ffn_gate_up",
        grid=(f // tf, m // tm),
        in_specs=[
            pl.BlockSpec((tm, D_MODEL), lambda j, i: (i, 0)),
            pl.BlockSpec((BF16_ROWS, D_MODEL), lambda j, i: (jnp.maximum(i * halo_blocks - 1, 0), 0)),
            _weight_spec(w_gate, layer, tf, lambda j, i: j),
            _weight_spec(w_up, layer, tf, lambda j, i: j),
            pl.BlockSpec((conv_w.shape[0], tf), lambda j, i: (0, j)),
            pl.BlockSpec((1, tf), lambda j, i: (0, j)),
        ],
        out_specs=pl.BlockSpec((tm, tf), lambda j, i: (i, j)),
        out_shape=jax.ShapeDtypeStruct((m, f), BF16),
        compiler_params=_params("parallel", "parallel"),
    )(h, h, w_gate, w_up, conv_w, conv_b.reshape(1, f))


def _mixer_weight(w_in_l):
    s5_u, lru_x, lru_g, m2_z, m2_xbc, m2_dt, rw_p = jnp.split(
        w_in_l, [512, 1024, 1536, 2048, 2048 + M2_CONV_DIM, 2048 + M2_CONV_DIM + N_HEADS], axis=1)
    d = w_in_l.shape[0]
    padc = lambda t, n: jnp.concatenate([t, jnp.zeros((d, n - t.shape[1]), t.dtype)], axis=1)
    proj_w = jnp.concatenate([m2_xbc, m2_z, rw_p[:, :RW_RKV_W], lru_x, lru_g,
                              padc(rw_p[:, RW_RKV_W:], RW_LORA_W), padc(m2_dt, PROJ_W - PC_DT)], axis=1)
    assert proj_w.shape[1] == PROJ_W
    return s5_u.astype(BF16), proj_w.astype(BF16)


def kernel(x, norm_mix_g, w_in, s5_lambda_re, s5_lambda_im, s5_b_re, s5_b_im, s5_c_re, s5_c_im, s5_d, s5_log_dt, s5_w_glu, lru_conv_w, lru_conv_b, lru_w_a, lru_b_a, lru_w_x, lru_b_x, lru_lambda, m2_conv_w, m2_conv_b, m2_dt_bias, m2_a_log, m2_d, m2_norm_g, rw_mu, rw_w0, rw_w2, rw_a0, rw_a2, rw_g2, rw_k_k, rw_k_a, rw_r_k, rw_ln_g, rw_ln_b, w_branch, w_out, norm_ffn_g, w_ffn_gate, w_ffn_up, ffn_conv_w, ffn_conv_b, w_ffn_down, final_norm_g):
    bsz, seq, d = x.shape
    depth = w_in.shape[0]
    mixer_cols = w_in.shape[2] - w_branch.shape[1] * d
    xf = x.reshape(bsz * seq, d)
    wb_all, wo_all = w_branch.astype(BF16), w_out.astype(BF16)
    wg_all, wu_all, wd_all = w_ffn_gate.astype(BF16), w_ffn_up.astype(BF16), w_ffn_down.astype(BF16)
    for l in range(depth):
        h = _rms_norm(xf, norm_mix_g[l], BF16)
        w_s5, w_proj = _mixer_weight(w_in[l, :, :mixer_cols])
        proj = _matmul(h, w_proj)
        mats = _s5_matrices(s5_lambda_re[l], s5_lambda_im[l], s5_b_re[l], s5_b_im[l],
                            s5_c_re[l], s5_c_im[l], s5_d[l], s5_log_dt[l])
        y_a = _s5_mixer(h, w_s5, mats, s5_w_glu[l], bsz, seq)
        y_b = _lru_mixer(proj, lru_conv_w[l], lru_conv_b[l], lru_w_a[l], lru_b_a[l],
                         lru_w_x[l], lru_b_x[l], lru_lambda[l], bsz, seq)
        y_c = _m2_mixer(proj, m2_conv_w[l], m2_conv_b[l], m2_dt_bias[l], m2_a_log[l],
                        m2_d[l], m2_norm_g[l], bsz, seq)
        y_d = _rw_mixer(proj, rw_mu[l], rw_w0[l], rw_w2[l], rw_a0[l], rw_a2[l], rw_g2[l],
                        rw_k_k[l], rw_k_a[l], rw_r_k[l].reshape(-1), rw_ln_g[l], rw_ln_b[l], bsz, seq)
        w_gate = w_in[l, :, mixer_cols:].astype(BF16)
        merged = _merge(h, (y_a, y_b, y_c, y_d), w_gate, wb_all, layer=l)
        xf, h = _matmul_res_norm(merged, wo_all, xf, norm_ffn_g[l], l)
        act = _ffn_act(h, wg_all, wu_all, ffn_conv_w[l], ffn_conv_b[l], seq, layer=l)
        xf = _matmul(act, wd_all, res=xf, tm=512, n_outer=True, layer=l)
    return _rms_norm(xf, final_norm_g, F32).reshape(bsz, seq, d)
```

```python
import functools

import jax
import jax.numpy as jnp
from jax import lax
from jax.experimental import pallas as pl
from jax.experimental.pallas import tpu as pltpu

F32 = jnp.float32
BF16 = jnp.bfloat16

D_MODEL = 2048
MIX_W = 512
HEAD_W = 64
N_HEADS = MIX_W // HEAD_W
S5_GROUP = 16
S5_GROUPS = MIX_W // S5_GROUP
S5_STATE = 64
S5_Q = 8
S5_NSTATE = S5_GROUPS * S5_STATE
LANES = 128
S5_SG = MIX_W // LANES
S5_SGW = S5_Q * LANES
S5_SB = S5_NSTATE // S5_SG
S5_SCAN_ROWS = 512
LRU_HEADS = 8
LRU_C = 8.0
M2_GROUPS = 2
M2_STATE = 128
M2_CONV_DIM = MIX_W + 2 * M2_GROUPS * M2_STATE
M2_Q = 128
M2_BATCH = 2
RW_Q = 64
RW_BATCH = 4
RW_RKV_W = 3 * MIX_W
RW_LORA_RANKS = (32, 32, 96)
RW_LORA_W = 256
RW_LN_EPS = 64e-5
FFN_DIM = 3 * D_MODEL
EPS = 1e-6
HALO = 8
BF16_ROWS = 16

PC_XBC = 0
PC_Z = 1024
PC_RKV = 1536
PC_LRU_X = 3072
PC_LRU_G = 3584
PC_LORA = 4096
PC_DT = 4352
PROJ_W = 4608

VMEM_LIMIT_BYTES = 50 * 1024 * 1024


def _params(*sem):
    return pltpu.CompilerParams(dimension_semantics=sem, vmem_limit_bytes=VMEM_LIMIT_BYTES)


def _dot(a, b):
    return jnp.dot(a, b, preferred_element_type=F32)


def _dot_nt(a, b):
    return lax.dot_general(a, b, (((1,), (1,)), ((), ())), preferred_element_type=F32)


def _dot_tn(a, b):
    return lax.dot_general(a, b, (((0,), (0,)), ((), ())), preferred_element_type=F32)


def _split(x, terms):
    out = []
    for _ in range(terms - 1):
        hi = x.astype(BF16)
        out.append(hi)
        x = x - hi.astype(F32)
    out.append(x.astype(BF16))
    return out


def _exact_left(m, x, terms=3):
    return sum(_dot(m, p) for p in _split(x, terms))


def _exact_right(xs, m, terms=2):
    n = xs[0].shape[0]
    stacked = jnp.concatenate([p for x in xs for p in _split(x, terms)], axis=0)
    out = _dot(stacked, m)
    return [sum(out[(i * terms + t) * n:(i * terms + t + 1) * n] for t in range(terms))
            for i in range(len(xs))]


def _tril_mask(n, strict=False):
    r = lax.broadcasted_iota(jnp.int32, (n, n), 0)
    c = lax.broadcasted_iota(jnp.int32, (n, n), 1)
    return (r > c) if strict else (r >= c)


def _causal_conv(x, halo, w_ref, b_ref):
    k_w = w_ref.shape[0]
    t = x.shape[0]
    xe = jnp.concatenate([halo, x], axis=0)
    out = b_ref[...] + x * w_ref[k_w - 1:k_w, :]
    for j in range(k_w - 1):
        lag = k_w - 1 - j
        out = out + xe[HALO - lag:HALO - lag + t, :] * w_ref[j:j + 1, :]
    return out


def _mm_body(a_ref, b_ref, o_ref):
    o_ref[...] = _dot(a_ref[...], b_ref[...]).astype(o_ref.dtype)


def _mm_res_body(a_ref, b_ref, r_ref, o_ref):
    o_ref[...] = (_dot(a_ref[...], b_ref[...]) + r_ref[...]).astype(o_ref.dtype)


def _weight_spec(w, layer, tn, col_of):
    k = w.shape[-2]
    if w.ndim == 2:
        return pl.BlockSpec((k, tn), lambda *g: (0, col_of(*g)))
    return pl.BlockSpec((None, k, tn), lambda *g: (layer, 0, col_of(*g)))


def _matmul(a, b, res=None, out_dtype=F32, tm=1024, tn=512, n_outer=False, layer=None):
    m, k = a.shape
    n = b.shape[-1]
    tm, tn = min(tm, m), min(tn, n)
    assert m % tm == 0 and n % tn == 0
    if n_outer:
        grid = (n // tn, m // tm)
        row = lambda j, i: (i, 0)
        col = _weight_spec(b, layer, tn, lambda j, i: j)
        out = lambda j, i: (i, j)
    else:
        grid = (m // tm, n // tn)
        row = lambda i, j: (i, 0)
        col = _weight_spec(b, layer, tn, lambda i, j: j)
        out = lambda i, j: (i, j)
    in_specs = [pl.BlockSpec((tm, k), row), col]
    args = [a, b]
    body = _mm_body
    if res is not None:
        in_specs.append(pl.BlockSpec((tm, tn), out))
        args.append(res)
        body = _mm_res_body
    return pl.pallas_call(
        body,
        name="matmul",
        grid=grid,
        in_specs=in_specs,
        out_specs=pl.BlockSpec((tm, tn), out),
        out_shape=jax.ShapeDtypeStruct((m, n), out_dtype),
        compiler_params=_params("parallel", "parallel"),
    )(*args)


def _matmul_blockdiag(a, w, res=None, tm=1024):
    m = a.shape[0]
    n = w.shape[1]
    tm = min(tm, m)
    blk = pl.BlockSpec((tm, n), lambda i, s: (i, s))
    in_specs = [blk, pl.BlockSpec((n, n), lambda i, s: (s, 0))]
    args = [a, w]
    if res is not None:
        in_specs.append(blk)
        args.append(res)
    return pl.pallas_call(
        _mm_body if res is None else _mm_res_body,
        name="matmul_blockdiag",
        grid=(m // tm, a.shape[1] // n),
        in_specs=in_specs,
        out_specs=blk,
        out_shape=jax.ShapeDtypeStruct(a.shape, F32),
        compiler_params=_params("parallel", "parallel"),
    )(*args)


def _norm_body(x_ref, g_ref, o_ref):
    x = x_ref[...]
    ms = jnp.mean(x * x, axis=-1, keepdims=True)
    o_ref[...] = (x * lax.rsqrt(ms + EPS) * g_ref[...]).astype(o_ref.dtype)


def _mm_res_norm_body(a_ref, w_ref, r_ref, g_ref, x_ref, h_ref):
    x = _dot(a_ref[...], w_ref[...]) + r_ref[...]
    x_ref[...] = x
    ms = jnp.mean(x * x, axis=-1, keepdims=True)
    h_ref[...] = (x * lax.rsqrt(ms + EPS) * g_ref[...]).astype(h_ref.dtype)


def _matmul_res_norm(a, w, res, g, layer, tm=512):
    m, k = a.shape
    n = w.shape[-1]
    tm = min(tm, m)
    rows = lambda width: pl.BlockSpec((tm, width), lambda i: (i, 0))
    return pl.pallas_call(
        _mm_res_norm_body,
        name="matmul_residual_norm",
        grid=(m // tm,),
        in_specs=[rows(k), _weight_spec(w, layer, n, lambda i: 0), rows(n), pl.BlockSpec((1, n), lambda i: (0, 0))],
        out_specs=[rows(n), rows(n)],
        out_shape=[jax.ShapeDtypeStruct((m, n), F32), jax.ShapeDtypeStruct((m, n), BF16)],
        compiler_params=_params("parallel"),
    )(a, w, res, g.reshape(1, n))


def _rms_norm(x, g, out_dtype, tm=512):
    m, d = x.shape
    tm = min(tm, m)
    return pl.pallas_call(
        _norm_body,
        name="rms_norm",
        grid=(m // tm,),
        in_specs=[pl.BlockSpec((tm, d), lambda i: (i, 0)), pl.BlockSpec((1, d), lambda i: (0, 0))],
        out_specs=pl.BlockSpec((tm, d), lambda i: (i, 0)),
        out_shape=jax.ShapeDtypeStruct((m, d), out_dtype),
        compiler_params=_params("parallel"),
    )(x, g.reshape(1, d))


def _s5_matrices(lam_re, lam_im, b_re, b_im, c_re, c_im, d, log_dt):
    g_n, p_n, c_n, q = S5_GROUPS, S5_STATE, S5_GROUP, S5_Q
    hp = lax.Precision.HIGHEST
    dt = jnp.exp(log_dt)[:, None]
    n = jnp.arange(q + 1, dtype=F32)[:, None, None]
    mag = jnp.exp(n * (lam_re * dt))
    pw_re = mag * jnp.cos(n * (lam_im * dt))
    pw_im = mag * jnp.sin(n * (lam_im * dt))
    den = lam_re * lam_re + lam_im * lam_im
    nr, ni = pw_re[1] - 1.0, pw_im[1]
    f_re = (nr * lam_re + ni * lam_im) / den
    f_im = (ni * lam_re - nr * lam_im) / den
    e_re = f_re[..., None] * b_re - f_im[..., None] * b_im
    e_im = f_re[..., None] * b_im + f_im[..., None] * b_re
    cp_re = c_re[None] * pw_re[:, :, None, :] - c_im[None] * pw_im[:, :, None, :]
    cp_im = c_re[None] * pw_im[:, :, None, :] + c_im[None] * pw_re[:, :, None, :]
    kern = (jnp.einsum("tgop,gpi->tgio", cp_re[:q], e_re, precision=hp)
            - jnp.einsum("tgop,gpi->tgio", cp_im[:q], e_im, precision=hp))
    kern = kern.at[0].add(d.reshape(g_n, c_n)[:, :, None] * jnp.eye(c_n, dtype=F32))
    sg_n, gl_n = S5_SG, g_n // S5_SG
    rows = sg_n * S5_SGW
    lag = jnp.arange(q)[None, :] - jnp.arange(q)[:, None]
    kt = jnp.where((lag >= 0)[:, :, None, None, None], kern[jnp.clip(lag, 0, q - 1)], 0.0)
    kt = kt.reshape(q, q, sg_n, gl_n, c_n, c_n).transpose(2, 0, 3, 4, 1, 5)
    kt = kt.reshape(rows, q * c_n)
    rev_re, rev_im = pw_re[q - 1 - jnp.arange(q)], pw_im[q - 1 - jnp.arange(q)]
    et_re, et_im = e_re.transpose(0, 2, 1)[None], e_im.transpose(0, 2, 1)[None]
    ws_re = rev_re[:, :, None, :] * et_re - rev_im[:, :, None, :] * et_im
    ws_im = rev_re[:, :, None, :] * et_im + rev_im[:, :, None, :] * et_re
    ws = jnp.stack([ws_re, ws_im], axis=3)
    ws = ws.reshape(q, sg_n, gl_n, c_n, 2, p_n).transpose(1, 0, 2, 3, 4, 5).reshape(rows, 2 * p_n)
    wy = jnp.stack([cp_re[1:], -cp_im[1:]], axis=0)
    wy = wy.reshape(2, q, sg_n, gl_n, c_n, p_n).transpose(2, 0, 3, 5, 1, 4)
    wy = wy.reshape(rows, q * c_n)

    col = jnp.arange(S5_SGW)
    small = jnp.arange(q * c_n)[:, None]
    tok_rep = ((small // c_n == col[None, :] // LANES) & (small % c_n == col[None, :] % c_n)).astype(BF16)
    small = jnp.arange(2 * p_n)[:, None]
    st_rep = ((small // p_n == col[None, :] // S5_SB) & (small % p_n == col[None, :] % p_n)).astype(BF16)
    row = jnp.arange(rows)
    g_tok_r, g_tok_c = (row % LANES) // c_n, (col % LANES) // c_n
    g_st_r, g_st_c = (row % S5_SB) // p_n, (col % S5_SB) // p_n
    expand = lambda table, rep: jnp.dot(table.astype(BF16), rep, preferred_element_type=BF16)
    zero = jnp.zeros((), BF16)
    toep = jnp.where(g_tok_r[:, None] == g_tok_c[None, :], expand(kt, tok_rep), zero)
    w_state = jnp.where(g_tok_r[:, None] == g_st_c[None, :], expand(ws, st_rep), zero)
    w_out = jnp.where(g_st_r[:, None] == g_tok_c[None, :], expand(wy, tok_rep), zero)
    return toep, w_state, w_out, pw_re[q].reshape(1, S5_NSTATE), pw_im[q].reshape(1, S5_NSTATE)


def _s5_scan_body(ar_ref, ai_ref, s_ref, o_ref, c_ref):
    rows = BF16_ROWS
    nb = s_ref.shape[0]
    a_re, a_im = ar_ref[...], ai_ref[...]
    half = a_re.shape[1]

    @pl.when(pl.program_id(1) == 0)
    def _():
        c_ref[...] = jnp.zeros_like(c_ref)

    def block(i, carry):
        base = pl.multiple_of(i * rows, rows)
        xs = [s_ref[b, pl.ds(base, rows), :] for b in range(nb)]
        outs = [[] for _ in range(nb)]
        carry = list(carry)
        for r in range(rows):
            for b in range(nb):
                s_re, s_im = carry[b]
                outs[b].append(jnp.concatenate([s_re, s_im], axis=1))
                carry[b] = (a_re * s_re - a_im * s_im + xs[b][r:r + 1, :half],
                            a_re * s_im + a_im * s_re + xs[b][r:r + 1, half:])
        for b in range(nb):
            o_ref[b, pl.ds(base, rows), :] = jnp.concatenate(outs[b], axis=0).astype(o_ref.dtype)
        return tuple(carry)

    init = tuple((c_ref[b, :, :half], c_ref[b, :, half:]) for b in range(nb))
    last = lax.fori_loop(0, s_ref.shape[1] // rows, block, init)
    for b in range(nb):
        c_ref[b] = jnp.concatenate(last[b], axis=1)


def _s5_fold_body(h_ref, w_ref, o_ref, tok_ref):
    u = _dot(h_ref[...], w_ref[...])
    tc = o_ref.shape[0]
    for c in range(S5_SG):
        tok_ref[c] = u[:, c * LANES:(c + 1) * LANES]
        for j in range(S5_Q):
            lo = c * S5_SGW + j * LANES
            o_ref[:, lo:lo + LANES] = tok_ref[c, pl.ds(j, tc, stride=S5_Q), :].astype(o_ref.dtype)


def _s5_glu_body(y_ref, w_ref, o_ref, tok_ref):
    tc = y_ref.shape[0]
    for c in range(S5_SG):
        for j in range(S5_Q):
            lo = c * S5_SGW + j * LANES
            tok_ref[c, pl.ds(j, tc, stride=S5_Q), :] = y_ref[:, lo:lo + LANES]
    y = jax.nn.gelu(jnp.concatenate([tok_ref[c] for c in range(S5_SG)], axis=1))
    o_ref[...] = (y * jax.nn.sigmoid(_dot(y.astype(BF16), w_ref[...]))).astype(o_ref.dtype)


def _s5_mixer(h, w_u, mats, w_glu, bsz, seq):
    toep, w_state, w_out, aq_re, aq_im = mats
    m = bsz * seq
    nc = seq // S5_Q
    width = S5_Q * MIX_W
    tm = min(1024, m)
    tc = tm // S5_Q
    uc = pl.pallas_call(
        _s5_fold_body,
        name="s5_input_proj",
        grid=(m // tm,),
        in_specs=[pl.BlockSpec((tm, D_MODEL), lambda i: (i, 0)), pl.BlockSpec((D_MODEL, MIX_W), lambda i: (0, 0))],
        out_specs=pl.BlockSpec((tc, width), lambda i: (i, 0)),
        out_shape=jax.ShapeDtypeStruct((m // S5_Q, width), BF16),
        scratch_shapes=[pltpu.VMEM((MIX_W // LANES, tm, LANES), F32)],
        compiler_params=_params("parallel"),
    )(h, w_u)
    y1 = _matmul_blockdiag(uc, toep)
    s_end = _matmul_blockdiag(uc, w_state)
    rb = min(S5_SCAN_ROWS, nc)
    s_in = pl.pallas_call(
        _s5_scan_body,
        name="s5_chunk_scan",
        grid=(S5_SG, nc // rb),
        in_specs=[
            pl.BlockSpec((1, S5_SB), lambda j, r: (0, j)),
            pl.BlockSpec((1, S5_SB), lambda j, r: (0, j)),
            pl.BlockSpec((bsz, rb, S5_SGW), lambda j, r: (0, r, j)),
        ],
        out_specs=pl.BlockSpec((bsz, rb, S5_SGW), lambda j, r: (0, r, j)),
        out_shape=jax.ShapeDtypeStruct((bsz, nc, width), BF16),
        scratch_shapes=[pltpu.VMEM((bsz, 1, S5_SGW), F32)],
        compiler_params=_params("parallel", "arbitrary"),
    )(aq_re, aq_im, s_end.reshape(bsz, nc, width))
    y = _matmul_blockdiag(s_in.reshape(bsz * nc, width), w_out, res=y1)
    return pl.pallas_call(
        _s5_glu_body,
        name="s5_glu",
        grid=(m // tm,),
        in_specs=[pl.BlockSpec((tc, width), lambda i: (i, 0)), pl.BlockSpec((MIX_W, MIX_W), lambda i: (0, 0))],
        out_specs=pl.BlockSpec((tm, MIX_W), lambda i: (i, 0)),
        out_shape=jax.ShapeDtypeStruct((m, MIX_W), BF16),
        scratch_shapes=[pltpu.VMEM((MIX_W // LANES, tm, LANES), F32)],
        compiler_params=_params("parallel"),
    )(y, w_glu.astype(BF16))


def _lru_body(x_ref, g_ref, cw_ref, cb_ref, wa_ref, ba_ref, wx_ref, bx_ref, lam_ref, o_ref,
              halo_ref, h_ref, a_s, b_s):
    nb, t, _ = x_ref.shape

    @pl.when(pl.program_id(0) == 0)
    def _():
        halo_ref[...] = jnp.zeros_like(halo_ref)
        h_ref[...] = jnp.zeros_like(h_ref)

    soft = jax.nn.softplus(-lam_ref[...])
    for b in range(nb):
        x_in = x_ref[b]
        x = _causal_conv(x_in, halo_ref[b], cw_ref, cb_ref)
        halo_ref[b] = x_in[t - HALO:, :]
        xb = x.astype(BF16)
        r = jax.nn.sigmoid(_dot(xb, wa_ref[...]) + ba_ref[...])
        i = jax.nn.sigmoid(_dot(xb, wx_ref[...]) + bx_ref[...])
        log_a = (-LRU_C * r) * soft
        a_s[b] = jnp.exp(log_a)
        b_s[b] = x * i * jnp.sqrt(1.0 - jnp.exp(2.0 * log_a))
    rows = 8

    def block(k, hs):
        base = pl.multiple_of(k * rows, rows)
        av = [a_s[b, pl.ds(base, rows), :] for b in range(nb)]
        bv = [b_s[b, pl.ds(base, rows), :] for b in range(nb)]
        hs = list(hs)
        outs = [[] for _ in range(nb)]
        for rr in range(rows):
            for b in range(nb):
                hs[b] = av[b][rr:rr + 1, :] * hs[b] + bv[b][rr:rr + 1, :]
                outs[b].append(hs[b])
        for b in range(nb):
            b_s[b, pl.ds(base, rows), :] = jnp.concatenate(outs[b], axis=0)
        return tuple(hs)

    hs = lax.fori_loop(0, t // rows, block, tuple(h_ref[b] for b in range(nb)))
    for b in range(nb):
        h_ref[b] = hs[b]
        o_ref[b] = (b_s[b] * jax.nn.gelu(g_ref[b])).astype(o_ref.dtype)


def _block_diag(w):
    h_n, n, _ = w.shape
    eye = jnp.eye(h_n, dtype=w.dtype)
    return (w[:, :, None, :] * eye[:, None, :, None]).reshape(h_n * n, h_n * n)


def _lru_mixer(proj, conv_w, conv_b, w_a, b_a, w_x, b_x, lam, bsz, seq):
    t = min(256, seq)
    vec = lambda v: v.reshape(1, MIX_W)
    const = lambda shape: pl.BlockSpec(shape, lambda c: (0, 0))
    proj3 = proj.reshape(bsz, seq, proj.shape[1])
    out = pl.pallas_call(
        _lru_body,
        name="rglru_mixer",
        grid=(seq // t,),
        in_specs=[
            pl.BlockSpec((bsz, t, MIX_W), lambda c: (0, c, PC_LRU_X // MIX_W)),
            pl.BlockSpec((bsz, t, MIX_W), lambda c: (0, c, PC_LRU_G // MIX_W)),
            const(conv_w.shape), const((1, MIX_W)),
            const((MIX_W, MIX_W)), const((1, MIX_W)),
            const((MIX_W, MIX_W)), const((1, MIX_W)), const((1, MIX_W)),
        ],
        out_specs=pl.BlockSpec((bsz, t, MIX_W), lambda c: (0, c, 0)),
        out_shape=jax.ShapeDtypeStruct((bsz, seq, MIX_W), BF16),
        scratch_shapes=[
            pltpu.VMEM((bsz, HALO, MIX_W), F32), pltpu.VMEM((bsz, 1, MIX_W), F32),
            pltpu.VMEM((bsz, t, MIX_W), F32), pltpu.VMEM((bsz, t, MIX_W), F32),
        ],
        compiler_params=_params("arbitrary"),
    )(proj3, proj3, conv_w, vec(conv_b), _block_diag(w_a).astype(BF16), vec(b_a),
      _block_diag(w_x).astype(BF16), vec(b_x), vec(lam))
    return out.reshape(bsz * seq, MIX_W)


def _m2_body(z_ref, xbc_ref, dt_ref, rep_ref, cw_ref, cb_ref, dtb_ref, alog_ref, d_ref, ng_ref, o_ref,
             halo_ref, st_ref):
    nb, q, _ = z_ref.shape
    hg = N_HEADS // M2_GROUPS
    gw = hg * HEAD_W
    seqs = range(nb)

    @pl.when(pl.program_id(1) == 0)
    def _():
        halo_ref[...] = jnp.zeros_like(halo_ref)
        st_ref[...] = jnp.zeros_like(st_ref)

    causal = _tril_mask(q)
    causal_b = causal.astype(BF16)
    neg_a = -jnp.exp(alog_ref[...])
    dt_raw = _exact_right([dt_ref[b] for b in seqs], rep_ref[...])
    xc = []
    for b in seqs:
        xbc = xbc_ref[b]
        conv = _causal_conv(xbc, halo_ref[b], cw_ref, cb_ref)
        halo_ref[b] = xbc[q - HALO:, :]
        xc.append(conv * jax.nn.sigmoid(conv))
    dt = [jax.nn.softplus(dt_raw[b] + dtb_ref[...]) for b in seqs]
    a_cs = [_exact_left(causal_b, dt[b] * neg_a) for b in seqs]
    xs = [xc[b][:, :MIX_W] for b in seqs]
    xd = [xs[b] * dt[b] for b in seqs]
    a_cs_t = [a_cs[b].T for b in seqs]
    a_last = [a_cs[b][q - 1:q, :] for b in seqs]
    xd_st = [(xd[b] * jnp.exp(a_last[b] - a_cs[b])).astype(BF16) for b in seqs]
    e_cs = [jnp.exp(a_cs[b]) for b in seqs]
    bg, cg, cb, y_off = {}, {}, {}, {}
    for b in seqs:
        for g in range(M2_GROUPS):
            lo = MIX_W + g * M2_STATE
            bg[b, g] = xc[b][:, lo:lo + M2_STATE].astype(BF16)
            cg[b, g] = xc[b][:, lo + M2_GROUPS * M2_STATE:lo + (M2_GROUPS + 1) * M2_STATE].astype(BF16)
            cb[b, g] = _dot_nt(cg[b, g], bg[b, g])
            y_off[b, g] = _dot(cg[b, g], st_ref[b, g].astype(BF16)) * e_cs[b][:, g * gw:(g + 1) * gw]
    ys = [[] for _ in seqs]
    for g in range(M2_GROUPS):
        for hh in range(hg):
            for b in seqs:
                lo = (g * hg + hh) * HEAD_W
                col = a_cs[b][:, lo:lo + HEAD_W]
                col = jnp.concatenate([col] * (q // HEAD_W), axis=1)
                seg = col - a_cs_t[b][lo:lo + 1, :]
                dec = jnp.exp(jnp.where(causal, seg, -jnp.inf))
                y_d = _dot((cb[b, g] * dec).astype(BF16), xd[b][:, lo:lo + HEAD_W].astype(BF16))
                ys[b].append(y_d + y_off[b, g][:, hh * HEAD_W:(hh + 1) * HEAD_W])
    for b in seqs:
        for g in range(M2_GROUPS):
            upd = _dot_tn(bg[b, g], xd_st[b][:, g * gw:(g + 1) * gw])
            st_ref[b, g] = st_ref[b, g] * jnp.exp(a_last[b][:, g * gw:(g + 1) * gw]) + upd
    for b in seqs:
        y = jnp.concatenate(ys[b], axis=1) + d_ref[...] * xs[b]
        z = z_ref[b]
        y = y * (z * jax.nn.sigmoid(z))
        ms = jnp.mean(y * y, axis=-1, keepdims=True)
        o_ref[b] = (y * lax.rsqrt(ms + EPS) * ng_ref[...]).astype(o_ref.dtype)


def _per_head(v):
    return jnp.repeat(v, HEAD_W).reshape(1, -1)


def _m2_mixer(proj, conv_w, conv_b, dt_bias, a_log, d, norm_g, bsz, seq):
    q = min(M2_Q, seq)
    nc = seq // q
    nb = M2_BATCH if bsz % M2_BATCH == 0 else 1
    const = lambda shape: pl.BlockSpec(shape, lambda b, c: (0, 0))
    head_of_lane = jnp.arange(MIX_W)[None, :] // HEAD_W
    rep = (jnp.arange(LANES)[:, None] == head_of_lane).astype(BF16)
    proj3 = proj.reshape(bsz, seq, proj.shape[1])
    out = pl.pallas_call(
        _m2_body,
        name="mamba2_mixer",
        grid=(bsz // nb, nc),
        in_specs=[
            pl.BlockSpec((nb, q, MIX_W), lambda b, c: (b, c, PC_Z // MIX_W)),
            pl.BlockSpec((nb, q, M2_CONV_DIM), lambda b, c: (b, c, PC_XBC // M2_CONV_DIM)),
            pl.BlockSpec((nb, q, LANES), lambda b, c: (b, c, PC_DT // LANES)),
            const((LANES, MIX_W)), const(conv_w.shape), const((1, M2_CONV_DIM)),
            const((1, MIX_W)), const((1, MIX_W)), const((1, MIX_W)), const((1, MIX_W)),
        ],
        out_specs=pl.BlockSpec((nb, q, MIX_W), lambda b, c: (b, c, 0)),
        out_shape=jax.ShapeDtypeStruct((bsz, seq, MIX_W), BF16),
        scratch_shapes=[
            pltpu.VMEM((nb, HALO, M2_CONV_DIM), F32),
            pltpu.VMEM((nb, M2_GROUPS, M2_STATE, (N_HEADS // M2_GROUPS) * HEAD_W), F32),
        ],
        compiler_params=_params("parallel", "arbitrary"),
    )(proj3, proj3, proj3, rep, conv_w, conv_b.reshape(1, -1), _per_head(dt_bias), _per_head(a_log),
      _per_head(d), norm_g.reshape(1, -1))
    return out.reshape(bsz * seq, MIX_W)


def _rw_body(p_ref, pl_ref, mu_ref, mul_ref, w0_ref, w2_ref, a0_ref, a2_ref, g2_ref, kk_ref, ka_ref, rk_ref,
             lng_ref, lnb_ref, ones_ref, o_ref, prev_ref, prevl_ref, st_ref):
    nb, q, _ = p_ref.shape
    rows = nb * q

    @pl.when(pl.program_id(1) == 0)
    def _():
        prev_ref[...] = jnp.zeros_like(prev_ref)
        prevl_ref[...] = jnp.zeros_like(prevl_ref)
        st_ref[...] = jnp.zeros_like(st_ref)

    def token_mix(x_ref, last_ref, m_ref):
        x = x_ref[...].reshape(rows, x_ref.shape[2])
        row = lax.broadcasted_iota(jnp.int32, x.shape, 0)
        shifted = pltpu.roll(x, 1, axis=0)
        for b in range(nb):
            shifted = jnp.where(row == b * q, last_ref[b, 0:1, :], shifted)
            last_ref[b, 0:1, :] = x[(b + 1) * q - 1:(b + 1) * q, :]
        return x + (shifted - x) * m_ref[...]

    p = token_mix(p_ref, prev_ref, mu_ref)
    lora = token_mix(pl_ref, prevl_ref, mul_ref)
    r = p[:, 0:MIX_W]
    k = p[:, MIX_W:2 * MIX_W]
    v = p[:, 2 * MIX_W:3 * MIX_W]
    w = -jax.nn.softplus(-(w0_ref[...] + _dot(jnp.tanh(lora).astype(BF16), w2_ref[...]))) - 0.5
    log_w = -jnp.exp(w)
    a_sig = jax.nn.sigmoid(a0_ref[...] + _dot(lora.astype(BF16), a2_ref[...]))
    gate = _dot(jax.nn.sigmoid(lora).astype(BF16), g2_ref[...])
    ones = ones_ref[...]
    kk = k * kk_ref[...]
    k = k * (1.0 + (a_sig - 1.0) * ka_ref[...])
    kk_sq, rk_sum = _exact_right([kk * kk, r * k * rk_ref[...]], ones)
    kk = kk / jnp.maximum(jnp.sqrt(kk_sq), 1e-12)
    a_vec = -kk
    b_vec = kk * a_sig

    rr = lax.broadcasted_iota(jnp.int32, (rows, rows), 0)
    cc = lax.broadcasted_iota(jnp.int32, (rows, rows), 1)
    cum_mask = jnp.where((rr >= cc) & ((rr // q) == (cc // q)), 1.0, 0.0).astype(BF16)
    gam = _exact_left(cum_mask, log_w)
    g_last = jnp.concatenate(
        [jnp.broadcast_to(gam[(b + 1) * q - 1:(b + 1) * q, :], (q, MIX_W)) for b in range(nb)], axis=0)
    e_neg = jnp.exp(-gam)
    e_rem = jnp.exp(g_last - gam)
    a_t = a_vec * jnp.exp(gam - log_w)
    r_t = r * jnp.exp(gam)
    b_t = b_vec * e_neg
    k_t = k * e_neg
    b_h = b_vec * e_rem
    k_h = k * e_rem
    e_last = jnp.exp(g_last)
    assert q == HEAD_W
    row2 = lax.broadcasted_iota(jnp.int32, (q, 2 * HEAD_W), 0)
    lane2 = lax.broadcasted_iota(jnp.int32, (q, 2 * HEAD_W), 1)
    low = lane2 < HEAD_W
    col2 = jnp.where(low, lane2, lane2 - HEAD_W)
    keep_z0 = low | (col2 < row2)
    keep_t = col2 <= row2
    strict = _tril_mask(q, strict=True)
    pairs = [(b, h) for b in range(nb) for h in range(N_HEADS)]
    zeros_w = jnp.zeros((HEAD_W, 2 * HEAD_W), BF16)
    zeros_x = jnp.zeros((q, HEAD_W), F32)

    def blk(t, key):
        b, h = key
        return t[b * q:(b + 1) * q, h * HEAD_W:(h + 1) * HEAD_W]

    p0, p1, s0, z = {}, {}, {}, {}
    for key in pairs:
        ar = jnp.concatenate([blk(a_t, key), blk(r_t, key)], axis=0).astype(BF16)
        s0[key] = st_ref[key[0], key[1]]
        rhs = jnp.concatenate([s0[key], blk(b_t, key), blk(k_t, key), zeros_x], axis=0).astype(BF16)
        prod = _dot_nt(ar, rhs)
        p0[key] = prod[:, :2 * HEAD_W]
        p1[key] = prod[:, 2 * HEAD_W:]
    for key in pairs:
        a_k = jnp.where(strict, p1[key][:q, :HEAD_W], 0.0).astype(BF16)
        x0 = _dot(a_k, blk(v, key).astype(BF16))
        z[key] = jnp.where(keep_z0, p0[key][:q], 0.0) + jnp.concatenate([x0, zeros_x], axis=1)
    steps = max(1, (q - 1).bit_length())
    for _ in range(steps):
        for key in pairs:
            zb = z[key].astype(BF16)
            z[key] = jnp.where(low, z[key], 0.0) + _dot(zb, jnp.concatenate([zeros_w, zb], axis=0))
    ys = {}
    for key in pairs:
        vu = jnp.concatenate([blk(v, key), z[key][:, :HEAD_W]], axis=0).astype(BF16)
        t_mat = jnp.where(keep_t, jnp.where(low, p1[key][q:], p0[key][q:]), 0.0)
        ys[key] = p0[key][q:, :HEAD_W] + _dot(t_mat.astype(BF16), vu)
        khb = jnp.concatenate([blk(k_h, key), blk(b_h, key)], axis=0).astype(BF16)
        st_ref[key[0], key[1]] = s0[key] * blk(e_last, key)[0:1, :] + _dot_tn(vu, khb)
    y = jnp.concatenate(
        [jnp.concatenate([ys[(b, h)] for h in range(N_HEADS)], axis=1) for b in range(nb)], axis=0)
    inv_n = 1.0 / HEAD_W
    yc = y - _exact_right([y], ones)[0] * inv_n
    var = _exact_right([yc * yc], ones)[0] * inv_n
    y = yc * lax.rsqrt(var + RW_LN_EPS) * lng_ref[...] + lnb_ref[...]
    bonus = rk_sum * v
    o_ref[...] = ((y + bonus) * gate).astype(o_ref.dtype).reshape(o_ref.shape)


def _pad_rows(w, rows, off):
    return jnp.zeros((rows, w.shape[1]), w.dtype).at[off:off + w.shape[0]].set(w)


def _rw_mixer(proj, mu, w0, w2, a0, a2, g2, k_k, k_a, r_k, ln_g, ln_b, bsz, seq):
    q = min(RW_Q, seq)
    nc = seq // q
    nb = RW_BATCH if bsz % RW_BATCH == 0 else 1
    vec = lambda t: t.reshape(1, MIX_W)
    const = lambda shape: pl.BlockSpec(shape, lambda b, c: (0, 0))
    offs = (0, RW_LORA_RANKS[0], RW_LORA_RANKS[0] + RW_LORA_RANKS[1])
    lora = lambda t, off: _pad_rows(t, RW_LORA_W, off).astype(BF16)
    mu_rkv, mu_lora = mu[:RW_RKV_W].reshape(1, RW_RKV_W), _pad_rows(mu[RW_RKV_W:, None], RW_LORA_W, 0).reshape(1, RW_LORA_W)
    ones = _block_diag(jnp.ones((N_HEADS, HEAD_W, HEAD_W), F32)).astype(BF16)
    proj3 = proj.reshape(bsz, seq, proj.shape[1])
    out = pl.pallas_call(
        _rw_body,
        name="rwkv7_mixer",
        grid=(bsz // nb, nc),
        in_specs=[
            pl.BlockSpec((nb, q, RW_RKV_W), lambda b, c: (b, c, PC_RKV // RW_RKV_W)),
            pl.BlockSpec((nb, q, RW_LORA_W), lambda b, c: (b, c, PC_LORA // RW_LORA_W)),
            const((1, RW_RKV_W)), const((1, RW_LORA_W)), const((1, MIX_W)), const((RW_LORA_W, MIX_W)),
            const((1, MIX_W)), const((RW_LORA_W, MIX_W)), const((RW_LORA_W, MIX_W)),
            const((1, MIX_W)), const((1, MIX_W)), const((1, MIX_W)),
            const((1, MIX_W)), const((1, MIX_W)), const((MIX_W, MIX_W)),
        ],
        out_specs=pl.BlockSpec((nb, q, MIX_W), lambda b, c: (b, c, 0)),
        out_shape=jax.ShapeDtypeStruct((bsz, seq, MIX_W), BF16),
        scratch_shapes=[pltpu.VMEM((nb, HALO, RW_RKV_W), F32), pltpu.VMEM((nb, HALO, RW_LORA_W), F32),
                        pltpu.VMEM((nb, N_HEADS, HEAD_W, HEAD_W), F32)],
        compiler_params=_params("parallel", "arbitrary"),
    )(proj3, proj3, mu_rkv, mu_lora, vec(w0), lora(w2, offs[0]), vec(a0), lora(a2, offs[1]), lora(g2, offs[2]),
      vec(k_k), vec(k_a), vec(r_k), vec(ln_g), vec(ln_b), ones)
    return out.reshape(bsz * seq, MIX_W)


def _merge_body(h_ref, *refs):
    nb = (len(refs) - 2) // 2
    y_refs, wg_refs, wp_ref, o_ref = refs[:nb], refs[nb:2 * nb], refs[2 * nb], refs[2 * nb + 1]
    h = h_ref[...]
    acc = None
    for kk in range(nb):
        term = jax.nn.sigmoid(_dot(h, wg_refs[kk][...])) * _dot(y_refs[kk][...], wp_ref[kk])
        acc = term if acc is None else acc + term
    o_ref[...] = acc.astype(o_ref.dtype)


def _merge(h, ys, w_gate, w_branch, layer=None, tm=512, tn=512):
    m = h.shape[0]
    tm = min(tm, m)
    nb = len(ys)
    y_spec = pl.BlockSpec((tm, MIX_W), lambda j, i: (i, 0))
    gate_specs = [_weight_spec(w_gate, layer, tn, lambda j, i, k=k: k * (D_MODEL // tn) + j) for k in range(nb)]
    if w_branch.ndim == 3:
        branch_spec = pl.BlockSpec((nb, MIX_W, tn), lambda j, i: (0, 0, j))
    else:
        branch_spec = pl.BlockSpec((None, nb, MIX_W, tn), lambda j, i: (layer, 0, 0, j))
    return pl.pallas_call(
        _merge_body,
        name="gated_merge",
        grid=(D_MODEL // tn, m // tm),
        in_specs=[pl.BlockSpec((tm, D_MODEL), lambda j, i: (i, 0))] + [y_spec] * nb + gate_specs + [branch_spec],
        out_specs=pl.BlockSpec((tm, tn), lambda j, i: (i, j)),
        out_shape=jax.ShapeDtypeStruct((m, D_MODEL), BF16),
        compiler_params=_params("parallel", "parallel"),
    )(h, *ys, *([w_gate] * nb), w_branch)


def _ffn_body(h_ref, hh_ref, wg_ref, wu_ref, cw_ref, cb_ref, o_ref, *, tiles_per_seq):
    t = h_ref.shape[0]
    nh = hh_ref.shape[0]
    h = h_ref[...]
    seq_start = (pl.program_id(1) % tiles_per_seq) == 0
    halo = jnp.where(seq_start, jnp.zeros_like(hh_ref[...]), hh_ref[...])
    g_all = _dot(jnp.concatenate([halo, h], axis=0), wg_ref[...])
    k_w = cw_ref.shape[0]
    u = cb_ref[...] + g_all[nh:, :] * cw_ref[k_w - 1:k_w, :]
    for j in range(k_w - 1):
        lag = k_w - 1 - j
        u = u + g_all[nh - lag:nh - lag + t, :] * cw_ref[j:j + 1, :]
    o_ref[...] = (jax.nn.gelu(u) * _dot(h, wu_ref[...])).astype(o_ref.dtype)


def _ffn_act(h, w_gate, w_up, conv_w, conv_b, seq, layer=None, tm=1024, tf=512):
    m = h.shape[0]
    tm = min(tm, seq)
    f = w_gate.shape[-1]
    halo_blocks = tm // BF16_ROWS
    return pl.pallas_call(
        functools.partial(_ffn_body, tiles_per_seq=seq // tm),
        name="ffn_gate_up",
        grid=(f // tf, m // tm),
        in_specs=[
            pl.BlockSpec((tm, D_MODEL), lambda j, i: (i, 0)),
            pl.BlockSpec((BF16_ROWS, D_MODEL), lambda j, i: (jnp.maximum(i * halo_blocks - 1, 0), 0)),
            _weight_spec(w_gate, layer, tf, lambda j, i: j),
            _weight_spec(w_up, layer, tf, lambda j, i: j),
            pl.BlockSpec((conv_w.shape[0], tf), lambda j, i: (0, j)),
            pl.BlockSpec((1, tf), lambda j, i: (0, j)),
        ],
        out_specs=pl.BlockSpec((tm, tf), lambda j, i: (i, j)),
        out_shape=jax.ShapeDtypeStruct((m, f), BF16),
        compiler_params=_params("parallel", "parallel"),
    )(h, h, w_gate, w_up, conv_w, conv_b.reshape(1, f))


def _mixer_weight(w_in_l):
    s5_u, lru_x, lru_g, m2_z, m2_xbc, m2_dt, rw_p = jnp.split(
        w_in_l, [512, 1024, 1536, 2048, 2048 + M2_CONV_DIM, 2048 + M2_CONV_DIM + N_HEADS], axis=1)
    d = w_in_l.shape[0]
    padc = lambda t, n: jnp.concatenate([t, jnp.zeros((d, n - t.shape[1]), t.dtype)], axis=1)
    proj_w = jnp.concatenate([m2_xbc, m2_z, rw_p[:, :RW_RKV_W], lru_x, lru_g,
                              padc(rw_p[:, RW_RKV_W:], RW_LORA_W), padc(m2_dt, PROJ_W - PC_DT)], axis=1)
    assert proj_w.shape[1] == PROJ_W
    return s5_u.astype(BF16), proj_w.astype(BF16)


def kernel(x, norm_mix_g, w_in, s5_lambda_re, s5_lambda_im, s5_b_re, s5_b_im, s5_c_re, s5_c_im, s5_d, s5_log_dt, s5_w_glu, lru_conv_w, lru_conv_b, lru_w_a, lru_b_a, lru_w_x, lru_b_x, lru_lambda, m2_conv_w, m2_conv_b, m2_dt_bias, m2_a_log, m2_d, m2_norm_g, rw_mu, rw_w0, rw_w2, rw_a0, rw_a2, rw_g2, rw_k_k, rw_k_a, rw_r_k, rw_ln_g, rw_ln_b, w_branch, w_out, norm_ffn_g, w_ffn_gate, w_ffn_up, ffn_conv_w, ffn_conv_b, w_ffn_down, final_norm_g):
    bsz, seq, d = x.shape
    depth = w_in.shape[0]
    mixer_cols = w_in.shape[2] - w_branch.shape[1] * d
    xf = x.reshape(bsz * seq, d)
    wb_all, wo_all = w_branch.astype(BF16), w_out.astype(BF16)
    wg_all, wu_all, wd_all = w_ffn_gate.astype(BF16), w_ffn_up.astype(BF16), w_ffn_down.astype(BF16)
    for l in range(depth):
        h = _rms_norm(xf, norm_mix_g[l], BF16)
        w_s5, w_proj = _mixer_weight(w_in[l, :, :mixer_cols])
        proj = _matmul(h, w_proj)
        mats = _s5_matrices(s5_lambda_re[l], s5_lambda_im[l], s5_b_re[l], s5_b_im[l],
                            s5_c_re[l], s5_c_im[l], s5_d[l], s5_log_dt[l])
        y_a = _s5_mixer(h, w_s5, mats, s5_w_glu[l], bsz, seq)
        y_b = _lru_mixer(proj, lru_conv_w[l], lru_conv_b[l], lru_w_a[l], lru_b_a[l],
                         lru_w_x[l], lru_b_x[l], lru_lambda[l], bsz, seq)
        y_c = _m2_mixer(proj, m2_conv_w[l], m2_conv_b[l], m2_dt_bias[l], m2_a_log[l],
                        m2_d[l], m2_norm_g[l], bsz, seq)
        y_d = _rw_mixer(proj, rw_mu[l], rw_w0[l], rw_w2[l], rw_a0[l], rw_a2[l], rw_g2[l],
                        rw_k_k[l], rw_k_a[l], rw_r_k[l].reshape(-1), rw_ln_g[l], rw_ln_b[l], bsz, seq)
        w_gate = w_in[l, :, mixer_cols:].astype(BF16)
        merged = _merge(h, (y_a, y_b, y_c, y_d), w_gate, wb_all, layer=l)
        xf, h = _matmul_res_norm(merged, wo_all, xf, norm_ffn_g[l], l)
        act = _ffn_act(h, wg_all, wu_all, ffn_conv_w[l], ffn_conv_b[l], seq, layer=l)
        xf = _matmul(act, wd_all, res=xf, tm=512, n_outer=True, layer=l)
    return _rms_norm(xf, final_norm_g, F32).reshape(bsz, seq, d)
```

```python
import functools

import jax
import jax.numpy as jnp
from jax import lax
from jax.experimental import pallas as pl
from jax.experimental.pallas import tpu as pltpu

F32 = jnp.float32
BF16 = jnp.bfloat16

D_MODEL = 2048
MIX_W = 512
HEAD_W = 64
N_HEADS = MIX_W // HEAD_W
S5_GROUP = 16
S5_GROUPS = MIX_W // S5_GROUP
S5_STATE = 64
S5_Q = 8
S5_NSTATE = S5_GROUPS * S5_STATE
LANES = 128
S5_SG = MIX_W // LANES
S5_SGW = S5_Q * LANES
S5_SB = S5_NSTATE // S5_SG
S5_SCAN_ROWS = 512
LRU_HEADS = 8
LRU_C = 8.0
M2_GROUPS = 2
M2_STATE = 128
M2_CONV_DIM = MIX_W + 2 * M2_GROUPS * M2_STATE
M2_Q = 128
M2_BATCH = 2
RW_Q = 64
RW_BATCH = 4
RW_RKV_W = 3 * MIX_W
RW_LORA_RANKS = (32, 32, 96)
RW_LORA_W = 256
RW_LN_EPS = 64e-5
FFN_DIM = 3 * D_MODEL
EPS = 1e-6
HALO = 8
BF16_ROWS = 16

PC_XBC = 0
PC_Z = 1024
PC_RKV = 1536
PC_LRU_X = 3072
PC_LRU_G = 3584
PC_LORA = 4096
PC_DT = 4352
PROJ_W = 4608

VMEM_LIMIT_BYTES = 50 * 1024 * 1024


def _params(*sem):
    return pltpu.CompilerParams(dimension_semantics=sem, vmem_limit_bytes=VMEM_LIMIT_BYTES)


def _dot(a, b):
    return jnp.dot(a, b, preferred_element_type=F32)


def _dot_nt(a, b):
    return lax.dot_general(a, b, (((1,), (1,)), ((), ())), preferred_element_type=F32)


def _dot_tn(a, b):
    return lax.dot_general(a, b, (((0,), (0,)), ((), ())), preferred_element_type=F32)


def _split(x, terms):
    out = []
    for _ in range(terms - 1):
        hi = x.astype(BF16)
        out.append(hi)
        x = x - hi.astype(F32)
    out.append(x.astype(BF16))
    return out


def _exact_left(m, x, terms=3):
    return sum(_dot(m, p) for p in _split(x, terms))


def _exact_right(xs, m, terms=2):
    n = xs[0].shape[0]
    stacked = jnp.concatenate([p for x in xs for p in _split(x, terms)], axis=0)
    out = _dot(stacked, m)
    return [sum(out[(i * terms + t) * n:(i * terms + t + 1) * n] for t in range(terms))
            for i in range(len(xs))]


def _tril_mask(n, strict=False):
    r = lax.broadcasted_iota(jnp.int32, (n, n), 0)
    c = lax.broadcasted_iota(jnp.int32, (n, n), 1)
    return (r > c) if strict else (r >= c)


def _causal_conv(x, halo, w_ref, b_ref):
    k_w = w_ref.shape[0]
    t = x.shape[0]
    xe = jnp.concatenate([halo, x], axis=0)
    out = b_ref[...] + x * w_ref[k_w - 1:k_w, :]
    for j in range(k_w - 1):
        lag = k_w - 1 - j
        out = out + xe[HALO - lag:HALO - lag + t, :] * w_ref[j:j + 1, :]
    return out


def _mm_body(a_ref, b_ref, o_ref):
    o_ref[...] = _dot(a_ref[...], b_ref[...]).astype(o_ref.dtype)


def _mm_res_body(a_ref, b_ref, r_ref, o_ref):
    o_ref[...] = (_dot(a_ref[...], b_ref[...]) + r_ref[...]).astype(o_ref.dtype)


def _weight_spec(w, layer, tn, col_of):
    k = w.shape[-2]
    if w.ndim == 2:
        return pl.BlockSpec((k, tn), lambda *g: (0, col_of(*g)))
    return pl.BlockSpec((None, k, tn), lambda *g: (layer, 0, col_of(*g)))


def _matmul(a, b, res=None, out_dtype=F32, tm=1024, tn=512, n_outer=False, layer=None):
    m, k = a.shape
    n = b.shape[-1]
    tm, tn = min(tm, m), min(tn, n)
    assert m % tm == 0 and n % tn == 0
    if n_outer:
        grid = (n // tn, m // tm)
        row = lambda j, i: (i, 0)
        col = _weight_spec(b, layer, tn, lambda j, i: j)
        out = lambda j, i: (i, j)
    else:
        grid = (m // tm, n // tn)
        row = lambda i, j: (i, 0)
        col = _weight_spec(b, layer, tn, lambda i, j: j)
        out = lambda i, j: (i, j)
    in_specs = [pl.BlockSpec((tm, k), row), col]
    args = [a, b]
    body = _mm_body
    if res is not None:
        in_specs.append(pl.BlockSpec((tm, tn), out))
        args.append(res)
        body = _mm_res_body
    return pl.pallas_call(
        body,
        name="matmul",
        grid=grid,
        in_specs=in_specs,
        out_specs=pl.BlockSpec((tm, tn), out),
        out_shape=jax.ShapeDtypeStruct((m, n), out_dtype),
        compiler_params=_params("parallel", "parallel"),
    )(*args)


def _matmul_blockdiag(a, w, groups, res=None, res_block=None, tm=1024):
    m = a.shape[0]
    k, n = a.shape[1] // groups, w.shape[1]
    tm = min(tm, m)
    out_spec = pl.BlockSpec((tm, n), lambda i, s: (i, s))
    in_specs = [pl.BlockSpec((tm, k), lambda i, s: (i, s)), pl.BlockSpec((k, n), lambda i, s: (s, 0))]
    args = [a, w]
    if res is not None:
        in_specs.append(pl.BlockSpec((tm, n), lambda i, s: (i, res_block(s))))
        args.append(res)
    return pl.pallas_call(
        _mm_body if res is None else _mm_res_body,
        name="matmul_blockdiag",
        grid=(m // tm, groups),
        in_specs=in_specs,
        out_specs=out_spec,
        out_shape=jax.ShapeDtypeStruct((m, groups * n), F32),
        compiler_params=_params("parallel", "parallel"),
    )(*args)


def _norm_body(x_ref, g_ref, o_ref):
    x = x_ref[...]
    ms = jnp.mean(x * x, axis=-1, keepdims=True)
    o_ref[...] = (x * lax.rsqrt(ms + EPS) * g_ref[...]).astype(o_ref.dtype)


def _mm_res_norm_body(a_ref, w_ref, r_ref, g_ref, x_ref, h_ref):
    x = _dot(a_ref[...], w_ref[...]) + r_ref[...]
    x_ref[...] = x
    ms = jnp.mean(x * x, axis=-1, keepdims=True)
    h_ref[...] = (x * lax.rsqrt(ms + EPS) * g_ref[...]).astype(h_ref.dtype)


def _matmul_res_norm(a, w, res, g, layer, tm=512):
    m, k = a.shape
    n = w.shape[-1]
    tm = min(tm, m)
    rows = lambda width: pl.BlockSpec((tm, width), lambda i: (i, 0))
    return pl.pallas_call(
        _mm_res_norm_body,
        name="matmul_residual_norm",
        grid=(m // tm,),
        in_specs=[rows(k), _weight_spec(w, layer, n, lambda i: 0), rows(n), pl.BlockSpec((1, n), lambda i: (0, 0))],
        out_specs=[rows(n), rows(n)],
        out_shape=[jax.ShapeDtypeStruct((m, n), F32), jax.ShapeDtypeStruct((m, n), BF16)],
        compiler_params=_params("parallel"),
    )(a, w, res, g.reshape(1, n))


def _rms_norm(x, g, out_dtype, tm=512):
    m, d = x.shape
    tm = min(tm, m)
    return pl.pallas_call(
        _norm_body,
        name="rms_norm",
        grid=(m // tm,),
        in_specs=[pl.BlockSpec((tm, d), lambda i: (i, 0)), pl.BlockSpec((1, d), lambda i: (0, 0))],
        out_specs=pl.BlockSpec((tm, d), lambda i: (i, 0)),
        out_shape=jax.ShapeDtypeStruct((m, d), out_dtype),
        compiler_params=_params("parallel"),
    )(x, g.reshape(1, d))


def _s5_matrices(lam_re, lam_im, b_re, b_im, c_re, c_im, d, log_dt):
    g_n, p_n, c_n, q = S5_GROUPS, S5_STATE, S5_GROUP, S5_Q
    hp = lax.Precision.HIGHEST
    dt = jnp.exp(log_dt)[:, None]
    n = jnp.arange(q + 1, dtype=F32)[:, None, None]
    mag = jnp.exp(n * (lam_re * dt))
    pw_re = mag * jnp.cos(n * (lam_im * dt))
    pw_im = mag * jnp.sin(n * (lam_im * dt))
    den = lam_re * lam_re + lam_im * lam_im
    nr, ni = pw_re[1] - 1.0, pw_im[1]
    f_re = (nr * lam_re + ni * lam_im) / den
    f_im = (ni * lam_re - nr * lam_im) / den
    e_re = f_re[..., None] * b_re - f_im[..., None] * b_im
    e_im = f_re[..., None] * b_im + f_im[..., None] * b_re
    cp_re = c_re[None] * pw_re[:, :, None, :] - c_im[None] * pw_im[:, :, None, :]
    cp_im = c_re[None] * pw_im[:, :, None, :] + c_im[None] * pw_re[:, :, None, :]
    kern = (jnp.einsum("tgop,gpi->tgio", cp_re[:q], e_re, precision=hp)
            - jnp.einsum("tgop,gpi->tgio", cp_im[:q], e_im, precision=hp))
    kern = kern.at[0].add(d.reshape(g_n, c_n)[:, :, None] * jnp.eye(c_n, dtype=F32))
    sg_n, gl_n = S5_SG, g_n // S5_SG
    rows = sg_n * S5_SGW
    lag = jnp.arange(q)[None, :] - jnp.arange(q)[:, None]
    kt = jnp.where((lag >= 0)[:, :, None, None, None], kern[jnp.clip(lag, 0, q - 1)], 0.0)
    kt = kt.reshape(q, q, sg_n, gl_n, c_n, c_n).transpose(2, 0, 3, 4, 1, 5)
    kt = kt.reshape(rows, q * c_n)
    rev_re, rev_im = pw_re[q - 1 - jnp.arange(q)], pw_im[q - 1 - jnp.arange(q)]
    et_re, et_im = e_re.transpose(0, 2, 1)[None], e_im.transpose(0, 2, 1)[None]
    ws_re = rev_re[:, :, None, :] * et_re - rev_im[:, :, None, :] * et_im
    ws_im = rev_re[:, :, None, :] * et_im + rev_im[:, :, None, :] * et_re
    ws = jnp.stack([ws_re, ws_im], axis=3)
    ws = ws.reshape(q, sg_n, gl_n, c_n, 2, p_n).transpose(1, 0, 2, 3, 4, 5).reshape(rows, 2 * p_n)
    wy = jnp.stack([cp_re[1:], -cp_im[1:]], axis=0)
    wy = wy.reshape(2, q, sg_n, gl_n, c_n, p_n).transpose(2, 0, 3, 5, 1, 4)
    wy = wy.reshape(rows, q * c_n)

    col = jnp.arange(S5_SGW)
    small = jnp.arange(q * c_n)[:, None]
    tok_rep = ((small // c_n == col[None, :] // LANES) & (small % c_n == col[None, :] % c_n)).astype(BF16)
    small = jnp.arange(2 * p_n)[:, None]
    st_rep = ((small // p_n == col[None, :] // S5_SB) & (small % p_n == col[None, :] % p_n)).astype(BF16)
    row = jnp.arange(rows)
    g_tok_r, g_tok_c = (row % LANES) // c_n, (col % LANES) // c_n
    g_st_r, g_st_c = (row % S5_SB) // p_n, (col % S5_SB) // p_n
    expand = lambda table, rep: jnp.dot(table.astype(BF16), rep, preferred_element_type=BF16)
    zero = jnp.zeros((), BF16)
    toep = jnp.where(g_tok_r[:, None] == g_tok_c[None, :], expand(kt, tok_rep), zero)
    w_state = jnp.where(g_tok_r[:, None] == g_st_c[None, :], expand(ws, st_rep), zero)
    w_out = jnp.where(g_st_r[:, None] == g_tok_c[None, :], expand(wy, tok_rep), zero)
    return toep, w_state, w_out, pw_re[q].reshape(1, S5_NSTATE), pw_im[q].reshape(1, S5_NSTATE)


def _s5_scan_body(ar_ref, ai_ref, s_ref, o_ref, c_ref):
    rows = BF16_ROWS
    nb = s_ref.shape[0]
    a_re, a_im = ar_ref[...], ai_ref[...]
    half = a_re.shape[1]

    @pl.when(pl.program_id(1) == 0)
    def _():
        c_ref[...] = jnp.zeros_like(c_ref)

    def block(i, carry):
        base = pl.multiple_of(i * rows, rows)
        xs = [s_ref[b, pl.ds(base, rows), :] for b in range(nb)]
        outs = [[] for _ in range(nb)]
        carry = list(carry)
        for r in range(rows):
            for b in range(nb):
                s_re, s_im = carry[b]
                outs[b].append(jnp.concatenate([s_re, s_im], axis=1))
                carry[b] = (a_re * s_re - a_im * s_im + xs[b][r:r + 1, :half],
                            a_re * s_im + a_im * s_re + xs[b][r:r + 1, half:])
        for b in range(nb):
            o_ref[b, pl.ds(base, rows), :] = jnp.concatenate(outs[b], axis=0).astype(o_ref.dtype)
        return tuple(carry)

    init = tuple((c_ref[b, :, :half], c_ref[b, :, half:]) for b in range(nb))
    last = lax.fori_loop(0, s_ref.shape[1] // rows, block, init)
    for b in range(nb):
        c_ref[b] = jnp.concatenate(last[b], axis=1)


def _s5_fold_body(h_ref, w_ref, o_ref, tok_ref):
    u = _dot(h_ref[...], w_ref[...])
    tc = o_ref.shape[0]
    for c in range(S5_SG):
        tok_ref[c] = u[:, c * LANES:(c + 1) * LANES]
        for j in range(S5_Q):
            lo = c * S5_SGW + j * LANES
            o_ref[:, lo:lo + LANES] = tok_ref[c, pl.ds(j, tc, stride=S5_Q), :].astype(o_ref.dtype)


def _s5_glu_body(y_ref, w_ref, o_ref, tok_ref):
    tc = y_ref.shape[0]
    for c in range(S5_SG):
        for j in range(S5_Q):
            lo = c * S5_SGW + j * LANES
            tok_ref[c, pl.ds(j, tc, stride=S5_Q), :] = y_ref[:, lo:lo + LANES]
    y = jax.nn.gelu(jnp.concatenate([tok_ref[c] for c in range(S5_SG)], axis=1))
    o_ref[...] = (y * jax.nn.sigmoid(_dot(y.astype(BF16), w_ref[...]))).astype(o_ref.dtype)


def _s5_mixer(h, w_u, mats, w_glu, bsz, seq):
    toep, w_state, w_out, aq_re, aq_im = mats
    m = bsz * seq
    nc = seq // S5_Q
    width = S5_Q * MIX_W
    tm = min(1024, m)
    tc = tm // S5_Q
    uc = pl.pallas_call(
        _s5_fold_body,
        name="s5_input_proj",
        grid=(m // tm,),
        in_specs=[pl.BlockSpec((tm, D_MODEL), lambda i: (i, 0)), pl.BlockSpec((D_MODEL, MIX_W), lambda i: (0, 0))],
        out_specs=pl.BlockSpec((tc, width), lambda i: (i, 0)),
        out_shape=jax.ShapeDtypeStruct((m // S5_Q, width), BF16),
        scratch_shapes=[pltpu.VMEM((MIX_W // LANES, tm, LANES), F32)],
        compiler_params=_params("parallel"),
    )(h, w_u)
    y1s = _matmul_blockdiag(uc, jnp.concatenate([toep, w_state], axis=1), S5_SG)
    rb = min(S5_SCAN_ROWS, nc)
    s_in = pl.pallas_call(
        _s5_scan_body,
        name="s5_chunk_scan",
        grid=(S5_SG, nc // rb),
        in_specs=[
            pl.BlockSpec((1, S5_SB), lambda j, r: (0, j)),
            pl.BlockSpec((1, S5_SB), lambda j, r: (0, j)),
            pl.BlockSpec((bsz, rb, S5_SGW), lambda j, r: (0, r, 2 * j + 1)),
        ],
        out_specs=pl.BlockSpec((bsz, rb, S5_SGW), lambda j, r: (0, r, j)),
        out_shape=jax.ShapeDtypeStruct((bsz, nc, width), BF16),
        scratch_shapes=[pltpu.VMEM((bsz, 1, S5_SGW), F32)],
        compiler_params=_params("parallel", "arbitrary"),
    )(aq_re, aq_im, y1s.reshape(bsz, nc, 2 * width))
    y = _matmul_blockdiag(s_in.reshape(bsz * nc, width), w_out, S5_SG, res=y1s, res_block=lambda s: 2 * s)
    return pl.pallas_call(
        _s5_glu_body,
        name="s5_glu",
        grid=(m // tm,),
        in_specs=[pl.BlockSpec((tc, width), lambda i: (i, 0)), pl.BlockSpec((MIX_W, MIX_W), lambda i: (0, 0))],
        out_specs=pl.BlockSpec((tm, MIX_W), lambda i: (i, 0)),
        out_shape=jax.ShapeDtypeStruct((m, MIX_W), BF16),
        scratch_shapes=[pltpu.VMEM((MIX_W // LANES, tm, LANES), F32)],
        compiler_params=_params("parallel"),
    )(y, w_glu.astype(BF16))


def _lru_body(x_ref, g_ref, cw_ref, cb_ref, wa_ref, ba_ref, wx_ref, bx_ref, lam_ref, o_ref,
              halo_ref, h_ref, a_s, b_s):
    nb, t, _ = x_ref.shape

    @pl.when(pl.program_id(0) == 0)
    def _():
        halo_ref[...] = jnp.zeros_like(halo_ref)
        h_ref[...] = jnp.zeros_like(h_ref)

    soft = jax.nn.softplus(-lam_ref[...])
    for b in range(nb):
        x_in = x_ref[b]
        x = _causal_conv(x_in, halo_ref[b], cw_ref, cb_ref)
        halo_ref[b] = x_in[t - HALO:, :]
        xb = x.astype(BF16)
        r = jax.nn.sigmoid(_dot(xb, wa_ref[...]) + ba_ref[...])
        i = jax.nn.sigmoid(_dot(xb, wx_ref[...]) + bx_ref[...])
        log_a = (-LRU_C * r) * soft
        a = jnp.exp(log_a)
        a_s[b] = a
        b_s[b] = x * i * jnp.sqrt(1.0 - a * a)
    rows = 8

    def block(k, hs):
        base = pl.multiple_of(k * rows, rows)
        av = [a_s[b, pl.ds(base, rows), :] for b in range(nb)]
        bv = [b_s[b, pl.ds(base, rows), :] for b in range(nb)]
        hs = list(hs)
        outs = [[] for _ in range(nb)]
        for rr in range(rows):
            for b in range(nb):
                hs[b] = av[b][rr:rr + 1, :] * hs[b] + bv[b][rr:rr + 1, :]
                outs[b].append(hs[b])
        for b in range(nb):
            b_s[b, pl.ds(base, rows), :] = jnp.concatenate(outs[b], axis=0)
        return tuple(hs)

    hs = lax.fori_loop(0, t // rows, block, tuple(h_ref[b] for b in range(nb)))
    for b in range(nb):
        h_ref[b] = hs[b]
        o_ref[b] = (b_s[b] * jax.nn.gelu(g_ref[b])).astype(o_ref.dtype)


def _block_diag(w):
    h_n, n, _ = w.shape
    eye = jnp.eye(h_n, dtype=w.dtype)
    return (w[:, :, None, :] * eye[:, None, :, None]).reshape(h_n * n, h_n * n)


def _lru_mixer(proj, conv_w, conv_b, w_a, b_a, w_x, b_x, lam, bsz, seq):
    t = min(256, seq)
    vec = lambda v: v.reshape(1, MIX_W)
    const = lambda shape: pl.BlockSpec(shape, lambda c: (0, 0))
    proj3 = proj.reshape(bsz, seq, proj.shape[1])
    out = pl.pallas_call(
        _lru_body,
        name="rglru_mixer",
        grid=(seq // t,),
        in_specs=[
            pl.BlockSpec((bsz, t, MIX_W), lambda c: (0, c, PC_LRU_X // MIX_W)),
            pl.BlockSpec((bsz, t, MIX_W), lambda c: (0, c, PC_LRU_G // MIX_W)),
            const(conv_w.shape), const((1, MIX_W)),
            const((MIX_W, MIX_W)), const((1, MIX_W)),
            const((MIX_W, MIX_W)), const((1, MIX_W)), const((1, MIX_W)),
        ],
        out_specs=pl.BlockSpec((bsz, t, MIX_W), lambda c: (0, c, 0)),
        out_shape=jax.ShapeDtypeStruct((bsz, seq, MIX_W), BF16),
        scratch_shapes=[
            pltpu.VMEM((bsz, HALO, MIX_W), F32), pltpu.VMEM((bsz, 1, MIX_W), F32),
            pltpu.VMEM((bsz, t, MIX_W), F32), pltpu.VMEM((bsz, t, MIX_W), F32),
        ],
        compiler_params=_params("arbitrary"),
    )(proj3, proj3, conv_w, vec(conv_b), _block_diag(w_a).astype(BF16), vec(b_a),
      _block_diag(w_x).astype(BF16), vec(b_x), vec(lam))
    return out.reshape(bsz * seq, MIX_W)


def _m2_body(z_ref, xbc_ref, dt_ref, rep_ref, cw_ref, cb_ref, dtb_ref, alog_ref, d_ref, ng_ref, o_ref,
             halo_ref, st_ref):
    nb, q, _ = z_ref.shape
    hg = N_HEADS // M2_GROUPS
    gw = hg * HEAD_W
    seqs = range(nb)

    @pl.when(pl.program_id(1) == 0)
    def _():
        halo_ref[...] = jnp.zeros_like(halo_ref)
        st_ref[...] = jnp.zeros_like(st_ref)

    causal = _tril_mask(q)
    causal_b = causal.astype(BF16)
    neg_a = -jnp.exp(alog_ref[...])
    dt_raw = _exact_right([dt_ref[b] for b in seqs], rep_ref[...])
    xc = []
    for b in seqs:
        xbc = xbc_ref[b]
        conv = _causal_conv(xbc, halo_ref[b], cw_ref, cb_ref)
        halo_ref[b] = xbc[q - HALO:, :]
        xc.append(conv * jax.nn.sigmoid(conv))
    dt = [jax.nn.softplus(dt_raw[b] + dtb_ref[...]) for b in seqs]
    a_cs = [_exact_left(causal_b, dt[b] * neg_a) for b in seqs]
    xs = [xc[b][:, :MIX_W] for b in seqs]
    xd = [xs[b] * dt[b] for b in seqs]
    a_cs_t = [a_cs[b].T for b in seqs]
    a_last = [a_cs[b][q - 1:q, :] for b in seqs]
    xd_st = [(xd[b] * jnp.exp(a_last[b] - a_cs[b])).astype(BF16) for b in seqs]
    e_cs = [jnp.exp(a_cs[b]) for b in seqs]
    bg, cg, cb, y_off = {}, {}, {}, {}
    for b in seqs:
        for g in range(M2_GROUPS):
            lo = MIX_W + g * M2_STATE
            bg[b, g] = xc[b][:, lo:lo + M2_STATE].astype(BF16)
            cg[b, g] = xc[b][:, lo + M2_GROUPS * M2_STATE:lo + (M2_GROUPS + 1) * M2_STATE].astype(BF16)
            cb[b, g] = _dot_nt(cg[b, g], bg[b, g])
            y_off[b, g] = _dot(cg[b, g], st_ref[b, g].astype(BF16)) * e_cs[b][:, g * gw:(g + 1) * gw]
    ys = [[] for _ in seqs]
    for g in range(M2_GROUPS):
        for hh in range(hg):
            for b in seqs:
                lo = (g * hg + hh) * HEAD_W
                col = a_cs[b][:, lo:lo + HEAD_W]
                col = jnp.concatenate([col] * (q // HEAD_W), axis=1)
                seg = col - a_cs_t[b][lo:lo + 1, :]
                dec = jnp.exp(jnp.where(causal, seg, -jnp.inf))
                y_d = _dot((cb[b, g] * dec).astype(BF16), xd[b][:, lo:lo + HEAD_W].astype(BF16))
                ys[b].append(y_d + y_off[b, g][:, hh * HEAD_W:(hh + 1) * HEAD_W])
    for b in seqs:
        for g in range(M2_GROUPS):
            upd = _dot_tn(bg[b, g], xd_st[b][:, g * gw:(g + 1) * gw])
            st_ref[b, g] = st_ref[b, g] * jnp.exp(a_last[b][:, g * gw:(g + 1) * gw]) + upd
    for b in seqs:
        y = jnp.concatenate(ys[b], axis=1) + d_ref[...] * xs[b]
        z = z_ref[b]
        y = y * (z * jax.nn.sigmoid(z))
        ms = jnp.mean(y * y, axis=-1, keepdims=True)
        o_ref[b] = (y * lax.rsqrt(ms + EPS) * ng_ref[...]).astype(o_ref.dtype)


def _per_head(v):
    return jnp.repeat(v, HEAD_W).reshape(1, -1)


def _m2_mixer(proj, conv_w, conv_b, dt_bias, a_log, d, norm_g, bsz, seq):
    q = min(M2_Q, seq)
    nc = seq // q
    nb = M2_BATCH if bsz % M2_BATCH == 0 else 1
    const = lambda shape: pl.BlockSpec(shape, lambda b, c: (0, 0))
    head_of_lane = jnp.arange(MIX_W)[None, :] // HEAD_W
    rep = (jnp.arange(LANES)[:, None] == head_of_lane).astype(BF16)
    proj3 = proj.reshape(bsz, seq, proj.shape[1])
    out = pl.pallas_call(
        _m2_body,
        name="mamba2_mixer",
        grid=(bsz // nb, nc),
        in_specs=[
            pl.BlockSpec((nb, q, MIX_W), lambda b, c: (b, c, PC_Z // MIX_W)),
            pl.BlockSpec((nb, q, M2_CONV_DIM), lambda b, c: (b, c, PC_XBC // M2_CONV_DIM)),
            pl.BlockSpec((nb, q, LANES), lambda b, c: (b, c, PC_DT // LANES)),
            const((LANES, MIX_W)), const(conv_w.shape), const((1, M2_CONV_DIM)),
            const((1, MIX_W)), const((1, MIX_W)), const((1, MIX_W)), const((1, MIX_W)),
        ],
        out_specs=pl.BlockSpec((nb, q, MIX_W), lambda b, c: (b, c, 0)),
        out_shape=jax.ShapeDtypeStruct((bsz, seq, MIX_W), BF16),
        scratch_shapes=[
            pltpu.VMEM((nb, HALO, M2_CONV_DIM), F32),
            pltpu.VMEM((nb, M2_GROUPS, M2_STATE, (N_HEADS // M2_GROUPS) * HEAD_W), F32),
        ],
        compiler_params=_params("parallel", "arbitrary"),
    )(proj3, proj3, proj3, rep, conv_w, conv_b.reshape(1, -1), _per_head(dt_bias), _per_head(a_log),
      _per_head(d), norm_g.reshape(1, -1))
    return out.reshape(bsz * seq, MIX_W)


def _rw_body(p_ref, pl_ref, mu_ref, mul_ref, w0_ref, w2_ref, a0_ref, a2_ref, g2_ref, kk_ref, ka_ref, rk_ref,
             lng_ref, lnb_ref, ones_ref, o_ref, prev_ref, prevl_ref, st_ref):
    nb, q, _ = p_ref.shape
    rows = nb * q

    @pl.when(pl.program_id(1) == 0)
    def _():
        prev_ref[...] = jnp.zeros_like(prev_ref)
        prevl_ref[...] = jnp.zeros_like(prevl_ref)
        st_ref[...] = jnp.zeros_like(st_ref)

    def token_mix(x_ref, last_ref, m_ref):
        x = x_ref[...].reshape(rows, x_ref.shape[2])
        row = lax.broadcasted_iota(jnp.int32, x.shape, 0)
        shifted = pltpu.roll(x, 1, axis=0)
        for b in range(nb):
            shifted = jnp.where(row == b * q, last_ref[b, 0:1, :], shifted)
            last_ref[b, 0:1, :] = x[(b + 1) * q - 1:(b + 1) * q, :]
        return x + (shifted - x) * m_ref[...]

    p = token_mix(p_ref, prev_ref, mu_ref)
    lora = token_mix(pl_ref, prevl_ref, mul_ref)
    r = p[:, 0:MIX_W]
    k = p[:, MIX_W:2 * MIX_W]
    v = p[:, 2 * MIX_W:3 * MIX_W]
    w = -jax.nn.softplus(-(w0_ref[...] + _dot(jnp.tanh(lora).astype(BF16), w2_ref[...]))) - 0.5
    log_w = -jnp.exp(w)
    a_sig = jax.nn.sigmoid(a0_ref[...] + _dot(lora.astype(BF16), a2_ref[...]))
    gate = _dot(jax.nn.sigmoid(lora).astype(BF16), g2_ref[...])
    ones = ones_ref[...]
    kk = k * kk_ref[...]
    k = k * (1.0 + (a_sig - 1.0) * ka_ref[...])
    kk_sq, rk_sum = _exact_right([kk * kk, r * k * rk_ref[...]], ones)
    kk = kk / jnp.maximum(jnp.sqrt(kk_sq), 1e-12)
    a_vec = -kk
    b_vec = kk * a_sig

    rr = lax.broadcasted_iota(jnp.int32, (rows, rows), 0)
    cc = lax.broadcasted_iota(jnp.int32, (rows, rows), 1)
    cum_mask = jnp.where((rr >= cc) & ((rr // q) == (cc // q)), 1.0, 0.0).astype(BF16)
    gam = _exact_left(cum_mask, log_w)
    g_last = jnp.concatenate(
        [jnp.broadcast_to(gam[(b + 1) * q - 1:(b + 1) * q, :], (q, MIX_W)) for b in range(nb)], axis=0)
    e_neg = jnp.exp(-gam)
    e_last = jnp.exp(g_last)
    e_rem = e_last * e_neg
    a_t = a_vec * jnp.exp(gam - log_w)
    r_t = r * jnp.exp(gam)
    b_t = b_vec * e_neg
    k_t = k * e_neg
    b_h = b_vec * e_rem
    k_h = k * e_rem
    assert q == HEAD_W
    row2 = lax.broadcasted_iota(jnp.int32, (q, 2 * HEAD_W), 0)
    lane2 = lax.broadcasted_iota(jnp.int32, (q, 2 * HEAD_W), 1)
    low = lane2 < HEAD_W
    col2 = jnp.where(low, lane2, lane2 - HEAD_W)
    keep_z0 = low | (col2 < row2)
    keep_t = col2 <= row2
    strict = _tril_mask(q, strict=True)
    pairs = [(b, h) for b in range(nb) for h in range(N_HEADS)]
    zeros_w = jnp.zeros((HEAD_W, 2 * HEAD_W), BF16)
    zeros_x = jnp.zeros((q, HEAD_W), F32)

    def blk(t, key):
        b, h = key
        return t[b * q:(b + 1) * q, h * HEAD_W:(h + 1) * HEAD_W]

    p0, p1, s0, z = {}, {}, {}, {}
    for key in pairs:
        ar = jnp.concatenate([blk(a_t, key), blk(r_t, key)], axis=0).astype(BF16)
        s0[key] = st_ref[key[0], key[1]]
        rhs = jnp.concatenate([s0[key], blk(b_t, key), blk(k_t, key), zeros_x], axis=0).astype(BF16)
        prod = _dot_nt(ar, rhs)
        p0[key] = prod[:, :2 * HEAD_W]
        p1[key] = prod[:, 2 * HEAD_W:]
    for key in pairs:
        a_k = jnp.where(strict, p1[key][:q, :HEAD_W], 0.0).astype(BF16)
        x0 = _dot(a_k, blk(v, key).astype(BF16))
        z[key] = jnp.where(keep_z0, p0[key][:q], 0.0) + jnp.concatenate([x0, zeros_x], axis=1)
    steps = max(1, (q - 1).bit_length())
    for _ in range(steps - 1):
        for key in pairs:
            zb = z[key].astype(BF16)
            z[key] = jnp.where(low, z[key], 0.0) + _dot(zb, jnp.concatenate([zeros_w, zb], axis=0))
    u = {}
    for key in pairs:
        zb = z[key].astype(BF16)
        u[key] = z[key][:, :HEAD_W] + _dot(zb, jnp.concatenate([zeros_w[:, :HEAD_W], zb[:, :HEAD_W]], axis=0))
    ys = {}
    for key in pairs:
        vu = jnp.concatenate([blk(v, key), u[key]], axis=0).astype(BF16)
        t_mat = jnp.where(keep_t, jnp.where(low, p1[key][q:], p0[key][q:]), 0.0)
        ys[key] = p0[key][q:, :HEAD_W] + _dot(t_mat.astype(BF16), vu)
        khb = jnp.concatenate([blk(k_h, key), blk(b_h, key)], axis=0).astype(BF16)
        st_ref[key[0], key[1]] = s0[key] * blk(e_last, key)[0:1, :] + _dot_tn(vu, khb)
    y = jnp.concatenate(
        [jnp.concatenate([ys[(b, h)] for h in range(N_HEADS)], axis=1) for b in range(nb)], axis=0)
    inv_n = 1.0 / HEAD_W
    yc = y - _exact_right([y], ones)[0] * inv_n
    var = _exact_right([yc * yc], ones)[0] * inv_n
    y = yc * lax.rsqrt(var + RW_LN_EPS) * lng_ref[...] + lnb_ref[...]
    bonus = rk_sum * v
    o_ref[...] = ((y + bonus) * gate).astype(o_ref.dtype).reshape(o_ref.shape)


def _pad_rows(w, rows, off):
    return jnp.zeros((rows, w.shape[1]), w.dtype).at[off:off + w.shape[0]].set(w)


def _rw_mixer(proj, mu, w0, w2, a0, a2, g2, k_k, k_a, r_k, ln_g, ln_b, bsz, seq):
    q = min(RW_Q, seq)
    nc = seq // q
    nb = RW_BATCH if bsz % RW_BATCH == 0 else 1
    vec = lambda t: t.reshape(1, MIX_W)
    const = lambda shape: pl.BlockSpec(shape, lambda b, c: (0, 0))
    offs = (0, RW_LORA_RANKS[0], RW_LORA_RANKS[0] + RW_LORA_RANKS[1])
    lora = lambda t, off: _pad_rows(t, RW_LORA_W, off).astype(BF16)
    mu_rkv, mu_lora = mu[:RW_RKV_W].reshape(1, RW_RKV_W), _pad_rows(mu[RW_RKV_W:, None], RW_LORA_W, 0).reshape(1, RW_LORA_W)
    ones = _block_diag(jnp.ones((N_HEADS, HEAD_W, HEAD_W), F32)).astype(BF16)
    proj3 = proj.reshape(bsz, seq, proj.shape[1])
    out = pl.pallas_call(
        _rw_body,
        name="rwkv7_mixer",
        grid=(bsz // nb, nc),
        in_specs=[
            pl.BlockSpec((nb, q, RW_RKV_W), lambda b, c: (b, c, PC_RKV // RW_RKV_W)),
            pl.BlockSpec((nb, q, RW_LORA_W), lambda b, c: (b, c, PC_LORA // RW_LORA_W)),
            const((1, RW_RKV_W)), const((1, RW_LORA_W)), const((1, MIX_W)), const((RW_LORA_W, MIX_W)),
            const((1, MIX_W)), const((RW_LORA_W, MIX_W)), const((RW_LORA_W, MIX_W)),
            const((1, MIX_W)), const((1, MIX_W)), const((1, MIX_W)),
            const((1, MIX_W)), const((1, MIX_W)), const((MIX_W, MIX_W)),
        ],
        out_specs=pl.BlockSpec((nb, q, MIX_W), lambda b, c: (b, c, 0)),
        out_shape=jax.ShapeDtypeStruct((bsz, seq, MIX_W), BF16),
        scratch_shapes=[pltpu.VMEM((nb, HALO, RW_RKV_W), F32), pltpu.VMEM((nb, HALO, RW_LORA_W), F32),
                        pltpu.VMEM((nb, N_HEADS, HEAD_W, HEAD_W), F32)],
        compiler_params=_params("parallel", "arbitrary"),
    )(proj3, proj3, mu_rkv, mu_lora, vec(w0), lora(w2, offs[0]), vec(a0), lora(a2, offs[1]), lora(g2, offs[2]),
      vec(k_k), vec(k_a), vec(r_k), vec(ln_g), vec(ln_b), ones)
    return out.reshape(bsz * seq, MIX_W)


def _merge_body(h_ref, *refs):
    nb = (len(refs) - 2) // 2
    y_refs, wg_refs, wp_ref, o_ref = refs[:nb], refs[nb:2 * nb], refs[2 * nb], refs[2 * nb + 1]
    h = h_ref[...]
    acc = None
    for kk in range(nb):
        term = jax.nn.sigmoid(_dot(h, wg_refs[kk][...])) * _dot(y_refs[kk][...], wp_ref[kk])
        acc = term if acc is None else acc + term
    o_ref[...] = acc.astype(o_ref.dtype)


def _merge(h, ys, w_gate, w_branch, layer=None, tm=512, tn=512):
    m = h.shape[0]
    tm = min(tm, m)
    nb = len(ys)
    y_spec = pl.BlockSpec((tm, MIX_W), lambda j, i: (i, 0))
    gate_specs = [_weight_spec(w_gate, layer, tn, lambda j, i, k=k: k * (D_MODEL // tn) + j) for k in range(nb)]
    if w_branch.ndim == 3:
        branch_spec = pl.BlockSpec((nb, MIX_W, tn), lambda j, i: (0, 0, j))
    else:
        branch_spec = pl.BlockSpec((None, nb, MIX_W, tn), lambda j, i: (layer, 0, 0, j))
    return pl.pallas_call(
        _merge_body,
        name="gated_merge",
        grid=(D_MODEL // tn, m // tm),
        in_specs=[pl.BlockSpec((tm, D_MODEL), lambda j, i: (i, 0))] + [y_spec] * nb + gate_specs + [branch_spec],
        out_specs=pl.BlockSpec((tm, tn), lambda j, i: (i, j)),
        out_shape=jax.ShapeDtypeStruct((m, D_MODEL), BF16),
        compiler_params=_params("parallel", "parallel"),
    )(h, *ys, *([w_gate] * nb), w_branch)


def _ffn_body(h_ref, hh_ref, wg_ref, wu_ref, cw_ref, cb_ref, o_ref, wg_s, wu_s, *, tiles_per_seq):
    t = h_ref.shape[0]
    nh = hh_ref.shape[0]

    @pl.when(pl.program_id(1) == 0)
    def _():
        wg_s[...] = wg_ref[...].astype(BF16)
        wu_s[...] = wu_ref[...].astype(BF16)

    h = h_ref[...]
    seq_start = (pl.program_id(1) % tiles_per_seq) == 0
    halo = jnp.where(seq_start, jnp.zeros_like(hh_ref[...]), hh_ref[...])
    g_all = _dot(jnp.concatenate([halo, h], axis=0), wg_s[...])
    k_w = cw_ref.shape[0]
    u = cb_ref[...] + g_all[nh:, :] * cw_ref[k_w - 1:k_w, :]
    for j in range(k_w - 1):
        lag = k_w - 1 - j
        u = u + g_all[nh - lag:nh - lag + t, :] * cw_ref[j:j + 1, :]
    o_ref[...] = (jax.nn.gelu(u) * _dot(h, wu_s[...])).astype(o_ref.dtype)


def _ffn_act(h, w_gate, w_up, conv_w, conv_b, seq, layer=None, tm=1024, tf=512):
    m = h.shape[0]
    tm = min(tm, seq)
    f = w_gate.shape[-1]
    halo_blocks = tm // BF16_ROWS
    return pl.pallas_call(
        functools.partial(_ffn_body, tiles_per_seq=seq // tm),
        name="ffn_gate_up",
        grid=(f // tf, m // tm),
        in_specs=[
            pl.BlockSpec((tm, D_MODEL), lambda j, i: (i, 0)),
            pl.BlockSpec((BF16_ROWS, D_MODEL), lambda j, i: (jnp.maximum(i * halo_blocks - 1, 0), 0)),
            _weight_spec(w_gate, layer, tf, lambda j, i: j),
            _weight_spec(w_up, layer, tf, lambda j, i: j),
            pl.BlockSpec((conv_w.shape[0], tf), lambda j, i: (0, j)),
            pl.BlockSpec((1, tf), lambda j, i: (0, j)),
        ],
        out_specs=pl.BlockSpec((tm, tf), lambda j, i: (i, j)),
        out_shape=jax.ShapeDtypeStruct((m, f), BF16),
        scratch_shapes=[pltpu.VMEM((D_MODEL, tf), BF16), pltpu.VMEM((D_MODEL, tf), BF16)],
        compiler_params=_params("parallel", "arbitrary"),
    )(h, h, w_gate, w_up, conv_w, conv_b.reshape(1, f))


def _mixer_weight(w_in_l):
    s5_u, lru_x, lru_g, m2_z, m2_xbc, m2_dt, rw_p = jnp.split(
        w_in_l, [512, 1024, 1536, 2048, 2048 + M2_CONV_DIM, 2048 + M2_CONV_DIM + N_HEADS], axis=1)
    d = w_in_l.shape[0]
    padc = lambda t, n: jnp.concatenate([t, jnp.zeros((d, n - t.shape[1]), t.dtype)], axis=1)
    proj_w = jnp.concatenate([m2_xbc, m2_z, rw_p[:, :RW_RKV_W], lru_x, lru_g,
                              padc(rw_p[:, RW_RKV_W:], RW_LORA_W), padc(m2_dt, PROJ_W - PC_DT)], axis=1)
    assert proj_w.shape[1] == PROJ_W
    return s5_u.astype(BF16), proj_w.astype(BF16)


def kernel(x, norm_mix_g, w_in, s5_lambda_re, s5_lambda_im, s5_b_re, s5_b_im, s5_c_re, s5_c_im, s5_d, s5_log_dt, s5_w_glu, lru_conv_w, lru_conv_b, lru_w_a, lru_b_a, lru_w_x, lru_b_x, lru_lambda, m2_conv_w, m2_conv_b, m2_dt_bias, m2_a_log, m2_d, m2_norm_g, rw_mu, rw_w0, rw_w2, rw_a0, rw_a2, rw_g2, rw_k_k, rw_k_a, rw_r_k, rw_ln_g, rw_ln_b, w_branch, w_out, norm_ffn_g, w_ffn_gate, w_ffn_up, ffn_conv_w, ffn_conv_b, w_ffn_down, final_norm_g):
    bsz, seq, d = x.shape
    depth = w_in.shape[0]
    mixer_cols = w_in.shape[2] - w_branch.shape[1] * d
    xf = x.reshape(bsz * seq, d)
    wb_all, wo_all, wd_all = w_branch.astype(BF16), w_out.astype(BF16), w_ffn_down.astype(BF16)
    for l in range(depth):
        h = _rms_norm(xf, norm_mix_g[l], BF16)
        w_s5, w_proj = _mixer_weight(w_in[l, :, :mixer_cols])
        proj = _matmul(h, w_proj)
        mats = _s5_matrices(s5_lambda_re[l], s5_lambda_im[l], s5_b_re[l], s5_b_im[l],
                            s5_c_re[l], s5_c_im[l], s5_d[l], s5_log_dt[l])
        y_a = _s5_mixer(h, w_s5, mats, s5_w_glu[l], bsz, seq)
        y_b = _lru_mixer(proj, lru_conv_w[l], lru_conv_b[l], lru_w_a[l], lru_b_a[l],
                         lru_w_x[l], lru_b_x[l], lru_lambda[l], bsz, seq)
        y_c = _m2_mixer(proj, m2_conv_w[l], m2_conv_b[l], m2_dt_bias[l], m2_a_log[l],
                        m2_d[l], m2_norm_g[l], bsz, seq)
        y_d = _rw_mixer(proj, rw_mu[l], rw_w0[l], rw_w2[l], rw_a0[l], rw_a2[l], rw_g2[l],
                        rw_k_k[l], rw_k_a[l], rw_r_k[l].reshape(-1), rw_ln_g[l], rw_ln_b[l], bsz, seq)
        w_gate = w_in[l, :, mixer_cols:].astype(BF16)
        merged = _merge(h, (y_a, y_b, y_c, y_d), w_gate, wb_all, layer=l)
        xf, h = _matmul_res_norm(merged, wo_all, xf, norm_ffn_g[l], l)
        act = _ffn_act(h, w_ffn_gate, w_ffn_up, ffn_conv_w[l], ffn_conv_b[l], seq, layer=l)
        xf = _matmul(act, wd_all, res=xf, tm=512, n_outer=True, layer=l)
    return _rms_norm(xf, final_norm_g, F32).reshape(bsz, seq, d)
```

```python
import functools

import jax
import jax.numpy as jnp
from jax import lax
from jax.experimental import pallas as pl
from jax.experimental.pallas import tpu as pltpu

F32 = jnp.float32
BF16 = jnp.bfloat16

D_MODEL = 2048
MIX_W = 512
HEAD_W = 64
N_HEADS = MIX_W // HEAD_W
S5_GROUP = 16
S5_GROUPS = MIX_W // S5_GROUP
S5_STATE = 64
S5_Q = 8
S5_NSTATE = S5_GROUPS * S5_STATE
LANES = 128
S5_SG = MIX_W // LANES
S5_SGW = S5_Q * LANES
S5_SB = S5_NSTATE // S5_SG
S5_SCAN_ROWS = 512
LRU_HEADS = 8
LRU_C = 8.0
M2_GROUPS = 2
M2_STATE = 128
M2_CONV_DIM = MIX_W + 2 * M2_GROUPS * M2_STATE
M2_Q = 128
M2_BATCH = 2
RW_Q = 64
RW_BATCH = 4
RW_RKV_W = 3 * MIX_W
RW_LORA_RANKS = (32, 32, 96)
RW_LORA_W = 256
RW_LN_EPS = 64e-5
FFN_DIM = 3 * D_MODEL
EPS = 1e-6
HALO = 8
BF16_ROWS = 16

PC_XBC = 0
PC_Z = 1024
PC_RKV = 1536
PC_LRU_X = 3072
PC_LRU_G = 3584
PC_LORA = 4096
PC_DT = 4352
PROJ_W = 4608

VMEM_LIMIT_BYTES = 50 * 1024 * 1024


def _params(*sem):
    return pltpu.CompilerParams(dimension_semantics=sem, vmem_limit_bytes=VMEM_LIMIT_BYTES)


def _dot(a, b):
    return jnp.dot(a, b, preferred_element_type=F32)


def _dot_nt(a, b):
    return lax.dot_general(a, b, (((1,), (1,)), ((), ())), preferred_element_type=F32)


def _dot_tn(a, b):
    return lax.dot_general(a, b, (((0,), (0,)), ((), ())), preferred_element_type=F32)


def _split(x, terms):
    out = []
    for _ in range(terms - 1):
        hi = x.astype(BF16)
        out.append(hi)
        x = x - hi.astype(F32)
    out.append(x.astype(BF16))
    return out


def _exact_left(m, x, terms=3):
    return sum(_dot(m, p) for p in _split(x, terms))


def _exact_right(xs, m, terms=2):
    n = xs[0].shape[0]
    stacked = jnp.concatenate([p for x in xs for p in _split(x, terms)], axis=0)
    out = _dot(stacked, m)
    return [sum(out[(i * terms + t) * n:(i * terms + t + 1) * n] for t in range(terms))
            for i in range(len(xs))]


def _tril_mask(n, strict=False):
    r = lax.broadcasted_iota(jnp.int32, (n, n), 0)
    c = lax.broadcasted_iota(jnp.int32, (n, n), 1)
    return (r > c) if strict else (r >= c)


def _causal_conv(x, halo, w_ref, b_ref):
    k_w = w_ref.shape[0]
    t = x.shape[0]
    xe = jnp.concatenate([halo, x], axis=0)
    out = b_ref[...] + x * w_ref[k_w - 1:k_w, :]
    for j in range(k_w - 1):
        lag = k_w - 1 - j
        out = out + xe[HALO - lag:HALO - lag + t, :] * w_ref[j:j + 1, :]
    return out


def _mm_body(a_ref, b_ref, o_ref):
    o_ref[...] = _dot(a_ref[...], b_ref[...]).astype(o_ref.dtype)


def _mm_res_body(a_ref, b_ref, r_ref, o_ref):
    o_ref[...] = (_dot(a_ref[...], b_ref[...]) + r_ref[...]).astype(o_ref.dtype)


def _weight_spec(w, layer, tn, col_of):
    k = w.shape[-2]
    if w.ndim == 2:
        return pl.BlockSpec((k, tn), lambda *g: (0, col_of(*g)))
    return pl.BlockSpec((None, k, tn), lambda *g: (layer, 0, col_of(*g)))


def _matmul(a, b, res=None, out_dtype=F32, tm=1024, tn=512, n_outer=False, layer=None):
    m, k = a.shape
    n = b.shape[-1]
    tm, tn = min(tm, m), min(tn, n)
    assert m % tm == 0 and n % tn == 0
    if n_outer:
        grid = (n // tn, m // tm)
        row = lambda j, i: (i, 0)
        col = _weight_spec(b, layer, tn, lambda j, i: j)
        out = lambda j, i: (i, j)
    else:
        grid = (m // tm, n // tn)
        row = lambda i, j: (i, 0)
        col = _weight_spec(b, layer, tn, lambda i, j: j)
        out = lambda i, j: (i, j)
    in_specs = [pl.BlockSpec((tm, k), row), col]
    args = [a, b]
    body = _mm_body
    if res is not None:
        in_specs.append(pl.BlockSpec((tm, tn), out))
        args.append(res)
        body = _mm_res_body
    return pl.pallas_call(
        body,
        name="matmul",
        grid=grid,
        in_specs=in_specs,
        out_specs=pl.BlockSpec((tm, tn), out),
        out_shape=jax.ShapeDtypeStruct((m, n), out_dtype),
        compiler_params=_params("parallel", "parallel"),
    )(*args)


def _matmul_blockdiag(a, w, groups, res=None, res_block=None, tm=1024):
    m = a.shape[0]
    k, n = a.shape[1] // groups, w.shape[1]
    tm = min(tm, m)
    out_spec = pl.BlockSpec((tm, n), lambda i, s: (i, s))
    in_specs = [pl.BlockSpec((tm, k), lambda i, s: (i, s)), pl.BlockSpec((k, n), lambda i, s: (s, 0))]
    args = [a, w]
    if res is not None:
        in_specs.append(pl.BlockSpec((tm, n), lambda i, s: (i, res_block(s))))
        args.append(res)
    return pl.pallas_call(
        _mm_body if res is None else _mm_res_body,
        name="matmul_blockdiag",
        grid=(m // tm, groups),
        in_specs=in_specs,
        out_specs=out_spec,
        out_shape=jax.ShapeDtypeStruct((m, groups * n), F32),
        compiler_params=_params("parallel", "parallel"),
    )(*args)


def _norm_body(x_ref, g_ref, o_ref):
    x = x_ref[...]
    ms = jnp.mean(x * x, axis=-1, keepdims=True)
    o_ref[...] = (x * lax.rsqrt(ms + EPS) * g_ref[...]).astype(o_ref.dtype)


def _mm_res_norm_body(a_ref, w_ref, r_ref, g_ref, x_ref, h_ref):
    x = _dot(a_ref[...], w_ref[...]) + r_ref[...]
    x_ref[...] = x
    ms = jnp.mean(x * x, axis=-1, keepdims=True)
    h_ref[...] = (x * lax.rsqrt(ms + EPS) * g_ref[...]).astype(h_ref.dtype)


def _matmul_res_norm(a, w, res, g, layer, tm=512):
    m, k = a.shape
    n = w.shape[-1]
    tm = min(tm, m)
    rows = lambda width: pl.BlockSpec((tm, width), lambda i: (i, 0))
    return pl.pallas_call(
        _mm_res_norm_body,
        name="matmul_residual_norm",
        grid=(m // tm,),
        in_specs=[rows(k), _weight_spec(w, layer, n, lambda i: 0), rows(n), pl.BlockSpec((1, n), lambda i: (0, 0))],
        out_specs=[rows(n), rows(n)],
        out_shape=[jax.ShapeDtypeStruct((m, n), F32), jax.ShapeDtypeStruct((m, n), BF16)],
        compiler_params=_params("parallel"),
    )(a, w, res, g.reshape(1, n))


def _rms_norm(x, g, out_dtype, tm=512):
    m, d = x.shape
    tm = min(tm, m)
    return pl.pallas_call(
        _norm_body,
        name="rms_norm",
        grid=(m // tm,),
        in_specs=[pl.BlockSpec((tm, d), lambda i: (i, 0)), pl.BlockSpec((1, d), lambda i: (0, 0))],
        out_specs=pl.BlockSpec((tm, d), lambda i: (i, 0)),
        out_shape=jax.ShapeDtypeStruct((m, d), out_dtype),
        compiler_params=_params("parallel"),
    )(x, g.reshape(1, d))


def _s5_matrices(lam_re, lam_im, b_re, b_im, c_re, c_im, d, log_dt):
    g_n, p_n, c_n, q = S5_GROUPS, S5_STATE, S5_GROUP, S5_Q
    hp = lax.Precision.HIGHEST
    dt = jnp.exp(log_dt)[:, None]
    n = jnp.arange(q + 1, dtype=F32)[:, None, None]
    mag = jnp.exp(n * (lam_re * dt))
    pw_re = mag * jnp.cos(n * (lam_im * dt))
    pw_im = mag * jnp.sin(n * (lam_im * dt))
    den = lam_re * lam_re + lam_im * lam_im
    nr, ni = pw_re[1] - 1.0, pw_im[1]
    f_re = (nr * lam_re + ni * lam_im) / den
    f_im = (ni * lam_re - nr * lam_im) / den
    e_re = f_re[..., None] * b_re - f_im[..., None] * b_im
    e_im = f_re[..., None] * b_im + f_im[..., None] * b_re
    cp_re = c_re[None] * pw_re[:, :, None, :] - c_im[None] * pw_im[:, :, None, :]
    cp_im = c_re[None] * pw_im[:, :, None, :] + c_im[None] * pw_re[:, :, None, :]
    kern = (jnp.einsum("tgop,gpi->tgio", cp_re[:q], e_re, precision=hp)
            - jnp.einsum("tgop,gpi->tgio", cp_im[:q], e_im, precision=hp))
    kern = kern.at[0].add(d.reshape(g_n, c_n)[:, :, None] * jnp.eye(c_n, dtype=F32))
    sg_n, gl_n = S5_SG, g_n // S5_SG
    rows = sg_n * S5_SGW
    lag = jnp.arange(q)[None, :] - jnp.arange(q)[:, None]
    kt = jnp.where((lag >= 0)[:, :, None, None, None], kern[jnp.clip(lag, 0, q - 1)], 0.0)
    kt = kt.reshape(q, q, sg_n, gl_n, c_n, c_n).transpose(2, 0, 3, 4, 1, 5)
    kt = kt.reshape(rows, q * c_n)
    rev_re, rev_im = pw_re[q - 1 - jnp.arange(q)], pw_im[q - 1 - jnp.arange(q)]
    et_re, et_im = e_re.transpose(0, 2, 1)[None], e_im.transpose(0, 2, 1)[None]
    ws_re = rev_re[:, :, None, :] * et_re - rev_im[:, :, None, :] * et_im
    ws_im = rev_re[:, :, None, :] * et_im + rev_im[:, :, None, :] * et_re
    ws = jnp.stack([ws_re, ws_im], axis=3)
    ws = ws.reshape(q, sg_n, gl_n, c_n, 2, p_n).transpose(1, 0, 2, 3, 4, 5).reshape(rows, 2 * p_n)
    wy = jnp.stack([cp_re[1:], -cp_im[1:]], axis=0)
    wy = wy.reshape(2, q, sg_n, gl_n, c_n, p_n).transpose(2, 0, 3, 5, 1, 4)
    wy = wy.reshape(rows, q * c_n)

    col = jnp.arange(S5_SGW)
    small = jnp.arange(q * c_n)[:, None]
    tok_rep = ((small // c_n == col[None, :] // LANES) & (small % c_n == col[None, :] % c_n)).astype(BF16)
    small = jnp.arange(2 * p_n)[:, None]
    st_rep = ((small // p_n == col[None, :] // S5_SB) & (small % p_n == col[None, :] % p_n)).astype(BF16)
    row = jnp.arange(rows)
    g_tok_r, g_tok_c = (row % LANES) // c_n, (col % LANES) // c_n
    g_st_r, g_st_c = (row % S5_SB) // p_n, (col % S5_SB) // p_n
    expand = lambda table, rep: jnp.dot(table.astype(BF16), rep, preferred_element_type=BF16)
    zero = jnp.zeros((), BF16)
    toep = jnp.where(g_tok_r[:, None] == g_tok_c[None, :], expand(kt, tok_rep), zero)
    w_state = jnp.where(g_tok_r[:, None] == g_st_c[None, :], expand(ws, st_rep), zero)
    w_out = jnp.where(g_st_r[:, None] == g_tok_c[None, :], expand(wy, tok_rep), zero)
    return toep, w_state, w_out, pw_re[q].reshape(1, S5_NSTATE), pw_im[q].reshape(1, S5_NSTATE)


def _s5_scan_body(ar_ref, ai_ref, s_ref, o_ref, c_ref):
    rows = BF16_ROWS
    nb = s_ref.shape[0]
    a_re, a_im = ar_ref[...], ai_ref[...]
    half = a_re.shape[1]

    @pl.when(pl.program_id(1) == 0)
    def _():
        c_ref[...] = jnp.zeros_like(c_ref)

    def block(i, carry):
        base = pl.multiple_of(i * rows, rows)
        xs = [s_ref[b, pl.ds(base, rows), :] for b in range(nb)]
        outs = [[] for _ in range(nb)]
        carry = list(carry)
        for r in range(rows):
            for b in range(nb):
                s_re, s_im = carry[b]
                outs[b].append(jnp.concatenate([s_re, s_im], axis=1))
                carry[b] = (a_re * s_re - a_im * s_im + xs[b][r:r + 1, :half],
                            a_re * s_im + a_im * s_re + xs[b][r:r + 1, half:])
        for b in range(nb):
            o_ref[b, pl.ds(base, rows), :] = jnp.concatenate(outs[b], axis=0).astype(o_ref.dtype)
        return tuple(carry)

    init = tuple((c_ref[b, :, :half], c_ref[b, :, half:]) for b in range(nb))
    last = lax.fori_loop(0, s_ref.shape[1] // rows, block, init)
    for b in range(nb):
        c_ref[b] = jnp.concatenate(last[b], axis=1)


def _s5_fold_body(h_ref, w_ref, o_ref, tok_ref):
    u = _dot(h_ref[...], w_ref[...])
    tc = o_ref.shape[0]
    for c in range(S5_SG):
        tok_ref[c] = u[:, c * LANES:(c + 1) * LANES]
        for j in range(S5_Q):
            lo = c * S5_SGW + j * LANES
            o_ref[:, lo:lo + LANES] = tok_ref[c, pl.ds(j, tc, stride=S5_Q), :].astype(o_ref.dtype)


def _s5_glu_body(y_ref, w_ref, o_ref, tok_ref):
    tc = y_ref.shape[0]
    for c in range(S5_SG):
        for j in range(S5_Q):
            lo = c * S5_SGW + j * LANES
            tok_ref[c, pl.ds(j, tc, stride=S5_Q), :] = y_ref[:, lo:lo + LANES]
    y = jax.nn.gelu(jnp.concatenate([tok_ref[c] for c in range(S5_SG)], axis=1))
    o_ref[...] = (y * jax.nn.sigmoid(_dot(y.astype(BF16), w_ref[...]))).astype(o_ref.dtype)


def _s5_mixer(h, w_u, mats, w_glu, bsz, seq):
    toep, w_state, w_out, aq_re, aq_im = mats
    m = bsz * seq
    nc = seq // S5_Q
    width = S5_Q * MIX_W
    tm = min(1024, m)
    tc = tm // S5_Q
    uc = pl.pallas_call(
        _s5_fold_body,
        name="s5_input_proj",
        grid=(m // tm,),
        in_specs=[pl.BlockSpec((tm, D_MODEL), lambda i: (i, 0)), pl.BlockSpec((D_MODEL, MIX_W), lambda i: (0, 0))],
        out_specs=pl.BlockSpec((tc, width), lambda i: (i, 0)),
        out_shape=jax.ShapeDtypeStruct((m // S5_Q, width), BF16),
        scratch_shapes=[pltpu.VMEM((MIX_W // LANES, tm, LANES), F32)],
        compiler_params=_params("parallel"),
    )(h, w_u)
    y1s = _matmul_blockdiag(uc, jnp.concatenate([toep, w_state], axis=1), S5_SG)
    rb = min(S5_SCAN_ROWS, nc)
    s_in = pl.pallas_call(
        _s5_scan_body,
        name="s5_chunk_scan",
        grid=(S5_SG, nc // rb),
        in_specs=[
            pl.BlockSpec((1, S5_SB), lambda j, r: (0, j)),
            pl.BlockSpec((1, S5_SB), lambda j, r: (0, j)),
            pl.BlockSpec((bsz, rb, S5_SGW), lambda j, r: (0, r, 2 * j + 1)),
        ],
        out_specs=pl.BlockSpec((bsz, rb, S5_SGW), lambda j, r: (0, r, j)),
        out_shape=jax.ShapeDtypeStruct((bsz, nc, width), BF16),
        scratch_shapes=[pltpu.VMEM((bsz, 1, S5_SGW), F32)],
        compiler_params=_params("parallel", "arbitrary"),
    )(aq_re, aq_im, y1s.reshape(bsz, nc, 2 * width))
    y = _matmul_blockdiag(s_in.reshape(bsz * nc, width), w_out, S5_SG, res=y1s, res_block=lambda s: 2 * s)
    return pl.pallas_call(
        _s5_glu_body,
        name="s5_glu",
        grid=(m // tm,),
        in_specs=[pl.BlockSpec((tc, width), lambda i: (i, 0)), pl.BlockSpec((MIX_W, MIX_W), lambda i: (0, 0))],
        out_specs=pl.BlockSpec((tm, MIX_W), lambda i: (i, 0)),
        out_shape=jax.ShapeDtypeStruct((m, MIX_W), BF16),
        scratch_shapes=[pltpu.VMEM((MIX_W // LANES, tm, LANES), F32)],
        compiler_params=_params("parallel"),
    )(y, w_glu.astype(BF16))


def _lru_body(x_ref, g_ref, cw_ref, cb_ref, wa_ref, ba_ref, wx_ref, bx_ref, lam_ref, o_ref,
              halo_ref, h_ref, a_s, b_s):
    nb, t, _ = x_ref.shape

    @pl.when(pl.program_id(0) == 0)
    def _():
        halo_ref[...] = jnp.zeros_like(halo_ref)
        h_ref[...] = jnp.zeros_like(h_ref)

    soft = jax.nn.softplus(-lam_ref[...])
    for b in range(nb):
        x_in = x_ref[b]
        x = _causal_conv(x_in, halo_ref[b], cw_ref, cb_ref)
        halo_ref[b] = x_in[t - HALO:, :]
        xb = x.astype(BF16)
        r = jax.nn.sigmoid(_dot(xb, wa_ref[...]) + ba_ref[...])
        i = jax.nn.sigmoid(_dot(xb, wx_ref[...]) + bx_ref[...])
        log_a = (-LRU_C * r) * soft
        a = jnp.exp(log_a)
        a_s[b] = a
        b_s[b] = x * i * jnp.sqrt(1.0 - a * a)
    rows = 8

    def block(k, hs):
        base = pl.multiple_of(k * rows, rows)
        av = [a_s[b, pl.ds(base, rows), :] for b in range(nb)]
        bv = [b_s[b, pl.ds(base, rows), :] for b in range(nb)]
        hs = list(hs)
        outs = [[] for _ in range(nb)]
        for rr in range(rows):
            for b in range(nb):
                hs[b] = av[b][rr:rr + 1, :] * hs[b] + bv[b][rr:rr + 1, :]
                outs[b].append(hs[b])
        for b in range(nb):
            b_s[b, pl.ds(base, rows), :] = jnp.concatenate(outs[b], axis=0)
        return tuple(hs)

    hs = lax.fori_loop(0, t // rows, block, tuple(h_ref[b] for b in range(nb)))
    for b in range(nb):
        h_ref[b] = hs[b]
        o_ref[b] = (b_s[b] * jax.nn.gelu(g_ref[b])).astype(o_ref.dtype)


def _block_diag(w):
    h_n, n, _ = w.shape
    eye = jnp.eye(h_n, dtype=w.dtype)
    return (w[:, :, None, :] * eye[:, None, :, None]).reshape(h_n * n, h_n * n)


def _lru_mixer(proj, conv_w, conv_b, w_a, b_a, w_x, b_x, lam, bsz, seq):
    t = min(256, seq)
    vec = lambda v: v.reshape(1, MIX_W)
    const = lambda shape: pl.BlockSpec(shape, lambda c: (0, 0))
    proj3 = proj.reshape(bsz, seq, proj.shape[1])
    out = pl.pallas_call(
        _lru_body,
        name="rglru_mixer",
        grid=(seq // t,),
        in_specs=[
            pl.BlockSpec((bsz, t, MIX_W), lambda c: (0, c, PC_LRU_X // MIX_W)),
            pl.BlockSpec((bsz, t, MIX_W), lambda c: (0, c, PC_LRU_G // MIX_W)),
            const(conv_w.shape), const((1, MIX_W)),
            const((MIX_W, MIX_W)), const((1, MIX_W)),
            const((MIX_W, MIX_W)), const((1, MIX_W)), const((1, MIX_W)),
        ],
        out_specs=pl.BlockSpec((bsz, t, MIX_W), lambda c: (0, c, 0)),
        out_shape=jax.ShapeDtypeStruct((bsz, seq, MIX_W), BF16),
        scratch_shapes=[
            pltpu.VMEM((bsz, HALO, MIX_W), F32), pltpu.VMEM((bsz, 1, MIX_W), F32),
            pltpu.VMEM((bsz, t, MIX_W), F32), pltpu.VMEM((bsz, t, MIX_W), F32),
        ],
        compiler_params=_params("arbitrary"),
    )(proj3, proj3, conv_w, vec(conv_b), _block_diag(w_a).astype(BF16), vec(b_a),
      _block_diag(w_x).astype(BF16), vec(b_x), vec(lam))
    return out.reshape(bsz * seq, MIX_W)


def _m2_body(z_ref, xbc_ref, dt_ref, rep_ref, cw_ref, cb_ref, dtb_ref, alog_ref, d_ref, ng_ref, o_ref,
             halo_ref, st_ref):
    nb, q, _ = z_ref.shape
    hg = N_HEADS // M2_GROUPS
    gw = hg * HEAD_W
    seqs = range(nb)

    @pl.when(pl.program_id(1) == 0)
    def _():
        halo_ref[...] = jnp.zeros_like(halo_ref)
        st_ref[...] = jnp.zeros_like(st_ref)

    causal = _tril_mask(q)
    causal_b = causal.astype(BF16)
    neg_a = -jnp.exp(alog_ref[...])
    dt_raw = _exact_right([dt_ref[b] for b in seqs], rep_ref[...])
    xc = []
    for b in seqs:
        xbc = xbc_ref[b]
        conv = _causal_conv(xbc, halo_ref[b], cw_ref, cb_ref)
        halo_ref[b] = xbc[q - HALO:, :]
        xc.append(conv * jax.nn.sigmoid(conv))
    dt = [jax.nn.softplus(dt_raw[b] + dtb_ref[...]) for b in seqs]
    a_cs = [_exact_left(causal_b, dt[b] * neg_a) for b in seqs]
    xs = [xc[b][:, :MIX_W] for b in seqs]
    xd = [xs[b] * dt[b] for b in seqs]
    a_cs_t = [a_cs[b].T for b in seqs]
    a_last = [a_cs[b][q - 1:q, :] for b in seqs]
    xd_st = [(xd[b] * jnp.exp(a_last[b] - a_cs[b])).astype(BF16) for b in seqs]
    e_cs = [jnp.exp(a_cs[b]) for b in seqs]
    bg, cg, cb, y_off = {}, {}, {}, {}
    for b in seqs:
        for g in range(M2_GROUPS):
            lo = MIX_W + g * M2_STATE
            bg[b, g] = xc[b][:, lo:lo + M2_STATE].astype(BF16)
            cg[b, g] = xc[b][:, lo + M2_GROUPS * M2_STATE:lo + (M2_GROUPS + 1) * M2_STATE].astype(BF16)
            cb[b, g] = _dot_nt(cg[b, g], bg[b, g])
            y_off[b, g] = _dot(cg[b, g], st_ref[b, g].astype(BF16)) * e_cs[b][:, g * gw:(g + 1) * gw]
    ys = [[] for _ in seqs]
    for g in range(M2_GROUPS):
        for hh in range(hg):
            for b in seqs:
                lo = (g * hg + hh) * HEAD_W
                col = a_cs[b][:, lo:lo + HEAD_W]
                col = jnp.concatenate([col] * (q // HEAD_W), axis=1)
                seg = col - a_cs_t[b][lo:lo + 1, :]
                dec = jnp.exp(jnp.where(causal, seg, -jnp.inf))
                y_d = _dot((cb[b, g] * dec).astype(BF16), xd[b][:, lo:lo + HEAD_W].astype(BF16))
                ys[b].append(y_d + y_off[b, g][:, hh * HEAD_W:(hh + 1) * HEAD_W])
    for b in seqs:
        for g in range(M2_GROUPS):
            upd = _dot_tn(bg[b, g], xd_st[b][:, g * gw:(g + 1) * gw])
            st_ref[b, g] = st_ref[b, g] * jnp.exp(a_last[b][:, g * gw:(g + 1) * gw]) + upd
    for b in seqs:
        y = jnp.concatenate(ys[b], axis=1) + d_ref[...] * xs[b]
        z = z_ref[b]
        y = y * (z * jax.nn.sigmoid(z))
        ms = jnp.mean(y * y, axis=-1, keepdims=True)
        o_ref[b] = (y * lax.rsqrt(ms + EPS) * ng_ref[...]).astype(o_ref.dtype)


def _per_head(v):
    return jnp.repeat(v, HEAD_W).reshape(1, -1)


def _m2_mixer(proj, conv_w, conv_b, dt_bias, a_log, d, norm_g, bsz, seq):
    q = min(M2_Q, seq)
    nc = seq // q
    nb = M2_BATCH if bsz % M2_BATCH == 0 else 1
    const = lambda shape: pl.BlockSpec(shape, lambda b, c: (0, 0))
    head_of_lane = jnp.arange(MIX_W)[None, :] // HEAD_W
    rep = (jnp.arange(LANES)[:, None] == head_of_lane).astype(BF16)
    proj3 = proj.reshape(bsz, seq, proj.shape[1])
    out = pl.pallas_call(
        _m2_body,
        name="mamba2_mixer",
        grid=(bsz // nb, nc),
        in_specs=[
            pl.BlockSpec((nb, q, MIX_W), lambda b, c: (b, c, PC_Z // MIX_W)),
            pl.BlockSpec((nb, q, M2_CONV_DIM), lambda b, c: (b, c, PC_XBC // M2_CONV_DIM)),
            pl.BlockSpec((nb, q, LANES), lambda b, c: (b, c, PC_DT // LANES)),
            const((LANES, MIX_W)), const(conv_w.shape), const((1, M2_CONV_DIM)),
            const((1, MIX_W)), const((1, MIX_W)), const((1, MIX_W)), const((1, MIX_W)),
        ],
        out_specs=pl.BlockSpec((nb, q, MIX_W), lambda b, c: (b, c, 0)),
        out_shape=jax.ShapeDtypeStruct((bsz, seq, MIX_W), BF16),
        scratch_shapes=[
            pltpu.VMEM((nb, HALO, M2_CONV_DIM), F32),
            pltpu.VMEM((nb, M2_GROUPS, M2_STATE, (N_HEADS // M2_GROUPS) * HEAD_W), F32),
        ],
        compiler_params=_params("parallel", "arbitrary"),
    )(proj3, proj3, proj3, rep, conv_w, conv_b.reshape(1, -1), _per_head(dt_bias), _per_head(a_log),
      _per_head(d), norm_g.reshape(1, -1))
    return out.reshape(bsz * seq, MIX_W)


def _rw_body(p_ref, pl_ref, mu_ref, mul_ref, w0_ref, w2_ref, a0_ref, a2_ref, g2_ref, kk_ref, ka_ref, rk_ref,
             lng_ref, lnb_ref, ones_ref, o_ref, prev_ref, prevl_ref, st_ref):
    nb, q, _ = p_ref.shape
    rows = nb * q

    @pl.when(pl.program_id(1) == 0)
    def _():
        prev_ref[...] = jnp.zeros_like(prev_ref)
        prevl_ref[...] = jnp.zeros_like(prevl_ref)
        st_ref[...] = jnp.zeros_like(st_ref)

    def token_mix(x_ref, last_ref, m_ref):
        x = x_ref[...].reshape(rows, x_ref.shape[2])
        row = lax.broadcasted_iota(jnp.int32, x.shape, 0)
        shifted = pltpu.roll(x, 1, axis=0)
        for b in range(nb):
            shifted = jnp.where(row == b * q, last_ref[b, 0:1, :], shifted)
            last_ref[b, 0:1, :] = x[(b + 1) * q - 1:(b + 1) * q, :]
        return x + (shifted - x) * m_ref[...]

    p = token_mix(p_ref, prev_ref, mu_ref)
    lora = token_mix(pl_ref, prevl_ref, mul_ref)
    r = p[:, 0:MIX_W]
    k = p[:, MIX_W:2 * MIX_W]
    v = p[:, 2 * MIX_W:3 * MIX_W]
    w = -jax.nn.softplus(-(w0_ref[...] + _dot(jnp.tanh(lora).astype(BF16), w2_ref[...]))) - 0.5
    log_w = -jnp.exp(w)
    a_sig = jax.nn.sigmoid(a0_ref[...] + _dot(lora.astype(BF16), a2_ref[...]))
    gate = _dot(jax.nn.sigmoid(lora).astype(BF16), g2_ref[...])
    ones = ones_ref[...]
    kk = k * kk_ref[...]
    k = k * (1.0 + (a_sig - 1.0) * ka_ref[...])
    kk_sq, rk_sum = _exact_right([kk * kk, r * k * rk_ref[...]], ones)
    kk = kk / jnp.maximum(jnp.sqrt(kk_sq), 1e-12)
    a_vec = -kk
    b_vec = kk * a_sig

    rr = lax.broadcasted_iota(jnp.int32, (rows, rows), 0)
    cc = lax.broadcasted_iota(jnp.int32, (rows, rows), 1)
    cum_mask = jnp.where((rr >= cc) & ((rr // q) == (cc // q)), 1.0, 0.0).astype(BF16)
    gam = _exact_left(cum_mask, log_w)
    g_last = jnp.concatenate(
        [jnp.broadcast_to(gam[(b + 1) * q - 1:(b + 1) * q, :], (q, MIX_W)) for b in range(nb)], axis=0)
    e_neg = jnp.exp(-gam)
    e_last = jnp.exp(g_last)
    e_rem = e_last * e_neg
    a_t = a_vec * jnp.exp(gam - log_w)
    r_t = r * jnp.exp(gam)
    b_t = b_vec * e_neg
    k_t = k * e_neg
    b_h = b_vec * e_rem
    k_h = k * e_rem
    assert q == HEAD_W
    row2 = lax.broadcasted_iota(jnp.int32, (q, 2 * HEAD_W), 0)
    lane2 = lax.broadcasted_iota(jnp.int32, (q, 2 * HEAD_W), 1)
    low = lane2 < HEAD_W
    col2 = jnp.where(low, lane2, lane2 - HEAD_W)
    keep_z0 = low | (col2 < row2)
    keep_t = col2 <= row2
    strict = _tril_mask(q, strict=True)
    pairs = [(b, h) for b in range(nb) for h in range(N_HEADS)]
    zeros_w = jnp.zeros((HEAD_W, 2 * HEAD_W), BF16)
    zeros_x = jnp.zeros((q, HEAD_W), F32)

    def blk(t, key):
        b, h = key
        return t[b * q:(b + 1) * q, h * HEAD_W:(h + 1) * HEAD_W]

    p0, p1, s0, z = {}, {}, {}, {}
    for key in pairs:
        ar = jnp.concatenate([blk(a_t, key), blk(r_t, key)], axis=0).astype(BF16)
        s0[key] = st_ref[key[0], key[1]]
        rhs = jnp.concatenate([s0[key], blk(b_t, key), blk(k_t, key), zeros_x], axis=0).astype(BF16)
        prod = _dot_nt(ar, rhs)
        p0[key] = prod[:, :2 * HEAD_W]
        p1[key] = prod[:, 2 * HEAD_W:]
    for key in pairs:
        a_k = jnp.where(strict, p1[key][:q, :HEAD_W], 0.0).astype(BF16)
        x0 = _dot(a_k, blk(v, key).astype(BF16))
        z[key] = jnp.where(keep_z0, p0[key][:q], 0.0) + jnp.concatenate([x0, zeros_x], axis=1)
    steps = max(1, (q - 1).bit_length())
    for _ in range(steps - 1):
        for key in pairs:
            zb = z[key].astype(BF16)
            z[key] = jnp.where(low, z[key], 0.0) + _dot(zb, jnp.concatenate([zeros_w, zb], axis=0))
    u = {}
    for key in pairs:
        zb = z[key].astype(BF16)
        u[key] = z[key][:, :HEAD_W] + _dot(zb, jnp.concatenate([zeros_w[:, :HEAD_W], zb[:, :HEAD_W]], axis=0))
    ys = {}
    for key in pairs:
        vu = jnp.concatenate([blk(v, key), u[key]], axis=0).astype(BF16)
        t_mat = jnp.where(keep_t, jnp.where(low, p1[key][q:], p0[key][q:]), 0.0)
        ys[key] = p0[key][q:, :HEAD_W] + _dot(t_mat.astype(BF16), vu)
        khb = jnp.concatenate([blk(k_h, key), blk(b_h, key)], axis=0).astype(BF16)
        st_ref[key[0], key[1]] = s0[key] * blk(e_last, key)[0:1, :] + _dot_tn(vu, khb)
    y = jnp.concatenate(
        [jnp.concatenate([ys[(b, h)] for h in range(N_HEADS)], axis=1) for b in range(nb)], axis=0)
    inv_n = 1.0 / HEAD_W
    yc = y - _exact_right([y], ones)[0] * inv_n
    var = _exact_right([yc * yc], ones)[0] * inv_n
    y = yc * lax.rsqrt(var + RW_LN_EPS) * lng_ref[...] + lnb_ref[...]
    bonus = rk_sum * v
    o_ref[...] = ((y + bonus) * gate).astype(o_ref.dtype).reshape(o_ref.shape)


def _pad_rows(w, rows, off):
    return jnp.zeros((rows, w.shape[1]), w.dtype).at[off:off + w.shape[0]].set(w)


def _rw_mixer(proj, mu, w0, w2, a0, a2, g2, k_k, k_a, r_k, ln_g, ln_b, bsz, seq):
    q = min(RW_Q, seq)
    nc = seq // q
    nb = RW_BATCH if bsz % RW_BATCH == 0 else 1
    vec = lambda t: t.reshape(1, MIX_W)
    const = lambda shape: pl.BlockSpec(shape, lambda b, c: (0, 0))
    offs = (0, RW_LORA_RANKS[0], RW_LORA_RANKS[0] + RW_LORA_RANKS[1])
    lora = lambda t, off: _pad_rows(t, RW_LORA_W, off).astype(BF16)
    mu_rkv, mu_lora = mu[:RW_RKV_W].reshape(1, RW_RKV_W), _pad_rows(mu[RW_RKV_W:, None], RW_LORA_W, 0).reshape(1, RW_LORA_W)
    ones = _block_diag(jnp.ones((N_HEADS, HEAD_W, HEAD_W), F32)).astype(BF16)
    proj3 = proj.reshape(bsz, seq, proj.shape[1])
    out = pl.pallas_call(
        _rw_body,
        name="rwkv7_mixer",
        grid=(bsz // nb, nc),
        in_specs=[
            pl.BlockSpec((nb, q, RW_RKV_W), lambda b, c: (b, c, PC_RKV // RW_RKV_W)),
            pl.BlockSpec((nb, q, RW_LORA_W), lambda b, c: (b, c, PC_LORA // RW_LORA_W)),
            const((1, RW_RKV_W)), const((1, RW_LORA_W)), const((1, MIX_W)), const((RW_LORA_W, MIX_W)),
            const((1, MIX_W)), const((RW_LORA_W, MIX_W)), const((RW_LORA_W, MIX_W)),
            const((1, MIX_W)), const((1, MIX_W)), const((1, MIX_W)),
            const((1, MIX_W)), const((1, MIX_W)), const((MIX_W, MIX_W)),
        ],
        out_specs=pl.BlockSpec((nb, q, MIX_W), lambda b, c: (b, c, 0)),
        out_shape=jax.ShapeDtypeStruct((bsz, seq, MIX_W), BF16),
        scratch_shapes=[pltpu.VMEM((nb, HALO, RW_RKV_W), F32), pltpu.VMEM((nb, HALO, RW_LORA_W), F32),
                        pltpu.VMEM((nb, N_HEADS, HEAD_W, HEAD_W), F32)],
        compiler_params=_params("parallel", "arbitrary"),
    )(proj3, proj3, mu_rkv, mu_lora, vec(w0), lora(w2, offs[0]), vec(a0), lora(a2, offs[1]), lora(g2, offs[2]),
      vec(k_k), vec(k_a), vec(r_k), vec(ln_g), vec(ln_b), ones)
    return out.reshape(bsz * seq, MIX_W)


def _merge_body(h_ref, *refs):
    nb = (len(refs) - 2) // 2
    y_refs, wg_refs, wp_ref, o_ref = refs[:nb], refs[nb:2 * nb], refs[2 * nb], refs[2 * nb + 1]
    h = h_ref[...]
    acc = None
    for kk in range(nb):
        term = jax.nn.sigmoid(_dot(h, wg_refs[kk][...])) * _dot(y_refs[kk][...], wp_ref[kk])
        acc = term if acc is None else acc + term
    o_ref[...] = acc.astype(o_ref.dtype)


def _merge(h, ys, w_gate, w_branch, layer=None, tm=512, tn=512):
    m = h.shape[0]
    tm = min(tm, m)
    nb = len(ys)
    y_spec = pl.BlockSpec((tm, MIX_W), lambda j, i: (i, 0))
    gate_specs = [_weight_spec(w_gate, layer, tn, lambda j, i, k=k: k * (D_MODEL // tn) + j) for k in range(nb)]
    if w_branch.ndim == 3:
        branch_spec = pl.BlockSpec((nb, MIX_W, tn), lambda j, i: (0, 0, j))
    else:
        branch_spec = pl.BlockSpec((None, nb, MIX_W, tn), lambda j, i: (layer, 0, 0, j))
    return pl.pallas_call(
        _merge_body,
        name="gated_merge",
        grid=(D_MODEL // tn, m // tm),
        in_specs=[pl.BlockSpec((tm, D_MODEL), lambda j, i: (i, 0))] + [y_spec] * nb + gate_specs + [branch_spec],
        out_specs=pl.BlockSpec((tm, tn), lambda j, i: (i, j)),
        out_shape=jax.ShapeDtypeStruct((m, D_MODEL), BF16),
        compiler_params=_params("parallel", "parallel"),
    )(h, *ys, *([w_gate] * nb), w_branch)


def _ffn_body(h_ref, hh_ref, wg_ref, wu_ref, cw_ref, cb_ref, o_ref, wg_s, wu_s, *, tiles_per_seq):
    t = h_ref.shape[0]
    nh = hh_ref.shape[0]

    @pl.when(pl.program_id(1) == 0)
    def _():
        wg_s[...] = wg_ref[...].astype(BF16)
        wu_s[...] = wu_ref[...].astype(BF16)

    h = h_ref[...]
    seq_start = (pl.program_id(1) % tiles_per_seq) == 0
    halo = jnp.where(seq_start, jnp.zeros_like(hh_ref[...]), hh_ref[...])
    g_all = _dot(jnp.concatenate([halo, h], axis=0), wg_s[...])
    k_w = cw_ref.shape[0]
    u = cb_ref[...] + g_all[nh:, :] * cw_ref[k_w - 1:k_w, :]
    for j in range(k_w - 1):
        lag = k_w - 1 - j
        u = u + g_all[nh - lag:nh - lag + t, :] * cw_ref[j:j + 1, :]
    o_ref[...] = (jax.nn.gelu(u) * _dot(h, wu_s[...])).astype(o_ref.dtype)


def _ffn_act(h, w_gate, w_up, conv_w, conv_b, seq, layer=None, tm=1024, tf=512):
    m = h.shape[0]
    tm = min(tm, seq)
    f = w_gate.shape[-1]
    halo_blocks = tm // BF16_ROWS
    return pl.pallas_call(
        functools.partial(_ffn_body, tiles_per_seq=seq // tm),
        name="ffn_gate_up",
        grid=(f // tf, m // tm),
        in_specs=[
            pl.BlockSpec((tm, D_MODEL), lambda j, i: (i, 0)),
            pl.BlockSpec((BF16_ROWS, D_MODEL), lambda j, i: (jnp.maximum(i * halo_blocks - 1, 0), 0)),
            _weight_spec(w_gate, layer, tf, lambda j, i: j),
            _weight_spec(w_up, layer, tf, lambda j, i: j),
            pl.BlockSpec((conv_w.shape[0], tf), lambda j, i: (0, j)),
            pl.BlockSpec((1, tf), lambda j, i: (0, j)),
        ],
        out_specs=pl.BlockSpec((tm, tf), lambda j, i: (i, j)),
        out_shape=jax.ShapeDtypeStruct((m, f), BF16),
        scratch_shapes=[pltpu.VMEM((D_MODEL, tf), BF16), pltpu.VMEM((D_MODEL, tf), BF16)],
        compiler_params=_params("parallel", "arbitrary"),
    )(h, h, w_gate, w_up, conv_w, conv_b.reshape(1, f))


def _mixer_weight(w_in_l):
    s5_u, lru_x, lru_g, m2_z, m2_xbc, m2_dt, rw_p = jnp.split(
        w_in_l, [512, 1024, 1536, 2048, 2048 + M2_CONV_DIM, 2048 + M2_CONV_DIM + N_HEADS], axis=1)
    d = w_in_l.shape[0]
    padc = lambda t, n: jnp.concatenate([t, jnp.zeros((d, n - t.shape[1]), t.dtype)], axis=1)
    proj_w = jnp.concatenate([m2_xbc, m2_z, rw_p[:, :RW_RKV_W], lru_x, lru_g,
                              padc(rw_p[:, RW_RKV_W:], RW_LORA_W), padc(m2_dt, PROJ_W - PC_DT)], axis=1)
    assert proj_w.shape[1] == PROJ_W
    return s5_u.astype(BF16), proj_w.astype(BF16)


def kernel(x, norm_mix_g, w_in, s5_lambda_re, s5_lambda_im, s5_b_re, s5_b_im, s5_c_re, s5_c_im, s5_d, s5_log_dt, s5_w_glu, lru_conv_w, lru_conv_b, lru_w_a, lru_b_a, lru_w_x, lru_b_x, lru_lambda, m2_conv_w, m2_conv_b, m2_dt_bias, m2_a_log, m2_d, m2_norm_g, rw_mu, rw_w0, rw_w2, rw_a0, rw_a2, rw_g2, rw_k_k, rw_k_a, rw_r_k, rw_ln_g, rw_ln_b, w_branch, w_out, norm_ffn_g, w_ffn_gate, w_ffn_up, ffn_conv_w, ffn_conv_b, w_ffn_down, final_norm_g):
    bsz, seq, d = x.shape
    depth = w_in.shape[0]
    mixer_cols = w_in.shape[2] - w_branch.shape[1] * d
    xf = x.reshape(bsz * seq, d)
    wb_all, wo_all, wd_all = w_branch.astype(BF16), w_out.astype(BF16), w_ffn_down.astype(BF16)
    for l in range(depth):
        h = _rms_norm(xf, norm_mix_g[l], BF16)
        w_s5, w_proj = _mixer_weight(w_in[l, :, :mixer_cols])
        proj = _matmul(h, w_proj, tm=2048)
        mats = _s5_matrices(s5_lambda_re[l], s5_lambda_im[l], s5_b_re[l], s5_b_im[l],
                            s5_c_re[l], s5_c_im[l], s5_d[l], s5_log_dt[l])
        y_a = _s5_mixer(h, w_s5, mats, s5_w_glu[l], bsz, seq)
        y_b = _lru_mixer(proj, lru_conv_w[l], lru_conv_b[l], lru_w_a[l], lru_b_a[l],
                         lru_w_x[l], lru_b_x[l], lru_lambda[l], bsz, seq)
        y_c = _m2_mixer(proj, m2_conv_w[l], m2_conv_b[l], m2_dt_bias[l], m2_a_log[l],
                        m2_d[l], m2_norm_g[l], bsz, seq)
        y_d = _rw_mixer(proj, rw_mu[l], rw_w0[l], rw_w2[l], rw_a0[l], rw_a2[l], rw_g2[l],
                        rw_k_k[l], rw_k_a[l], rw_r_k[l].reshape(-1), rw_ln_g[l], rw_ln_b[l], bsz, seq)
        w_gate = w_in[l, :, mixer_cols:].astype(BF16)
        merged = _merge(h, (y_a, y_b, y_c, y_d), w_gate, wb_all, layer=l)
        xf, h = _matmul_res_norm(merged, wo_all, xf, norm_ffn_g[l], l)
        act = _ffn_act(h, w_ffn_gate, w_ffn_up, ffn_conv_w[l], ffn_conv_b[l], seq, layer=l)
        xf = _matmul(act, wd_all, res=xf, tm=1024, layer=l)
    return _rms_norm(xf, final_norm_g, F32).reshape(bsz, seq, d)
```

```python
import functools

import jax
import jax.numpy as jnp
from jax import lax
from jax.experimental import pallas as pl
from jax.experimental.pallas import tpu as pltpu

F32 = jnp.float32
BF16 = jnp.bfloat16

D_MODEL = 2048
MIX_W = 512
HEAD_W = 64
N_HEADS = MIX_W // HEAD_W
S5_GROUP = 16
S5_GROUPS = MIX_W // S5_GROUP
S5_STATE = 64
S5_Q = 8
S5_NSTATE = S5_GROUPS * S5_STATE
LANES = 128
S5_SG = MIX_W // LANES
S5_SGW = S5_Q * LANES
S5_SB = S5_NSTATE // S5_SG
S5_SCAN_ROWS = 512
LRU_C = 8.0
M2_GROUPS = 2
M2_STATE = 128
M2_CONV_DIM = MIX_W + 2 * M2_GROUPS * M2_STATE
M2_Q = 128
M2_BATCH = 2
RW_Q = 64
RW_BATCH = 4
RW_RKV_W = 3 * MIX_W
RW_LORA_RANKS = (32, 32, 96)
RW_LORA_W = 256
RW_LN_EPS = 64e-5
EPS = 1e-6
HALO = 8
BF16_ROWS = 16

PC_XBC = 0
PC_Z = 1024
PC_RKV = 1536
PC_LRU_X = 3072
PC_LRU_G = 3584
PC_LORA = 4096
PC_DT = 4352
PROJ_W = 4608

VMEM_LIMIT_BYTES = 50 * 1024 * 1024


def _params(*sem):
    return pltpu.CompilerParams(dimension_semantics=sem, vmem_limit_bytes=VMEM_LIMIT_BYTES)


def _dot(a, b):
    return jnp.dot(a, b, preferred_element_type=F32)


def _dot_nt(a, b):
    return lax.dot_general(a, b, (((1,), (1,)), ((), ())), preferred_element_type=F32)


def _dot_tn(a, b):
    return lax.dot_general(a, b, (((0,), (0,)), ((), ())), preferred_element_type=F32)


def _split(x, terms):
    out = []
    for _ in range(terms - 1):
        hi = x.astype(BF16)
        out.append(hi)
        x = x - hi.astype(F32)
    out.append(x.astype(BF16))
    return out


def _exact_left(m, x, terms=3):
    return sum(_dot(m, p) for p in _split(x, terms))


def _exact_right(xs, m, terms=2):
    n = xs[0].shape[0]
    stacked = jnp.concatenate([p for x in xs for p in _split(x, terms)], axis=0)
    out = _dot(stacked, m)
    return [sum(out[(i * terms + t) * n:(i * terms + t + 1) * n] for t in range(terms))
            for i in range(len(xs))]


def _tril_mask(n, strict=False):
    r = lax.broadcasted_iota(jnp.int32, (n, n), 0)
    c = lax.broadcasted_iota(jnp.int32, (n, n), 1)
    return (r > c) if strict else (r >= c)


def _causal_conv(x, halo, w_ref, b_ref):
    k_w = w_ref.shape[0]
    t = x.shape[0]
    xe = jnp.concatenate([halo, x], axis=0)
    out = b_ref[...] + x * w_ref[k_w - 1:k_w, :]
    for j in range(k_w - 1):
        lag = k_w - 1 - j
        out = out + xe[HALO - lag:HALO - lag + t, :] * w_ref[j:j + 1, :]
    return out


def _mm_body(a_ref, b_ref, o_ref):
    o_ref[...] = _dot(a_ref[...], b_ref[...]).astype(o_ref.dtype)


def _mm_res_body(a_ref, b_ref, r_ref, o_ref):
    o_ref[...] = (_dot(a_ref[...], b_ref[...]) + r_ref[...]).astype(o_ref.dtype)


def _weight_spec(w, layer, tn, col_of):
    k = w.shape[-2]
    if w.ndim == 2:
        return pl.BlockSpec((k, tn), lambda *g: (0, col_of(*g)))
    return pl.BlockSpec((None, k, tn), lambda *g: (layer, 0, col_of(*g)))


def _matmul(a, b, res=None, out_dtype=F32, tm=1024, tn=512, n_outer=False, layer=None):
    m, k = a.shape
    n = b.shape[-1]
    tm, tn = min(tm, m), min(tn, n)
    assert m % tm == 0 and n % tn == 0
    if n_outer:
        grid = (n // tn, m // tm)
        row = lambda j, i: (i, 0)
        col = _weight_spec(b, layer, tn, lambda j, i: j)
        out = lambda j, i: (i, j)
    else:
        grid = (m // tm, n // tn)
        row = lambda i, j: (i, 0)
        col = _weight_spec(b, layer, tn, lambda i, j: j)
        out = lambda i, j: (i, j)
    in_specs = [pl.BlockSpec((tm, k), row), col]
    args = [a, b]
    body = _mm_body
    if res is not None:
        in_specs.append(pl.BlockSpec((tm, tn), out))
        args.append(res)
        body = _mm_res_body
    return pl.pallas_call(
        body,
        name="matmul",
        grid=grid,
        in_specs=in_specs,
        out_specs=pl.BlockSpec((tm, tn), out),
        out_shape=jax.ShapeDtypeStruct((m, n), out_dtype),
        compiler_params=_params("parallel", "parallel"),
    )(*args)


def _matmul_blockdiag(a, w, groups, res=None, res_block=None, tm=1024):
    m = a.shape[0]
    k, n = a.shape[1] // groups, w.shape[1]
    tm = min(tm, m)
    out_spec = pl.BlockSpec((tm, n), lambda i, s: (i, s))
    in_specs = [pl.BlockSpec((tm, k), lambda i, s: (i, s)), pl.BlockSpec((k, n), lambda i, s: (s, 0))]
    args = [a, w]
    if res is not None:
        in_specs.append(pl.BlockSpec((tm, n), lambda i, s: (i, res_block(s))))
        args.append(res)
    return pl.pallas_call(
        _mm_body if res is None else _mm_res_body,
        name="matmul_blockdiag",
        grid=(m // tm, groups),
        in_specs=in_specs,
        out_specs=out_spec,
        out_shape=jax.ShapeDtypeStruct((m, groups * n), F32),
        compiler_params=_params("parallel", "parallel"),
    )(*args)


def _norm_body(x_ref, g_ref, o_ref):
    x = x_ref[...]
    ms = jnp.mean(x * x, axis=-1, keepdims=True)
    o_ref[...] = (x * lax.rsqrt(ms + EPS) * g_ref[...]).astype(o_ref.dtype)


def _mm_res_norm_body(a_ref, w_ref, r_ref, g_ref, x_ref, h_ref):
    x = _dot(a_ref[...], w_ref[...]) + r_ref[...]
    x_ref[...] = x
    ms = jnp.mean(x * x, axis=-1, keepdims=True)
    h_ref[...] = (x * lax.rsqrt(ms + EPS) * g_ref[...]).astype(h_ref.dtype)


def _matmul_res_norm(a, w, res, g, layer, tm=512):
    m, k = a.shape
    n = w.shape[-1]
    tm = min(tm, m)
    rows = lambda width: pl.BlockSpec((tm, width), lambda i: (i, 0))
    return pl.pallas_call(
        _mm_res_norm_body,
        name="matmul_residual_norm",
        grid=(m // tm,),
        in_specs=[rows(k), _weight_spec(w, layer, n, lambda i: 0), rows(n), pl.BlockSpec((1, n), lambda i: (0, 0))],
        out_specs=[rows(n), rows(n)],
        out_shape=[jax.ShapeDtypeStruct((m, n), F32), jax.ShapeDtypeStruct((m, n), BF16)],
        compiler_params=_params("parallel"),
    )(a, w, res, g.reshape(1, n))


def _rms_norm(x, g, out_dtype, tm=512):
    m, d = x.shape
    tm = min(tm, m)
    return pl.pallas_call(
        _norm_body,
        name="rms_norm",
        grid=(m // tm,),
        in_specs=[pl.BlockSpec((tm, d), lambda i: (i, 0)), pl.BlockSpec((1, d), lambda i: (0, 0))],
        out_specs=pl.BlockSpec((tm, d), lambda i: (i, 0)),
        out_shape=jax.ShapeDtypeStruct((m, d), out_dtype),
        compiler_params=_params("parallel"),
    )(x, g.reshape(1, d))


def _s5_matrices(lam_re, lam_im, b_re, b_im, c_re, c_im, d, log_dt):
    g_n, p_n, c_n, q = S5_GROUPS, S5_STATE, S5_GROUP, S5_Q
    hp = lax.Precision.HIGHEST
    dt = jnp.exp(log_dt)[:, None]
    n = jnp.arange(q + 1, dtype=F32)[:, None, None]
    mag = jnp.exp(n * (lam_re * dt))
    pw_re = mag * jnp.cos(n * (lam_im * dt))
    pw_im = mag * jnp.sin(n * (lam_im * dt))
    den = lam_re * lam_re + lam_im * lam_im
    nr, ni = pw_re[1] - 1.0, pw_im[1]
    f_re = (nr * lam_re + ni * lam_im) / den
    f_im = (ni * lam_re - nr * lam_im) / den
    e_re = f_re[..., None] * b_re - f_im[..., None] * b_im
    e_im = f_re[..., None] * b_im + f_im[..., None] * b_re
    cp_re = c_re[None] * pw_re[:, :, None, :] - c_im[None] * pw_im[:, :, None, :]
    cp_im = c_re[None] * pw_im[:, :, None, :] + c_im[None] * pw_re[:, :, None, :]
    kern = (jnp.einsum("tgop,gpi->tgio", cp_re[:q], e_re, precision=hp)
            - jnp.einsum("tgop,gpi->tgio", cp_im[:q], e_im, precision=hp))
    kern = kern.at[0].add(d.reshape(g_n, c_n)[:, :, None] * jnp.eye(c_n, dtype=F32))
    sg_n, gl_n = S5_SG, g_n // S5_SG
    rows = sg_n * S5_SGW
    lag = jnp.arange(q)[None, :] - jnp.arange(q)[:, None]
    kt = jnp.where((lag >= 0)[:, :, None, None, None], kern[jnp.clip(lag, 0, q - 1)], 0.0)
    kt = kt.reshape(q, q, sg_n, gl_n, c_n, c_n).transpose(2, 0, 3, 4, 1, 5)
    kt = kt.reshape(rows, q * c_n)
    rev_re, rev_im = pw_re[q - 1 - jnp.arange(q)], pw_im[q - 1 - jnp.arange(q)]
    et_re, et_im = e_re.transpose(0, 2, 1)[None], e_im.transpose(0, 2, 1)[None]
    ws_re = rev_re[:, :, None, :] * et_re - rev_im[:, :, None, :] * et_im
    ws_im = rev_re[:, :, None, :] * et_im + rev_im[:, :, None, :] * et_re
    ws = jnp.stack([ws_re, ws_im], axis=3)
    ws = ws.reshape(q, sg_n, gl_n, c_n, 2, p_n).transpose(1, 0, 2, 3, 4, 5).reshape(rows, 2 * p_n)
    wy = jnp.stack([cp_re[1:], -cp_im[1:]], axis=0)
    wy = wy.reshape(2, q, sg_n, gl_n, c_n, p_n).transpose(2, 0, 3, 5, 1, 4)
    wy = wy.reshape(rows, q * c_n)

    col = jnp.arange(S5_SGW)
    small = jnp.arange(q * c_n)[:, None]
    tok_rep = ((small // c_n == col[None, :] // LANES) & (small % c_n == col[None, :] % c_n)).astype(BF16)
    small = jnp.arange(2 * p_n)[:, None]
    st_rep = ((small // p_n == col[None, :] // S5_SB) & (small % p_n == col[None, :] % p_n)).astype(BF16)
    row = jnp.arange(rows)
    g_tok_r, g_tok_c = (row % LANES) // c_n, (col % LANES) // c_n
    g_st_r, g_st_c = (row % S5_SB) // p_n, (col % S5_SB) // p_n
    expand = lambda table, rep: jnp.dot(table.astype(BF16), rep, preferred_element_type=BF16)
    zero = jnp.zeros((), BF16)
    toep = jnp.where(g_tok_r[:, None] == g_tok_c[None, :], expand(kt, tok_rep), zero)
    w_state = jnp.where(g_tok_r[:, None] == g_st_c[None, :], expand(ws, st_rep), zero)
    w_out = jnp.where(g_st_r[:, None] == g_tok_c[None, :], expand(wy, tok_rep), zero)
    return toep, w_state, w_out, pw_re[q].reshape(1, S5_NSTATE), pw_im[q].reshape(1, S5_NSTATE)


def _s5_scan_body(ar_ref, ai_ref, s_ref, o_ref, c_ref):
    rows = BF16_ROWS
    nb = s_ref.shape[0]
    a_re, a_im = ar_ref[...], ai_ref[...]
    half = a_re.shape[1]

    @pl.when(pl.program_id(1) == 0)
    def _():
        c_ref[...] = jnp.zeros_like(c_ref)

    def block(i, carry):
        base = pl.multiple_of(i * rows, rows)
        xs = [s_ref[b, pl.ds(base, rows), :] for b in range(nb)]
        outs = [[] for _ in range(nb)]
        carry = list(carry)
        for r in range(rows):
            for b in range(nb):
                s_re, s_im = carry[b]
                outs[b].append(jnp.concatenate([s_re, s_im], axis=1))
                carry[b] = (a_re * s_re - a_im * s_im + xs[b][r:r + 1, :half],
                            a_re * s_im + a_im * s_re + xs[b][r:r + 1, half:])
        for b in range(nb):
            o_ref[b, pl.ds(base, rows), :] = jnp.concatenate(outs[b], axis=0).astype(o_ref.dtype)
        return tuple(carry)

    init = tuple((c_ref[b, :, :half], c_ref[b, :, half:]) for b in range(nb))
    last = lax.fori_loop(0, s_ref.shape[1] // rows, block, init)
    for b in range(nb):
        c_ref[b] = jnp.concatenate(last[b], axis=1)


def _s5_fold_body(h_ref, w_ref, o_ref, tok_ref):
    u = _dot(h_ref[...], w_ref[...])
    tc = o_ref.shape[0]
    for c in range(S5_SG):
        tok_ref[c] = u[:, c * LANES:(c + 1) * LANES]
        for j in range(S5_Q):
            lo = c * S5_SGW + j * LANES
            o_ref[:, lo:lo + LANES] = tok_ref[c, pl.ds(j, tc, stride=S5_Q), :].astype(o_ref.dtype)


def _s5_glu_body(y_ref, w_ref, o_ref, tok_ref):
    tc = y_ref.shape[0]
    for c in range(S5_SG):
        for j in range(S5_Q):
            lo = c * S5_SGW + j * LANES
            tok_ref[c, pl.ds(j, tc, stride=S5_Q), :] = y_ref[:, lo:lo + LANES]
    y = jax.nn.gelu(jnp.concatenate([tok_ref[c] for c in range(S5_SG)], axis=1))
    o_ref[...] = (y * jax.nn.sigmoid(_dot(y.astype(BF16), w_ref[...]))).astype(o_ref.dtype)


def _s5_mixer(h, w_u, mats, w_glu, bsz, seq, layer=None):
    toep, w_state, w_out, aq_re, aq_im = mats
    m = bsz * seq
    nc = seq // S5_Q
    width = S5_Q * MIX_W
    tm = min(1024, m)
    tc = tm // S5_Q
    uc = pl.pallas_call(
        _s5_fold_body,
        name="s5_input_proj",
        grid=(m // tm,),
        in_specs=[pl.BlockSpec((tm, D_MODEL), lambda i: (i, 0)), _weight_spec(w_u, layer, MIX_W, lambda i: 0)],
        out_specs=pl.BlockSpec((tc, width), lambda i: (i, 0)),
        out_shape=jax.ShapeDtypeStruct((m // S5_Q, width), BF16),
        scratch_shapes=[pltpu.VMEM((MIX_W // LANES, tm, LANES), F32)],
        compiler_params=_params("parallel"),
    )(h, w_u)
    y1s = _matmul_blockdiag(uc, jnp.concatenate([toep, w_state], axis=1), S5_SG)
    rb = min(S5_SCAN_ROWS, nc)
    s_in = pl.pallas_call(
        _s5_scan_body,
        name="s5_chunk_scan",
        grid=(S5_SG, nc // rb),
        in_specs=[
            pl.BlockSpec((1, S5_SB), lambda j, r: (0, j)),
            pl.BlockSpec((1, S5_SB), lambda j, r: (0, j)),
            pl.BlockSpec((bsz, rb, S5_SGW), lambda j, r: (0, r, 2 * j + 1)),
        ],
        out_specs=pl.BlockSpec((bsz, rb, S5_SGW), lambda j, r: (0, r, j)),
        out_shape=jax.ShapeDtypeStruct((bsz, nc, width), BF16),
        scratch_shapes=[pltpu.VMEM((bsz, 1, S5_SGW), F32)],
        compiler_params=_params("parallel", "arbitrary"),
    )(aq_re, aq_im, y1s.reshape(bsz, nc, 2 * width))
    y = _matmul_blockdiag(s_in.reshape(bsz * nc, width), w_out, S5_SG, res=y1s, res_block=lambda s: 2 * s)
    return pl.pallas_call(
        _s5_glu_body,
        name="s5_glu",
        grid=(m // tm,),
        in_specs=[pl.BlockSpec((tc, width), lambda i: (i, 0)), pl.BlockSpec((MIX_W, MIX_W), lambda i: (0, 0))],
        out_specs=pl.BlockSpec((tm, MIX_W), lambda i: (i, 0)),
        out_shape=jax.ShapeDtypeStruct((m, MIX_W), BF16),
        scratch_shapes=[pltpu.VMEM((MIX_W // LANES, tm, LANES), F32)],
        compiler_params=_params("parallel"),
    )(y, w_glu.astype(BF16))


def _lru_body(x_ref, g_ref, cw_ref, cb_ref, wa_ref, ba_ref, wx_ref, bx_ref, lam_ref, o_ref,
              halo_ref, h_ref, a_s, b_s):
    nb, t, _ = x_ref.shape

    @pl.when(pl.program_id(0) == 0)
    def _():
        halo_ref[...] = jnp.zeros_like(halo_ref)
        h_ref[...] = jnp.zeros_like(h_ref)

    soft = jax.nn.softplus(-lam_ref[...])
    for b in range(nb):
        x_in = x_ref[b]
        x = _causal_conv(x_in, halo_ref[b], cw_ref, cb_ref)
        halo_ref[b] = x_in[t - HALO:, :]
        xb = x.astype(BF16)
        r = jax.nn.sigmoid(_dot(xb, wa_ref[...]) + ba_ref[...])
        i = jax.nn.sigmoid(_dot(xb, wx_ref[...]) + bx_ref[...])
        log_a = (-LRU_C * r) * soft
        a = jnp.exp(log_a)
        a_s[b] = a
        b_s[b] = x * i * jnp.sqrt(1.0 - a * a)
    rows = 8

    def block(k, hs):
        base = pl.multiple_of(k * rows, rows)
        av = [a_s[b, pl.ds(base, rows), :] for b in range(nb)]
        bv = [b_s[b, pl.ds(base, rows), :] for b in range(nb)]
        hs = list(hs)
        outs = [[] for _ in range(nb)]
        for rr in range(rows):
            for b in range(nb):
                hs[b] = av[b][rr:rr + 1, :] * hs[b] + bv[b][rr:rr + 1, :]
                outs[b].append(hs[b])
        for b in range(nb):
            b_s[b, pl.ds(base, rows), :] = jnp.concatenate(outs[b], axis=0)
        return tuple(hs)

    hs = lax.fori_loop(0, t // rows, block, tuple(h_ref[b] for b in range(nb)))
    for b in range(nb):
        h_ref[b] = hs[b]
        o_ref[b] = (b_s[b] * jax.nn.gelu(g_ref[b])).astype(o_ref.dtype)


def _block_diag(w):
    h_n, n, _ = w.shape
    eye = jnp.eye(h_n, dtype=w.dtype)
    return (w[:, :, None, :] * eye[:, None, :, None]).reshape(h_n * n, h_n * n)


def _lru_mixer(proj, conv_w, conv_b, w_a, b_a, w_x, b_x, lam, bsz, seq):
    t = min(256, seq)
    vec = lambda v: v.reshape(1, MIX_W)
    const = lambda shape: pl.BlockSpec(shape, lambda c: (0, 0))
    proj3 = proj.reshape(bsz, seq, proj.shape[1])
    out = pl.pallas_call(
        _lru_body,
        name="rglru_mixer",
        grid=(seq // t,),
        in_specs=[
            pl.BlockSpec((bsz, t, MIX_W), lambda c: (0, c, PC_LRU_X // MIX_W)),
            pl.BlockSpec((bsz, t, MIX_W), lambda c: (0, c, PC_LRU_G // MIX_W)),
            const(conv_w.shape), const((1, MIX_W)),
            const((MIX_W, MIX_W)), const((1, MIX_W)),
            const((MIX_W, MIX_W)), const((1, MIX_W)), const((1, MIX_W)),
        ],
        out_specs=pl.BlockSpec((bsz, t, MIX_W), lambda c: (0, c, 0)),
        out_shape=jax.ShapeDtypeStruct((bsz, seq, MIX_W), BF16),
        scratch_shapes=[
            pltpu.VMEM((bsz, HALO, MIX_W), F32), pltpu.VMEM((bsz, 1, MIX_W), F32),
            pltpu.VMEM((bsz, t, MIX_W), F32), pltpu.VMEM((bsz, t, MIX_W), F32),
        ],
        compiler_params=_params("arbitrary"),
    )(proj3, proj3, conv_w, vec(conv_b), _block_diag(w_a).astype(BF16), vec(b_a),
      _block_diag(w_x).astype(BF16), vec(b_x), vec(lam))
    return out.reshape(bsz * seq, MIX_W)


def _m2_body(z_ref, xbc_ref, dt_ref, rep_ref, cw_ref, cb_ref, dtb_ref, alog_ref, d_ref, ng_ref, o_ref,
             halo_ref, st_ref):
    nb, q, _ = z_ref.shape
    hg = N_HEADS // M2_GROUPS
    gw = hg * HEAD_W
    seqs = range(nb)

    @pl.when(pl.program_id(1) == 0)
    def _():
        halo_ref[...] = jnp.zeros_like(halo_ref)
        st_ref[...] = jnp.zeros_like(st_ref)

    causal = _tril_mask(q)
    causal_b = causal.astype(BF16)
    neg_a = -jnp.exp(alog_ref[...])
    dt_raw = _exact_right([dt_ref[b] for b in seqs], rep_ref[...])
    xc = []
    for b in seqs:
        xbc = xbc_ref[b]
        conv = _causal_conv(xbc, halo_ref[b], cw_ref, cb_ref)
        halo_ref[b] = xbc[q - HALO:, :]
        xc.append(conv * jax.nn.sigmoid(conv))
    dt = [jax.nn.softplus(dt_raw[b] + dtb_ref[...]) for b in seqs]
    a_cs = [_exact_left(causal_b, dt[b] * neg_a) for b in seqs]
    xs = [xc[b][:, :MIX_W] for b in seqs]
    xd = [xs[b] * dt[b] for b in seqs]
    a_cs_t = [a_cs[b].T for b in seqs]
    a_last = [a_cs[b][q - 1:q, :] for b in seqs]
    xd_st = [(xd[b] * jnp.exp(a_last[b] - a_cs[b])).astype(BF16) for b in seqs]
    e_cs = [jnp.exp(a_cs[b]) for b in seqs]
    bg, cg, cb, y_off = {}, {}, {}, {}
    for b in seqs:
        for g in range(M2_GROUPS):
            lo = MIX_W + g * M2_STATE
            bg[b, g] = xc[b][:, lo:lo + M2_STATE].astype(BF16)
            cg[b, g] = xc[b][:, lo + M2_GROUPS * M2_STATE:lo + (M2_GROUPS + 1) * M2_STATE].astype(BF16)
            cb[b, g] = _dot_nt(cg[b, g], bg[b, g])
            y_off[b, g] = _dot(cg[b, g], st_ref[b, g].astype(BF16)) * e_cs[b][:, g * gw:(g + 1) * gw]
    ys = [[] for _ in seqs]
    for g in range(M2_GROUPS):
        for hh in range(hg):
            for b in seqs:
                lo = (g * hg + hh) * HEAD_W
                col = a_cs[b][:, lo:lo + HEAD_W]
                col = jnp.concatenate([col] * (q // HEAD_W), axis=1)
                seg = col - a_cs_t[b][lo:lo + 1, :]
                dec = jnp.exp(jnp.where(causal, seg, -jnp.inf))
                y_d = _dot((cb[b, g] * dec).astype(BF16), xd[b][:, lo:lo + HEAD_W].astype(BF16))
                ys[b].append(y_d + y_off[b, g][:, hh * HEAD_W:(hh + 1) * HEAD_W])
    for b in seqs:
        for g in range(M2_GROUPS):
            upd = _dot_tn(bg[b, g], xd_st[b][:, g * gw:(g + 1) * gw])
            st_ref[b, g] = st_ref[b, g] * jnp.exp(a_last[b][:, g * gw:(g + 1) * gw]) + upd
    for b in seqs:
        y = jnp.concatenate(ys[b], axis=1) + d_ref[...] * xs[b]
        z = z_ref[b]
        y = y * (z * jax.nn.sigmoid(z))
        ms = jnp.mean(y * y, axis=-1, keepdims=True)
        o_ref[b] = (y * lax.rsqrt(ms + EPS) * ng_ref[...]).astype(o_ref.dtype)


def _per_head(v):
    return jnp.repeat(v, HEAD_W).reshape(1, -1)


def _m2_mixer(proj, conv_w, conv_b, dt_bias, a_log, d, norm_g, bsz, seq):
    q = min(M2_Q, seq)
    nc = seq // q
    nb = M2_BATCH if bsz % M2_BATCH == 0 else 1
    const = lambda shape: pl.BlockSpec(shape, lambda b, c: (0, 0))
    head_of_lane = jnp.arange(MIX_W)[None, :] // HEAD_W
    rep = (jnp.arange(LANES)[:, None] == head_of_lane).astype(BF16)
    proj3 = proj.reshape(bsz, seq, proj.shape[1])
    out = pl.pallas_call(
        _m2_body,
        name="mamba2_mixer",
        grid=(bsz // nb, nc),
        in_specs=[
            pl.BlockSpec((nb, q, MIX_W), lambda b, c: (b, c, PC_Z // MIX_W)),
            pl.BlockSpec((nb, q, M2_CONV_DIM), lambda b, c: (b, c, PC_XBC // M2_CONV_DIM)),
            pl.BlockSpec((nb, q, LANES), lambda b, c: (b, c, PC_DT // LANES)),
            const((LANES, MIX_W)), const(conv_w.shape), const((1, M2_CONV_DIM)),
            const((1, MIX_W)), const((1, MIX_W)), const((1, MIX_W)), const((1, MIX_W)),
        ],
        out_specs=pl.BlockSpec((nb, q, MIX_W), lambda b, c: (b, c, 0)),
        out_shape=jax.ShapeDtypeStruct((bsz, seq, MIX_W), BF16),
        scratch_shapes=[
            pltpu.VMEM((nb, HALO, M2_CONV_DIM), F32),
            pltpu.VMEM((nb, M2_GROUPS, M2_STATE, (N_HEADS // M2_GROUPS) * HEAD_W), F32),
        ],
        compiler_params=_params("parallel", "arbitrary"),
    )(proj3, proj3, proj3, rep, conv_w, conv_b.reshape(1, -1), _per_head(dt_bias), _per_head(a_log),
      _per_head(d), norm_g.reshape(1, -1))
    return out.reshape(bsz * seq, MIX_W)


def _rw_body(p_ref, pl_ref, mu_ref, mul_ref, w0_ref, w2_ref, a0_ref, a2_ref, g2_ref, kk_ref, ka_ref, rk_ref,
             lng_ref, lnb_ref, ones_ref, o_ref, prev_ref, prevl_ref, st_ref):
    nb, q, _ = p_ref.shape
    rows = nb * q

    @pl.when(pl.program_id(1) == 0)
    def _():
        prev_ref[...] = jnp.zeros_like(prev_ref)
        prevl_ref[...] = jnp.zeros_like(prevl_ref)
        st_ref[...] = jnp.zeros_like(st_ref)

    def token_mix(x_ref, last_ref, m_ref):
        x = x_ref[...].reshape(rows, x_ref.shape[2])
        row = lax.broadcasted_iota(jnp.int32, x.shape, 0)
        shifted = pltpu.roll(x, 1, axis=0)
        for b in range(nb):
            shifted = jnp.where(row == b * q, last_ref[b, 0:1, :], shifted)
            last_ref[b, 0:1, :] = x[(b + 1) * q - 1:(b + 1) * q, :]
        return x + (shifted - x) * m_ref[...]

    p = token_mix(p_ref, prev_ref, mu_ref)
    lora = token_mix(pl_ref, prevl_ref, mul_ref)
    r = p[:, 0:MIX_W]
    k = p[:, MIX_W:2 * MIX_W]
    v = p[:, 2 * MIX_W:3 * MIX_W]
    w = -jax.nn.softplus(-(w0_ref[...] + _dot(jnp.tanh(lora).astype(BF16), w2_ref[...]))) - 0.5
    log_w = -jnp.exp(w)
    a_sig = jax.nn.sigmoid(a0_ref[...] + _dot(lora.astype(BF16), a2_ref[...]))
    gate = _dot(jax.nn.sigmoid(lora).astype(BF16), g2_ref[...])
    ones = ones_ref[...]
    kk = k * kk_ref[...]
    k = k * (1.0 + (a_sig - 1.0) * ka_ref[...])
    kk_sq, rk_sum = _exact_right([kk * kk, r * k * rk_ref[...]], ones)
    kk = kk / jnp.maximum(jnp.sqrt(kk_sq), 1e-12)
    a_vec = -kk
    b_vec = kk * a_sig

    rr = lax.broadcasted_iota(jnp.int32, (rows, rows), 0)
    cc = lax.broadcasted_iota(jnp.int32, (rows, rows), 1)
    cum_mask = jnp.where((rr >= cc) & ((rr // q) == (cc // q)), 1.0, 0.0).astype(BF16)
    gam = _exact_left(cum_mask, log_w)
    g_last = jnp.concatenate(
        [jnp.broadcast_to(gam[(b + 1) * q - 1:(b + 1) * q, :], (q, MIX_W)) for b in range(nb)], axis=0)
    e_neg = jnp.exp(-gam)
    e_last = jnp.exp(g_last)
    e_rem = e_last * e_neg
    a_t = a_vec * jnp.exp(gam - log_w)
    r_t = r * jnp.exp(gam)
    b_t = b_vec * e_neg
    k_t = k * e_neg
    b_h = b_vec * e_rem
    k_h = k * e_rem
    assert q == HEAD_W
    row2 = lax.broadcasted_iota(jnp.int32, (q, 2 * HEAD_W), 0)
    lane2 = lax.broadcasted_iota(jnp.int32, (q, 2 * HEAD_W), 1)
    low = lane2 < HEAD_W
    col2 = jnp.where(low, lane2, lane2 - HEAD_W)
    keep_z0 = low | (col2 < row2)
    keep_t = col2 <= row2
    strict = _tril_mask(q, strict=True)
    pairs = [(b, h) for b in range(nb) for h in range(N_HEADS)]
    zeros_w = jnp.zeros((HEAD_W, 2 * HEAD_W), BF16)
    zeros_x = jnp.zeros((q, HEAD_W), F32)

    def blk(t, key):
        b, h = key
        return t[b * q:(b + 1) * q, h * HEAD_W:(h + 1) * HEAD_W]

    p0, p1, s0, z = {}, {}, {}, {}
    for key in pairs:
        ar = jnp.concatenate([blk(a_t, key), blk(r_t, key)], axis=0).astype(BF16)
        s0[key] = st_ref[key[0], key[1]]
        rhs = jnp.concatenate([s0[key], blk(b_t, key), blk(k_t, key), zeros_x], axis=0).astype(BF16)
        prod = _dot_nt(ar, rhs)
        p0[key] = prod[:, :2 * HEAD_W]
        p1[key] = prod[:, 2 * HEAD_W:]
    for key in pairs:
        a_k = jnp.where(strict, p1[key][:q, :HEAD_W], 0.0).astype(BF16)
        x0 = _dot(a_k, blk(v, key).astype(BF16))
        z[key] = jnp.where(keep_z0, p0[key][:q], 0.0) + jnp.concatenate([x0, zeros_x], axis=1)
    steps = max(1, (q - 1).bit_length())
    for _ in range(steps - 1):
        for key in pairs:
            zb = z[key].astype(BF16)
            z[key] = jnp.where(low, z[key], 0.0) + _dot(zb, jnp.concatenate([zeros_w, zb], axis=0))
    u = {}
    for key in pairs:
        zb = z[key].astype(BF16)
        u[key] = z[key][:, :HEAD_W] + _dot(zb, jnp.concatenate([zeros_w[:, :HEAD_W], zb[:, :HEAD_W]], axis=0))
    ys = {}
    for key in pairs:
        vu = jnp.concatenate([blk(v, key), u[key]], axis=0).astype(BF16)
        t_mat = jnp.where(keep_t, jnp.where(low, p1[key][q:], p0[key][q:]), 0.0)
        ys[key] = p0[key][q:, :HEAD_W] + _dot(t_mat.astype(BF16), vu)
        khb = jnp.concatenate([blk(k_h, key), blk(b_h, key)], axis=0).astype(BF16)
        st_ref[key[0], key[1]] = s0[key] * blk(e_last, key)[0:1, :] + _dot_tn(vu, khb)
    y = jnp.concatenate(
        [jnp.concatenate([ys[(b, h)] for h in range(N_HEADS)], axis=1) for b in range(nb)], axis=0)
    inv_n = 1.0 / HEAD_W
    yc = y - _exact_right([y], ones)[0] * inv_n
    var = _exact_right([yc * yc], ones)[0] * inv_n
    y = yc * lax.rsqrt(var + RW_LN_EPS) * lng_ref[...] + lnb_ref[...]
    bonus = rk_sum * v
    o_ref[...] = ((y + bonus) * gate).astype(o_ref.dtype).reshape(o_ref.shape)


def _pad_rows(w, rows, off):
    return jnp.zeros((rows, w.shape[1]), w.dtype).at[off:off + w.shape[0]].set(w)


def _rw_mixer(proj, mu, w0, w2, a0, a2, g2, k_k, k_a, r_k, ln_g, ln_b, bsz, seq):
    q = min(RW_Q, seq)
    nc = seq // q
    nb = RW_BATCH if bsz % RW_BATCH == 0 else 1
    vec = lambda t: t.reshape(1, MIX_W)
    const = lambda shape: pl.BlockSpec(shape, lambda b, c: (0, 0))
    offs = (0, RW_LORA_RANKS[0], RW_LORA_RANKS[0] + RW_LORA_RANKS[1])
    lora = lambda t, off: _pad_rows(t, RW_LORA_W, off).astype(BF16)
    mu_rkv, mu_lora = mu[:RW_RKV_W].reshape(1, RW_RKV_W), _pad_rows(mu[RW_RKV_W:, None], RW_LORA_W, 0).reshape(1, RW_LORA_W)
    ones = _block_diag(jnp.ones((N_HEADS, HEAD_W, HEAD_W), F32)).astype(BF16)
    proj3 = proj.reshape(bsz, seq, proj.shape[1])
    out = pl.pallas_call(
        _rw_body,
        name="rwkv7_mixer",
        grid=(bsz // nb, nc),
        in_specs=[
            pl.BlockSpec((nb, q, RW_RKV_W), lambda b, c: (b, c, PC_RKV // RW_RKV_W)),
            pl.BlockSpec((nb, q, RW_LORA_W), lambda b, c: (b, c, PC_LORA // RW_LORA_W)),
            const((1, RW_RKV_W)), const((1, RW_LORA_W)), const((1, MIX_W)), const((RW_LORA_W, MIX_W)),
            const((1, MIX_W)), const((RW_LORA_W, MIX_W)), const((RW_LORA_W, MIX_W)),
            const((1, MIX_W)), const((1, MIX_W)), const((1, MIX_W)),
            const((1, MIX_W)), const((1, MIX_W)), const((MIX_W, MIX_W)),
        ],
        out_specs=pl.BlockSpec((nb, q, MIX_W), lambda b, c: (b, c, 0)),
        out_shape=jax.ShapeDtypeStruct((bsz, seq, MIX_W), BF16),
        scratch_shapes=[pltpu.VMEM((nb, HALO, RW_RKV_W), F32), pltpu.VMEM((nb, HALO, RW_LORA_W), F32),
                        pltpu.VMEM((nb, N_HEADS, HEAD_W, HEAD_W), F32)],
        compiler_params=_params("parallel", "arbitrary"),
    )(proj3, proj3, mu_rkv, mu_lora, vec(w0), lora(w2, offs[0]), vec(a0), lora(a2, offs[1]), lora(g2, offs[2]),
      vec(k_k), vec(k_a), vec(r_k), vec(ln_g), vec(ln_b), ones)
    return out.reshape(bsz * seq, MIX_W)


def _merge_body(h_ref, *refs):
    nb = (len(refs) - 2) // 2
    y_refs, wg_refs, wp_ref, o_ref = refs[:nb], refs[nb:2 * nb], refs[2 * nb], refs[2 * nb + 1]
    h = h_ref[...]
    acc = None
    for kk in range(nb):
        term = jax.nn.sigmoid(_dot(h, wg_refs[kk][...])) * _dot(y_refs[kk][...], wp_ref[kk])
        acc = term if acc is None else acc + term
    o_ref[...] = acc.astype(o_ref.dtype)


def _merge(h, ys, w_gate, w_branch, layer=None, tm=512, tn=512):
    m = h.shape[0]
    tm = min(tm, m)
    nb = len(ys)
    y_spec = pl.BlockSpec((tm, MIX_W), lambda j, i: (i, 0))
    gate_specs = [_weight_spec(w_gate, layer, tn, lambda j, i, k=k: k * (D_MODEL // tn) + j) for k in range(nb)]
    if w_branch.ndim == 3:
        branch_spec = pl.BlockSpec((nb, MIX_W, tn), lambda j, i: (0, 0, j))
    else:
        branch_spec = pl.BlockSpec((None, nb, MIX_W, tn), lambda j, i: (layer, 0, 0, j))
    return pl.pallas_call(
        _merge_body,
        name="gated_merge",
        grid=(D_MODEL // tn, m // tm),
        in_specs=[pl.BlockSpec((tm, D_MODEL), lambda j, i: (i, 0))] + [y_spec] * nb + gate_specs + [branch_spec],
        out_specs=pl.BlockSpec((tm, tn), lambda j, i: (i, j)),
        out_shape=jax.ShapeDtypeStruct((m, D_MODEL), BF16),
        compiler_params=_params("parallel", "parallel"),
    )(h, *ys, *([w_gate] * nb), w_branch)


def _ffn_body(h_ref, hh_ref, wg_ref, wu_ref, cw_ref, cb_ref, o_ref, wg_s, wu_s, *, tiles_per_seq):
    t = h_ref.shape[0]
    nh = hh_ref.shape[0]

    @pl.when(pl.program_id(1) == 0)
    def _():
        wg_s[...] = wg_ref[...].astype(BF16)
        wu_s[...] = wu_ref[...].astype(BF16)

    h = h_ref[...]
    seq_start = (pl.program_id(1) % tiles_per_seq) == 0
    halo = jnp.where(seq_start, jnp.zeros_like(hh_ref[...]), hh_ref[...])
    g_all = _dot(jnp.concatenate([halo, h], axis=0), wg_s[...])
    k_w = cw_ref.shape[0]
    u = cb_ref[...] + g_all[nh:, :] * cw_ref[k_w - 1:k_w, :]
    for j in range(k_w - 1):
        lag = k_w - 1 - j
        u = u + g_all[nh - lag:nh - lag + t, :] * cw_ref[j:j + 1, :]
    o_ref[...] = (jax.nn.gelu(u) * _dot(h, wu_s[...])).astype(o_ref.dtype)


def _ffn_act(h, w_gate, w_up, conv_w, conv_b, seq, layer=None, tm=1024, tf=512):
    m = h.shape[0]
    tm = min(tm, seq)
    f = w_gate.shape[-1]
    halo_blocks = tm // BF16_ROWS
    return pl.pallas_call(
        functools.partial(_ffn_body, tiles_per_seq=seq // tm),
        name="ffn_gate_up",
        grid=(f // tf, m // tm),
        in_specs=[
            pl.BlockSpec((tm, D_MODEL), lambda j, i: (i, 0)),
            pl.BlockSpec((BF16_ROWS, D_MODEL), lambda j, i: (jnp.maximum(i * halo_blocks - 1, 0), 0)),
            _weight_spec(w_gate, layer, tf, lambda j, i: j),
            _weight_spec(w_up, layer, tf, lambda j, i: j),
            pl.BlockSpec((conv_w.shape[0], tf), lambda j, i: (0, j)),
            pl.BlockSpec((1, tf), lambda j, i: (0, j)),
        ],
        out_specs=pl.BlockSpec((tm, tf), lambda j, i: (i, j)),
        out_shape=jax.ShapeDtypeStruct((m, f), BF16),
        scratch_shapes=[pltpu.VMEM((D_MODEL, tf), BF16), pltpu.VMEM((D_MODEL, tf), BF16)],
        compiler_params=_params("parallel", "arbitrary"),
    )(h, h, w_gate, w_up, conv_w, conv_b.reshape(1, f))


def _w_in_body(w_ref, s5_ref, proj_ref, gate_ref):
    w = w_ref[...]
    rows = w.shape[0]
    col = 0
    piece = {}
    for name, width in (("s5_u", MIX_W), ("lru_x", MIX_W), ("lru_g", MIX_W), ("z", MIX_W), ("xbc", M2_CONV_DIM),
                        ("dt", N_HEADS), ("rkv", RW_RKV_W), ("lora", sum(RW_LORA_RANKS))):
        piece[name] = w[:, col:col + width]
        col += width
    pad = lambda t, n: jnp.concatenate([t, jnp.zeros((rows, n - t.shape[1]), t.dtype)], axis=1)
    s5_ref[...] = piece["s5_u"].astype(BF16)
    proj = [piece["xbc"], piece["z"], piece["rkv"], piece["lru_x"], piece["lru_g"],
            pad(piece["lora"], RW_LORA_W), pad(piece["dt"], PROJ_W - PC_DT)]
    proj_ref[...] = jnp.concatenate(proj, axis=1).astype(BF16)
    gate_ref[...] = w[:, col:].astype(BF16)


def _prepare_w_in(w_in, tr=128):
    depth, d, cols = w_in.shape
    gate_w = cols - (4 * MIX_W + M2_CONV_DIM + N_HEADS + RW_RKV_W + sum(RW_LORA_RANKS))
    out = lambda width: pl.BlockSpec((None, tr, width), lambda l, r: (l, r, 0))
    return pl.pallas_call(
        _w_in_body,
        name="w_in_layout",
        grid=(depth, d // tr),
        in_specs=[pl.BlockSpec((None, tr, cols), lambda l, r: (l, r, 0))],
        out_specs=[out(MIX_W), out(PROJ_W), out(gate_w)],
        out_shape=[jax.ShapeDtypeStruct((depth, d, MIX_W), BF16), jax.ShapeDtypeStruct((depth, d, PROJ_W), BF16),
                   jax.ShapeDtypeStruct((depth, d, gate_w), BF16)],
        compiler_params=_params("parallel", "parallel"),
    )(w_in)


def kernel(x, norm_mix_g, w_in, s5_lambda_re, s5_lambda_im, s5_b_re, s5_b_im, s5_c_re, s5_c_im, s5_d, s5_log_dt, s5_w_glu, lru_conv_w, lru_conv_b, lru_w_a, lru_b_a, lru_w_x, lru_b_x, lru_lambda, m2_conv_w, m2_conv_b, m2_dt_bias, m2_a_log, m2_d, m2_norm_g, rw_mu, rw_w0, rw_w2, rw_a0, rw_a2, rw_g2, rw_k_k, rw_k_a, rw_r_k, rw_ln_g, rw_ln_b, w_branch, w_out, norm_ffn_g, w_ffn_gate, w_ffn_up, ffn_conv_w, ffn_conv_b, w_ffn_down, final_norm_g):
    bsz, seq, d = x.shape
    depth = w_in.shape[0]
    xf = x.reshape(bsz * seq, d)
    wb_all, wo_all, wd_all = w_branch.astype(BF16), w_out.astype(BF16), w_ffn_down.astype(BF16)
    w_s5_all, w_proj_all, w_gate_all = _prepare_w_in(w_in)
    for l in range(depth):
        h = _rms_norm(xf, norm_mix_g[l], BF16)
        proj = _matmul(h, w_proj_all, tm=2048, layer=l)
        mats = _s5_matrices(s5_lambda_re[l], s5_lambda_im[l], s5_b_re[l], s5_b_im[l],
                            s5_c_re[l], s5_c_im[l], s5_d[l], s5_log_dt[l])
        y_a = _s5_mixer(h, w_s5_all, mats, s5_w_glu[l], bsz, seq, layer=l)
        y_b = _lru_mixer(proj, lru_conv_w[l], lru_conv_b[l], lru_w_a[l], lru_b_a[l],
                         lru_w_x[l], lru_b_x[l], lru_lambda[l], bsz, seq)
        y_c = _m2_mixer(proj, m2_conv_w[l], m2_conv_b[l], m2_dt_bias[l], m2_a_log[l],
                        m2_d[l], m2_norm_g[l], bsz, seq)
        y_d = _rw_mixer(proj, rw_mu[l], rw_w0[l], rw_w2[l], rw_a0[l], rw_a2[l], rw_g2[l],
                        rw_k_k[l], rw_k_a[l], rw_r_k[l].reshape(-1), rw_ln_g[l], rw_ln_b[l], bsz, seq)
        merged = _merge(h, (y_a, y_b, y_c, y_d), w_gate_all, wb_all, layer=l)
        xf, h = _matmul_res_norm(merged, wo_all, xf, norm_ffn_g[l], l)
        act = _ffn_act(h, w_ffn_gate, w_ffn_up, ffn_conv_w[l], ffn_conv_b[l], seq, layer=l)
        xf = _matmul(act, wd_all, res=xf, tm=1024, layer=l)
    return _rms_norm(xf, final_norm_g, F32).reshape(bsz, seq, d)
```

```python
import functools

import jax
import jax.numpy as jnp
from jax import lax
from jax.experimental import pallas as pl
from jax.experimental.pallas import tpu as pltpu

F32 = jnp.float32
BF16 = jnp.bfloat16

D_MODEL = 2048
MIX_W = 512
HEAD_W = 64
N_HEADS = MIX_W // HEAD_W
S5_GROUP = 16
S5_GROUPS = MIX_W // S5_GROUP
S5_STATE = 64
S5_Q = 8
S5_NSTATE = S5_GROUPS * S5_STATE
LANES = 128
S5_SG = MIX_W // LANES
S5_SGW = S5_Q * LANES
S5_SB = S5_NSTATE // S5_SG
S5_SCAN_ROWS = 512
LRU_C = 8.0
M2_GROUPS = 2
M2_STATE = 128
M2_CONV_DIM = MIX_W + 2 * M2_GROUPS * M2_STATE
M2_Q = 128
M2_BATCH = 2
RW_Q = 64
RW_BATCH = 4
RW_RKV_W = 3 * MIX_W
RW_LORA_RANKS = (32, 32, 96)
RW_LORA_W = 256
RW_LN_EPS = 64e-5
EPS = 1e-6
HALO = 8
BF16_ROWS = 16

PC_XBC = 0
PC_Z = 1024
PC_RKV = 1536
PC_LRU_X = 3072
PC_LRU_G = 3584
PC_LORA = 4096
PC_DT = 4352
PROJ_W = 4608

VMEM_LIMIT_BYTES = 50 * 1024 * 1024


def _params(*sem):
    return pltpu.CompilerParams(dimension_semantics=sem, vmem_limit_bytes=VMEM_LIMIT_BYTES)


def _dot(a, b):
    return jnp.dot(a, b, preferred_element_type=F32)


def _dot_nt(a, b):
    return lax.dot_general(a, b, (((1,), (1,)), ((), ())), preferred_element_type=F32)


def _dot_tn(a, b):
    return lax.dot_general(a, b, (((0,), (0,)), ((), ())), preferred_element_type=F32)


def _split(x, terms):
    out = []
    for _ in range(terms - 1):
        hi = x.astype(BF16)
        out.append(hi)
        x = x - hi.astype(F32)
    out.append(x.astype(BF16))
    return out


def _exact_left(m, x, terms=3):
    return sum(_dot(m, p) for p in _split(x, terms))


def _exact_right(xs, m, terms=2):
    n = xs[0].shape[0]
    stacked = jnp.concatenate([p for x in xs for p in _split(x, terms)], axis=0)
    out = _dot(stacked, m)
    return [sum(out[(i * terms + t) * n:(i * terms + t + 1) * n] for t in range(terms))
            for i in range(len(xs))]


def _tril_mask(n, strict=False):
    r = lax.broadcasted_iota(jnp.int32, (n, n), 0)
    c = lax.broadcasted_iota(jnp.int32, (n, n), 1)
    return (r > c) if strict else (r >= c)


def _causal_conv(x, halo, w_ref, b_ref):
    k_w = w_ref.shape[0]
    t = x.shape[0]
    xe = jnp.concatenate([halo, x], axis=0)
    out = b_ref[...] + x * w_ref[k_w - 1:k_w, :]
    for j in range(k_w - 1):
        lag = k_w - 1 - j
        out = out + xe[HALO - lag:HALO - lag + t, :] * w_ref[j:j + 1, :]
    return out


def _mm_body(a_ref, b_ref, o_ref):
    o_ref[...] = _dot(a_ref[...], b_ref[...]).astype(o_ref.dtype)


def _mm_res_body(a_ref, b_ref, r_ref, o_ref):
    o_ref[...] = (_dot(a_ref[...], b_ref[...]) + r_ref[...]).astype(o_ref.dtype)


def _weight_spec(w, layer, tn, col_of):
    k = w.shape[-2]
    if w.ndim == 2:
        return pl.BlockSpec((k, tn), lambda *g: (0, col_of(*g)))
    return pl.BlockSpec((None, k, tn), lambda *g: (layer, 0, col_of(*g)))


def _matmul(a, b, res=None, out_dtype=F32, tm=1024, tn=512, n_outer=False, layer=None):
    m, k = a.shape
    n = b.shape[-1]
    tm, tn = min(tm, m), min(tn, n)
    assert m % tm == 0 and n % tn == 0
    if n_outer:
        grid = (n // tn, m // tm)
        row = lambda j, i: (i, 0)
        col = _weight_spec(b, layer, tn, lambda j, i: j)
        out = lambda j, i: (i, j)
    else:
        grid = (m // tm, n // tn)
        row = lambda i, j: (i, 0)
        col = _weight_spec(b, layer, tn, lambda i, j: j)
        out = lambda i, j: (i, j)
    in_specs = [pl.BlockSpec((tm, k), row), col]
    args = [a, b]
    body = _mm_body
    if res is not None:
        in_specs.append(pl.BlockSpec((tm, tn), out))
        args.append(res)
        body = _mm_res_body
    return pl.pallas_call(
        body,
        name="matmul",
        grid=grid,
        in_specs=in_specs,
        out_specs=pl.BlockSpec((tm, tn), out),
        out_shape=jax.ShapeDtypeStruct((m, n), out_dtype),
        compiler_params=_params("parallel", "parallel"),
    )(*args)


def _matmul_blockdiag(a, w, groups, res=None, res_block=None, tm=1024):
    m = a.shape[0]
    k, n = a.shape[1] // groups, w.shape[1]
    tm = min(tm, m)
    out_spec = pl.BlockSpec((tm, n), lambda i, s: (i, s))
    in_specs = [pl.BlockSpec((tm, k), lambda i, s: (i, s)), pl.BlockSpec((k, n), lambda i, s: (s, 0))]
    args = [a, w]
    if res is not None:
        in_specs.append(pl.BlockSpec((tm, n), lambda i, s: (i, res_block(s))))
        args.append(res)
    return pl.pallas_call(
        _mm_body if res is None else _mm_res_body,
        name="matmul_blockdiag",
        grid=(m // tm, groups),
        in_specs=in_specs,
        out_specs=out_spec,
        out_shape=jax.ShapeDtypeStruct((m, groups * n), F32),
        compiler_params=_params("parallel", "parallel"),
    )(*args)


def _norm_body(x_ref, g_ref, o_ref):
    x = x_ref[...]
    ms = jnp.mean(x * x, axis=-1, keepdims=True)
    o_ref[...] = (x * lax.rsqrt(ms + EPS) * g_ref[...]).astype(o_ref.dtype)


def _mm_res_norm_body(a_ref, w_ref, r_ref, g_ref, x_ref, h_ref):
    x = _dot(a_ref[...], w_ref[...]) + r_ref[...]
    x_ref[...] = x
    ms = jnp.mean(x * x, axis=-1, keepdims=True)
    h_ref[...] = (x * lax.rsqrt(ms + EPS) * g_ref[...]).astype(h_ref.dtype)


def _matmul_res_norm(a, w, res, g, layer, tm=512):
    m, k = a.shape
    n = w.shape[-1]
    tm = min(tm, m)
    rows = lambda width: pl.BlockSpec((tm, width), lambda i: (i, 0))
    return pl.pallas_call(
        _mm_res_norm_body,
        name="matmul_residual_norm",
        grid=(m // tm,),
        in_specs=[rows(k), _weight_spec(w, layer, n, lambda i: 0), rows(n), pl.BlockSpec((1, n), lambda i: (0, 0))],
        out_specs=[rows(n), rows(n)],
        out_shape=[jax.ShapeDtypeStruct((m, n), F32), jax.ShapeDtypeStruct((m, n), BF16)],
        compiler_params=_params("parallel"),
    )(a, w, res, g.reshape(1, n))


def _rms_norm(x, g, out_dtype, tm=512):
    m, d = x.shape
    tm = min(tm, m)
    return pl.pallas_call(
        _norm_body,
        name="rms_norm",
        grid=(m // tm,),
        in_specs=[pl.BlockSpec((tm, d), lambda i: (i, 0)), pl.BlockSpec((1, d), lambda i: (0, 0))],
        out_specs=pl.BlockSpec((tm, d), lambda i: (i, 0)),
        out_shape=jax.ShapeDtypeStruct((m, d), out_dtype),
        compiler_params=_params("parallel"),
    )(x, g.reshape(1, d))


def _s5_matrices(lam_re, lam_im, b_re, b_im, c_re, c_im, d, log_dt):
    g_n, p_n, c_n, q = S5_GROUPS, S5_STATE, S5_GROUP, S5_Q
    hp = lax.Precision.HIGHEST
    dt = jnp.exp(log_dt)[:, None]
    n = jnp.arange(q + 1, dtype=F32)[:, None, None]
    mag = jnp.exp(n * (lam_re * dt))
    pw_re = mag * jnp.cos(n * (lam_im * dt))
    pw_im = mag * jnp.sin(n * (lam_im * dt))
    den = lam_re * lam_re + lam_im * lam_im
    nr, ni = pw_re[1] - 1.0, pw_im[1]
    f_re = (nr * lam_re + ni * lam_im) / den
    f_im = (ni * lam_re - nr * lam_im) / den
    e_re = f_re[..., None] * b_re - f_im[..., None] * b_im
    e_im = f_re[..., None] * b_im + f_im[..., None] * b_re
    cp_re = c_re[None] * pw_re[:, :, None, :] - c_im[None] * pw_im[:, :, None, :]
    cp_im = c_re[None] * pw_im[:, :, None, :] + c_im[None] * pw_re[:, :, None, :]
    kern = (jnp.einsum("tgop,gpi->tgio", cp_re[:q], e_re, precision=hp)
            - jnp.einsum("tgop,gpi->tgio", cp_im[:q], e_im, precision=hp))
    kern = kern.at[0].add(d.reshape(g_n, c_n)[:, :, None] * jnp.eye(c_n, dtype=F32))
    sg_n, gl_n = S5_SG, g_n // S5_SG
    rows = sg_n * S5_SGW
    lag = jnp.arange(q)[None, :] - jnp.arange(q)[:, None]
    kt = jnp.where((lag >= 0)[:, :, None, None, None], kern[jnp.clip(lag, 0, q - 1)], 0.0)
    kt = kt.reshape(q, q, sg_n, gl_n, c_n, c_n).transpose(2, 0, 3, 4, 1, 5)
    kt = kt.reshape(rows, q * c_n)
    rev_re, rev_im = pw_re[q - 1 - jnp.arange(q)], pw_im[q - 1 - jnp.arange(q)]
    et_re, et_im = e_re.transpose(0, 2, 1)[None], e_im.transpose(0, 2, 1)[None]
    ws_re = rev_re[:, :, None, :] * et_re - rev_im[:, :, None, :] * et_im
    ws_im = rev_re[:, :, None, :] * et_im + rev_im[:, :, None, :] * et_re
    ws = jnp.stack([ws_re, ws_im], axis=3)
    ws = ws.reshape(q, sg_n, gl_n, c_n, 2, p_n).transpose(1, 0, 2, 3, 4, 5).reshape(rows, 2 * p_n)
    wy = jnp.stack([cp_re[1:], -cp_im[1:]], axis=0)
    wy = wy.reshape(2, q, sg_n, gl_n, c_n, p_n).transpose(2, 0, 3, 5, 1, 4)
    wy = wy.reshape(rows, q * c_n)

    col = jnp.arange(S5_SGW)
    small = jnp.arange(q * c_n)[:, None]
    tok_rep = ((small // c_n == col[None, :] // LANES) & (small % c_n == col[None, :] % c_n)).astype(BF16)
    small = jnp.arange(2 * p_n)[:, None]
    st_rep = ((small // p_n == col[None, :] // S5_SB) & (small % p_n == col[None, :] % p_n)).astype(BF16)
    row = jnp.arange(rows)
    g_tok_r, g_tok_c = (row % LANES) // c_n, (col % LANES) // c_n
    g_st_r, g_st_c = (row % S5_SB) // p_n, (col % S5_SB) // p_n
    expand = lambda table, rep: jnp.dot(table.astype(BF16), rep, preferred_element_type=BF16)
    zero = jnp.zeros((), BF16)
    toep = jnp.where(g_tok_r[:, None] == g_tok_c[None, :], expand(kt, tok_rep), zero)
    w_state = jnp.where(g_tok_r[:, None] == g_st_c[None, :], expand(ws, st_rep), zero)
    w_out = jnp.where(g_st_r[:, None] == g_tok_c[None, :], expand(wy, tok_rep), zero)
    return toep, w_state, w_out, pw_re[q].reshape(1, S5_NSTATE), pw_im[q].reshape(1, S5_NSTATE)


def _s5_scan_body(ar_ref, ai_ref, s_ref, o_ref, c_ref):
    rows = BF16_ROWS
    nb = s_ref.shape[0]
    a_re, a_im = ar_ref[...], ai_ref[...]
    half = a_re.shape[1]

    @pl.when(pl.program_id(1) == 0)
    def _():
        c_ref[...] = jnp.zeros_like(c_ref)

    def block(i, carry):
        base = pl.multiple_of(i * rows, rows)
        xs = [s_ref[b, pl.ds(base, rows), :] for b in range(nb)]
        outs = [[] for _ in range(nb)]
        carry = list(carry)
        for r in range(rows):
            for b in range(nb):
                s_re, s_im = carry[b]
                outs[b].append(jnp.concatenate([s_re, s_im], axis=1))
                carry[b] = (a_re * s_re - a_im * s_im + xs[b][r:r + 1, :half],
                            a_re * s_im + a_im * s_re + xs[b][r:r + 1, half:])
        for b in range(nb):
            o_ref[b, pl.ds(base, rows), :] = jnp.concatenate(outs[b], axis=0).astype(o_ref.dtype)
        return tuple(carry)

    init = tuple((c_ref[b, :, :half], c_ref[b, :, half:]) for b in range(nb))
    last = lax.fori_loop(0, s_ref.shape[1] // rows, block, init)
    for b in range(nb):
        c_ref[b] = jnp.concatenate(last[b], axis=1)


def _s5_fold_body(h_ref, w_ref, o_ref, tok_ref):
    u = _dot(h_ref[...], w_ref[...])
    tc = o_ref.shape[0]
    for c in range(S5_SG):
        tok_ref[c] = u[:, c * LANES:(c + 1) * LANES]
        for j in range(S5_Q):
            lo = c * S5_SGW + j * LANES
            o_ref[:, lo:lo + LANES] = tok_ref[c, pl.ds(j, tc, stride=S5_Q), :].astype(o_ref.dtype)


def _s5_glu_body(y_ref, w_ref, o_ref, tok_ref):
    tc = y_ref.shape[0]
    for c in range(S5_SG):
        for j in range(S5_Q):
            lo = c * S5_SGW + j * LANES
            tok_ref[c, pl.ds(j, tc, stride=S5_Q), :] = y_ref[:, lo:lo + LANES]
    y = jax.nn.gelu(jnp.concatenate([tok_ref[c] for c in range(S5_SG)], axis=1))
    o_ref[...] = (y * jax.nn.sigmoid(_dot(y.astype(BF16), w_ref[...]))).astype(o_ref.dtype)


def _s5_mixer(h, w_u, mats, w_glu, bsz, seq, layer=None):
    toep, w_state, w_out, aq_re, aq_im = mats
    m = bsz * seq
    nc = seq // S5_Q
    width = S5_Q * MIX_W
    tm = min(1024, m)
    tc = tm // S5_Q
    uc = pl.pallas_call(
        _s5_fold_body,
        name="s5_input_proj",
        grid=(m // tm,),
        in_specs=[pl.BlockSpec((tm, D_MODEL), lambda i: (i, 0)), _weight_spec(w_u, layer, MIX_W, lambda i: 0)],
        out_specs=pl.BlockSpec((tc, width), lambda i: (i, 0)),
        out_shape=jax.ShapeDtypeStruct((m // S5_Q, width), BF16),
        scratch_shapes=[pltpu.VMEM((MIX_W // LANES, tm, LANES), F32)],
        compiler_params=_params("parallel"),
    )(h, w_u)
    y1s = _matmul_blockdiag(uc, jnp.concatenate([toep, w_state], axis=1), S5_SG)
    rb = min(S5_SCAN_ROWS, nc)
    s_in = pl.pallas_call(
        _s5_scan_body,
        name="s5_chunk_scan",
        grid=(S5_SG, nc // rb),
        in_specs=[
            pl.BlockSpec((1, S5_SB), lambda j, r: (0, j)),
            pl.BlockSpec((1, S5_SB), lambda j, r: (0, j)),
            pl.BlockSpec((bsz, rb, S5_SGW), lambda j, r: (0, r, 2 * j + 1)),
        ],
        out_specs=pl.BlockSpec((bsz, rb, S5_SGW), lambda j, r: (0, r, j)),
        out_shape=jax.ShapeDtypeStruct((bsz, nc, width), BF16),
        scratch_shapes=[pltpu.VMEM((bsz, 1, S5_SGW), F32)],
        compiler_params=_params("parallel", "arbitrary"),
    )(aq_re, aq_im, y1s.reshape(bsz, nc, 2 * width))
    y = _matmul_blockdiag(s_in.reshape(bsz * nc, width), w_out, S5_SG, res=y1s, res_block=lambda s: 2 * s)
    return pl.pallas_call(
        _s5_glu_body,
        name="s5_glu",
        grid=(m // tm,),
        in_specs=[pl.BlockSpec((tc, width), lambda i: (i, 0)), pl.BlockSpec((MIX_W, MIX_W), lambda i: (0, 0))],
        out_specs=pl.BlockSpec((tm, MIX_W), lambda i: (i, 0)),
        out_shape=jax.ShapeDtypeStruct((m, MIX_W), BF16),
        scratch_shapes=[pltpu.VMEM((MIX_W // LANES, tm, LANES), F32)],
        compiler_params=_params("parallel"),
    )(y, w_glu.astype(BF16))


def _lru_body(x_ref, g_ref, cw_ref, cb_ref, wa_ref, ba_ref, wx_ref, bx_ref, lam_ref, o_ref,
              halo_ref, h_ref, a_s, b_s):
    nb, t, _ = x_ref.shape

    @pl.when(pl.program_id(0) == 0)
    def _():
        halo_ref[...] = jnp.zeros_like(halo_ref)
        h_ref[...] = jnp.zeros_like(h_ref)

    soft = jax.nn.softplus(-lam_ref[...])
    for b in range(nb):
        x_in = x_ref[b]
        x = _causal_conv(x_in, halo_ref[b], cw_ref, cb_ref)
        halo_ref[b] = x_in[t - HALO:, :]
        xb = x.astype(BF16)
        r = jax.nn.sigmoid(_dot(xb, wa_ref[...]) + ba_ref[...])
        i = jax.nn.sigmoid(_dot(xb, wx_ref[...]) + bx_ref[...])
        log_a = (-LRU_C * r) * soft
        a = jnp.exp(log_a)
        a_s[b] = a
        b_s[b] = x * i * jnp.sqrt(1.0 - a * a)
    rows = 8

    def block(k, hs):
        base = pl.multiple_of(k * rows, rows)
        av = [a_s[b, pl.ds(base, rows), :] for b in range(nb)]
        bv = [b_s[b, pl.ds(base, rows), :] for b in range(nb)]
        hs = list(hs)
        outs = [[] for _ in range(nb)]
        for rr in range(rows):
            for b in range(nb):
                hs[b] = av[b][rr:rr + 1, :] * hs[b] + bv[b][rr:rr + 1, :]
                outs[b].append(hs[b])
        for b in range(nb):
            b_s[b, pl.ds(base, rows), :] = jnp.concatenate(outs[b], axis=0)
        return tuple(hs)

    hs = lax.fori_loop(0, t // rows, block, tuple(h_ref[b] for b in range(nb)))
    for b in range(nb):
        h_ref[b] = hs[b]
        o_ref[b] = (b_s[b] * jax.nn.gelu(g_ref[b])).astype(o_ref.dtype)


def _block_diag(w):
    h_n, n, _ = w.shape
    eye = jnp.eye(h_n, dtype=w.dtype)
    return (w[:, :, None, :] * eye[:, None, :, None]).reshape(h_n * n, h_n * n)


def _lru_mixer(proj, conv_w, conv_b, w_a, b_a, w_x, b_x, lam, bsz, seq):
    t = min(256, seq)
    vec = lambda v: v.reshape(1, MIX_W)
    const = lambda shape: pl.BlockSpec(shape, lambda c: (0, 0))
    proj3 = proj.reshape(bsz, seq, proj.shape[1])
    out = pl.pallas_call(
        _lru_body,
        name="rglru_mixer",
        grid=(seq // t,),
        in_specs=[
            pl.BlockSpec((bsz, t, MIX_W), lambda c: (0, c, PC_LRU_X // MIX_W)),
            pl.BlockSpec((bsz, t, MIX_W), lambda c: (0, c, PC_LRU_G // MIX_W)),
            const(conv_w.shape), const((1, MIX_W)),
            const((MIX_W, MIX_W)), const((1, MIX_W)),
            const((MIX_W, MIX_W)), const((1, MIX_W)), const((1, MIX_W)),
        ],
        out_specs=pl.BlockSpec((bsz, t, MIX_W), lambda c: (0, c, 0)),
        out_shape=jax.ShapeDtypeStruct((bsz, seq, MIX_W), BF16),
        scratch_shapes=[
            pltpu.VMEM((bsz, HALO, MIX_W), F32), pltpu.VMEM((bsz, 1, MIX_W), F32),
            pltpu.VMEM((bsz, t, MIX_W), F32), pltpu.VMEM((bsz, t, MIX_W), F32),
        ],
        compiler_params=_params("arbitrary"),
    )(proj3, proj3, conv_w, vec(conv_b), _block_diag(w_a).astype(BF16), vec(b_a),
      _block_diag(w_x).astype(BF16), vec(b_x), vec(lam))
    return out.reshape(bsz * seq, MIX_W)


def _m2_body(z_ref, xbc_ref, dt_ref, rep_ref, cw_ref, cb_ref, dtb_ref, alog_ref, d_ref, ng_ref, o_ref,
             halo_ref, st_ref):
    nb, q, _ = z_ref.shape
    hg = N_HEADS // M2_GROUPS
    gw = hg * HEAD_W
    seqs = range(nb)

    @pl.when(pl.program_id(1) == 0)
    def _():
        halo_ref[...] = jnp.zeros_like(halo_ref)
        st_ref[...] = jnp.zeros_like(st_ref)

    causal = _tril_mask(q)
    causal_b = causal.astype(BF16)
    neg_a = -jnp.exp(alog_ref[...])
    dt_raw = _exact_right([dt_ref[b] for b in seqs], rep_ref[...])
    xc = []
    for b in seqs:
        xbc = xbc_ref[b]
        conv = _causal_conv(xbc, halo_ref[b], cw_ref, cb_ref)
        halo_ref[b] = xbc[q - HALO:, :]
        xc.append(conv * jax.nn.sigmoid(conv))
    dt = [jax.nn.softplus(dt_raw[b] + dtb_ref[...]) for b in seqs]
    a_cs = [_exact_left(causal_b, dt[b] * neg_a) for b in seqs]
    xs = [xc[b][:, :MIX_W] for b in seqs]
    xd = [xs[b] * dt[b] for b in seqs]
    a_cs_t = [a_cs[b].T for b in seqs]
    a_last = [a_cs[b][q - 1:q, :] for b in seqs]
    xd_st = [(xd[b] * jnp.exp(a_last[b] - a_cs[b])).astype(BF16) for b in seqs]
    e_cs = [jnp.exp(a_cs[b]) for b in seqs]
    bg, cg, cb, y_off = {}, {}, {}, {}
    for b in seqs:
        for g in range(M2_GROUPS):
            lo = MIX_W + g * M2_STATE
            bg[b, g] = xc[b][:, lo:lo + M2_STATE].astype(BF16)
            cg[b, g] = xc[b][:, lo + M2_GROUPS * M2_STATE:lo + (M2_GROUPS + 1) * M2_STATE].astype(BF16)
            cb[b, g] = _dot_nt(cg[b, g], bg[b, g])
            y_off[b, g] = _dot(cg[b, g], st_ref[b, g].astype(BF16)) * e_cs[b][:, g * gw:(g + 1) * gw]
    ys = [[] for _ in seqs]
    for g in range(M2_GROUPS):
        for hh in range(hg):
            for b in seqs:
                lo = (g * hg + hh) * HEAD_W
                col = a_cs[b][:, lo:lo + HEAD_W]
                col = jnp.concatenate([col] * (q // HEAD_W), axis=1)
                seg = col - a_cs_t[b][lo:lo + 1, :]
                dec = jnp.exp(jnp.where(causal, seg, -jnp.inf))
                y_d = _dot((cb[b, g] * dec).astype(BF16), xd[b][:, lo:lo + HEAD_W].astype(BF16))
                ys[b].append(y_d + y_off[b, g][:, hh * HEAD_W:(hh + 1) * HEAD_W])
    for b in seqs:
        for g in range(M2_GROUPS):
            upd = _dot_tn(bg[b, g], xd_st[b][:, g * gw:(g + 1) * gw])
            st_ref[b, g] = st_ref[b, g] * jnp.exp(a_last[b][:, g * gw:(g + 1) * gw]) + upd
    for b in seqs:
        y = jnp.concatenate(ys[b], axis=1) + d_ref[...] * xs[b]
        z = z_ref[b]
        y = y * (z * jax.nn.sigmoid(z))
        ms = jnp.mean(y * y, axis=-1, keepdims=True)
        o_ref[b] = (y * lax.rsqrt(ms + EPS) * ng_ref[...]).astype(o_ref.dtype)


def _per_head(v):
    return jnp.repeat(v, HEAD_W).reshape(1, -1)


def _m2_mixer(proj, conv_w, conv_b, dt_bias, a_log, d, norm_g, bsz, seq):
    q = min(M2_Q, seq)
    nc = seq // q
    nb = M2_BATCH if bsz % M2_BATCH == 0 else 1
    const = lambda shape: pl.BlockSpec(shape, lambda b, c: (0, 0))
    head_of_lane = jnp.arange(MIX_W)[None, :] // HEAD_W
    rep = (jnp.arange(LANES)[:, None] == head_of_lane).astype(BF16)
    proj3 = proj.reshape(bsz, seq, proj.shape[1])
    out = pl.pallas_call(
        _m2_body,
        name="mamba2_mixer",
        grid=(bsz // nb, nc),
        in_specs=[
            pl.BlockSpec((nb, q, MIX_W), lambda b, c: (b, c, PC_Z // MIX_W)),
            pl.BlockSpec((nb, q, M2_CONV_DIM), lambda b, c: (b, c, PC_XBC // M2_CONV_DIM)),
            pl.BlockSpec((nb, q, LANES), lambda b, c: (b, c, PC_DT // LANES)),
            const((LANES, MIX_W)), const(conv_w.shape), const((1, M2_CONV_DIM)),
            const((1, MIX_W)), const((1, MIX_W)), const((1, MIX_W)), const((1, MIX_W)),
        ],
        out_specs=pl.BlockSpec((nb, q, MIX_W), lambda b, c: (b, c, 0)),
        out_shape=jax.ShapeDtypeStruct((bsz, seq, MIX_W), BF16),
        scratch_shapes=[
            pltpu.VMEM((nb, HALO, M2_CONV_DIM), F32),
            pltpu.VMEM((nb, M2_GROUPS, M2_STATE, (N_HEADS // M2_GROUPS) * HEAD_W), F32),
        ],
        compiler_params=_params("parallel", "arbitrary"),
    )(proj3, proj3, proj3, rep, conv_w, conv_b.reshape(1, -1), _per_head(dt_bias), _per_head(a_log),
      _per_head(d), norm_g.reshape(1, -1))
    return out.reshape(bsz * seq, MIX_W)


def _rw_body(p_ref, pl_ref, mu_ref, mul_ref, w0_ref, w2_ref, a0_ref, a2_ref, g2_ref, kk_ref, ka_ref, rk_ref,
             lng_ref, lnb_ref, ones_ref, o_ref, prev_ref, prevl_ref, st_ref):
    nb, q, _ = p_ref.shape
    rows = nb * q

    @pl.when(pl.program_id(1) == 0)
    def _():
        prev_ref[...] = jnp.zeros_like(prev_ref)
        prevl_ref[...] = jnp.zeros_like(prevl_ref)
        st_ref[...] = jnp.zeros_like(st_ref)

    def token_mix(x_ref, last_ref, m_ref):
        x = x_ref[...].reshape(rows, x_ref.shape[2])
        row = lax.broadcasted_iota(jnp.int32, x.shape, 0)
        shifted = pltpu.roll(x, 1, axis=0)
        for b in range(nb):
            shifted = jnp.where(row == b * q, last_ref[b, 0:1, :], shifted)
            last_ref[b, 0:1, :] = x[(b + 1) * q - 1:(b + 1) * q, :]
        return x + (shifted - x) * m_ref[...]

    p = token_mix(p_ref, prev_ref, mu_ref)
    lora = token_mix(pl_ref, prevl_ref, mul_ref)
    r = p[:, 0:MIX_W]
    k = p[:, MIX_W:2 * MIX_W]
    v = p[:, 2 * MIX_W:3 * MIX_W]
    w = -jax.nn.softplus(-(w0_ref[...] + _dot(jnp.tanh(lora).astype(BF16), w2_ref[...]))) - 0.5
    log_w = -jnp.exp(w)
    a_sig = jax.nn.sigmoid(a0_ref[...] + _dot(lora.astype(BF16), a2_ref[...]))
    gate = _dot(jax.nn.sigmoid(lora).astype(BF16), g2_ref[...])
    ones = ones_ref[...]
    kk = k * kk_ref[...]
    k = k * (1.0 + (a_sig - 1.0) * ka_ref[...])
    kk_sq, rk_sum = _exact_right([kk * kk, r * k * rk_ref[...]], ones)
    kk = kk / jnp.maximum(jnp.sqrt(kk_sq), 1e-12)
    a_vec = -kk
    b_vec = kk * a_sig

    rr = lax.broadcasted_iota(jnp.int32, (rows, rows), 0)
    cc = lax.broadcasted_iota(jnp.int32, (rows, rows), 1)
    cum_mask = jnp.where((rr >= cc) & ((rr // q) == (cc // q)), 1.0, 0.0).astype(BF16)
    gam = _exact_left(cum_mask, log_w)
    g_last = jnp.concatenate(
        [jnp.broadcast_to(gam[(b + 1) * q - 1:(b + 1) * q, :], (q, MIX_W)) for b in range(nb)], axis=0)
    e_neg = jnp.exp(-gam)
    e_last = jnp.exp(g_last)
    e_rem = e_last * e_neg
    a_t = a_vec * jnp.exp(gam - log_w)
    r_t = r * jnp.exp(gam)
    b_t = b_vec * e_neg
    k_t = k * e_neg
    b_h = b_vec * e_rem
    k_h = k * e_rem
    assert q == HEAD_W
    row2 = lax.broadcasted_iota(jnp.int32, (q, 2 * HEAD_W), 0)
    lane2 = lax.broadcasted_iota(jnp.int32, (q, 2 * HEAD_W), 1)
    low = lane2 < HEAD_W
    col2 = jnp.where(low, lane2, lane2 - HEAD_W)
    keep_z0 = low | (col2 < row2)
    keep_t = col2 <= row2
    strict = _tril_mask(q, strict=True)
    pairs = [(b, h) for b in range(nb) for h in range(N_HEADS)]
    zeros_w = jnp.zeros((HEAD_W, 2 * HEAD_W), BF16)
    zeros_x = jnp.zeros((q, HEAD_W), F32)

    def blk(t, key):
        b, h = key
        return t[b * q:(b + 1) * q, h * HEAD_W:(h + 1) * HEAD_W]

    p0, p1, s0, z = {}, {}, {}, {}
    for key in pairs:
        ar = jnp.concatenate([blk(a_t, key), blk(r_t, key)], axis=0).astype(BF16)
        s0[key] = st_ref[key[0], key[1]]
        rhs = jnp.concatenate([s0[key], blk(b_t, key), blk(k_t, key), zeros_x], axis=0).astype(BF16)
        prod = _dot_nt(ar, rhs)
        p0[key] = prod[:, :2 * HEAD_W]
        p1[key] = prod[:, 2 * HEAD_W:]
    for key in pairs:
        a_k = jnp.where(strict, p1[key][:q, :HEAD_W], 0.0).astype(BF16)
        x0 = _dot(a_k, blk(v, key).astype(BF16))
        z[key] = jnp.where(keep_z0, p0[key][:q], 0.0) + jnp.concatenate([x0, zeros_x], axis=1)
    steps = max(1, (q - 1).bit_length())
    for _ in range(steps - 1):
        for key in pairs:
            zb = z[key].astype(BF16)
            z[key] = jnp.where(low, z[key], 0.0) + _dot(zb, jnp.concatenate([zeros_w, zb], axis=0))
    u = {}
    for key in pairs:
        zb = z[key].astype(BF16)
        u[key] = z[key][:, :HEAD_W] + _dot(zb, jnp.concatenate([zeros_w[:, :HEAD_W], zb[:, :HEAD_W]], axis=0))
    ys = {}
    for key in pairs:
        vu = jnp.concatenate([blk(v, key), u[key]], axis=0).astype(BF16)
        t_mat = jnp.where(keep_t, jnp.where(low, p1[key][q:], p0[key][q:]), 0.0)
        ys[key] = p0[key][q:, :HEAD_W] + _dot(t_mat.astype(BF16), vu)
        khb = jnp.concatenate([blk(k_h, key), blk(b_h, key)], axis=0).astype(BF16)
        st_ref[key[0], key[1]] = s0[key] * blk(e_last, key)[0:1, :] + _dot_tn(vu, khb)
    y = jnp.concatenate(
        [jnp.concatenate([ys[(b, h)] for h in range(N_HEADS)], axis=1) for b in range(nb)], axis=0)
    inv_n = 1.0 / HEAD_W
    yc = y - _exact_right([y], ones)[0] * inv_n
    var = _exact_right([yc * yc], ones)[0] * inv_n
    y = yc * lax.rsqrt(var + RW_LN_EPS) * lng_ref[...] + lnb_ref[...]
    bonus = rk_sum * v
    o_ref[...] = ((y + bonus) * gate).astype(o_ref.dtype).reshape(o_ref.shape)


def _pad_rows(w, rows, off):
    return jnp.zeros((rows, w.shape[1]), w.dtype).at[off:off + w.shape[0]].set(w)


def _rw_mixer(proj, mu, w0, w2, a0, a2, g2, k_k, k_a, r_k, ln_g, ln_b, bsz, seq):
    q = min(RW_Q, seq)
    nc = seq // q
    nb = RW_BATCH if bsz % RW_BATCH == 0 else 1
    vec = lambda t: t.reshape(1, MIX_W)
    const = lambda shape: pl.BlockSpec(shape, lambda b, c: (0, 0))
    offs = (0, RW_LORA_RANKS[0], RW_LORA_RANKS[0] + RW_LORA_RANKS[1])
    lora = lambda t, off: _pad_rows(t, RW_LORA_W, off).astype(BF16)
    mu_rkv, mu_lora = mu[:RW_RKV_W].reshape(1, RW_RKV_W), _pad_rows(mu[RW_RKV_W:, None], RW_LORA_W, 0).reshape(1, RW_LORA_W)
    ones = _block_diag(jnp.ones((N_HEADS, HEAD_W, HEAD_W), F32)).astype(BF16)
    proj3 = proj.reshape(bsz, seq, proj.shape[1])
    out = pl.pallas_call(
        _rw_body,
        name="rwkv7_mixer",
        grid=(bsz // nb, nc),
        in_specs=[
            pl.BlockSpec((nb, q, RW_RKV_W), lambda b, c: (b, c, PC_RKV // RW_RKV_W)),
            pl.BlockSpec((nb, q, RW_LORA_W), lambda b, c: (b, c, PC_LORA // RW_LORA_W)),
            const((1, RW_RKV_W)), const((1, RW_LORA_W)), const((1, MIX_W)), const((RW_LORA_W, MIX_W)),
            const((1, MIX_W)), const((RW_LORA_W, MIX_W)), const((RW_LORA_W, MIX_W)),
            const((1, MIX_W)), const((1, MIX_W)), const((1, MIX_W)),
            const((1, MIX_W)), const((1, MIX_W)), const((MIX_W, MIX_W)),
        ],
        out_specs=pl.BlockSpec((nb, q, MIX_W), lambda b, c: (b, c, 0)),
        out_shape=jax.ShapeDtypeStruct((bsz, seq, MIX_W), BF16),
        scratch_shapes=[pltpu.VMEM((nb, HALO, RW_RKV_W), F32), pltpu.VMEM((nb, HALO, RW_LORA_W), F32),
                        pltpu.VMEM((nb, N_HEADS, HEAD_W, HEAD_W), F32)],
        compiler_params=_params("parallel", "arbitrary"),
    )(proj3, proj3, mu_rkv, mu_lora, vec(w0), lora(w2, offs[0]), vec(a0), lora(a2, offs[1]), lora(g2, offs[2]),
      vec(k_k), vec(k_a), vec(r_k), vec(ln_g), vec(ln_b), ones)
    return out.reshape(bsz * seq, MIX_W)


def _merge_body(h_ref, *refs):
    nb = (len(refs) - 2) // 2
    y_refs, wg_refs, wp_ref, o_ref = refs[:nb], refs[nb:2 * nb], refs[2 * nb], refs[2 * nb + 1]
    h = h_ref[...]
    acc = None
    for kk in range(nb):
        term = jax.nn.sigmoid(_dot(h, wg_refs[kk][...])) * _dot(y_refs[kk][...], wp_ref[kk])
        acc = term if acc is None else acc + term
    o_ref[...] = acc.astype(o_ref.dtype)


def _merge(h, ys, w_gate, w_branch, layer=None, tm=512, tn=512):
    m = h.shape[0]
    tm = min(tm, m)
    nb = len(ys)
    y_spec = pl.BlockSpec((tm, MIX_W), lambda j, i: (i, 0))
    gate_specs = [_weight_spec(w_gate, layer, tn, lambda j, i, k=k: k * (D_MODEL // tn) + j) for k in range(nb)]
    if w_branch.ndim == 3:
        branch_spec = pl.BlockSpec((nb, MIX_W, tn), lambda j, i: (0, 0, j))
    else:
        branch_spec = pl.BlockSpec((None, nb, MIX_W, tn), lambda j, i: (layer, 0, 0, j))
    return pl.pallas_call(
        _merge_body,
        name="gated_merge",
        grid=(D_MODEL // tn, m // tm),
        in_specs=[pl.BlockSpec((tm, D_MODEL), lambda j, i: (i, 0))] + [y_spec] * nb + gate_specs + [branch_spec],
        out_specs=pl.BlockSpec((tm, tn), lambda j, i: (i, j)),
        out_shape=jax.ShapeDtypeStruct((m, D_MODEL), BF16),
        compiler_params=_params("parallel", "parallel"),
    )(h, *ys, *([w_gate] * nb), w_branch)


def _ffn_body(h_ref, hh_ref, wg_ref, wu_ref, cw_ref, cb_ref, o_ref, wg_s, wu_s, *, tiles_per_seq):
    t = h_ref.shape[0]
    nh = hh_ref.shape[0]

    @pl.when(pl.program_id(1) == 0)
    def _():
        wg_s[...] = wg_ref[...].astype(BF16)
        wu_s[...] = wu_ref[...].astype(BF16)

    h = h_ref[...]
    seq_start = (pl.program_id(1) % tiles_per_seq) == 0
    halo = jnp.where(seq_start, jnp.zeros_like(hh_ref[...]), hh_ref[...])
    g_all = _dot(jnp.concatenate([halo, h], axis=0), wg_s[...])
    k_w = cw_ref.shape[0]
    u = cb_ref[...] + g_all[nh:, :] * cw_ref[k_w - 1:k_w, :]
    for j in range(k_w - 1):
        lag = k_w - 1 - j
        u = u + g_all[nh - lag:nh - lag + t, :] * cw_ref[j:j + 1, :]
    o_ref[...] = (jax.nn.gelu(u) * _dot(h, wu_s[...])).astype(o_ref.dtype)


def _ffn_act(h, w_gate, w_up, conv_w, conv_b, seq, layer=None, tm=1024, tf=512):
    m = h.shape[0]
    tm = min(tm, seq)
    f = w_gate.shape[-1]
    halo_blocks = tm // BF16_ROWS
    return pl.pallas_call(
        functools.partial(_ffn_body, tiles_per_seq=seq // tm),
        name="ffn_gate_up",
        grid=(f // tf, m // tm),
        in_specs=[
            pl.BlockSpec((tm, D_MODEL), lambda j, i: (i, 0)),
            pl.BlockSpec((BF16_ROWS, D_MODEL), lambda j, i: (jnp.maximum(i * halo_blocks - 1, 0), 0)),
            _weight_spec(w_gate, layer, tf, lambda j, i: j),
            _weight_spec(w_up, layer, tf, lambda j, i: j),
            pl.BlockSpec((conv_w.shape[0], tf), lambda j, i: (0, j)),
            pl.BlockSpec((1, tf), lambda j, i: (0, j)),
        ],
        out_specs=pl.BlockSpec((tm, tf), lambda j, i: (i, j)),
        out_shape=jax.ShapeDtypeStruct((m, f), BF16),
        scratch_shapes=[pltpu.VMEM((D_MODEL, tf), BF16), pltpu.VMEM((D_MODEL, tf), BF16)],
        compiler_params=_params("parallel", "arbitrary"),
    )(h, h, w_gate, w_up, conv_w, conv_b.reshape(1, f))


def _w_in_body(w_ref, s5_ref, proj_ref, gate_ref):
    w = w_ref[...]
    tk = w.shape[1]
    row = 0
    piece = {}
    for name, width in (("s5_u", MIX_W), ("lru_x", MIX_W), ("lru_g", MIX_W), ("z", MIX_W), ("xbc", M2_CONV_DIM),
                        ("dt", N_HEADS), ("rkv", RW_RKV_W), ("lora", sum(RW_LORA_RANKS))):
        piece[name] = w[row:row + width, :]
        row += width

    def put(ref, off, t, width):
        if t.shape[0] < width:
            t = jnp.concatenate([t, jnp.zeros((width - t.shape[0], tk), t.dtype)], axis=0)
        ref[:, off:off + width] = t.T.astype(ref.dtype)

    put(s5_ref, 0, piece["s5_u"], MIX_W)
    put(proj_ref, PC_XBC, piece["xbc"], M2_CONV_DIM)
    put(proj_ref, PC_Z, piece["z"], MIX_W)
    put(proj_ref, PC_RKV, piece["rkv"], RW_RKV_W)
    put(proj_ref, PC_LRU_X, piece["lru_x"], MIX_W)
    put(proj_ref, PC_LRU_G, piece["lru_g"], MIX_W)
    put(proj_ref, PC_LORA, piece["lora"], RW_LORA_W)
    put(proj_ref, PC_DT, piece["dt"], PROJ_W - PC_DT)
    put(gate_ref, 0, w[row:, :], w.shape[0] - row)


def _prepare_w_in(w_in, tk=128):
    depth, d, cols = w_in.shape
    gate_w = cols - (4 * MIX_W + M2_CONV_DIM + N_HEADS + RW_RKV_W + sum(RW_LORA_RANKS))
    out = lambda width: pl.BlockSpec((None, tk, width), lambda l, r: (l, r, 0))
    return pl.pallas_call(
        _w_in_body,
        name="w_in_layout",
        grid=(depth, d // tk),
        in_specs=[pl.BlockSpec((None, cols, tk), lambda l, r: (l, 0, r))],
        out_specs=[out(MIX_W), out(PROJ_W), out(gate_w)],
        out_shape=[jax.ShapeDtypeStruct((depth, d, MIX_W), BF16), jax.ShapeDtypeStruct((depth, d, PROJ_W), BF16),
                   jax.ShapeDtypeStruct((depth, d, gate_w), BF16)],
        compiler_params=_params("parallel", "parallel"),
    )(jnp.swapaxes(w_in, 1, 2))


def kernel(x, norm_mix_g, w_in, s5_lambda_re, s5_lambda_im, s5_b_re, s5_b_im, s5_c_re, s5_c_im, s5_d, s5_log_dt, s5_w_glu, lru_conv_w, lru_conv_b, lru_w_a, lru_b_a, lru_w_x, lru_b_x, lru_lambda, m2_conv_w, m2_conv_b, m2_dt_bias, m2_a_log, m2_d, m2_norm_g, rw_mu, rw_w0, rw_w2, rw_a0, rw_a2, rw_g2, rw_k_k, rw_k_a, rw_r_k, rw_ln_g, rw_ln_b, w_branch, w_out, norm_ffn_g, w_ffn_gate, w_ffn_up, ffn_conv_w, ffn_conv_b, w_ffn_down, final_norm_g):
    bsz, seq, d = x.shape
    depth = w_in.shape[0]
    xf = x.reshape(bsz * seq, d)
    wb_all, wo_all, wd_all = w_branch.astype(BF16), w_out.astype(BF16), w_ffn_down.astype(BF16)
    w_s5_all, w_proj_all, w_gate_all = _prepare_w_in(w_in)
    for l in range(depth):
        h = _rms_norm(xf, norm_mix_g[l], BF16)
        proj = _matmul(h, w_proj_all, tm=2048, layer=l)
        mats = _s5_matrices(s5_lambda_re[l], s5_lambda_im[l], s5_b_re[l], s5_b_im[l],
                            s5_c_re[l], s5_c_im[l], s5_d[l], s5_log_dt[l])
        y_a = _s5_mixer(h, w_s5_all, mats, s5_w_glu[l], bsz, seq, layer=l)
        y_b = _lru_mixer(proj, lru_conv_w[l], lru_conv_b[l], lru_w_a[l], lru_b_a[l],
                         lru_w_x[l], lru_b_x[l], lru_lambda[l], bsz, seq)
        y_c = _m2_mixer(proj, m2_conv_w[l], m2_conv_b[l], m2_dt_bias[l], m2_a_log[l],
                        m2_d[l], m2_norm_g[l], bsz, seq)
        y_d = _rw_mixer(proj, rw_mu[l], rw_w0[l], rw_w2[l], rw_a0[l], rw_a2[l], rw_g2[l],
                        rw_k_k[l], rw_k_a[l], rw_r_k[l].reshape(-1), rw_ln_g[l], rw_ln_b[l], bsz, seq)
        merged = _merge(h, (y_a, y_b, y_c, y_d), w_gate_all, wb_all, layer=l)
        xf, h = _matmul_res_norm(merged, wo_all, xf, norm_ffn_g[l], l)
        act = _ffn_act(h, w_ffn_gate, w_ffn_up, ffn_conv_w[l], ffn_conv_b[l], seq, layer=l)
        xf = _matmul(act, wd_all, res=xf, tm=1024, layer=l)
    return _rms_norm(xf, final_norm_g, F32).reshape(bsz, seq, d)
```

```python
import functools

import jax
import jax.numpy as jnp
from jax import lax
from jax.experimental import pallas as pl
from jax.experimental.pallas import tpu as pltpu

F32 = jnp.float32
BF16 = jnp.bfloat16

D_MODEL = 2048
MIX_W = 512
HEAD_W = 64
N_HEADS = MIX_W // HEAD_W
S5_GROUP = 16
S5_GROUPS = MIX_W // S5_GROUP
S5_STATE = 64
S5_Q = 8
S5_NSTATE = S5_GROUPS * S5_STATE
LANES = 128
S5_SG = MIX_W // LANES
S5_SGW = S5_Q * LANES
S5_SB = S5_NSTATE // S5_SG
S5_SCAN_ROWS = 512
LRU_C = 8.0
M2_GROUPS = 2
M2_STATE = 128
M2_CONV_DIM = MIX_W + 2 * M2_GROUPS * M2_STATE
M2_Q = 128
M2_BATCH = 2
RW_Q = 64
RW_BATCH = 4
RW_RKV_W = 3 * MIX_W
RW_LORA_RANKS = (32, 32, 96)
RW_LORA_W = 256
RW_LN_EPS = 64e-5
EPS = 1e-6
HALO = 8
BF16_ROWS = 16

PC_XBC = 0
PC_Z = 1024
PC_RKV = 1536
PC_LRU_X = 3072
PC_LRU_G = 3584
PC_LORA = 4096
PC_DT = 4352
PROJ_W = 4608

VMEM_LIMIT_BYTES = 50 * 1024 * 1024


def _params(*sem):
    return pltpu.CompilerParams(dimension_semantics=sem, vmem_limit_bytes=VMEM_LIMIT_BYTES)


def _dot(a, b):
    return jnp.dot(a, b, preferred_element_type=F32)


def _dot_nt(a, b):
    return lax.dot_general(a, b, (((1,), (1,)), ((), ())), preferred_element_type=F32)


def _dot_tn(a, b):
    return lax.dot_general(a, b, (((0,), (0,)), ((), ())), preferred_element_type=F32)


def _split(x, terms):
    out = []
    for _ in range(terms - 1):
        hi = x.astype(BF16)
        out.append(hi)
        x = x - hi.astype(F32)
    out.append(x.astype(BF16))
    return out


def _exact_left(m, x, terms=3):
    return sum(_dot(m, p) for p in _split(x, terms))


def _exact_right(xs, m, terms=2):
    n = xs[0].shape[0]
    stacked = jnp.concatenate([p for x in xs for p in _split(x, terms)], axis=0)
    out = _dot(stacked, m)
    return [sum(out[(i * terms + t) * n:(i * terms + t + 1) * n] for t in range(terms))
            for i in range(len(xs))]


def _tril_mask(n, strict=False):
    r = lax.broadcasted_iota(jnp.int32, (n, n), 0)
    c = lax.broadcasted_iota(jnp.int32, (n, n), 1)
    return (r > c) if strict else (r >= c)


def _causal_conv(x, halo, w_ref, b_ref):
    k_w = w_ref.shape[0]
    t = x.shape[0]
    xe = jnp.concatenate([halo, x], axis=0)
    out = b_ref[...] + x * w_ref[k_w - 1:k_w, :]
    for j in range(k_w - 1):
        lag = k_w - 1 - j
        out = out + xe[HALO - lag:HALO - lag + t, :] * w_ref[j:j + 1, :]
    return out


def _mm_body(a_ref, b_ref, o_ref):
    o_ref[...] = _dot(a_ref[...], b_ref[...]).astype(o_ref.dtype)


def _mm_res_body(a_ref, b_ref, r_ref, o_ref):
    o_ref[...] = (_dot(a_ref[...], b_ref[...]) + r_ref[...]).astype(o_ref.dtype)


def _weight_spec(w, layer, tn, col_of):
    k = w.shape[-2]
    if w.ndim == 2:
        return pl.BlockSpec((k, tn), lambda *g: (0, col_of(*g)))
    return pl.BlockSpec((None, k, tn), lambda *g: (layer, 0, col_of(*g)))


def _matmul(a, b, res=None, out_dtype=F32, tm=1024, tn=512, n_outer=False, layer=None):
    m, k = a.shape
    n = b.shape[-1]
    tm, tn = min(tm, m), min(tn, n)
    assert m % tm == 0 and n % tn == 0
    if n_outer:
        grid = (n // tn, m // tm)
        row = lambda j, i: (i, 0)
        col = _weight_spec(b, layer, tn, lambda j, i: j)
        out = lambda j, i: (i, j)
    else:
        grid = (m // tm, n // tn)
        row = lambda i, j: (i, 0)
        col = _weight_spec(b, layer, tn, lambda i, j: j)
        out = lambda i, j: (i, j)
    in_specs = [pl.BlockSpec((tm, k), row), col]
    args = [a, b]
    body = _mm_body
    if res is not None:
        in_specs.append(pl.BlockSpec((tm, tn), out))
        args.append(res)
        body = _mm_res_body
    return pl.pallas_call(
        body,
        name="matmul",
        grid=grid,
        in_specs=in_specs,
        out_specs=pl.BlockSpec((tm, tn), out),
        out_shape=jax.ShapeDtypeStruct((m, n), out_dtype),
        compiler_params=_params("parallel", "parallel"),
    )(*args)


def _matmul_blockdiag(a, w, groups, res=None, res_block=None, tm=1024):
    m = a.shape[0]
    k, n = a.shape[1] // groups, w.shape[1]
    tm = min(tm, m)
    out_spec = pl.BlockSpec((tm, n), lambda i, s: (i, s))
    in_specs = [pl.BlockSpec((tm, k), lambda i, s: (i, s)), pl.BlockSpec((k, n), lambda i, s: (s, 0))]
    args = [a, w]
    if res is not None:
        in_specs.append(pl.BlockSpec((tm, n), lambda i, s: (i, res_block(s))))
        args.append(res)
    return pl.pallas_call(
        _mm_body if res is None else _mm_res_body,
        name="matmul_blockdiag",
        grid=(m // tm, groups),
        in_specs=in_specs,
        out_specs=out_spec,
        out_shape=jax.ShapeDtypeStruct((m, groups * n), F32),
        compiler_params=_params("parallel", "parallel"),
    )(*args)


def _norm_body(x_ref, g_ref, o_ref):
    x = x_ref[...]
    ms = jnp.mean(x * x, axis=-1, keepdims=True)
    o_ref[...] = (x * lax.rsqrt(ms + EPS) * g_ref[...]).astype(o_ref.dtype)


def _mm_res_norm_body(a_ref, w_ref, r_ref, g_ref, x_ref, h_ref):
    x = _dot(a_ref[...], w_ref[...]) + r_ref[...]
    x_ref[...] = x
    ms = jnp.mean(x * x, axis=-1, keepdims=True)
    h_ref[...] = (x * lax.rsqrt(ms + EPS) * g_ref[...]).astype(h_ref.dtype)


def _matmul_res_norm(a, w, res, g, layer, tm=512):
    m, k = a.shape
    n = w.shape[-1]
    tm = min(tm, m)
    rows = lambda width: pl.BlockSpec((tm, width), lambda i: (i, 0))
    return pl.pallas_call(
        _mm_res_norm_body,
        name="matmul_residual_norm",
        grid=(m // tm,),
        in_specs=[rows(k), _weight_spec(w, layer, n, lambda i: 0), rows(n), pl.BlockSpec((1, n), lambda i: (0, 0))],
        out_specs=[rows(n), rows(n)],
        out_shape=[jax.ShapeDtypeStruct((m, n), F32), jax.ShapeDtypeStruct((m, n), BF16)],
        compiler_params=_params("parallel"),
    )(a, w, res, g.reshape(1, n))


def _rms_norm(x, g, out_dtype, tm=512):
    m, d = x.shape
    tm = min(tm, m)
    return pl.pallas_call(
        _norm_body,
        name="rms_norm",
        grid=(m // tm,),
        in_specs=[pl.BlockSpec((tm, d), lambda i: (i, 0)), pl.BlockSpec((1, d), lambda i: (0, 0))],
        out_specs=pl.BlockSpec((tm, d), lambda i: (i, 0)),
        out_shape=jax.ShapeDtypeStruct((m, d), out_dtype),
        compiler_params=_params("parallel"),
    )(x, g.reshape(1, d))


def _s5_matrices(lam_re, lam_im, b_re, b_im, c_re, c_im, d, log_dt):
    g_n, p_n, c_n, q = S5_GROUPS, S5_STATE, S5_GROUP, S5_Q
    hp = lax.Precision.HIGHEST
    dt = jnp.exp(log_dt)[:, None]
    n = jnp.arange(q + 1, dtype=F32)[:, None, None]
    mag = jnp.exp(n * (lam_re * dt))
    pw_re = mag * jnp.cos(n * (lam_im * dt))
    pw_im = mag * jnp.sin(n * (lam_im * dt))
    den = lam_re * lam_re + lam_im * lam_im
    nr, ni = pw_re[1] - 1.0, pw_im[1]
    f_re = (nr * lam_re + ni * lam_im) / den
    f_im = (ni * lam_re - nr * lam_im) / den
    e_re = f_re[..., None] * b_re - f_im[..., None] * b_im
    e_im = f_re[..., None] * b_im + f_im[..., None] * b_re
    cp_re = c_re[None] * pw_re[:, :, None, :] - c_im[None] * pw_im[:, :, None, :]
    cp_im = c_re[None] * pw_im[:, :, None, :] + c_im[None] * pw_re[:, :, None, :]
    kern = (jnp.einsum("tgop,gpi->tgio", cp_re[:q], e_re, precision=hp)
            - jnp.einsum("tgop,gpi->tgio", cp_im[:q], e_im, precision=hp))
    kern = kern.at[0].add(d.reshape(g_n, c_n)[:, :, None] * jnp.eye(c_n, dtype=F32))
    sg_n, gl_n = S5_SG, g_n // S5_SG
    rows = sg_n * S5_SGW
    lag = jnp.arange(q)[None, :] - jnp.arange(q)[:, None]
    kt = jnp.where((lag >= 0)[:, :, None, None, None], kern[jnp.clip(lag, 0, q - 1)], 0.0)
    kt = kt.reshape(q, q, sg_n, gl_n, c_n, c_n).transpose(2, 0, 3, 4, 1, 5)
    kt = kt.reshape(rows, q * c_n)
    rev_re, rev_im = pw_re[q - 1 - jnp.arange(q)], pw_im[q - 1 - jnp.arange(q)]
    et_re, et_im = e_re.transpose(0, 2, 1)[None], e_im.transpose(0, 2, 1)[None]
    ws_re = rev_re[:, :, None, :] * et_re - rev_im[:, :, None, :] * et_im
    ws_im = rev_re[:, :, None, :] * et_im + rev_im[:, :, None, :] * et_re
    ws = jnp.stack([ws_re, ws_im], axis=3)
    ws = ws.reshape(q, sg_n, gl_n, c_n, 2, p_n).transpose(1, 0, 2, 3, 4, 5).reshape(rows, 2 * p_n)
    wy = jnp.stack([cp_re[1:], -cp_im[1:]], axis=0)
    wy = wy.reshape(2, q, sg_n, gl_n, c_n, p_n).transpose(2, 0, 3, 5, 1, 4)
    wy = wy.reshape(rows, q * c_n)

    col = jnp.arange(S5_SGW)
    small = jnp.arange(q * c_n)[:, None]
    tok_rep = ((small // c_n == col[None, :] // LANES) & (small % c_n == col[None, :] % c_n)).astype(BF16)
    small = jnp.arange(2 * p_n)[:, None]
    st_rep = ((small // p_n == col[None, :] // S5_SB) & (small % p_n == col[None, :] % p_n)).astype(BF16)
    row = jnp.arange(rows)
    g_tok_r, g_tok_c = (row % LANES) // c_n, (col % LANES) // c_n
    g_st_r, g_st_c = (row % S5_SB) // p_n, (col % S5_SB) // p_n
    expand = lambda table, rep: jnp.dot(table.astype(BF16), rep, preferred_element_type=BF16)
    zero = jnp.zeros((), BF16)
    toep = jnp.where(g_tok_r[:, None] == g_tok_c[None, :], expand(kt, tok_rep), zero)
    w_state = jnp.where(g_tok_r[:, None] == g_st_c[None, :], expand(ws, st_rep), zero)
    w_out = jnp.where(g_st_r[:, None] == g_tok_c[None, :], expand(wy, tok_rep), zero)
    return toep, w_state, w_out, pw_re[q].reshape(1, S5_NSTATE), pw_im[q].reshape(1, S5_NSTATE)


def _s5_scan_body(ar_ref, ai_ref, s_ref, o_ref, c_ref):
    rows = BF16_ROWS
    nb = s_ref.shape[0]
    a_re, a_im = ar_ref[...], ai_ref[...]
    half = a_re.shape[1]

    @pl.when(pl.program_id(1) == 0)
    def _():
        c_ref[...] = jnp.zeros_like(c_ref)

    def block(i, carry):
        base = pl.multiple_of(i * rows, rows)
        xs = [s_ref[b, pl.ds(base, rows), :] for b in range(nb)]
        outs = [[] for _ in range(nb)]
        carry = list(carry)
        for r in range(rows):
            for b in range(nb):
                s_re, s_im = carry[b]
                outs[b].append(jnp.concatenate([s_re, s_im], axis=1))
                carry[b] = (a_re * s_re - a_im * s_im + xs[b][r:r + 1, :half],
                            a_re * s_im + a_im * s_re + xs[b][r:r + 1, half:])
        for b in range(nb):
            o_ref[b, pl.ds(base, rows), :] = jnp.concatenate(outs[b], axis=0).astype(o_ref.dtype)
        return tuple(carry)

    init = tuple((c_ref[b, :, :half], c_ref[b, :, half:]) for b in range(nb))
    last = lax.fori_loop(0, s_ref.shape[1] // rows, block, init)
    for b in range(nb):
        c_ref[b] = jnp.concatenate(last[b], axis=1)


def _s5_fold_body(h_ref, w_ref, o_ref, tok_ref):
    u = _dot(h_ref[...], w_ref[...])
    tc = o_ref.shape[0]
    for c in range(S5_SG):
        tok_ref[c] = u[:, c * LANES:(c + 1) * LANES]
        for j in range(S5_Q):
            lo = c * S5_SGW + j * LANES
            o_ref[:, lo:lo + LANES] = tok_ref[c, pl.ds(j, tc, stride=S5_Q), :].astype(o_ref.dtype)


def _s5_glu_body(y_ref, w_ref, o_ref, tok_ref):
    tc = y_ref.shape[0]
    for c in range(S5_SG):
        for j in range(S5_Q):
            lo = c * S5_SGW + j * LANES
            tok_ref[c, pl.ds(j, tc, stride=S5_Q), :] = y_ref[:, lo:lo + LANES]
    y = jax.nn.gelu(jnp.concatenate([tok_ref[c] for c in range(S5_SG)], axis=1))
    o_ref[...] = (y * jax.nn.sigmoid(_dot(y.astype(BF16), w_ref[...]))).astype(o_ref.dtype)


def _s5_mixer(h, w_u, mats, w_glu, bsz, seq, layer=None):
    toep, w_state, w_out, aq_re, aq_im = mats
    m = bsz * seq
    nc = seq // S5_Q
    width = S5_Q * MIX_W
    tm = min(1024, m)
    tc = tm // S5_Q
    uc = pl.pallas_call(
        _s5_fold_body,
        name="s5_input_proj",
        grid=(m // tm,),
        in_specs=[pl.BlockSpec((tm, D_MODEL), lambda i: (i, 0)), _weight_spec(w_u, layer, MIX_W, lambda i: 0)],
        out_specs=pl.BlockSpec((tc, width), lambda i: (i, 0)),
        out_shape=jax.ShapeDtypeStruct((m // S5_Q, width), BF16),
        scratch_shapes=[pltpu.VMEM((MIX_W // LANES, tm, LANES), F32)],
        compiler_params=_params("parallel"),
    )(h, w_u)
    y1s = _matmul_blockdiag(uc, jnp.concatenate([toep, w_state], axis=1), S5_SG)
    rb = min(S5_SCAN_ROWS, nc)
    s_in = pl.pallas_call(
        _s5_scan_body,
        name="s5_chunk_scan",
        grid=(S5_SG, nc // rb),
        in_specs=[
            pl.BlockSpec((1, S5_SB), lambda j, r: (0, j)),
            pl.BlockSpec((1, S5_SB), lambda j, r: (0, j)),
            pl.BlockSpec((bsz, rb, S5_SGW), lambda j, r: (0, r, 2 * j + 1)),
        ],
        out_specs=pl.BlockSpec((bsz, rb, S5_SGW), lambda j, r: (0, r, j)),
        out_shape=jax.ShapeDtypeStruct((bsz, nc, width), BF16),
        scratch_shapes=[pltpu.VMEM((bsz, 1, S5_SGW), F32)],
        compiler_params=_params("parallel", "arbitrary"),
    )(aq_re, aq_im, y1s.reshape(bsz, nc, 2 * width))
    y = _matmul_blockdiag(s_in.reshape(bsz * nc, width), w_out, S5_SG, res=y1s, res_block=lambda s: 2 * s)
    return pl.pallas_call(
        _s5_glu_body,
        name="s5_glu",
        grid=(m // tm,),
        in_specs=[pl.BlockSpec((tc, width), lambda i: (i, 0)), pl.BlockSpec((MIX_W, MIX_W), lambda i: (0, 0))],
        out_specs=pl.BlockSpec((tm, MIX_W), lambda i: (i, 0)),
        out_shape=jax.ShapeDtypeStruct((m, MIX_W), BF16),
        scratch_shapes=[pltpu.VMEM((MIX_W // LANES, tm, LANES), F32)],
        compiler_params=_params("parallel"),
    )(y, w_glu.astype(BF16))


def _lru_body(x_ref, g_ref, cw_ref, cb_ref, wa_ref, ba_ref, wx_ref, bx_ref, lam_ref, o_ref,
              halo_ref, h_ref, a_s, b_s):
    nb, t, _ = x_ref.shape

    @pl.when(pl.program_id(0) == 0)
    def _():
        halo_ref[...] = jnp.zeros_like(halo_ref)
        h_ref[...] = jnp.zeros_like(h_ref)

    soft = jax.nn.softplus(-lam_ref[...])
    for b in range(nb):
        x_in = x_ref[b]
        x = _causal_conv(x_in, halo_ref[b], cw_ref, cb_ref)
        halo_ref[b] = x_in[t - HALO:, :]
        xb = x.astype(BF16)
        r = jax.nn.sigmoid(_dot(xb, wa_ref[...]) + ba_ref[...])
        i = jax.nn.sigmoid(_dot(xb, wx_ref[...]) + bx_ref[...])
        log_a = (-LRU_C * r) * soft
        a = jnp.exp(log_a)
        a_s[b] = a
        b_s[b] = x * i * jnp.sqrt(1.0 - a * a)
    rows = 8

    def block(k, hs):
        base = pl.multiple_of(k * rows, rows)
        av = [a_s[b, pl.ds(base, rows), :] for b in range(nb)]
        bv = [b_s[b, pl.ds(base, rows), :] for b in range(nb)]
        hs = list(hs)
        outs = [[] for _ in range(nb)]
        for rr in range(rows):
            for b in range(nb):
                hs[b] = av[b][rr:rr + 1, :] * hs[b] + bv[b][rr:rr + 1, :]
                outs[b].append(hs[b])
        for b in range(nb):
            b_s[b, pl.ds(base, rows), :] = jnp.concatenate(outs[b], axis=0)
        return tuple(hs)

    hs = lax.fori_loop(0, t // rows, block, tuple(h_ref[b] for b in range(nb)))
    for b in range(nb):
        h_ref[b] = hs[b]
        o_ref[b] = (b_s[b] * jax.nn.gelu(g_ref[b])).astype(o_ref.dtype)


def _block_diag(w):
    h_n, n, _ = w.shape
    eye = jnp.eye(h_n, dtype=w.dtype)
    return (w[:, :, None, :] * eye[:, None, :, None]).reshape(h_n * n, h_n * n)


def _lru_mixer(proj, conv_w, conv_b, w_a, b_a, w_x, b_x, lam, bsz, seq):
    t = min(256, seq)
    vec = lambda v: v.reshape(1, MIX_W)
    const = lambda shape: pl.BlockSpec(shape, lambda c: (0, 0))
    proj3 = proj.reshape(bsz, seq, proj.shape[1])
    out = pl.pallas_call(
        _lru_body,
        name="rglru_mixer",
        grid=(seq // t,),
        in_specs=[
            pl.BlockSpec((bsz, t, MIX_W), lambda c: (0, c, PC_LRU_X // MIX_W)),
            pl.BlockSpec((bsz, t, MIX_W), lambda c: (0, c, PC_LRU_G // MIX_W)),
            const(conv_w.shape), const((1, MIX_W)),
            const((MIX_W, MIX_W)), const((1, MIX_W)),
            const((MIX_W, MIX_W)), const((1, MIX_W)), const((1, MIX_W)),
        ],
        out_specs=pl.BlockSpec((bsz, t, MIX_W), lambda c: (0, c, 0)),
        out_shape=jax.ShapeDtypeStruct((bsz, seq, MIX_W), BF16),
        scratch_shapes=[
            pltpu.VMEM((bsz, HALO, MIX_W), F32), pltpu.VMEM((bsz, 1, MIX_W), F32),
            pltpu.VMEM((bsz, t, MIX_W), F32), pltpu.VMEM((bsz, t, MIX_W), F32),
        ],
        compiler_params=_params("arbitrary"),
    )(proj3, proj3, conv_w, vec(conv_b), _block_diag(w_a).astype(BF16), vec(b_a),
      _block_diag(w_x).astype(BF16), vec(b_x), vec(lam))
    return out.reshape(bsz * seq, MIX_W)


def _m2_body(z_ref, xbc_ref, dt_ref, rep_ref, cw_ref, cb_ref, dtb_ref, alog_ref, d_ref, ng_ref, o_ref,
             halo_ref, st_ref):
    nb, q, _ = z_ref.shape
    hg = N_HEADS // M2_GROUPS
    gw = hg * HEAD_W
    seqs = range(nb)

    @pl.when(pl.program_id(1) == 0)
    def _():
        halo_ref[...] = jnp.zeros_like(halo_ref)
        st_ref[...] = jnp.zeros_like(st_ref)

    causal = _tril_mask(q)
    causal_b = causal.astype(BF16)
    neg_a = -jnp.exp(alog_ref[...])
    dt_raw = _exact_right([dt_ref[b] for b in seqs], rep_ref[...])
    xc = []
    for b in seqs:
        xbc = xbc_ref[b]
        conv = _causal_conv(xbc, halo_ref[b], cw_ref, cb_ref)
        halo_ref[b] = xbc[q - HALO:, :]
        xc.append(conv * jax.nn.sigmoid(conv))
    dt = [jax.nn.softplus(dt_raw[b] + dtb_ref[...]) for b in seqs]
    a_cs = [_exact_left(causal_b, dt[b] * neg_a) for b in seqs]
    xs = [xc[b][:, :MIX_W] for b in seqs]
    xd = [xs[b] * dt[b] for b in seqs]
    a_cs_t = [a_cs[b].T for b in seqs]
    a_last = [a_cs[b][q - 1:q, :] for b in seqs]
    xd_st = [(xd[b] * jnp.exp(a_last[b] - a_cs[b])).astype(BF16) for b in seqs]
    e_cs = [jnp.exp(a_cs[b]) for b in seqs]
    bg, cg, cb, y_off = {}, {}, {}, {}
    for b in seqs:
        for g in range(M2_GROUPS):
            lo = MIX_W + g * M2_STATE
            bg[b, g] = xc[b][:, lo:lo + M2_STATE].astype(BF16)
            cg[b, g] = xc[b][:, lo + M2_GROUPS * M2_STATE:lo + (M2_GROUPS + 1) * M2_STATE].astype(BF16)
            cb[b, g] = _dot_nt(cg[b, g], bg[b, g])
            y_off[b, g] = _dot(cg[b, g], st_ref[b, g].astype(BF16)) * e_cs[b][:, g * gw:(g + 1) * gw]
    ys = [[] for _ in seqs]
    for g in range(M2_GROUPS):
        for hh in range(hg):
            for b in seqs:
                lo = (g * hg + hh) * HEAD_W
                col = a_cs[b][:, lo:lo + HEAD_W]
                col = jnp.concatenate([col] * (q // HEAD_W), axis=1)
                seg = col - a_cs_t[b][lo:lo + 1, :]
                dec = jnp.exp(jnp.where(causal, seg, -jnp.inf))
                y_d = _dot((cb[b, g] * dec).astype(BF16), xd[b][:, lo:lo + HEAD_W].astype(BF16))
                ys[b].append(y_d + y_off[b, g][:, hh * HEAD_W:(hh + 1) * HEAD_W])
    for b in seqs:
        for g in range(M2_GROUPS):
            upd = _dot_tn(bg[b, g], xd_st[b][:, g * gw:(g + 1) * gw])
            st_ref[b, g] = st_ref[b, g] * jnp.exp(a_last[b][:, g * gw:(g + 1) * gw]) + upd
    for b in seqs:
        y = jnp.concatenate(ys[b], axis=1) + d_ref[...] * xs[b]
        z = z_ref[b]
        y = y * (z * jax.nn.sigmoid(z))
        ms = jnp.mean(y * y, axis=-1, keepdims=True)
        o_ref[b] = (y * lax.rsqrt(ms + EPS) * ng_ref[...]).astype(o_ref.dtype)


def _per_head(v):
    return jnp.repeat(v, HEAD_W).reshape(1, -1)


def _m2_mixer(proj, conv_w, conv_b, dt_bias, a_log, d, norm_g, bsz, seq):
    q = min(M2_Q, seq)
    nc = seq // q
    nb = M2_BATCH if bsz % M2_BATCH == 0 else 1
    const = lambda shape: pl.BlockSpec(shape, lambda b, c: (0, 0))
    head_of_lane = jnp.arange(MIX_W)[None, :] // HEAD_W
    rep = (jnp.arange(LANES)[:, None] == head_of_lane).astype(BF16)
    proj3 = proj.reshape(bsz, seq, proj.shape[1])
    out = pl.pallas_call(
        _m2_body,
        name="mamba2_mixer",
        grid=(bsz // nb, nc),
        in_specs=[
            pl.BlockSpec((nb, q, MIX_W), lambda b, c: (b, c, PC_Z // MIX_W)),
            pl.BlockSpec((nb, q, M2_CONV_DIM), lambda b, c: (b, c, PC_XBC // M2_CONV_DIM)),
            pl.BlockSpec((nb, q, LANES), lambda b, c: (b, c, PC_DT // LANES)),
            const((LANES, MIX_W)), const(conv_w.shape), const((1, M2_CONV_DIM)),
            const((1, MIX_W)), const((1, MIX_W)), const((1, MIX_W)), const((1, MIX_W)),
        ],
        out_specs=pl.BlockSpec((nb, q, MIX_W), lambda b, c: (b, c, 0)),
        out_shape=jax.ShapeDtypeStruct((bsz, seq, MIX_W), BF16),
        scratch_shapes=[
            pltpu.VMEM((nb, HALO, M2_CONV_DIM), F32),
            pltpu.VMEM((nb, M2_GROUPS, M2_STATE, (N_HEADS // M2_GROUPS) * HEAD_W), F32),
        ],
        compiler_params=_params("parallel", "arbitrary"),
    )(proj3, proj3, proj3, rep, conv_w, conv_b.reshape(1, -1), _per_head(dt_bias), _per_head(a_log),
      _per_head(d), norm_g.reshape(1, -1))
    return out.reshape(bsz * seq, MIX_W)


def _rw_body(p_ref, pl_ref, mu_ref, mul_ref, w0_ref, w2_ref, a0_ref, a2_ref, g2_ref, kk_ref, ka_ref, rk_ref,
             lng_ref, lnb_ref, ones_ref, o_ref, prev_ref, prevl_ref, st_ref):
    nb, q, _ = p_ref.shape
    rows = nb * q

    @pl.when(pl.program_id(1) == 0)
    def _():
        prev_ref[...] = jnp.zeros_like(prev_ref)
        prevl_ref[...] = jnp.zeros_like(prevl_ref)
        st_ref[...] = jnp.zeros_like(st_ref)

    def token_mix(x_ref, last_ref, m_ref):
        x = x_ref[...].reshape(rows, x_ref.shape[2])
        row = lax.broadcasted_iota(jnp.int32, x.shape, 0)
        shifted = pltpu.roll(x, 1, axis=0)
        for b in range(nb):
            shifted = jnp.where(row == b * q, last_ref[b, 0:1, :], shifted)
            last_ref[b, 0:1, :] = x[(b + 1) * q - 1:(b + 1) * q, :]
        return x + (shifted - x) * m_ref[...]

    p = token_mix(p_ref, prev_ref, mu_ref)
    lora = token_mix(pl_ref, prevl_ref, mul_ref)
    r = p[:, 0:MIX_W]
    k = p[:, MIX_W:2 * MIX_W]
    v = p[:, 2 * MIX_W:3 * MIX_W]
    w = -jax.nn.softplus(-(w0_ref[...] + _dot(jnp.tanh(lora).astype(BF16), w2_ref[...]))) - 0.5
    log_w = -jnp.exp(w)
    a_sig = jax.nn.sigmoid(a0_ref[...] + _dot(lora.astype(BF16), a2_ref[...]))
    gate = _dot(jax.nn.sigmoid(lora).astype(BF16), g2_ref[...])
    ones = ones_ref[...]
    kk = k * kk_ref[...]
    k = k * (1.0 + (a_sig - 1.0) * ka_ref[...])
    kk_sq, rk_sum = _exact_right([kk * kk, r * k * rk_ref[...]], ones)
    kk = kk / jnp.maximum(jnp.sqrt(kk_sq), 1e-12)
    a_vec = -kk
    b_vec = kk * a_sig

    rr = lax.broadcasted_iota(jnp.int32, (rows, rows), 0)
    cc = lax.broadcasted_iota(jnp.int32, (rows, rows), 1)
    cum_mask = jnp.where((rr >= cc) & ((rr // q) == (cc // q)), 1.0, 0.0).astype(BF16)
    gam = _exact_left(cum_mask, log_w)
    g_last = jnp.concatenate(
        [jnp.broadcast_to(gam[(b + 1) * q - 1:(b + 1) * q, :], (q, MIX_W)) for b in range(nb)], axis=0)
    e_neg = jnp.exp(-gam)
    e_last = jnp.exp(g_last)
    e_rem = e_last * e_neg
    a_t = a_vec * jnp.exp(gam - log_w)
    r_t = r * jnp.exp(gam)
    b_t = b_vec * e_neg
    k_t = k * e_neg
    b_h = b_vec * e_rem
    k_h = k * e_rem
    assert q == HEAD_W
    row2 = lax.broadcasted_iota(jnp.int32, (q, 2 * HEAD_W), 0)
    lane2 = lax.broadcasted_iota(jnp.int32, (q, 2 * HEAD_W), 1)
    low = lane2 < HEAD_W
    keep_n = lane2 < row2
    keep_t = jnp.where(low, lane2, lane2 - HEAD_W) <= row2
    strict = _tril_mask(q, strict=True)
    pairs = [(b, h) for b in range(nb) for h in range(N_HEADS)]
    zeros_x = jnp.zeros((q, HEAD_W), F32)

    def blk(t, key):
        b, h = key
        return t[b * q:(b + 1) * q, h * HEAD_W:(h + 1) * HEAD_W]

    p0, p1, s0, z, w_v = {}, {}, {}, {}, {}
    for key in pairs:
        ar = jnp.concatenate([blk(a_t, key), blk(r_t, key)], axis=0).astype(BF16)
        s0[key] = st_ref[key[0], key[1]]
        b_blk = blk(b_t, key)
        rhs = jnp.concatenate([blk(k_t, key), s0[key], b_blk, b_blk], axis=0).astype(BF16)
        prod = _dot_nt(ar, rhs)
        p0[key] = prod[:, :2 * HEAD_W]
        p1[key] = prod[:, 2 * HEAD_W:]
    for key in pairs:
        a_k = jnp.where(strict, p0[key][:q, :HEAD_W], 0.0).astype(BF16)
        w_v[key] = jnp.concatenate([zeros_x, blk(v, key)], axis=1).astype(BF16)
        z[key] = jnp.where(keep_n, p1[key][:q], jnp.where(low, 0.0, p0[key][:q])) + _dot(a_k, w_v[key])
    steps = max(1, (q - 1).bit_length())
    for _ in range(steps):
        for key in pairs:
            zb = z[key].astype(BF16)
            z[key] = jnp.where(low, 0.0, z[key]) + _dot(zb[:, :HEAD_W], zb)
    ys = {}
    for key in pairs:
        w_vu = jnp.concatenate([w_v[key], jnp.where(low, 0.0, z[key]).astype(BF16)], axis=0)
        t_mat = jnp.where(keep_t, jnp.where(low, p0[key][q:], p1[key][q:]), 0.0)
        y_hi = jnp.where(low, 0.0, p0[key][q:]) + _dot(t_mat.astype(BF16), w_vu)
        ys[key] = y_hi[:, HEAD_W:]
        khb = jnp.concatenate([blk(k_h, key), blk(b_h, key)], axis=0).astype(BF16)
        st_ref[key[0], key[1]] = s0[key] * blk(e_last, key)[0:1, :] + _dot_tn(w_vu, khb)[HEAD_W:, :]
    y = jnp.concatenate(
        [jnp.concatenate([ys[(b, h)] for h in range(N_HEADS)], axis=1) for b in range(nb)], axis=0)
    inv_n = 1.0 / HEAD_W
    yc = y - _exact_right([y], ones)[0] * inv_n
    var = _exact_right([yc * yc], ones)[0] * inv_n
    y = yc * lax.rsqrt(var + RW_LN_EPS) * lng_ref[...] + lnb_ref[...]
    bonus = rk_sum * v
    o_ref[...] = ((y + bonus) * gate).astype(o_ref.dtype).reshape(o_ref.shape)


def _pad_rows(w, rows, off):
    return jnp.zeros((rows, w.shape[1]), w.dtype).at[off:off + w.shape[0]].set(w)


def _rw_mixer(proj, mu, w0, w2, a0, a2, g2, k_k, k_a, r_k, ln_g, ln_b, bsz, seq):
    q = min(RW_Q, seq)
    nc = seq // q
    nb = RW_BATCH if bsz % RW_BATCH == 0 else 1
    vec = lambda t: t.reshape(1, MIX_W)
    const = lambda shape: pl.BlockSpec(shape, lambda b, c: (0, 0))
    offs = (0, RW_LORA_RANKS[0], RW_LORA_RANKS[0] + RW_LORA_RANKS[1])
    lora = lambda t, off: _pad_rows(t, RW_LORA_W, off).astype(BF16)
    mu_rkv, mu_lora = mu[:RW_RKV_W].reshape(1, RW_RKV_W), _pad_rows(mu[RW_RKV_W:, None], RW_LORA_W, 0).reshape(1, RW_LORA_W)
    ones = _block_diag(jnp.ones((N_HEADS, HEAD_W, HEAD_W), F32)).astype(BF16)
    proj3 = proj.reshape(bsz, seq, proj.shape[1])
    out = pl.pallas_call(
        _rw_body,
        name="rwkv7_mixer",
        grid=(bsz // nb, nc),
        in_specs=[
            pl.BlockSpec((nb, q, RW_RKV_W), lambda b, c: (b, c, PC_RKV // RW_RKV_W)),
            pl.BlockSpec((nb, q, RW_LORA_W), lambda b, c: (b, c, PC_LORA // RW_LORA_W)),
            const((1, RW_RKV_W)), const((1, RW_LORA_W)), const((1, MIX_W)), const((RW_LORA_W, MIX_W)),
            const((1, MIX_W)), const((RW_LORA_W, MIX_W)), const((RW_LORA_W, MIX_W)),
            const((1, MIX_W)), const((1, MIX_W)), const((1, MIX_W)),
            const((1, MIX_W)), const((1, MIX_W)), const((MIX_W, MIX_W)),
        ],
        out_specs=pl.BlockSpec((nb, q, MIX_W), lambda b, c: (b, c, 0)),
        out_shape=jax.ShapeDtypeStruct((bsz, seq, MIX_W), BF16),
        scratch_shapes=[pltpu.VMEM((nb, HALO, RW_RKV_W), F32), pltpu.VMEM((nb, HALO, RW_LORA_W), F32),
                        pltpu.VMEM((nb, N_HEADS, HEAD_W, HEAD_W), F32)],
        compiler_params=_params("parallel", "arbitrary"),
    )(proj3, proj3, mu_rkv, mu_lora, vec(w0), lora(w2, offs[0]), vec(a0), lora(a2, offs[1]), lora(g2, offs[2]),
      vec(k_k), vec(k_a), vec(r_k), vec(ln_g), vec(ln_b), ones)
    return out.reshape(bsz * seq, MIX_W)


def _merge_body(h_ref, *refs):
    nb = (len(refs) - 2) // 2
    y_refs, wg_refs, wp_ref, o_ref = refs[:nb], refs[nb:2 * nb], refs[2 * nb], refs[2 * nb + 1]
    h = h_ref[...]
    acc = None
    for kk in range(nb):
        term = jax.nn.sigmoid(_dot(h, wg_refs[kk][...])) * _dot(y_refs[kk][...], wp_ref[kk])
        acc = term if acc is None else acc + term
    o_ref[...] = acc.astype(o_ref.dtype)


def _merge(h, ys, w_gate, w_branch, layer=None, tm=512, tn=512):
    m = h.shape[0]
    tm = min(tm, m)
    nb = len(ys)
    y_spec = pl.BlockSpec((tm, MIX_W), lambda j, i: (i, 0))
    gate_specs = [_weight_spec(w_gate, layer, tn, lambda j, i, k=k: k * (D_MODEL // tn) + j) for k in range(nb)]
    if w_branch.ndim == 3:
        branch_spec = pl.BlockSpec((nb, MIX_W, tn), lambda j, i: (0, 0, j))
    else:
        branch_spec = pl.BlockSpec((None, nb, MIX_W, tn), lambda j, i: (layer, 0, 0, j))
    return pl.pallas_call(
        _merge_body,
        name="gated_merge",
        grid=(D_MODEL // tn, m // tm),
        in_specs=[pl.BlockSpec((tm, D_MODEL), lambda j, i: (i, 0))] + [y_spec] * nb + gate_specs + [branch_spec],
        out_specs=pl.BlockSpec((tm, tn), lambda j, i: (i, j)),
        out_shape=jax.ShapeDtypeStruct((m, D_MODEL), BF16),
        compiler_params=_params("parallel", "parallel"),
    )(h, *ys, *([w_gate] * nb), w_branch)


def _ffn_body(h_ref, hh_ref, wg_ref, wu_ref, cw_ref, cb_ref, o_ref, wg_s, wu_s, *, tiles_per_seq):
    t = h_ref.shape[0]
    nh = hh_ref.shape[0]

    @pl.when(pl.program_id(1) == 0)
    def _():
        wg_s[...] = wg_ref[...].astype(BF16)
        wu_s[...] = wu_ref[...].astype(BF16)

    h = h_ref[...]
    seq_start = (pl.program_id(1) % tiles_per_seq) == 0
    halo = jnp.where(seq_start, jnp.zeros_like(hh_ref[...]), hh_ref[...])
    g_all = _dot(jnp.concatenate([halo, h], axis=0), wg_s[...])
    k_w = cw_ref.shape[0]
    u = cb_ref[...] + g_all[nh:, :] * cw_ref[k_w - 1:k_w, :]
    for j in range(k_w - 1):
        lag = k_w - 1 - j
        u = u + g_all[nh - lag:nh - lag + t, :] * cw_ref[j:j + 1, :]
    o_ref[...] = (jax.nn.gelu(u) * _dot(h, wu_s[...])).astype(o_ref.dtype)


def _ffn_act(h, w_gate, w_up, conv_w, conv_b, seq, layer=None, tm=1024, tf=512):
    m = h.shape[0]
    tm = min(tm, seq)
    f = w_gate.shape[-1]
    halo_blocks = tm // BF16_ROWS
    return pl.pallas_call(
        functools.partial(_ffn_body, tiles_per_seq=seq // tm),
        name="ffn_gate_up",
        grid=(f // tf, m // tm),
        in_specs=[
            pl.BlockSpec((tm, D_MODEL), lambda j, i: (i, 0)),
            pl.BlockSpec((BF16_ROWS, D_MODEL), lambda j, i: (jnp.maximum(i * halo_blocks - 1, 0), 0)),
            _weight_spec(w_gate, layer, tf, lambda j, i: j),
            _weight_spec(w_up, layer, tf, lambda j, i: j),
            pl.BlockSpec((conv_w.shape[0], tf), lambda j, i: (0, j)),
            pl.BlockSpec((1, tf), lambda j, i: (0, j)),
        ],
        out_specs=pl.BlockSpec((tm, tf), lambda j, i: (i, j)),
        out_shape=jax.ShapeDtypeStruct((m, f), BF16),
        scratch_shapes=[pltpu.VMEM((D_MODEL, tf), BF16), pltpu.VMEM((D_MODEL, tf), BF16)],
        compiler_params=_params("parallel", "arbitrary"),
    )(h, h, w_gate, w_up, conv_w, conv_b.reshape(1, f))


def _w_in_body(w_ref, s5_ref, proj_ref, gate_ref):
    w = w_ref[...]
    tk = w.shape[1]
    row = 0
    piece = {}
    for name, width in (("s5_u", MIX_W), ("lru_x", MIX_W), ("lru_g", MIX_W), ("z", MIX_W), ("xbc", M2_CONV_DIM),
                        ("dt", N_HEADS), ("rkv", RW_RKV_W), ("lora", sum(RW_LORA_RANKS))):
        piece[name] = w[row:row + width, :]
        row += width

    def put(ref, off, t, width):
        if t.shape[0] < width:
            t = jnp.concatenate([t, jnp.zeros((width - t.shape[0], tk), t.dtype)], axis=0)
        ref[:, off:off + width] = t.T.astype(ref.dtype)

    put(s5_ref, 0, piece["s5_u"], MIX_W)
    put(proj_ref, PC_XBC, piece["xbc"], M2_CONV_DIM)
    put(proj_ref, PC_Z, piece["z"], MIX_W)
    put(proj_ref, PC_RKV, piece["rkv"], RW_RKV_W)
    put(proj_ref, PC_LRU_X, piece["lru_x"], MIX_W)
    put(proj_ref, PC_LRU_G, piece["lru_g"], MIX_W)
    put(proj_ref, PC_LORA, piece["lora"], RW_LORA_W)
    put(proj_ref, PC_DT, piece["dt"], PROJ_W - PC_DT)
    put(gate_ref, 0, w[row:, :], w.shape[0] - row)


def _prepare_w_in(w_in, tk=128):
    depth, d, cols = w_in.shape
    gate_w = cols - (4 * MIX_W + M2_CONV_DIM + N_HEADS + RW_RKV_W + sum(RW_LORA_RANKS))
    out = lambda width: pl.BlockSpec((None, tk, width), lambda l, r: (l, r, 0))
    return pl.pallas_call(
        _w_in_body,
        name="w_in_layout",
        grid=(depth, d // tk),
        in_specs=[pl.BlockSpec((None, cols, tk), lambda l, r: (l, 0, r))],
        out_specs=[out(MIX_W), out(PROJ_W), out(gate_w)],
        out_shape=[jax.ShapeDtypeStruct((depth, d, MIX_W), BF16), jax.ShapeDtypeStruct((depth, d, PROJ_W), BF16),
                   jax.ShapeDtypeStruct((depth, d, gate_w), BF16)],
        compiler_params=_params("parallel", "parallel"),
    )(jnp.swapaxes(w_in, 1, 2))


def kernel(x, norm_mix_g, w_in, s5_lambda_re, s5_lambda_im, s5_b_re, s5_b_im, s5_c_re, s5_c_im, s5_d, s5_log_dt, s5_w_glu, lru_conv_w, lru_conv_b, lru_w_a, lru_b_a, lru_w_x, lru_b_x, lru_lambda, m2_conv_w, m2_conv_b, m2_dt_bias, m2_a_log, m2_d, m2_norm_g, rw_mu, rw_w0, rw_w2, rw_a0, rw_a2, rw_g2, rw_k_k, rw_k_a, rw_r_k, rw_ln_g, rw_ln_b, w_branch, w_out, norm_ffn_g, w_ffn_gate, w_ffn_up, ffn_conv_w, ffn_conv_b, w_ffn_down, final_norm_g):
    bsz, seq, d = x.shape
    depth = w_in.shape[0]
    xf = x.reshape(bsz * seq, d)
    wb_all, wo_all, wd_all = w_branch.astype(BF16), w_out.astype(BF16), w_ffn_down.astype(BF16)
    w_s5_all, w_proj_all, w_gate_all = _prepare_w_in(w_in)
    for l in range(depth):
        h = _rms_norm(xf, norm_mix_g[l], BF16)
        proj = _matmul(h, w_proj_all, tm=2048, layer=l)
        mats = _s5_matrices(s5_lambda_re[l], s5_lambda_im[l], s5_b_re[l], s5_b_im[l],
                            s5_c_re[l], s5_c_im[l], s5_d[l], s5_log_dt[l])
        y_a = _s5_mixer(h, w_s5_all, mats, s5_w_glu[l], bsz, seq, layer=l)
        y_b = _lru_mixer(proj, lru_conv_w[l], lru_conv_b[l], lru_w_a[l], lru_b_a[l],
                         lru_w_x[l], lru_b_x[l], lru_lambda[l], bsz, seq)
        y_c = _m2_mixer(proj, m2_conv_w[l], m2_conv_b[l], m2_dt_bias[l], m2_a_log[l],
                        m2_d[l], m2_norm_g[l], bsz, seq)
        y_d = _rw_mixer(proj, rw_mu[l], rw_w0[l], rw_w2[l], rw_a0[l], rw_a2[l], rw_g2[l],
                        rw_k_k[l], rw_k_a[l], rw_r_k[l].reshape(-1), rw_ln_g[l], rw_ln_b[l], bsz, seq)
        merged = _merge(h, (y_a, y_b, y_c, y_d), w_gate_all, wb_all, layer=l)
        xf, h = _matmul_res_norm(merged, wo_all, xf, norm_ffn_g[l], l)
        act = _ffn_act(h, w_ffn_gate, w_ffn_up, ffn_conv_w[l], ffn_conv_b[l], seq, layer=l)
        xf = _matmul(act, wd_all, res=xf, tm=1024, layer=l)
    return _rms_norm(xf, final_norm_g, F32).reshape(bsz, seq, d)
```

```python
import functools

import jax
import jax.numpy as jnp
from jax import lax
from jax.experimental import pallas as pl
from jax.experimental.pallas import tpu as pltpu

F32 = jnp.float32
BF16 = jnp.bfloat16

D_MODEL = 2048
MIX_W = 512
HEAD_W = 64
N_HEADS = MIX_W // HEAD_W
S5_GROUP = 16
S5_GROUPS = MIX_W // S5_GROUP
S5_STATE = 64
S5_Q = 8
S5_NSTATE = S5_GROUPS * S5_STATE
LANES = 128
S5_SG = MIX_W // LANES
S5_SGW = S5_Q * LANES
S5_SB = S5_NSTATE // S5_SG
S5_SCAN_ROWS = 512
LRU_C = 8.0
M2_GROUPS = 2
M2_STATE = 128
M2_CONV_DIM = MIX_W + 2 * M2_GROUPS * M2_STATE
M2_Q = 128
M2_BATCH = 2
RW_Q = 64
RW_BATCH = 4
RW_RKV_W = 3 * MIX_W
RW_LORA_RANKS = (32, 32, 96)
RW_LORA_W = 256
RW_LN_EPS = 64e-5
EPS = 1e-6
HALO = 8
BF16_ROWS = 16

PC_XBC = 0
PC_Z = 1024
PC_RKV = 1536
PC_LRU_X = 3072
PC_LRU_G = 3584
PC_LORA = 4096
PC_DT = 4352
PROJ_W = 4608

VMEM_LIMIT_BYTES = 50 * 1024 * 1024


def _params(*sem):
    return pltpu.CompilerParams(dimension_semantics=sem, vmem_limit_bytes=VMEM_LIMIT_BYTES)


def _dot(a, b):
    return jnp.dot(a, b, preferred_element_type=F32)


def _dot_nt(a, b):
    return lax.dot_general(a, b, (((1,), (1,)), ((), ())), preferred_element_type=F32)


def _dot_tn(a, b):
    return lax.dot_general(a, b, (((0,), (0,)), ((), ())), preferred_element_type=F32)


def _split(x, terms):
    out = []
    for _ in range(terms - 1):
        hi = x.astype(BF16)
        out.append(hi)
        x = x - hi.astype(F32)
    out.append(x.astype(BF16))
    return out


def _exact_left(m, x, terms=3):
    return sum(_dot(m, p) for p in _split(x, terms))


def _exact_right(xs, m, terms=2):
    n = xs[0].shape[0]
    stacked = jnp.concatenate([p for x in xs for p in _split(x, terms)], axis=0)
    out = _dot(stacked, m)
    return [sum(out[(i * terms + t) * n:(i * terms + t + 1) * n] for t in range(terms))
            for i in range(len(xs))]


def _tril_mask(n, strict=False):
    r = lax.broadcasted_iota(jnp.int32, (n, n), 0)
    c = lax.broadcasted_iota(jnp.int32, (n, n), 1)
    return (r > c) if strict else (r >= c)


def _causal_conv(x, halo, w_ref, b_ref):
    k_w = w_ref.shape[0]
    t = x.shape[0]
    xe = jnp.concatenate([halo, x], axis=0)
    out = b_ref[...] + x * w_ref[k_w - 1:k_w, :]
    for j in range(k_w - 1):
        lag = k_w - 1 - j
        out = out + xe[HALO - lag:HALO - lag + t, :] * w_ref[j:j + 1, :]
    return out


def _mm_body(a_ref, b_ref, o_ref):
    o_ref[...] = _dot(a_ref[...], b_ref[...]).astype(o_ref.dtype)


def _mm_res_body(a_ref, b_ref, r_ref, o_ref):
    o_ref[...] = (_dot(a_ref[...], b_ref[...]) + r_ref[...]).astype(o_ref.dtype)


def _weight_spec(w, layer, tn, col_of):
    k = w.shape[-2]
    if w.ndim == 2:
        return pl.BlockSpec((k, tn), lambda *g: (0, col_of(*g)))
    return pl.BlockSpec((None, k, tn), lambda *g: (layer, 0, col_of(*g)))


def _matmul(a, b, res=None, out_dtype=F32, tm=1024, tn=512, n_outer=False, layer=None):
    m, k = a.shape
    n = b.shape[-1]
    tm, tn = min(tm, m), min(tn, n)
    assert m % tm == 0 and n % tn == 0
    if n_outer:
        grid = (n // tn, m // tm)
        row = lambda j, i: (i, 0)
        col = _weight_spec(b, layer, tn, lambda j, i: j)
        out = lambda j, i: (i, j)
    else:
        grid = (m // tm, n // tn)
        row = lambda i, j: (i, 0)
        col = _weight_spec(b, layer, tn, lambda i, j: j)
        out = lambda i, j: (i, j)
    in_specs = [pl.BlockSpec((tm, k), row), col]
    args = [a, b]
    body = _mm_body
    if res is not None:
        in_specs.append(pl.BlockSpec((tm, tn), out))
        args.append(res)
        body = _mm_res_body
    return pl.pallas_call(
        body,
        name="matmul",
        grid=grid,
        in_specs=in_specs,
        out_specs=pl.BlockSpec((tm, tn), out),
        out_shape=jax.ShapeDtypeStruct((m, n), out_dtype),
        compiler_params=_params("parallel", "parallel"),
    )(*args)


def _matmul_blockdiag(a, w, groups, res=None, res_block=None, tm=1024):
    m = a.shape[0]
    k, n = a.shape[1] // groups, w.shape[1]
    tm = min(tm, m)
    out_spec = pl.BlockSpec((tm, n), lambda i, s: (i, s))
    in_specs = [pl.BlockSpec((tm, k), lambda i, s: (i, s)), pl.BlockSpec((k, n), lambda i, s: (s, 0))]
    args = [a, w]
    if res is not None:
        in_specs.append(pl.BlockSpec((tm, n), lambda i, s: (i, res_block(s))))
        args.append(res)
    return pl.pallas_call(
        _mm_body if res is None else _mm_res_body,
        name="matmul_blockdiag",
        grid=(m // tm, groups),
        in_specs=in_specs,
        out_specs=out_spec,
        out_shape=jax.ShapeDtypeStruct((m, groups * n), F32),
        compiler_params=_params("parallel", "parallel"),
    )(*args)


def _norm_body(x_ref, g_ref, o_ref):
    x = x_ref[...]
    ms = jnp.mean(x * x, axis=-1, keepdims=True)
    o_ref[...] = (x * lax.rsqrt(ms + EPS) * g_ref[...]).astype(o_ref.dtype)


def _mm_res_norm_body(a_ref, w_ref, r_ref, g_ref, x_ref, h_ref):
    x = _dot(a_ref[...], w_ref[...]) + r_ref[...]
    x_ref[...] = x
    ms = jnp.mean(x * x, axis=-1, keepdims=True)
    h_ref[...] = (x * lax.rsqrt(ms + EPS) * g_ref[...]).astype(h_ref.dtype)


def _matmul_res_norm(a, w, res, g, layer, tm=512):
    m, k = a.shape
    n = w.shape[-1]
    tm = min(tm, m)
    rows = lambda width: pl.BlockSpec((tm, width), lambda i: (i, 0))
    return pl.pallas_call(
        _mm_res_norm_body,
        name="matmul_residual_norm",
        grid=(m // tm,),
        in_specs=[rows(k), _weight_spec(w, layer, n, lambda i: 0), rows(n), pl.BlockSpec((1, n), lambda i: (0, 0))],
        out_specs=[rows(n), rows(n)],
        out_shape=[jax.ShapeDtypeStruct((m, n), F32), jax.ShapeDtypeStruct((m, n), BF16)],
        compiler_params=_params("parallel"),
    )(a, w, res, g.reshape(1, n))


def _rms_norm(x, g, out_dtype, tm=512):
    m, d = x.shape
    tm = min(tm, m)
    return pl.pallas_call(
        _norm_body,
        name="rms_norm",
        grid=(m // tm,),
        in_specs=[pl.BlockSpec((tm, d), lambda i: (i, 0)), pl.BlockSpec((1, d), lambda i: (0, 0))],
        out_specs=pl.BlockSpec((tm, d), lambda i: (i, 0)),
        out_shape=jax.ShapeDtypeStruct((m, d), out_dtype),
        compiler_params=_params("parallel"),
    )(x, g.reshape(1, d))


def _s5_matrices(lam_re, lam_im, b_re, b_im, c_re, c_im, d, log_dt):
    g_n, p_n, c_n, q = S5_GROUPS, S5_STATE, S5_GROUP, S5_Q
    hp = lax.Precision.HIGHEST
    dt = jnp.exp(log_dt)[:, None]
    n = jnp.arange(q + 1, dtype=F32)[:, None, None]
    mag = jnp.exp(n * (lam_re * dt))
    pw_re = mag * jnp.cos(n * (lam_im * dt))
    pw_im = mag * jnp.sin(n * (lam_im * dt))
    den = lam_re * lam_re + lam_im * lam_im
    nr, ni = pw_re[1] - 1.0, pw_im[1]
    f_re = (nr * lam_re + ni * lam_im) / den
    f_im = (ni * lam_re - nr * lam_im) / den
    e_re = f_re[..., None] * b_re - f_im[..., None] * b_im
    e_im = f_re[..., None] * b_im + f_im[..., None] * b_re
    cp_re = c_re[None] * pw_re[:, :, None, :] - c_im[None] * pw_im[:, :, None, :]
    cp_im = c_re[None] * pw_im[:, :, None, :] + c_im[None] * pw_re[:, :, None, :]
    kern = (jnp.einsum("tgop,gpi->tgio", cp_re[:q], e_re, precision=hp)
            - jnp.einsum("tgop,gpi->tgio", cp_im[:q], e_im, precision=hp))
    kern = kern.at[0].add(d.reshape(g_n, c_n)[:, :, None] * jnp.eye(c_n, dtype=F32))
    sg_n, gl_n = S5_SG, g_n // S5_SG
    rows = sg_n * S5_SGW
    lag = jnp.arange(q)[None, :] - jnp.arange(q)[:, None]
    kt = jnp.where((lag >= 0)[:, :, None, None, None], kern[jnp.clip(lag, 0, q - 1)], 0.0)
    kt = kt.reshape(q, q, sg_n, gl_n, c_n, c_n).transpose(2, 0, 3, 4, 1, 5)
    kt = kt.reshape(rows, q * c_n)
    rev_re, rev_im = pw_re[q - 1 - jnp.arange(q)], pw_im[q - 1 - jnp.arange(q)]
    et_re, et_im = e_re.transpose(0, 2, 1)[None], e_im.transpose(0, 2, 1)[None]
    ws_re = rev_re[:, :, None, :] * et_re - rev_im[:, :, None, :] * et_im
    ws_im = rev_re[:, :, None, :] * et_im + rev_im[:, :, None, :] * et_re
    ws = jnp.stack([ws_re, ws_im], axis=3)
    ws = ws.reshape(q, sg_n, gl_n, c_n, 2, p_n).transpose(1, 0, 2, 3, 4, 5).reshape(rows, 2 * p_n)
    wy = jnp.stack([cp_re[1:], -cp_im[1:]], axis=0)
    wy = wy.reshape(2, q, sg_n, gl_n, c_n, p_n).transpose(2, 0, 3, 5, 1, 4)
    wy = wy.reshape(rows, q * c_n)

    col = jnp.arange(S5_SGW)
    small = jnp.arange(q * c_n)[:, None]
    tok_rep = ((small // c_n == col[None, :] // LANES) & (small % c_n == col[None, :] % c_n)).astype(BF16)
    small = jnp.arange(2 * p_n)[:, None]
    st_rep = ((small // p_n == col[None, :] // S5_SB) & (small % p_n == col[None, :] % p_n)).astype(BF16)
    row = jnp.arange(rows)
    g_tok_r, g_tok_c = (row % LANES) // c_n, (col % LANES) // c_n
    g_st_r, g_st_c = (row % S5_SB) // p_n, (col % S5_SB) // p_n
    expand = lambda table, rep: jnp.dot(table.astype(BF16), rep, preferred_element_type=BF16)
    zero = jnp.zeros((), BF16)
    toep = jnp.where(g_tok_r[:, None] == g_tok_c[None, :], expand(kt, tok_rep), zero)
    w_state = jnp.where(g_tok_r[:, None] == g_st_c[None, :], expand(ws, st_rep), zero)
    w_out = jnp.where(g_st_r[:, None] == g_tok_c[None, :], expand(wy, tok_rep), zero)
    return toep, w_state, w_out, pw_re[q].reshape(1, S5_NSTATE), pw_im[q].reshape(1, S5_NSTATE)


def _s5_scan_body(ar_ref, ai_ref, s_ref, o_ref, c_ref):
    rows = BF16_ROWS
    nb = s_ref.shape[0]
    a_re, a_im = ar_ref[...], ai_ref[...]
    half = a_re.shape[1]

    @pl.when(pl.program_id(1) == 0)
    def _():
        c_ref[...] = jnp.zeros_like(c_ref)

    def block(i, carry):
        base = pl.multiple_of(i * rows, rows)
        xs = [s_ref[b, pl.ds(base, rows), :] for b in range(nb)]
        outs = [[] for _ in range(nb)]
        carry = list(carry)
        for r in range(rows):
            for b in range(nb):
                s_re, s_im = carry[b]
                outs[b].append(jnp.concatenate([s_re, s_im], axis=1))
                carry[b] = (a_re * s_re - a_im * s_im + xs[b][r:r + 1, :half],
                            a_re * s_im + a_im * s_re + xs[b][r:r + 1, half:])
        for b in range(nb):
            o_ref[b, pl.ds(base, rows), :] = jnp.concatenate(outs[b], axis=0).astype(o_ref.dtype)
        return tuple(carry)

    init = tuple((c_ref[b, :, :half], c_ref[b, :, half:]) for b in range(nb))
    last = lax.fori_loop(0, s_ref.shape[1] // rows, block, init)
    for b in range(nb):
        c_ref[b] = jnp.concatenate(last[b], axis=1)


def _s5_fold_body(h_ref, w_ref, o_ref, tok_ref):
    u = _dot(h_ref[...], w_ref[...])
    tc = o_ref.shape[0]
    for c in range(S5_SG):
        tok_ref[c] = u[:, c * LANES:(c + 1) * LANES]
        for j in range(S5_Q):
            lo = c * S5_SGW + j * LANES
            o_ref[:, lo:lo + LANES] = tok_ref[c, pl.ds(j, tc, stride=S5_Q), :].astype(o_ref.dtype)


def _s5_glu_body(y_ref, w_ref, o_ref, tok_ref):
    tc = y_ref.shape[0]
    for c in range(S5_SG):
        for j in range(S5_Q):
            lo = c * S5_SGW + j * LANES
            tok_ref[c, pl.ds(j, tc, stride=S5_Q), :] = y_ref[:, lo:lo + LANES]
    y = jax.nn.gelu(jnp.concatenate([tok_ref[c] for c in range(S5_SG)], axis=1))
    o_ref[...] = (y * jax.nn.sigmoid(_dot(y.astype(BF16), w_ref[...]))).astype(o_ref.dtype)


def _s5_mixer(h, w_u, mats, w_glu, bsz, seq, layer=None):
    toep, w_state, w_out, aq_re, aq_im = mats
    m = bsz * seq
    nc = seq // S5_Q
    width = S5_Q * MIX_W
    tm = min(1024, m)
    tc = tm // S5_Q
    uc = pl.pallas_call(
        _s5_fold_body,
        name="s5_input_proj",
        grid=(m // tm,),
        in_specs=[pl.BlockSpec((tm, D_MODEL), lambda i: (i, 0)), _weight_spec(w_u, layer, MIX_W, lambda i: 0)],
        out_specs=pl.BlockSpec((tc, width), lambda i: (i, 0)),
        out_shape=jax.ShapeDtypeStruct((m // S5_Q, width), BF16),
        scratch_shapes=[pltpu.VMEM((MIX_W // LANES, tm, LANES), F32)],
        compiler_params=_params("parallel"),
    )(h, w_u)
    y1s = _matmul_blockdiag(uc, jnp.concatenate([toep, w_state], axis=1), S5_SG)
    rb = min(S5_SCAN_ROWS, nc)
    s_in = pl.pallas_call(
        _s5_scan_body,
        name="s5_chunk_scan",
        grid=(S5_SG, nc // rb),
        in_specs=[
            pl.BlockSpec((1, S5_SB), lambda j, r: (0, j)),
            pl.BlockSpec((1, S5_SB), lambda j, r: (0, j)),
            pl.BlockSpec((bsz, rb, S5_SGW), lambda j, r: (0, r, 2 * j + 1)),
        ],
        out_specs=pl.BlockSpec((bsz, rb, S5_SGW), lambda j, r: (0, r, j)),
        out_shape=jax.ShapeDtypeStruct((bsz, nc, width), BF16),
        scratch_shapes=[pltpu.VMEM((bsz, 1, S5_SGW), F32)],
        compiler_params=_params("parallel", "arbitrary"),
    )(aq_re, aq_im, y1s.reshape(bsz, nc, 2 * width))
    y = _matmul_blockdiag(s_in.reshape(bsz * nc, width), w_out, S5_SG, res=y1s, res_block=lambda s: 2 * s)
    return pl.pallas_call(
        _s5_glu_body,
        name="s5_glu",
        grid=(m // tm,),
        in_specs=[pl.BlockSpec((tc, width), lambda i: (i, 0)), pl.BlockSpec((MIX_W, MIX_W), lambda i: (0, 0))],
        out_specs=pl.BlockSpec((tm, MIX_W), lambda i: (i, 0)),
        out_shape=jax.ShapeDtypeStruct((m, MIX_W), BF16),
        scratch_shapes=[pltpu.VMEM((MIX_W // LANES, tm, LANES), F32)],
        compiler_params=_params("parallel"),
    )(y, w_glu.astype(BF16))


def _lru_body(x_ref, g_ref, cw_ref, cb_ref, wa_ref, ba_ref, wx_ref, bx_ref, lam_ref, o_ref,
              halo_ref, h_ref, a_s, b_s):
    nb, t, _ = x_ref.shape

    @pl.when(pl.program_id(0) == 0)
    def _():
        halo_ref[...] = jnp.zeros_like(halo_ref)
        h_ref[...] = jnp.zeros_like(h_ref)

    soft = jax.nn.softplus(-lam_ref[...])
    for b in range(nb):
        x_in = x_ref[b]
        x = _causal_conv(x_in, halo_ref[b], cw_ref, cb_ref)
        halo_ref[b] = x_in[t - HALO:, :]
        xb = x.astype(BF16)
        r = jax.nn.sigmoid(_dot(xb, wa_ref[...]) + ba_ref[...])
        i = jax.nn.sigmoid(_dot(xb, wx_ref[...]) + bx_ref[...])
        log_a = (-LRU_C * r) * soft
        a = jnp.exp(log_a)
        a_s[b] = a
        b_s[b] = x * i * jnp.sqrt(1.0 - a * a)
    rows = 8

    def block(k, hs):
        base = pl.multiple_of(k * rows, rows)
        av = [a_s[b, pl.ds(base, rows), :] for b in range(nb)]
        bv = [b_s[b, pl.ds(base, rows), :] for b in range(nb)]
        hs = list(hs)
        outs = [[] for _ in range(nb)]
        for rr in range(rows):
            for b in range(nb):
                hs[b] = av[b][rr:rr + 1, :] * hs[b] + bv[b][rr:rr + 1, :]
                outs[b].append(hs[b])
        for b in range(nb):
            b_s[b, pl.ds(base, rows), :] = jnp.concatenate(outs[b], axis=0)
        return tuple(hs)

    hs = lax.fori_loop(0, t // rows, block, tuple(h_ref[b] for b in range(nb)))
    for b in range(nb):
        h_ref[b] = hs[b]
        o_ref[b] = (b_s[b] * jax.nn.gelu(g_ref[b])).astype(o_ref.dtype)


def _block_diag(w):
    h_n, n, _ = w.shape
    eye = jnp.eye(h_n, dtype=w.dtype)
    return (w[:, :, None, :] * eye[:, None, :, None]).reshape(h_n * n, h_n * n)


def _lru_mixer(proj, conv_w, conv_b, w_a, b_a, w_x, b_x, lam, bsz, seq):
    t = min(256, seq)
    vec = lambda v: v.reshape(1, MIX_W)
    const = lambda shape: pl.BlockSpec(shape, lambda c: (0, 0))
    proj3 = proj.reshape(bsz, seq, proj.shape[1])
    out = pl.pallas_call(
        _lru_body,
        name="rglru_mixer",
        grid=(seq // t,),
        in_specs=[
            pl.BlockSpec((bsz, t, MIX_W), lambda c: (0, c, PC_LRU_X // MIX_W)),
            pl.BlockSpec((bsz, t, MIX_W), lambda c: (0, c, PC_LRU_G // MIX_W)),
            const(conv_w.shape), const((1, MIX_W)),
            const((MIX_W, MIX_W)), const((1, MIX_W)),
            const((MIX_W, MIX_W)), const((1, MIX_W)), const((1, MIX_W)),
        ],
        out_specs=pl.BlockSpec((bsz, t, MIX_W), lambda c: (0, c, 0)),
        out_shape=jax.ShapeDtypeStruct((bsz, seq, MIX_W), BF16),
        scratch_shapes=[
            pltpu.VMEM((bsz, HALO, MIX_W), F32), pltpu.VMEM((bsz, 1, MIX_W), F32),
            pltpu.VMEM((bsz, t, MIX_W), F32), pltpu.VMEM((bsz, t, MIX_W), F32),
        ],
        compiler_params=_params("arbitrary"),
    )(proj3, proj3, conv_w, vec(conv_b), _block_diag(w_a).astype(BF16), vec(b_a),
      _block_diag(w_x).astype(BF16), vec(b_x), vec(lam))
    return out.reshape(bsz * seq, MIX_W)


def _m2_body(z_ref, xbc_ref, dt_ref, rep_ref, cw_ref, cb_ref, dtb_ref, alog_ref, d_ref, ng_ref, o_ref,
             halo_ref, st_ref):
    nb, q, _ = z_ref.shape
    hg = N_HEADS // M2_GROUPS
    gw = hg * HEAD_W
    seqs = range(nb)

    @pl.when(pl.program_id(1) == 0)
    def _():
        halo_ref[...] = jnp.zeros_like(halo_ref)
        st_ref[...] = jnp.zeros_like(st_ref)

    causal = _tril_mask(q)
    causal_b = causal.astype(BF16)
    neg_a = -jnp.exp(alog_ref[...])
    dt_raw = _exact_right([dt_ref[b] for b in seqs], rep_ref[...])
    xc = []
    for b in seqs:
        xbc = xbc_ref[b]
        conv = _causal_conv(xbc, halo_ref[b], cw_ref, cb_ref)
        halo_ref[b] = xbc[q - HALO:, :]
        xc.append(conv * jax.nn.sigmoid(conv))
    dt = [jax.nn.softplus(dt_raw[b] + dtb_ref[...]) for b in seqs]
    a_cs = [_exact_left(causal_b, dt[b] * neg_a) for b in seqs]
    xs = [xc[b][:, :MIX_W] for b in seqs]
    xd = [xs[b] * dt[b] for b in seqs]
    a_cs_t = [a_cs[b].T for b in seqs]
    a_last = [a_cs[b][q - 1:q, :] for b in seqs]
    xd_st = [(xd[b] * jnp.exp(a_last[b] - a_cs[b])).astype(BF16) for b in seqs]
    e_cs = [jnp.exp(a_cs[b]) for b in seqs]
    bg, cg, cb, y_off = {}, {}, {}, {}
    for b in seqs:
        for g in range(M2_GROUPS):
            lo = MIX_W + g * M2_STATE
            bg[b, g] = xc[b][:, lo:lo + M2_STATE].astype(BF16)
            cg[b, g] = xc[b][:, lo + M2_GROUPS * M2_STATE:lo + (M2_GROUPS + 1) * M2_STATE].astype(BF16)
            cb[b, g] = _dot_nt(cg[b, g], bg[b, g])
            y_off[b, g] = _dot(cg[b, g], st_ref[b, g].astype(BF16)) * e_cs[b][:, g * gw:(g + 1) * gw]
    ys = [[] for _ in seqs]
    for g in range(M2_GROUPS):
        for hh in range(hg):
            for b in seqs:
                lo = (g * hg + hh) * HEAD_W
                col = a_cs[b][:, lo:lo + HEAD_W]
                col = jnp.concatenate([col] * (q // HEAD_W), axis=1)
                seg = col - a_cs_t[b][lo:lo + 1, :]
                dec = jnp.exp(jnp.where(causal, seg, -jnp.inf))
                y_d = _dot((cb[b, g] * dec).astype(BF16), xd[b][:, lo:lo + HEAD_W].astype(BF16))
                ys[b].append(y_d + y_off[b, g][:, hh * HEAD_W:(hh + 1) * HEAD_W])
    for b in seqs:
        for g in range(M2_GROUPS):
            upd = _dot_tn(bg[b, g], xd_st[b][:, g * gw:(g + 1) * gw])
            st_ref[b, g] = st_ref[b, g] * jnp.exp(a_last[b][:, g * gw:(g + 1) * gw]) + upd
    for b in seqs:
        y = jnp.concatenate(ys[b], axis=1) + d_ref[...] * xs[b]
        z = z_ref[b]
        y = y * (z * jax.nn.sigmoid(z))
        ms = jnp.mean(y * y, axis=-1, keepdims=True)
        o_ref[b] = (y * lax.rsqrt(ms + EPS) * ng_ref[...]).astype(o_ref.dtype)


def _per_head(v):
    return jnp.repeat(v, HEAD_W).reshape(1, -1)


def _m2_mixer(proj, conv_w, conv_b, dt_bias, a_log, d, norm_g, bsz, seq):
    q = min(M2_Q, seq)
    nc = seq // q
    nb = M2_BATCH if bsz % M2_BATCH == 0 else 1
    const = lambda shape: pl.BlockSpec(shape, lambda b, c: (0, 0))
    head_of_lane = jnp.arange(MIX_W)[None, :] // HEAD_W
    rep = (jnp.arange(LANES)[:, None] == head_of_lane).astype(BF16)
    proj3 = proj.reshape(bsz, seq, proj.shape[1])
    out = pl.pallas_call(
        _m2_body,
        name="mamba2_mixer",
        grid=(bsz // nb, nc),
        in_specs=[
            pl.BlockSpec((nb, q, MIX_W), lambda b, c: (b, c, PC_Z // MIX_W)),
            pl.BlockSpec((nb, q, M2_CONV_DIM), lambda b, c: (b, c, PC_XBC // M2_CONV_DIM)),
            pl.BlockSpec((nb, q, LANES), lambda b, c: (b, c, PC_DT // LANES)),
            const((LANES, MIX_W)), const(conv_w.shape), const((1, M2_CONV_DIM)),
            const((1, MIX_W)), const((1, MIX_W)), const((1, MIX_W)), const((1, MIX_W)),
        ],
        out_specs=pl.BlockSpec((nb, q, MIX_W), lambda b, c: (b, c, 0)),
        out_shape=jax.ShapeDtypeStruct((bsz, seq, MIX_W), BF16),
        scratch_shapes=[
            pltpu.VMEM((nb, HALO, M2_CONV_DIM), F32),
            pltpu.VMEM((nb, M2_GROUPS, M2_STATE, (N_HEADS // M2_GROUPS) * HEAD_W), F32),
        ],
        compiler_params=_params("parallel", "arbitrary"),
    )(proj3, proj3, proj3, rep, conv_w, conv_b.reshape(1, -1), _per_head(dt_bias), _per_head(a_log),
      _per_head(d), norm_g.reshape(1, -1))
    return out.reshape(bsz * seq, MIX_W)


def _rw_body(p_ref, pl_ref, mu_ref, mul_ref, w0_ref, w2_ref, a0_ref, a2_ref, g2_ref, kk_ref, ka_ref, rk_ref,
             lng_ref, lnb_ref, ones_ref, o_ref, prev_ref, prevl_ref, st_ref):
    nb, q, _ = p_ref.shape
    rows = nb * q

    @pl.when(pl.program_id(1) == 0)
    def _():
        prev_ref[...] = jnp.zeros_like(prev_ref)
        prevl_ref[...] = jnp.zeros_like(prevl_ref)
        st_ref[...] = jnp.zeros_like(st_ref)

    def token_mix(x_ref, last_ref, m_ref):
        x = x_ref[...].reshape(rows, x_ref.shape[2])
        row = lax.broadcasted_iota(jnp.int32, x.shape, 0)
        shifted = pltpu.roll(x, 1, axis=0)
        for b in range(nb):
            shifted = jnp.where(row == b * q, last_ref[b, 0:1, :], shifted)
            last_ref[b, 0:1, :] = x[(b + 1) * q - 1:(b + 1) * q, :]
        return x + (shifted - x) * m_ref[...]

    p = token_mix(p_ref, prev_ref, mu_ref)
    lora = token_mix(pl_ref, prevl_ref, mul_ref)
    r = p[:, 0:MIX_W]
    k = p[:, MIX_W:2 * MIX_W]
    v = p[:, 2 * MIX_W:3 * MIX_W]
    w = -jax.nn.softplus(-(w0_ref[...] + _dot(jnp.tanh(lora).astype(BF16), w2_ref[...]))) - 0.5
    log_w = -jnp.exp(w)
    a_sig = jax.nn.sigmoid(a0_ref[...] + _dot(lora.astype(BF16), a2_ref[...]))
    gate = _dot(jax.nn.sigmoid(lora).astype(BF16), g2_ref[...])
    ones = ones_ref[...]
    kk = k * kk_ref[...]
    k = k * (1.0 + (a_sig - 1.0) * ka_ref[...])
    kk_sq, rk_sum = _exact_right([kk * kk, r * k * rk_ref[...]], ones)
    kk = kk / jnp.maximum(jnp.sqrt(kk_sq), 1e-12)
    a_vec = -kk
    b_vec = kk * a_sig

    rr = lax.broadcasted_iota(jnp.int32, (rows, rows), 0)
    cc = lax.broadcasted_iota(jnp.int32, (rows, rows), 1)
    cum_mask = jnp.where((rr >= cc) & ((rr // q) == (cc // q)), 1.0, 0.0).astype(BF16)
    gam = _exact_left(cum_mask, log_w, terms=2)
    g_last = jnp.concatenate(
        [jnp.broadcast_to(gam[(b + 1) * q - 1:(b + 1) * q, :], (q, MIX_W)) for b in range(nb)], axis=0)
    e_neg = jnp.exp(-gam)
    e_last = jnp.exp(g_last)
    e_rem = e_last * e_neg
    a_t = a_vec * jnp.exp(gam - log_w)
    r_t = r * jnp.exp(gam)
    b_t = b_vec * e_neg
    k_t = k * e_neg
    b_h = b_vec * e_rem
    k_h = k * e_rem
    assert q == HEAD_W
    row2 = lax.broadcasted_iota(jnp.int32, (q, 2 * HEAD_W), 0)
    lane2 = lax.broadcasted_iota(jnp.int32, (q, 2 * HEAD_W), 1)
    low = lane2 < HEAD_W
    keep_n = lane2 < row2
    keep_t = jnp.where(low, lane2, lane2 - HEAD_W) <= row2
    strict = _tril_mask(q, strict=True)
    pairs = [(b, h) for b in range(nb) for h in range(N_HEADS)]
    zeros_x = jnp.zeros((q, HEAD_W), F32)

    def blk(t, key):
        b, h = key
        return t[b * q:(b + 1) * q, h * HEAD_W:(h + 1) * HEAD_W]

    p0, p1, s0, z, w_v = {}, {}, {}, {}, {}
    for key in pairs:
        ar = jnp.concatenate([blk(a_t, key), blk(r_t, key)], axis=0).astype(BF16)
        s0[key] = st_ref[key[0], key[1]]
        b_blk = blk(b_t, key)
        rhs = jnp.concatenate([blk(k_t, key), s0[key], b_blk, b_blk], axis=0).astype(BF16)
        prod = _dot_nt(ar, rhs)
        p0[key] = prod[:, :2 * HEAD_W]
        p1[key] = prod[:, 2 * HEAD_W:]
    for key in pairs:
        a_k = jnp.where(strict, p0[key][:q, :HEAD_W], 0.0).astype(BF16)
        w_v[key] = jnp.concatenate([zeros_x, blk(v, key)], axis=1).astype(BF16)
        z[key] = jnp.where(keep_n, p1[key][:q], jnp.where(low, 0.0, p0[key][:q])) + _dot(a_k, w_v[key])
    steps = max(1, (q - 1).bit_length())
    for _ in range(steps):
        for key in pairs:
            zb = z[key].astype(BF16)
            z[key] = jnp.where(low, 0.0, z[key]) + _dot(zb[:, :HEAD_W], zb)
    ys, w_vu = {}, {}
    for key in pairs:
        w_vu[key] = jnp.concatenate([w_v[key], jnp.where(low, 0.0, z[key]).astype(BF16)], axis=0)
        t_mat = jnp.where(keep_t, jnp.where(low, p0[key][q:], p1[key][q:]), 0.0)
        y_hi = jnp.where(low, 0.0, p0[key][q:]) + _dot(t_mat.astype(BF16), w_vu[key])
        ys[key] = y_hi[:, HEAD_W:]

    def update_states(keys):
        for key in keys:
            khb = jnp.concatenate([blk(k_h, key), blk(b_h, key)], axis=0).astype(BF16)
            st_ref[key[0], key[1]] = s0[key] * blk(e_last, key)[0:1, :] + _dot_tn(w_vu[key], khb)[HEAD_W:, :]

    y = jnp.concatenate(
        [jnp.concatenate([ys[(b, h)] for h in range(N_HEADS)], axis=1) for b in range(nb)], axis=0)
    inv_n = 1.0 / HEAD_W
    mean = _exact_right([y], ones)[0] * inv_n
    update_states(pairs[:len(pairs) // 2])
    yc = y - mean
    var = _exact_right([yc * yc], ones)[0] * inv_n
    update_states(pairs[len(pairs) // 2:])
    y = yc * lax.rsqrt(var + RW_LN_EPS) * lng_ref[...] + lnb_ref[...]
    bonus = rk_sum * v
    o_ref[...] = ((y + bonus) * gate).astype(o_ref.dtype).reshape(o_ref.shape)


def _pad_rows(w, rows, off):
    return jnp.zeros((rows, w.shape[1]), w.dtype).at[off:off + w.shape[0]].set(w)


def _rw_mixer(proj, mu, w0, w2, a0, a2, g2, k_k, k_a, r_k, ln_g, ln_b, bsz, seq):
    q = min(RW_Q, seq)
    nc = seq // q
    nb = RW_BATCH if bsz % RW_BATCH == 0 else 1
    vec = lambda t: t.reshape(1, MIX_W)
    const = lambda shape: pl.BlockSpec(shape, lambda b, c: (0, 0))
    offs = (0, RW_LORA_RANKS[0], RW_LORA_RANKS[0] + RW_LORA_RANKS[1])
    lora = lambda t, off: _pad_rows(t, RW_LORA_W, off).astype(BF16)
    mu_rkv, mu_lora = mu[:RW_RKV_W].reshape(1, RW_RKV_W), _pad_rows(mu[RW_RKV_W:, None], RW_LORA_W, 0).reshape(1, RW_LORA_W)
    ones = _block_diag(jnp.ones((N_HEADS, HEAD_W, HEAD_W), F32)).astype(BF16)
    proj3 = proj.reshape(bsz, seq, proj.shape[1])
    out = pl.pallas_call(
        _rw_body,
        name="rwkv7_mixer",
        grid=(bsz // nb, nc),
        in_specs=[
            pl.BlockSpec((nb, q, RW_RKV_W), lambda b, c: (b, c, PC_RKV // RW_RKV_W)),
            pl.BlockSpec((nb, q, RW_LORA_W), lambda b, c: (b, c, PC_LORA // RW_LORA_W)),
            const((1, RW_RKV_W)), const((1, RW_LORA_W)), const((1, MIX_W)), const((RW_LORA_W, MIX_W)),
            const((1, MIX_W)), const((RW_LORA_W, MIX_W)), const((RW_LORA_W, MIX_W)),
            const((1, MIX_W)), const((1, MIX_W)), const((1, MIX_W)),
            const((1, MIX_W)), const((1, MIX_W)), const((MIX_W, MIX_W)),
        ],
        out_specs=pl.BlockSpec((nb, q, MIX_W), lambda b, c: (b, c, 0)),
        out_shape=jax.ShapeDtypeStruct((bsz, seq, MIX_W), BF16),
        scratch_shapes=[pltpu.VMEM((nb, HALO, RW_RKV_W), F32), pltpu.VMEM((nb, HALO, RW_LORA_W), F32),
                        pltpu.VMEM((nb, N_HEADS, HEAD_W, HEAD_W), F32)],
        compiler_params=_params("parallel", "arbitrary"),
    )(proj3, proj3, mu_rkv, mu_lora, vec(w0), lora(w2, offs[0]), vec(a0), lora(a2, offs[1]), lora(g2, offs[2]),
      vec(k_k), vec(k_a), vec(r_k), vec(ln_g), vec(ln_b), ones)
    return out.reshape(bsz * seq, MIX_W)


def _merge_body(h_ref, *refs):
    nb = (len(refs) - 2) // 2
    y_refs, wg_refs, wp_ref, o_ref = refs[:nb], refs[nb:2 * nb], refs[2 * nb], refs[2 * nb + 1]
    h = h_ref[...]
    acc = None
    for kk in range(nb):
        term = jax.nn.sigmoid(_dot(h, wg_refs[kk][...])) * _dot(y_refs[kk][...], wp_ref[kk])
        acc = term if acc is None else acc + term
    o_ref[...] = acc.astype(o_ref.dtype)


def _merge(h, ys, w_gate, w_branch, layer=None, tm=512, tn=512):
    m = h.shape[0]
    tm = min(tm, m)
    nb = len(ys)
    y_spec = pl.BlockSpec((tm, MIX_W), lambda j, i: (i, 0))
    gate_specs = [_weight_spec(w_gate, layer, tn, lambda j, i, k=k: k * (D_MODEL // tn) + j) for k in range(nb)]
    if w_branch.ndim == 3:
        branch_spec = pl.BlockSpec((nb, MIX_W, tn), lambda j, i: (0, 0, j))
    else:
        branch_spec = pl.BlockSpec((None, nb, MIX_W, tn), lambda j, i: (layer, 0, 0, j))
    return pl.pallas_call(
        _merge_body,
        name="gated_merge",
        grid=(D_MODEL // tn, m // tm),
        in_specs=[pl.BlockSpec((tm, D_MODEL), lambda j, i: (i, 0))] + [y_spec] * nb + gate_specs + [branch_spec],
        out_specs=pl.BlockSpec((tm, tn), lambda j, i: (i, j)),
        out_shape=jax.ShapeDtypeStruct((m, D_MODEL), BF16),
        compiler_params=_params("parallel", "parallel"),
    )(h, *ys, *([w_gate] * nb), w_branch)


def _ffn_body(h_ref, hh_ref, wg_ref, wu_ref, cw_ref, cb_ref, o_ref, wg_s, wu_s, *, tiles_per_seq):
    t = h_ref.shape[0]
    nh = hh_ref.shape[0]

    @pl.when(pl.program_id(1) == 0)
    def _():
        wg_s[...] = wg_ref[...].astype(BF16)
        wu_s[...] = wu_ref[...].astype(BF16)

    h = h_ref[...]
    seq_start = (pl.program_id(1) % tiles_per_seq) == 0
    halo = jnp.where(seq_start, jnp.zeros_like(hh_ref[...]), hh_ref[...])
    g_all = _dot(jnp.concatenate([halo, h], axis=0), wg_s[...])
    k_w = cw_ref.shape[0]
    u = cb_ref[...] + g_all[nh:, :] * cw_ref[k_w - 1:k_w, :]
    for j in range(k_w - 1):
        lag = k_w - 1 - j
        u = u + g_all[nh - lag:nh - lag + t, :] * cw_ref[j:j + 1, :]
    o_ref[...] = (jax.nn.gelu(u) * _dot(h, wu_s[...])).astype(o_ref.dtype)


def _ffn_act(h, w_gate, w_up, conv_w, conv_b, seq, layer=None, tm=1024, tf=512):
    m = h.shape[0]
    tm = min(tm, seq)
    f = w_gate.shape[-1]
    halo_blocks = tm // BF16_ROWS
    return pl.pallas_call(
        functools.partial(_ffn_body, tiles_per_seq=seq // tm),
        name="ffn_gate_up",
        grid=(f // tf, m // tm),
        in_specs=[
            pl.BlockSpec((tm, D_MODEL), lambda j, i: (i, 0)),
            pl.BlockSpec((BF16_ROWS, D_MODEL), lambda j, i: (jnp.maximum(i * halo_blocks - 1, 0), 0)),
            _weight_spec(w_gate, layer, tf, lambda j, i: j),
            _weight_spec(w_up, layer, tf, lambda j, i: j),
            pl.BlockSpec((conv_w.shape[0], tf), lambda j, i: (0, j)),
            pl.BlockSpec((1, tf), lambda j, i: (0, j)),
        ],
        out_specs=pl.BlockSpec((tm, tf), lambda j, i: (i, j)),
        out_shape=jax.ShapeDtypeStruct((m, f), BF16),
        scratch_shapes=[pltpu.VMEM((D_MODEL, tf), BF16), pltpu.VMEM((D_MODEL, tf), BF16)],
        compiler_params=_params("parallel", "arbitrary"),
    )(h, h, w_gate, w_up, conv_w, conv_b.reshape(1, f))


def _w_in_body(w_ref, s5_ref, proj_ref, gate_ref):
    w = w_ref[...]
    tk = w.shape[1]
    row = 0
    piece = {}
    for name, width in (("s5_u", MIX_W), ("lru_x", MIX_W), ("lru_g", MIX_W), ("z", MIX_W), ("xbc", M2_CONV_DIM),
                        ("dt", N_HEADS), ("rkv", RW_RKV_W), ("lora", sum(RW_LORA_RANKS))):
        piece[name] = w[row:row + width, :]
        row += width

    def put(ref, off, t, width):
        if t.shape[0] < width:
            t = jnp.concatenate([t, jnp.zeros((width - t.shape[0], tk), t.dtype)], axis=0)
        ref[:, off:off + width] = t.T.astype(ref.dtype)

    put(s5_ref, 0, piece["s5_u"], MIX_W)
    put(proj_ref, PC_XBC, piece["xbc"], M2_CONV_DIM)
    put(proj_ref, PC_Z, piece["z"], MIX_W)
    put(proj_ref, PC_RKV, piece["rkv"], RW_RKV_W)
    put(proj_ref, PC_LRU_X, piece["lru_x"], MIX_W)
    put(proj_ref, PC_LRU_G, piece["lru_g"], MIX_W)
    put(proj_ref, PC_LORA, piece["lora"], RW_LORA_W)
    put(proj_ref, PC_DT, piece["dt"], PROJ_W - PC_DT)
    put(gate_ref, 0, w[row:, :], w.shape[0] - row)


def _prepare_w_in(w_in, tk=128):
    depth, d, cols = w_in.shape
    gate_w = cols - (4 * MIX_W + M2_CONV_DIM + N_HEADS + RW_RKV_W + sum(RW_LORA_RANKS))
    out = lambda width: pl.BlockSpec((None, tk, width), lambda l, r: (l, r, 0))
    return pl.pallas_call(
        _w_in_body,
        name="w_in_layout",
        grid=(depth, d // tk),
        in_specs=[pl.BlockSpec((None, cols, tk), lambda l, r: (l, 0, r))],
        out_specs=[out(MIX_W), out(PROJ_W), out(gate_w)],
        out_shape=[jax.ShapeDtypeStruct((depth, d, MIX_W), BF16), jax.ShapeDtypeStruct((depth, d, PROJ_W), BF16),
                   jax.ShapeDtypeStruct((depth, d, gate_w), BF16)],
        compiler_params=_params("parallel", "parallel"),
    )(jnp.swapaxes(w_in, 1, 2))


def kernel(x, norm_mix_g, w_in, s5_lambda_re, s5_lambda_im, s5_b_re, s5_b_im, s5_c_re, s5_c_im, s5_d, s5_log_dt, s5_w_glu, lru_conv_w, lru_conv_b, lru_w_a, lru_b_a, lru_w_x, lru_b_x, lru_lambda, m2_conv_w, m2_conv_b, m2_dt_bias, m2_a_log, m2_d, m2_norm_g, rw_mu, rw_w0, rw_w2, rw_a0, rw_a2, rw_g2, rw_k_k, rw_k_a, rw_r_k, rw_ln_g, rw_ln_b, w_branch, w_out, norm_ffn_g, w_ffn_gate, w_ffn_up, ffn_conv_w, ffn_conv_b, w_ffn_down, final_norm_g):
    bsz, seq, d = x.shape
    depth = w_in.shape[0]
    xf = x.reshape(bsz * seq, d)
    wb_all, wo_all, wd_all = w_branch.astype(BF16), w_out.astype(BF16), w_ffn_down.astype(BF16)
    w_s5_all, w_proj_all, w_gate_all = _prepare_w_in(w_in)
    for l in range(depth):
        h = _rms_norm(xf, norm_mix_g[l], BF16)
        proj = _matmul(h, w_proj_all, tm=2048, layer=l)
        mats = _s5_matrices(s5_lambda_re[l], s5_lambda_im[l], s5_b_re[l], s5_b_im[l],
                            s5_c_re[l], s5_c_im[l], s5_d[l], s5_log_dt[l])
        y_a = _s5_mixer(h, w_s5_all, mats, s5_w_glu[l], bsz, seq, layer=l)
        y_b = _lru_mixer(proj, lru_conv_w[l], lru_conv_b[l], lru_w_a[l], lru_b_a[l],
                         lru_w_x[l], lru_b_x[l], lru_lambda[l], bsz, seq)
        y_c = _m2_mixer(proj, m2_conv_w[l], m2_conv_b[l], m2_dt_bias[l], m2_a_log[l],
                        m2_d[l], m2_norm_g[l], bsz, seq)
        y_d = _rw_mixer(proj, rw_mu[l], rw_w0[l], rw_w2[l], rw_a0[l], rw_a2[l], rw_g2[l],
                        rw_k_k[l], rw_k_a[l], rw_r_k[l].reshape(-1), rw_ln_g[l], rw_ln_b[l], bsz, seq)
        merged = _merge(h, (y_a, y_b, y_c, y_d), w_gate_all, wb_all, layer=l)
        xf, h = _matmul_res_norm(merged, wo_all, xf, norm_ffn_g[l], l)
        act = _ffn_act(h, w_ffn_gate, w_ffn_up, ffn_conv_w[l], ffn_conv_b[l], seq, layer=l)
        xf = _matmul(act, wd_all, res=xf, tm=1024, layer=l)
    return _rms_norm(xf, final_norm_g, F32).reshape(bsz, seq, d)
```

```python
import functools

import jax
import jax.numpy as jnp
from jax import lax
from jax.experimental import pallas as pl
from jax.experimental.pallas import tpu as pltpu

F32 = jnp.float32
BF16 = jnp.bfloat16

D_MODEL = 2048
MIX_W = 512
HEAD_W = 64
N_HEADS = MIX_W // HEAD_W
S5_GROUP = 16
S5_GROUPS = MIX_W // S5_GROUP
S5_STATE = 64
S5_Q = 8
S5_NSTATE = S5_GROUPS * S5_STATE
LANES = 128
S5_SG = MIX_W // LANES
S5_SGW = S5_Q * LANES
S5_SB = S5_NSTATE // S5_SG
S5_SCAN_ROWS = 512
LRU_C = 8.0
M2_GROUPS = 2
M2_STATE = 128
M2_CONV_DIM = MIX_W + 2 * M2_GROUPS * M2_STATE
M2_Q = 128
M2_BATCH = 2
RW_Q = 64
RW_BATCH = 4
RW_RKV_W = 3 * MIX_W
RW_LORA_RANKS = (32, 32, 96)
RW_LORA_W = 256
RW_LN_EPS = 64e-5
EPS = 1e-6
HALO = 8
SUBLANES = 8
BF16_ROWS = 16

PC_XBC = 0
PC_Z = 1024
PC_RKV = 1536
PC_LRU_X = 3072
PC_LRU_G = 3584
PC_LORA = 4096
PC_DT = 4352
PROJ_W = 4608

VMEM_LIMIT_BYTES = 50 * 1024 * 1024


def _params(*sem):
    return pltpu.CompilerParams(dimension_semantics=sem, vmem_limit_bytes=VMEM_LIMIT_BYTES)


def _dot(a, b):
    return jnp.dot(a, b, preferred_element_type=F32)


def _dot_nt(a, b):
    return lax.dot_general(a, b, (((1,), (1,)), ((), ())), preferred_element_type=F32)


def _dot_tn(a, b):
    return lax.dot_general(a, b, (((0,), (0,)), ((), ())), preferred_element_type=F32)


def _split(x, terms):
    out = []
    for _ in range(terms - 1):
        hi = x.astype(BF16)
        out.append(hi)
        x = x - hi.astype(F32)
    out.append(x.astype(BF16))
    return out


def _exact_left(m, x, terms=3):
    return sum(_dot(m, p) for p in _split(x, terms))


def _exact_right(xs, m, terms=2):
    n = xs[0].shape[0]
    stacked = jnp.concatenate([p for x in xs for p in _split(x, terms)], axis=0)
    out = _dot(stacked, m)
    return [sum(out[(i * terms + t) * n:(i * terms + t + 1) * n] for t in range(terms))
            for i in range(len(xs))]


def _tril_mask(n, strict=False):
    r = lax.broadcasted_iota(jnp.int32, (n, n), 0)
    c = lax.broadcasted_iota(jnp.int32, (n, n), 1)
    return (r > c) if strict else (r >= c)


def _causal_conv(x, halo, w_ref, b_ref):
    k_w = w_ref.shape[0]
    t = x.shape[0]
    xe = jnp.concatenate([halo, x], axis=0)
    out = b_ref[...] + x * w_ref[k_w - 1:k_w, :]
    for j in range(k_w - 1):
        lag = k_w - 1 - j
        out = out + xe[HALO - lag:HALO - lag + t, :] * w_ref[j:j + 1, :]
    return out


def _mm_body(a_ref, b_ref, o_ref):
    o_ref[...] = _dot(a_ref[...], b_ref[...]).astype(o_ref.dtype)


def _mm_res_body(a_ref, b_ref, r_ref, o_ref):
    o_ref[...] = (_dot(a_ref[...], b_ref[...]) + r_ref[...]).astype(o_ref.dtype)


def _weight_spec(w, layer, tn, col_of):
    k = w.shape[-2]
    if w.ndim == 2:
        return pl.BlockSpec((k, tn), lambda *g: (0, col_of(*g)))
    return pl.BlockSpec((None, k, tn), lambda *g: (layer, 0, col_of(*g)))


def _matmul(a, b, res=None, out_dtype=F32, tm=1024, tn=512, n_outer=False, layer=None):
    m, k = a.shape
    n = b.shape[-1]
    tm, tn = min(tm, m), min(tn, n)
    assert m % tm == 0 and n % tn == 0
    if n_outer:
        grid = (n // tn, m // tm)
        row = lambda j, i: (i, 0)
        col = _weight_spec(b, layer, tn, lambda j, i: j)
        out = lambda j, i: (i, j)
    else:
        grid = (m // tm, n // tn)
        row = lambda i, j: (i, 0)
        col = _weight_spec(b, layer, tn, lambda i, j: j)
        out = lambda i, j: (i, j)
    in_specs = [pl.BlockSpec((tm, k), row), col]
    args = [a, b]
    body = _mm_body
    if res is not None:
        in_specs.append(pl.BlockSpec((tm, tn), out))
        args.append(res)
        body = _mm_res_body
    return pl.pallas_call(
        body,
        name="matmul",
        grid=grid,
        in_specs=in_specs,
        out_specs=pl.BlockSpec((tm, tn), out),
        out_shape=jax.ShapeDtypeStruct((m, n), out_dtype),
        compiler_params=_params("parallel", "parallel"),
    )(*args)


def _matmul_blockdiag(a, w, groups, res=None, res_block=None, tm=1024):
    m = a.shape[0]
    k, n = a.shape[1] // groups, w.shape[1]
    tm = min(tm, m)
    out_spec = pl.BlockSpec((tm, n), lambda i, s: (i, s))
    in_specs = [pl.BlockSpec((tm, k), lambda i, s: (i, s)), pl.BlockSpec((k, n), lambda i, s: (s, 0))]
    args = [a, w]
    if res is not None:
        in_specs.append(pl.BlockSpec((tm, n), lambda i, s: (i, res_block(s))))
        args.append(res)
    return pl.pallas_call(
        _mm_body if res is None else _mm_res_body,
        name="matmul_blockdiag",
        grid=(m // tm, groups),
        in_specs=in_specs,
        out_specs=out_spec,
        out_shape=jax.ShapeDtypeStruct((m, groups * n), F32),
        compiler_params=_params("parallel", "parallel"),
    )(*args)


def _norm_body(x_ref, g_ref, o_ref):
    x = x_ref[...]
    ms = jnp.mean(x * x, axis=-1, keepdims=True)
    o_ref[...] = (x * lax.rsqrt(ms + EPS) * g_ref[...]).astype(o_ref.dtype)


def _mm_res_norm_body(a_ref, w_ref, r_ref, g_ref, x_ref, h_ref):
    x = _dot(a_ref[...], w_ref[...]) + r_ref[...]
    x_ref[...] = x
    ms = jnp.mean(x * x, axis=-1, keepdims=True)
    h_ref[...] = (x * lax.rsqrt(ms + EPS) * g_ref[...]).astype(h_ref.dtype)


def _matmul_res_norm(a, w, res, g, layer, tm=512):
    m, k = a.shape
    n = w.shape[-1]
    tm = min(tm, m)
    rows = lambda width: pl.BlockSpec((tm, width), lambda i: (i, 0))
    return pl.pallas_call(
        _mm_res_norm_body,
        name="matmul_residual_norm",
        grid=(m // tm,),
        in_specs=[rows(k), _weight_spec(w, layer, n, lambda i: 0), rows(n), pl.BlockSpec((1, n), lambda i: (0, 0))],
        out_specs=[rows(n), rows(n)],
        out_shape=[jax.ShapeDtypeStruct((m, n), F32), jax.ShapeDtypeStruct((m, n), BF16)],
        compiler_params=_params("parallel"),
    )(a, w, res, g.reshape(1, n))


def _rms_norm(x, g, out_dtype, tm=512):
    m, d = x.shape
    tm = min(tm, m)
    return pl.pallas_call(
        _norm_body,
        name="rms_norm",
        grid=(m // tm,),
        in_specs=[pl.BlockSpec((tm, d), lambda i: (i, 0)), pl.BlockSpec((1, d), lambda i: (0, 0))],
        out_specs=pl.BlockSpec((tm, d), lambda i: (i, 0)),
        out_shape=jax.ShapeDtypeStruct((m, d), out_dtype),
        compiler_params=_params("parallel"),
    )(x, g.reshape(1, d))


def _s5_matrices(lam_re, lam_im, b_re, b_im, c_re, c_im, d, log_dt):
    g_n, p_n, c_n, q = S5_GROUPS, S5_STATE, S5_GROUP, S5_Q
    hp = lax.Precision.HIGHEST
    dt = jnp.exp(log_dt)[:, None]
    n = jnp.arange(q + 1, dtype=F32)[:, None, None]
    mag = jnp.exp(n * (lam_re * dt))
    pw_re = mag * jnp.cos(n * (lam_im * dt))
    pw_im = mag * jnp.sin(n * (lam_im * dt))
    den = lam_re * lam_re + lam_im * lam_im
    nr, ni = pw_re[1] - 1.0, pw_im[1]
    f_re = (nr * lam_re + ni * lam_im) / den
    f_im = (ni * lam_re - nr * lam_im) / den
    e_re = f_re[..., None] * b_re - f_im[..., None] * b_im
    e_im = f_re[..., None] * b_im + f_im[..., None] * b_re
    cp_re = c_re[None] * pw_re[:, :, None, :] - c_im[None] * pw_im[:, :, None, :]
    cp_im = c_re[None] * pw_im[:, :, None, :] + c_im[None] * pw_re[:, :, None, :]
    kern = (jnp.einsum("tgop,gpi->tgio", cp_re[:q], e_re, precision=hp)
            - jnp.einsum("tgop,gpi->tgio", cp_im[:q], e_im, precision=hp))
    kern = kern.at[0].add(d.reshape(g_n, c_n)[:, :, None] * jnp.eye(c_n, dtype=F32))
    sg_n, gl_n = S5_SG, g_n // S5_SG
    rows = sg_n * S5_SGW
    lag = jnp.arange(q)[None, :] - jnp.arange(q)[:, None]
    kt = jnp.where((lag >= 0)[:, :, None, None, None], kern[jnp.clip(lag, 0, q - 1)], 0.0)
    kt = kt.reshape(q, q, sg_n, gl_n, c_n, c_n).transpose(2, 0, 3, 4, 1, 5)
    kt = kt.reshape(rows, q * c_n)
    rev_re, rev_im = pw_re[q - 1 - jnp.arange(q)], pw_im[q - 1 - jnp.arange(q)]
    et_re, et_im = e_re.transpose(0, 2, 1)[None], e_im.transpose(0, 2, 1)[None]
    ws_re = rev_re[:, :, None, :] * et_re - rev_im[:, :, None, :] * et_im
    ws_im = rev_re[:, :, None, :] * et_im + rev_im[:, :, None, :] * et_re
    ws = jnp.stack([ws_re, ws_im], axis=3)
    ws = ws.reshape(q, sg_n, gl_n, c_n, 2, p_n).transpose(1, 0, 2, 3, 4, 5).reshape(rows, 2 * p_n)
    wy = jnp.stack([cp_re[1:], -cp_im[1:]], axis=0)
    wy = wy.reshape(2, q, sg_n, gl_n, c_n, p_n).transpose(2, 0, 3, 5, 1, 4)
    wy = wy.reshape(rows, q * c_n)

    col = jnp.arange(S5_SGW)
    small = jnp.arange(q * c_n)[:, None]
    tok_rep = ((small // c_n == col[None, :] // LANES) & (small % c_n == col[None, :] % c_n)).astype(BF16)
    small = jnp.arange(2 * p_n)[:, None]
    st_rep = ((small // p_n == col[None, :] // S5_SB) & (small % p_n == col[None, :] % p_n)).astype(BF16)
    row = jnp.arange(rows)
    g_tok_r, g_tok_c = (row % LANES) // c_n, (col % LANES) // c_n
    g_st_r, g_st_c = (row % S5_SB) // p_n, (col % S5_SB) // p_n
    expand = lambda table, rep: jnp.dot(table.astype(BF16), rep, preferred_element_type=BF16)
    zero = jnp.zeros((), BF16)
    toep = jnp.where(g_tok_r[:, None] == g_tok_c[None, :], expand(kt, tok_rep), zero)
    w_state = jnp.where(g_tok_r[:, None] == g_st_c[None, :], expand(ws, st_rep), zero)
    w_out = jnp.where(g_st_r[:, None] == g_tok_c[None, :], expand(wy, tok_rep), zero)
    return toep, w_state, w_out, pw_re[q].reshape(1, S5_NSTATE), pw_im[q].reshape(1, S5_NSTATE)


def _s5_scan_body(ar_ref, ai_ref, s_ref, o_ref, c_ref):
    rows = BF16_ROWS
    sub = SUBLANES
    nb = s_ref.shape[0]
    half = ar_ref.shape[1]
    row_id = lax.broadcasted_iota(jnp.int32, (sub, half), 0)

    @pl.when(pl.program_id(1) == 0)
    def _():
        c_ref[...] = jnp.zeros_like(c_ref)

    def cmul(p, q):
        return p[0] * q[0] - p[1] * q[1], p[0] * q[1] + p[1] * q[0]

    def shifted(t, d, fill):
        return jnp.where(row_id >= d, pltpu.roll(t, d, axis=0), fill)

    steps = []
    a_pow = (ar_ref[...], ai_ref[...])
    d = 1
    while d < sub:
        steps.append((d, a_pow))
        a_pow = cmul(a_pow, a_pow)
        d *= 2
    p_row = (jnp.broadcast_to(ar_ref[...], (sub, half)), jnp.broadcast_to(ai_ref[...], (sub, half)))
    for d, _ in steps:
        p_row = cmul(p_row, (shifted(p_row[0], d, 1.0), shifted(p_row[1], d, 0.0)))

    def block(i, carry):
        base = pl.multiple_of(i * rows, rows)
        carry = list(carry)
        outs = [[] for _ in range(nb)]
        for part in range(rows // sub):
            for b in range(nb):
                x = s_ref[b, pl.ds(base + part * sub, sub), :]
                s = (x[:, :half], x[:, half:])
                for d, a_d in steps:
                    inc = cmul(a_d, (shifted(s[0], d, 0.0), shifted(s[1], d, 0.0)))
                    s = (s[0] + inc[0], s[1] + inc[1])
                inc = cmul(p_row, carry[b])
                s = (s[0] + inc[0], s[1] + inc[1])
                outs[b].append(
                    jnp.concatenate([shifted(s[0], 1, carry[b][0]), shifted(s[1], 1, carry[b][1])], axis=1))
                carry[b] = (s[0][sub - 1:sub, :], s[1][sub - 1:sub, :])
        for b in range(nb):
            o_ref[b, pl.ds(base, rows), :] = jnp.concatenate(outs[b], axis=0).astype(o_ref.dtype)
        return tuple(carry)

    init = tuple((c_ref[b, :, :half], c_ref[b, :, half:]) for b in range(nb))
    last = lax.fori_loop(0, s_ref.shape[1] // rows, block, init)
    for b in range(nb):
        c_ref[b] = jnp.concatenate(last[b], axis=1)


def _s5_fold_body(h_ref, w_ref, o_ref, tok_ref):
    u = _dot(h_ref[...], w_ref[...])
    tc = o_ref.shape[0]
    for c in range(S5_SG):
        tok_ref[c] = u[:, c * LANES:(c + 1) * LANES]
        for j in range(S5_Q):
            lo = c * S5_SGW + j * LANES
            o_ref[:, lo:lo + LANES] = tok_ref[c, pl.ds(j, tc, stride=S5_Q), :].astype(o_ref.dtype)


def _s5_glu_body(y_ref, w_ref, o_ref, tok_ref):
    tc = y_ref.shape[0]
    for c in range(S5_SG):
        for j in range(S5_Q):
            lo = c * S5_SGW + j * LANES
            tok_ref[c, pl.ds(j, tc, stride=S5_Q), :] = y_ref[:, lo:lo + LANES]
    y = jax.nn.gelu(jnp.concatenate([tok_ref[c] for c in range(S5_SG)], axis=1))
    o_ref[...] = (y * jax.nn.sigmoid(_dot(y.astype(BF16), w_ref[...]))).astype(o_ref.dtype)


def _s5_mixer(h, w_u, mats, w_glu, bsz, seq, layer=None):
    toep, w_state, w_out, aq_re, aq_im = mats
    m = bsz * seq
    nc = seq // S5_Q
    width = S5_Q * MIX_W
    tm = min(1024, m)
    tc = tm // S5_Q
    uc = pl.pallas_call(
        _s5_fold_body,
        name="s5_input_proj",
        grid=(m // tm,),
        in_specs=[pl.BlockSpec((tm, D_MODEL), lambda i: (i, 0)), _weight_spec(w_u, layer, MIX_W, lambda i: 0)],
        out_specs=pl.BlockSpec((tc, width), lambda i: (i, 0)),
        out_shape=jax.ShapeDtypeStruct((m // S5_Q, width), BF16),
        scratch_shapes=[pltpu.VMEM((MIX_W // LANES, tm, LANES), F32)],
        compiler_params=_params("parallel"),
    )(h, w_u)
    y1s = _matmul_blockdiag(uc, jnp.concatenate([toep, w_state], axis=1), S5_SG)
    rb = min(S5_SCAN_ROWS, nc)
    s_in = pl.pallas_call(
        _s5_scan_body,
        name="s5_chunk_scan",
        grid=(S5_SG, nc // rb),
        in_specs=[
            pl.BlockSpec((1, S5_SB), lambda j, r: (0, j)),
            pl.BlockSpec((1, S5_SB), lambda j, r: (0, j)),
            pl.BlockSpec((bsz, rb, S5_SGW), lambda j, r: (0, r, 2 * j + 1)),
        ],
        out_specs=pl.BlockSpec((bsz, rb, S5_SGW), lambda j, r: (0, r, j)),
        out_shape=jax.ShapeDtypeStruct((bsz, nc, width), BF16),
        scratch_shapes=[pltpu.VMEM((bsz, 1, S5_SGW), F32)],
        compiler_params=_params("parallel", "arbitrary"),
    )(aq_re, aq_im, y1s.reshape(bsz, nc, 2 * width))
    y = _matmul_blockdiag(s_in.reshape(bsz * nc, width), w_out, S5_SG, res=y1s, res_block=lambda s: 2 * s)
    return pl.pallas_call(
        _s5_glu_body,
        name="s5_glu",
        grid=(m // tm,),
        in_specs=[pl.BlockSpec((tc, width), lambda i: (i, 0)), pl.BlockSpec((MIX_W, MIX_W), lambda i: (0, 0))],
        out_specs=pl.BlockSpec((tm, MIX_W), lambda i: (i, 0)),
        out_shape=jax.ShapeDtypeStruct((m, MIX_W), BF16),
        scratch_shapes=[pltpu.VMEM((MIX_W // LANES, tm, LANES), F32)],
        compiler_params=_params("parallel"),
    )(y, w_glu.astype(BF16))


def _lru_body(x_ref, g_ref, cw_ref, cb_ref, wa_ref, ba_ref, wx_ref, bx_ref, lam_ref, o_ref,
              halo_ref, h_ref, a_s, b_s):
    nb, t, _ = x_ref.shape

    @pl.when(pl.program_id(0) == 0)
    def _():
        halo_ref[...] = jnp.zeros_like(halo_ref)
        h_ref[...] = jnp.zeros_like(h_ref)

    soft = jax.nn.softplus(-lam_ref[...])
    for b in range(nb):
        x_in = x_ref[b]
        x = _causal_conv(x_in, halo_ref[b], cw_ref, cb_ref)
        halo_ref[b] = x_in[t - HALO:, :]
        xb = x.astype(BF16)
        r = jax.nn.sigmoid(_dot(xb, wa_ref[...]) + ba_ref[...])
        i = jax.nn.sigmoid(_dot(xb, wx_ref[...]) + bx_ref[...])
        log_a = (-LRU_C * r) * soft
        a = jnp.exp(log_a)
        a_s[b] = a
        b_s[b] = x * i * jnp.sqrt(1.0 - a * a)
    rows = SUBLANES
    row_id = lax.broadcasted_iota(jnp.int32, (rows, MIX_W), 0)

    def block(k, hs):
        base = pl.multiple_of(k * rows, rows)
        av = [a_s[b, pl.ds(base, rows), :] for b in range(nb)]
        bv = [b_s[b, pl.ds(base, rows), :] for b in range(nb)]
        d = 1
        while d < rows:
            for b in range(nb):
                a_prev = jnp.where(row_id >= d, pltpu.roll(av[b], d, axis=0), 1.0)
                b_prev = jnp.where(row_id >= d, pltpu.roll(bv[b], d, axis=0), 0.0)
                bv[b] = av[b] * b_prev + bv[b]
                av[b] = av[b] * a_prev
            d *= 2
        hs = list(hs)
        for b in range(nb):
            h_blk = bv[b] + av[b] * hs[b]
            b_s[b, pl.ds(base, rows), :] = h_blk
            hs[b] = h_blk[rows - 1:rows, :]
        return tuple(hs)

    hs = lax.fori_loop(0, t // rows, block, tuple(h_ref[b] for b in range(nb)))
    for b in range(nb):
        h_ref[b] = hs[b]
        o_ref[b] = (b_s[b] * jax.nn.gelu(g_ref[b])).astype(o_ref.dtype)


def _block_diag(w):
    h_n, n, _ = w.shape
    eye = jnp.eye(h_n, dtype=w.dtype)
    return (w[:, :, None, :] * eye[:, None, :, None]).reshape(h_n * n, h_n * n)


def _lru_mixer(proj, conv_w, conv_b, w_a, b_a, w_x, b_x, lam, bsz, seq):
    t = min(256, seq)
    vec = lambda v: v.reshape(1, MIX_W)
    const = lambda shape: pl.BlockSpec(shape, lambda c: (0, 0))
    proj3 = proj.reshape(bsz, seq, proj.shape[1])
    out = pl.pallas_call(
        _lru_body,
        name="rglru_mixer",
        grid=(seq // t,),
        in_specs=[
            pl.BlockSpec((bsz, t, MIX_W), lambda c: (0, c, PC_LRU_X // MIX_W)),
            pl.BlockSpec((bsz, t, MIX_W), lambda c: (0, c, PC_LRU_G // MIX_W)),
            const(conv_w.shape), const((1, MIX_W)),
            const((MIX_W, MIX_W)), const((1, MIX_W)),
            const((MIX_W, MIX_W)), const((1, MIX_W)), const((1, MIX_W)),
        ],
        out_specs=pl.BlockSpec((bsz, t, MIX_W), lambda c: (0, c, 0)),
        out_shape=jax.ShapeDtypeStruct((bsz, seq, MIX_W), BF16),
        scratch_shapes=[
            pltpu.VMEM((bsz, HALO, MIX_W), F32), pltpu.VMEM((bsz, 1, MIX_W), F32),
            pltpu.VMEM((bsz, t, MIX_W), F32), pltpu.VMEM((bsz, t, MIX_W), F32),
        ],
        compiler_params=_params("arbitrary"),
    )(proj3, proj3, conv_w, vec(conv_b), _block_diag(w_a).astype(BF16), vec(b_a),
      _block_diag(w_x).astype(BF16), vec(b_x), vec(lam))
    return out.reshape(bsz * seq, MIX_W)


def _m2_body(z_ref, xbc_ref, dt_ref, rep_ref, cw_ref, cb_ref, dtb_ref, alog_ref, d_ref, ng_ref, o_ref,
             halo_ref, st_ref):
    nb, q, _ = z_ref.shape
    hg = N_HEADS // M2_GROUPS
    gw = hg * HEAD_W
    seqs = range(nb)

    @pl.when(pl.program_id(1) == 0)
    def _():
        halo_ref[...] = jnp.zeros_like(halo_ref)
        st_ref[...] = jnp.zeros_like(st_ref)

    causal = _tril_mask(q)
    causal_b = causal.astype(BF16)
    neg_a = -jnp.exp(alog_ref[...])
    dt_raw = _exact_right([dt_ref[b] for b in seqs], rep_ref[...])
    xc = []
    for b in seqs:
        xbc = xbc_ref[b]
        conv = _causal_conv(xbc, halo_ref[b], cw_ref, cb_ref)
        halo_ref[b] = xbc[q - HALO:, :]
        xc.append(conv * jax.nn.sigmoid(conv))
    dt = [jax.nn.softplus(dt_raw[b] + dtb_ref[...]) for b in seqs]
    a_cs = [_exact_left(causal_b, dt[b] * neg_a) for b in seqs]
    xs = [xc[b][:, :MIX_W] for b in seqs]
    xd = [xs[b] * dt[b] for b in seqs]
    a_cs_t = [a_cs[b].T for b in seqs]
    a_last = [a_cs[b][q - 1:q, :] for b in seqs]
    xd_st = [(xd[b] * jnp.exp(a_last[b] - a_cs[b])).astype(BF16) for b in seqs]
    e_cs = [jnp.exp(a_cs[b]) for b in seqs]
    bg, cg, cb, y_off = {}, {}, {}, {}
    for b in seqs:
        for g in range(M2_GROUPS):
            lo = MIX_W + g * M2_STATE
            bg[b, g] = xc[b][:, lo:lo + M2_STATE].astype(BF16)
            cg[b, g] = xc[b][:, lo + M2_GROUPS * M2_STATE:lo + (M2_GROUPS + 1) * M2_STATE].astype(BF16)
            cb[b, g] = _dot_nt(cg[b, g], bg[b, g])
            y_off[b, g] = _dot(cg[b, g], st_ref[b, g].astype(BF16)) * e_cs[b][:, g * gw:(g + 1) * gw]
    ys = [[] for _ in seqs]
    for g in range(M2_GROUPS):
        for hh in range(hg):
            for b in seqs:
                lo = (g * hg + hh) * HEAD_W
                col = a_cs[b][:, lo:lo + HEAD_W]
                col = jnp.concatenate([col] * (q // HEAD_W), axis=1)
                seg = col - a_cs_t[b][lo:lo + 1, :]
                dec = jnp.exp(jnp.where(causal, seg, -jnp.inf))
                y_d = _dot((cb[b, g] * dec).astype(BF16), xd[b][:, lo:lo + HEAD_W].astype(BF16))
                ys[b].append(y_d + y_off[b, g][:, hh * HEAD_W:(hh + 1) * HEAD_W])
    for b in seqs:
        for g in range(M2_GROUPS):
            upd = _dot_tn(bg[b, g], xd_st[b][:, g * gw:(g + 1) * gw])
            st_ref[b, g] = st_ref[b, g] * jnp.exp(a_last[b][:, g * gw:(g + 1) * gw]) + upd
    for b in seqs:
        y = jnp.concatenate(ys[b], axis=1) + d_ref[...] * xs[b]
        z = z_ref[b]
        y = y * (z * jax.nn.sigmoid(z))
        ms = jnp.mean(y * y, axis=-1, keepdims=True)
        o_ref[b] = (y * lax.rsqrt(ms + EPS) * ng_ref[...]).astype(o_ref.dtype)


def _per_head(v):
    return jnp.repeat(v, HEAD_W).reshape(1, -1)


def _m2_mixer(proj, conv_w, conv_b, dt_bias, a_log, d, norm_g, bsz, seq):
    q = min(M2_Q, seq)
    nc = seq // q
    nb = M2_BATCH if bsz % M2_BATCH == 0 else 1
    const = lambda shape: pl.BlockSpec(shape, lambda b, c: (0, 0))
    head_of_lane = jnp.arange(MIX_W)[None, :] // HEAD_W
    rep = (jnp.arange(LANES)[:, None] == head_of_lane).astype(BF16)
    proj3 = proj.reshape(bsz, seq, proj.shape[1])
    out = pl.pallas_call(
        _m2_body,
        name="mamba2_mixer",
        grid=(bsz // nb, nc),
        in_specs=[
            pl.BlockSpec((nb, q, MIX_W), lambda b, c: (b, c, PC_Z // MIX_W)),
            pl.BlockSpec((nb, q, M2_CONV_DIM), lambda b, c: (b, c, PC_XBC // M2_CONV_DIM)),
            pl.BlockSpec((nb, q, LANES), lambda b, c: (b, c, PC_DT // LANES)),
            const((LANES, MIX_W)), const(conv_w.shape), const((1, M2_CONV_DIM)),
            const((1, MIX_W)), const((1, MIX_W)), const((1, MIX_W)), const((1, MIX_W)),
        ],
        out_specs=pl.BlockSpec((nb, q, MIX_W), lambda b, c: (b, c, 0)),
        out_shape=jax.ShapeDtypeStruct((bsz, seq, MIX_W), BF16),
        scratch_shapes=[
            pltpu.VMEM((nb, HALO, M2_CONV_DIM), F32),
            pltpu.VMEM((nb, M2_GROUPS, M2_STATE, (N_HEADS // M2_GROUPS) * HEAD_W), F32),
        ],
        compiler_params=_params("parallel", "arbitrary"),
    )(proj3, proj3, proj3, rep, conv_w, conv_b.reshape(1, -1), _per_head(dt_bias), _per_head(a_log),
      _per_head(d), norm_g.reshape(1, -1))
    return out.reshape(bsz * seq, MIX_W)


def _rw_body(p_ref, pl_ref, mu_ref, mul_ref, w0_ref, w2_ref, a0_ref, a2_ref, g2_ref, kk_ref, ka_ref, rk_ref,
             lng_ref, lnb_ref, ones_ref, o_ref, prev_ref, prevl_ref, st_ref):
    nb, q, _ = p_ref.shape
    rows = nb * q

    @pl.when(pl.program_id(1) == 0)
    def _():
        prev_ref[...] = jnp.zeros_like(prev_ref)
        prevl_ref[...] = jnp.zeros_like(prevl_ref)
        st_ref[...] = jnp.zeros_like(st_ref)

    def token_mix(x_ref, last_ref, m_ref):
        x = x_ref[...].reshape(rows, x_ref.shape[2])
        row = lax.broadcasted_iota(jnp.int32, x.shape, 0)
        shifted = pltpu.roll(x, 1, axis=0)
        for b in range(nb):
            shifted = jnp.where(row == b * q, last_ref[b, 0:1, :], shifted)
            last_ref[b, 0:1, :] = x[(b + 1) * q - 1:(b + 1) * q, :]
        return x + (shifted - x) * m_ref[...]

    p = token_mix(p_ref, prev_ref, mu_ref)
    lora = token_mix(pl_ref, prevl_ref, mul_ref)
    r = p[:, 0:MIX_W]
    k = p[:, MIX_W:2 * MIX_W]
    v = p[:, 2 * MIX_W:3 * MIX_W]
    w = -jax.nn.softplus(-(w0_ref[...] + _dot(jnp.tanh(lora).astype(BF16), w2_ref[...]))) - 0.5
    log_w = -jnp.exp(w)
    a_sig = jax.nn.sigmoid(a0_ref[...] + _dot(lora.astype(BF16), a2_ref[...]))
    gate = _dot(jax.nn.sigmoid(lora).astype(BF16), g2_ref[...])
    ones = ones_ref[...]
    kk = k * kk_ref[...]
    k = k * (1.0 + (a_sig - 1.0) * ka_ref[...])
    kk_sq, rk_sum = _exact_right([kk * kk, r * k * rk_ref[...]], ones)
    kk = kk / jnp.maximum(jnp.sqrt(kk_sq), 1e-12)
    a_vec = -kk
    b_vec = kk * a_sig

    rr = lax.broadcasted_iota(jnp.int32, (rows, rows), 0)
    cc = lax.broadcasted_iota(jnp.int32, (rows, rows), 1)
    cum_mask = jnp.where((rr >= cc) & ((rr // q) == (cc // q)), 1.0, 0.0).astype(BF16)
    gam = _exact_left(cum_mask, log_w, terms=2)
    g_last = jnp.concatenate(
        [jnp.broadcast_to(gam[(b + 1) * q - 1:(b + 1) * q, :], (q, MIX_W)) for b in range(nb)], axis=0)
    e_neg = jnp.exp(-gam)
    e_last = jnp.exp(g_last)
    e_rem = e_last * e_neg
    a_t = a_vec * jnp.exp(gam - log_w)
    r_t = r * jnp.exp(gam)
    b_t = b_vec * e_neg
    k_t = k * e_neg
    b_h = b_vec * e_rem
    k_h = k * e_rem
    assert q == HEAD_W
    row2 = lax.broadcasted_iota(jnp.int32, (q, 2 * HEAD_W), 0)
    lane2 = lax.broadcasted_iota(jnp.int32, (q, 2 * HEAD_W), 1)
    low = lane2 < HEAD_W
    keep_n = lane2 < row2
    keep_t = jnp.where(low, lane2, lane2 - HEAD_W) <= row2
    strict = _tril_mask(q, strict=True)
    pairs = [(b, h) for b in range(nb) for h in range(N_HEADS)]
    zeros_x = jnp.zeros((q, HEAD_W), F32)

    def blk(t, key):
        b, h = key
        return t[b * q:(b + 1) * q, h * HEAD_W:(h + 1) * HEAD_W]

    p0, p1, s0, z, w_v = {}, {}, {}, {}, {}
    for key in pairs:
        ar = jnp.concatenate([blk(a_t, key), blk(r_t, key)], axis=0).astype(BF16)
        s0[key] = st_ref[key[0], key[1]]
        b_blk = blk(b_t, key)
        rhs = jnp.concatenate([blk(k_t, key), s0[key], b_blk, b_blk], axis=0).astype(BF16)
        prod = _dot_nt(ar, rhs)
        p0[key] = prod[:, :2 * HEAD_W]
        p1[key] = prod[:, 2 * HEAD_W:]
    for key in pairs:
        a_k = jnp.where(strict, p0[key][:q, :HEAD_W], 0.0).astype(BF16)
        w_v[key] = jnp.concatenate([zeros_x, blk(v, key)], axis=1).astype(BF16)
        z[key] = jnp.where(keep_n, p1[key][:q], jnp.where(low, 0.0, p0[key][:q])) + _dot(a_k, w_v[key])
    steps = max(1, (q - 1).bit_length())
    for _ in range(steps):
        for key in pairs:
            zb = z[key].astype(BF16)
            z[key] = jnp.where(low, 0.0, z[key]) + _dot(zb[:, :HEAD_W], zb)
    ys, w_vu = {}, {}
    for key in pairs:
        w_vu[key] = jnp.concatenate([w_v[key], jnp.where(low, 0.0, z[key]).astype(BF16)], axis=0)
        t_mat = jnp.where(keep_t, jnp.where(low, p0[key][q:], p1[key][q:]), 0.0)
        y_hi = jnp.where(low, 0.0, p0[key][q:]) + _dot(t_mat.astype(BF16), w_vu[key])
        ys[key] = y_hi[:, HEAD_W:]

    def update_states(keys):
        for key in keys:
            khb = jnp.concatenate([blk(k_h, key), blk(b_h, key)], axis=0).astype(BF16)
            st_ref[key[0], key[1]] = s0[key] * blk(e_last, key)[0:1, :] + _dot_tn(w_vu[key], khb)[HEAD_W:, :]

    y = jnp.concatenate(
        [jnp.concatenate([ys[(b, h)] for h in range(N_HEADS)], axis=1) for b in range(nb)], axis=0)
    inv_n = 1.0 / HEAD_W
    mean = _exact_right([y], ones)[0] * inv_n
    update_states(pairs[:len(pairs) // 2])
    yc = y - mean
    var = _exact_right([yc * yc], ones)[0] * inv_n
    update_states(pairs[len(pairs) // 2:])
    y = yc * lax.rsqrt(var + RW_LN_EPS) * lng_ref[...] + lnb_ref[...]
    bonus = rk_sum * v
    o_ref[...] = ((y + bonus) * gate).astype(o_ref.dtype).reshape(o_ref.shape)


def _pad_rows(w, rows, off):
    return jnp.zeros((rows, w.shape[1]), w.dtype).at[off:off + w.shape[0]].set(w)


def _rw_mixer(proj, mu, w0, w2, a0, a2, g2, k_k, k_a, r_k, ln_g, ln_b, bsz, seq):
    q = min(RW_Q, seq)
    nc = seq // q
    nb = RW_BATCH if bsz % RW_BATCH == 0 else 1
    vec = lambda t: t.reshape(1, MIX_W)
    const = lambda shape: pl.BlockSpec(shape, lambda b, c: (0, 0))
    offs = (0, RW_LORA_RANKS[0], RW_LORA_RANKS[0] + RW_LORA_RANKS[1])
    lora = lambda t, off: _pad_rows(t, RW_LORA_W, off).astype(BF16)
    mu_rkv, mu_lora = mu[:RW_RKV_W].reshape(1, RW_RKV_W), _pad_rows(mu[RW_RKV_W:, None], RW_LORA_W, 0).reshape(1, RW_LORA_W)
    ones = _block_diag(jnp.ones((N_HEADS, HEAD_W, HEAD_W), F32)).astype(BF16)
    proj3 = proj.reshape(bsz, seq, proj.shape[1])
    out = pl.pallas_call(
        _rw_body,
        name="rwkv7_mixer",
        grid=(bsz // nb, nc),
        in_specs=[
            pl.BlockSpec((nb, q, RW_RKV_W), lambda b, c: (b, c, PC_RKV // RW_RKV_W)),
            pl.BlockSpec((nb, q, RW_LORA_W), lambda b, c: (b, c, PC_LORA // RW_LORA_W)),
            const((1, RW_RKV_W)), const((1, RW_LORA_W)), const((1, MIX_W)), const((RW_LORA_W, MIX_W)),
            const((1, MIX_W)), const((RW_LORA_W, MIX_W)), const((RW_LORA_W, MIX_W)),
            const((1, MIX_W)), const((1, MIX_W)), const((1, MIX_W)),
            const((1, MIX_W)), const((1, MIX_W)), const((MIX_W, MIX_W)),
        ],
        out_specs=pl.BlockSpec((nb, q, MIX_W), lambda b, c: (b, c, 0)),
        out_shape=jax.ShapeDtypeStruct((bsz, seq, MIX_W), BF16),
        scratch_shapes=[pltpu.VMEM((nb, HALO, RW_RKV_W), F32), pltpu.VMEM((nb, HALO, RW_LORA_W), F32),
                        pltpu.VMEM((nb, N_HEADS, HEAD_W, HEAD_W), F32)],
        compiler_params=_params("parallel", "arbitrary"),
    )(proj3, proj3, mu_rkv, mu_lora, vec(w0), lora(w2, offs[0]), vec(a0), lora(a2, offs[1]), lora(g2, offs[2]),
      vec(k_k), vec(k_a), vec(r_k), vec(ln_g), vec(ln_b), ones)
    return out.reshape(bsz * seq, MIX_W)


def _merge_body(h_ref, *refs):
    nb = (len(refs) - 2) // 2
    y_refs, wg_refs, wp_ref, o_ref = refs[:nb], refs[nb:2 * nb], refs[2 * nb], refs[2 * nb + 1]
    h = h_ref[...]
    acc = None
    for kk in range(nb):
        term = jax.nn.sigmoid(_dot(h, wg_refs[kk][...])) * _dot(y_refs[kk][...], wp_ref[kk])
        acc = term if acc is None else acc + term
    o_ref[...] = acc.astype(o_ref.dtype)


def _merge(h, ys, w_gate, w_branch, layer=None, tm=512, tn=512):
    m = h.shape[0]
    tm = min(tm, m)
    nb = len(ys)
    y_spec = pl.BlockSpec((tm, MIX_W), lambda j, i: (i, 0))
    gate_specs = [_weight_spec(w_gate, layer, tn, lambda j, i, k=k: k * (D_MODEL // tn) + j) for k in range(nb)]
    if w_branch.ndim == 3:
        branch_spec = pl.BlockSpec((nb, MIX_W, tn), lambda j, i: (0, 0, j))
    else:
        branch_spec = pl.BlockSpec((None, nb, MIX_W, tn), lambda j, i: (layer, 0, 0, j))
    return pl.pallas_call(
        _merge_body,
        name="gated_merge",
        grid=(D_MODEL // tn, m // tm),
        in_specs=[pl.BlockSpec((tm, D_MODEL), lambda j, i: (i, 0))] + [y_spec] * nb + gate_specs + [branch_spec],
        out_specs=pl.BlockSpec((tm, tn), lambda j, i: (i, j)),
        out_shape=jax.ShapeDtypeStruct((m, D_MODEL), BF16),
        compiler_params=_params("parallel", "parallel"),
    )(h, *ys, *([w_gate] * nb), w_branch)


def _ffn_body(h_ref, hh_ref, wg_ref, wu_ref, cw_ref, cb_ref, o_ref, wg_s, wu_s, *, tiles_per_seq):
    t = h_ref.shape[0]
    nh = hh_ref.shape[0]

    @pl.when(pl.program_id(1) == 0)
    def _():
        wg_s[...] = wg_ref[...].astype(BF16)
        wu_s[...] = wu_ref[...].astype(BF16)

    h = h_ref[...]
    seq_start = (pl.program_id(1) % tiles_per_seq) == 0
    halo = jnp.where(seq_start, jnp.zeros_like(hh_ref[...]), hh_ref[...])
    g_all = _dot(jnp.concatenate([halo, h], axis=0), wg_s[...])
    k_w = cw_ref.shape[0]
    u = cb_ref[...] + g_all[nh:, :] * cw_ref[k_w - 1:k_w, :]
    for j in range(k_w - 1):
        lag = k_w - 1 - j
        u = u + g_all[nh - lag:nh - lag + t, :] * cw_ref[j:j + 1, :]
    o_ref[...] = (jax.nn.gelu(u) * _dot(h, wu_s[...])).astype(o_ref.dtype)


def _ffn_act(h, w_gate, w_up, conv_w, conv_b, seq, layer=None, tm=1024, tf=512):
    m = h.shape[0]
    tm = min(tm, seq)
    f = w_gate.shape[-1]
    halo_blocks = tm // BF16_ROWS
    return pl.pallas_call(
        functools.partial(_ffn_body, tiles_per_seq=seq // tm),
        name="ffn_gate_up",
        grid=(f // tf, m // tm),
        in_specs=[
            pl.BlockSpec((tm, D_MODEL), lambda j, i: (i, 0)),
            pl.BlockSpec((BF16_ROWS, D_MODEL), lambda j, i: (jnp.maximum(i * halo_blocks - 1, 0), 0)),
            _weight_spec(w_gate, layer, tf, lambda j, i: j),
            _weight_spec(w_up, layer, tf, lambda j, i: j),
            pl.BlockSpec((conv_w.shape[0], tf), lambda j, i: (0, j)),
            pl.BlockSpec((1, tf), lambda j, i: (0, j)),
        ],
        out_specs=pl.BlockSpec((tm, tf), lambda j, i: (i, j)),
        out_shape=jax.ShapeDtypeStruct((m, f), BF16),
        scratch_shapes=[pltpu.VMEM((D_MODEL, tf), BF16), pltpu.VMEM((D_MODEL, tf), BF16)],
        compiler_params=_params("parallel", "arbitrary"),
    )(h, h, w_gate, w_up, conv_w, conv_b.reshape(1, f))


def _w_in_body(w_ref, s5_ref, proj_ref, gate_ref):
    w = w_ref[...]
    tk = w.shape[1]
    row = 0
    piece = {}
    for name, width in (("s5_u", MIX_W), ("lru_x", MIX_W), ("lru_g", MIX_W), ("z", MIX_W), ("xbc", M2_CONV_DIM),
                        ("dt", N_HEADS), ("rkv", RW_RKV_W), ("lora", sum(RW_LORA_RANKS))):
        piece[name] = w[row:row + width, :]
        row += width

    def put(ref, off, t, width):
        if t.shape[0] < width:
            t = jnp.concatenate([t, jnp.zeros((width - t.shape[0], tk), t.dtype)], axis=0)
        ref[:, off:off + width] = t.T.astype(ref.dtype)

    put(s5_ref, 0, piece["s5_u"], MIX_W)
    put(proj_ref, PC_XBC, piece["xbc"], M2_CONV_DIM)
    put(proj_ref, PC_Z, piece["z"], MIX_W)
    put(proj_ref, PC_RKV, piece["rkv"], RW_RKV_W)
    put(proj_ref, PC_LRU_X, piece["lru_x"], MIX_W)
    put(proj_ref, PC_LRU_G, piece["lru_g"], MIX_W)
    put(proj_ref, PC_LORA, piece["lora"], RW_LORA_W)
    put(proj_ref, PC_DT, piece["dt"], PROJ_W - PC_DT)
    put(gate_ref, 0, w[row:, :], w.shape[0] - row)


def _prepare_w_in(w_in, tk=128):
    depth, d, cols = w_in.shape
    gate_w = cols - (4 * MIX_W + M2_CONV_DIM + N_HEADS + RW_RKV_W + sum(RW_LORA_RANKS))
    out = lambda width: pl.BlockSpec((None, tk, width), lambda l, r: (l, r, 0))
    return pl.pallas_call(
        _w_in_body,
        name="w_in_layout",
        grid=(depth, d // tk),
        in_specs=[pl.BlockSpec((None, cols, tk), lambda l, r: (l, 0, r))],
        out_specs=[out(MIX_W), out(PROJ_W), out(gate_w)],
        out_shape=[jax.ShapeDtypeStruct((depth, d, MIX_W), BF16), jax.ShapeDtypeStruct((depth, d, PROJ_W), BF16),
                   jax.ShapeDtypeStruct((depth, d, gate_w), BF16)],
        compiler_params=_params("parallel", "parallel"),
    )(jnp.swapaxes(w_in, 1, 2))


def kernel(x, norm_mix_g, w_in, s5_lambda_re, s5_lambda_im, s5_b_re, s5_b_im, s5_c_re, s5_c_im, s5_d, s5_log_dt, s5_w_glu, lru_conv_w, lru_conv_b, lru_w_a, lru_b_a, lru_w_x, lru_b_x, lru_lambda, m2_conv_w, m2_conv_b, m2_dt_bias, m2_a_log, m2_d, m2_norm_g, rw_mu, rw_w0, rw_w2, rw_a0, rw_a2, rw_g2, rw_k_k, rw_k_a, rw_r_k, rw_ln_g, rw_ln_b, w_branch, w_out, norm_ffn_g, w_ffn_gate, w_ffn_up, ffn_conv_w, ffn_conv_b, w_ffn_down, final_norm_g):
    bsz, seq, d = x.shape
    depth = w_in.shape[0]
    xf = x.reshape(bsz * seq, d)
    wb_all, wo_all, wd_all = w_branch.astype(BF16), w_out.astype(BF16), w_ffn_down.astype(BF16)
    w_s5_all, w_proj_all, w_gate_all = _prepare_w_in(w_in)
    for l in range(depth):
        h = _rms_norm(xf, norm_mix_g[l], BF16)
        proj = _matmul(h, w_proj_all, tm=2048, layer=l)
        mats = _s5_matrices(s5_lambda_re[l], s5_lambda_im[l], s5_b_re[l], s5_b_im[l],
                            s5_c_re[l], s5_c_im[l], s5_d[l], s5_log_dt[l])
        y_a = _s5_mixer(h, w_s5_all, mats, s5_w_glu[l], bsz, seq, layer=l)
        y_b = _lru_mixer(proj, lru_conv_w[l], lru_conv_b[l], lru_w_a[l], lru_b_a[l],
                         lru_w_x[l], lru_b_x[l], lru_lambda[l], bsz, seq)
        y_c = _m2_mixer(proj, m2_conv_w[l], m2_conv_b[l], m2_dt_bias[l], m2_a_log[l],
                        m2_d[l], m2_norm_g[l], bsz, seq)
        y_d = _rw_mixer(proj, rw_mu[l], rw_w0[l], rw_w2[l], rw_a0[l], rw_a2[l], rw_g2[l],
                        rw_k_k[l], rw_k_a[l], rw_r_k[l].reshape(-1), rw_ln_g[l], rw_ln_b[l], bsz, seq)
        merged = _merge(h, (y_a, y_b, y_c, y_d), w_gate_all, wb_all, layer=l)
        xf, h = _matmul_res_norm(merged, wo_all, xf, norm_ffn_g[l], l)
        act = _ffn_act(h, w_ffn_gate, w_ffn_up, ffn_conv_w[l], ffn_conv_b[l], seq, layer=l)
        xf = _matmul(act, wd_all, res=xf, tm=1024, layer=l)
    return _rms_norm(xf, final_norm_g, F32).reshape(bsz, seq, d)
```

```python
import functools

import jax
import jax.numpy as jnp
from jax import lax
from jax.experimental import pallas as pl
from jax.experimental.pallas import tpu as pltpu

F32 = jnp.float32
BF16 = jnp.bfloat16

D_MODEL = 2048
MIX_W = 512
HEAD_W = 64
N_HEADS = MIX_W // HEAD_W
S5_GROUP = 16
S5_GROUPS = MIX_W // S5_GROUP
S5_STATE = 64
S5_Q = 8
S5_NSTATE = S5_GROUPS * S5_STATE
LANES = 128
S5_SG = MIX_W // LANES
S5_SGW = S5_Q * LANES
S5_SB = S5_NSTATE // S5_SG
S5_SCAN_ROWS = 512
LRU_C = 8.0
M2_GROUPS = 2
M2_STATE = 128
M2_CONV_DIM = MIX_W + 2 * M2_GROUPS * M2_STATE
M2_Q = 128
M2_BATCH = 2
RW_Q = 64
RW_BATCH = 4
RW_RKV_W = 3 * MIX_W
RW_LORA_RANKS = (32, 32, 96)
RW_LORA_W = 256
RW_LN_EPS = 64e-5
EPS = 1e-6
HALO = 8
SUBLANES = 8
BF16_ROWS = 16

PC_XBC = 0
PC_Z = 1024
PC_RKV = 1536
PC_LRU_X = 3072
PC_LRU_G = 3584
PC_LORA = 4096
PC_DT = 4352
PROJ_W = 4608

VMEM_LIMIT_BYTES = 50 * 1024 * 1024


def _params(*sem):
    return pltpu.CompilerParams(dimension_semantics=sem, vmem_limit_bytes=VMEM_LIMIT_BYTES)


def _dot(a, b):
    return jnp.dot(a, b, preferred_element_type=F32)


def _dot_nt(a, b):
    return lax.dot_general(a, b, (((1,), (1,)), ((), ())), preferred_element_type=F32)


def _dot_tn(a, b):
    return lax.dot_general(a, b, (((0,), (0,)), ((), ())), preferred_element_type=F32)


def _split(x, terms):
    out = []
    for _ in range(terms - 1):
        hi = x.astype(BF16)
        out.append(hi)
        x = x - hi.astype(F32)
    out.append(x.astype(BF16))
    return out


def _exact_left(m, x, terms=3):
    return sum(_dot(m, p) for p in _split(x, terms))


def _exact_right(xs, m, terms=2):
    n = xs[0].shape[0]
    stacked = jnp.concatenate([p for x in xs for p in _split(x, terms)], axis=0)
    out = _dot(stacked, m)
    return [sum(out[(i * terms + t) * n:(i * terms + t + 1) * n] for t in range(terms))
            for i in range(len(xs))]


def _tril_mask(n, strict=False):
    r = lax.broadcasted_iota(jnp.int32, (n, n), 0)
    c = lax.broadcasted_iota(jnp.int32, (n, n), 1)
    return (r > c) if strict else (r >= c)


def _causal_conv(x, halo, w_ref, b_ref):
    k_w = w_ref.shape[0]
    t = x.shape[0]
    xe = jnp.concatenate([halo, x], axis=0)
    out = b_ref[...] + x * w_ref[k_w - 1:k_w, :]
    for j in range(k_w - 1):
        lag = k_w - 1 - j
        out = out + xe[HALO - lag:HALO - lag + t, :] * w_ref[j:j + 1, :]
    return out


def _mm_body(a_ref, b_ref, o_ref):
    o_ref[...] = _dot(a_ref[...], b_ref[...]).astype(o_ref.dtype)


def _mm_res_body(a_ref, b_ref, r_ref, o_ref):
    o_ref[...] = (_dot(a_ref[...], b_ref[...]) + r_ref[...]).astype(o_ref.dtype)


def _weight_spec(w, layer, tn, col_of):
    k = w.shape[-2]
    if w.ndim == 2:
        return pl.BlockSpec((k, tn), lambda *g: (0, col_of(*g)))
    return pl.BlockSpec((None, k, tn), lambda *g: (layer, 0, col_of(*g)))


def _matmul(a, b, res=None, out_dtype=F32, tm=1024, tn=512, n_outer=False, layer=None):
    m, k = a.shape
    n = b.shape[-1]
    tm, tn = min(tm, m), min(tn, n)
    assert m % tm == 0 and n % tn == 0
    if n_outer:
        grid = (n // tn, m // tm)
        row = lambda j, i: (i, 0)
        col = _weight_spec(b, layer, tn, lambda j, i: j)
        out = lambda j, i: (i, j)
    else:
        grid = (m // tm, n // tn)
        row = lambda i, j: (i, 0)
        col = _weight_spec(b, layer, tn, lambda i, j: j)
        out = lambda i, j: (i, j)
    in_specs = [pl.BlockSpec((tm, k), row), col]
    args = [a, b]
    body = _mm_body
    if res is not None:
        in_specs.append(pl.BlockSpec((tm, tn), out))
        args.append(res)
        body = _mm_res_body
    return pl.pallas_call(
        body,
        name="matmul",
        grid=grid,
        in_specs=in_specs,
        out_specs=pl.BlockSpec((tm, tn), out),
        out_shape=jax.ShapeDtypeStruct((m, n), out_dtype),
        compiler_params=_params("parallel", "parallel"),
    )(*args)


def _matmul_blockdiag(a, w, groups, res=None, res_block=None, tm=1024):
    m = a.shape[0]
    k, n = a.shape[1] // groups, w.shape[1]
    tm = min(tm, m)
    out_spec = pl.BlockSpec((tm, n), lambda i, s: (i, s))
    in_specs = [pl.BlockSpec((tm, k), lambda i, s: (i, s)), pl.BlockSpec((k, n), lambda i, s: (s, 0))]
    args = [a, w]
    if res is not None:
        in_specs.append(pl.BlockSpec((tm, n), lambda i, s: (i, res_block(s))))
        args.append(res)
    return pl.pallas_call(
        _mm_body if res is None else _mm_res_body,
        name="matmul_blockdiag",
        grid=(m // tm, groups),
        in_specs=in_specs,
        out_specs=out_spec,
        out_shape=jax.ShapeDtypeStruct((m, groups * n), F32),
        compiler_params=_params("parallel", "parallel"),
    )(*args)


def _norm_body(x_ref, g_ref, o_ref):
    x = x_ref[...]
    ms = jnp.mean(x * x, axis=-1, keepdims=True)
    o_ref[...] = (x * lax.rsqrt(ms + EPS) * g_ref[...]).astype(o_ref.dtype)


def _mm_res_norm_body(a_ref, w_ref, r_ref, g_ref, x_ref, h_ref):
    x = _dot(a_ref[...], w_ref[...]) + r_ref[...]
    x_ref[...] = x
    ms = jnp.mean(x * x, axis=-1, keepdims=True)
    h_ref[...] = (x * lax.rsqrt(ms + EPS) * g_ref[...]).astype(h_ref.dtype)


def _matmul_res_norm(a, w, res, g, layer, tm=512):
    m, k = a.shape
    n = w.shape[-1]
    tm = min(tm, m)
    rows = lambda width: pl.BlockSpec((tm, width), lambda i: (i, 0))
    return pl.pallas_call(
        _mm_res_norm_body,
        name="matmul_residual_norm",
        grid=(m // tm,),
        in_specs=[rows(k), _weight_spec(w, layer, n, lambda i: 0), rows(n), pl.BlockSpec((1, n), lambda i: (0, 0))],
        out_specs=[rows(n), rows(n)],
        out_shape=[jax.ShapeDtypeStruct((m, n), F32), jax.ShapeDtypeStruct((m, n), BF16)],
        compiler_params=_params("parallel"),
    )(a, w, res, g.reshape(1, n))


def _rms_norm(x, g, out_dtype, tm=512):
    m, d = x.shape
    tm = min(tm, m)
    return pl.pallas_call(
        _norm_body,
        name="rms_norm",
        grid=(m // tm,),
        in_specs=[pl.BlockSpec((tm, d), lambda i: (i, 0)), pl.BlockSpec((1, d), lambda i: (0, 0))],
        out_specs=pl.BlockSpec((tm, d), lambda i: (i, 0)),
        out_shape=jax.ShapeDtypeStruct((m, d), out_dtype),
        compiler_params=_params("parallel"),
    )(x, g.reshape(1, d))


def _s5_matrices(lam_re, lam_im, b_re, b_im, c_re, c_im, d, log_dt):
    g_n, p_n, c_n, q = S5_GROUPS, S5_STATE, S5_GROUP, S5_Q
    hp = lax.Precision.HIGHEST
    dt = jnp.exp(log_dt)[:, None]
    n = jnp.arange(q + 1, dtype=F32)[:, None, None]
    mag = jnp.exp(n * (lam_re * dt))
    pw_re = mag * jnp.cos(n * (lam_im * dt))
    pw_im = mag * jnp.sin(n * (lam_im * dt))
    den = lam_re * lam_re + lam_im * lam_im
    nr, ni = pw_re[1] - 1.0, pw_im[1]
    f_re = (nr * lam_re + ni * lam_im) / den
    f_im = (ni * lam_re - nr * lam_im) / den
    e_re = f_re[..., None] * b_re - f_im[..., None] * b_im
    e_im = f_re[..., None] * b_im + f_im[..., None] * b_re
    cp_re = c_re[None] * pw_re[:, :, None, :] - c_im[None] * pw_im[:, :, None, :]
    cp_im = c_re[None] * pw_im[:, :, None, :] + c_im[None] * pw_re[:, :, None, :]
    kern = (jnp.einsum("tgop,gpi->tgio", cp_re[:q], e_re, precision=hp)
            - jnp.einsum("tgop,gpi->tgio", cp_im[:q], e_im, precision=hp))
    kern = kern.at[0].add(d.reshape(g_n, c_n)[:, :, None] * jnp.eye(c_n, dtype=F32))
    sg_n, gl_n = S5_SG, g_n // S5_SG
    rows = sg_n * S5_SGW
    lag = jnp.arange(q)[None, :] - jnp.arange(q)[:, None]
    kt = jnp.where((lag >= 0)[:, :, None, None, None], kern[jnp.clip(lag, 0, q - 1)], 0.0)
    kt = kt.reshape(q, q, sg_n, gl_n, c_n, c_n).transpose(2, 0, 3, 4, 1, 5)
    kt = kt.reshape(rows, q * c_n)
    rev_re, rev_im = pw_re[q - 1 - jnp.arange(q)], pw_im[q - 1 - jnp.arange(q)]
    et_re, et_im = e_re.transpose(0, 2, 1)[None], e_im.transpose(0, 2, 1)[None]
    ws_re = rev_re[:, :, None, :] * et_re - rev_im[:, :, None, :] * et_im
    ws_im = rev_re[:, :, None, :] * et_im + rev_im[:, :, None, :] * et_re
    ws = jnp.stack([ws_re, ws_im], axis=3)
    ws = ws.reshape(q, sg_n, gl_n, c_n, 2, p_n).transpose(1, 0, 2, 3, 4, 5).reshape(rows, 2 * p_n)
    wy = jnp.stack([cp_re[1:], -cp_im[1:]], axis=0)
    wy = wy.reshape(2, q, sg_n, gl_n, c_n, p_n).transpose(2, 0, 3, 5, 1, 4)
    wy = wy.reshape(rows, q * c_n)

    col = jnp.arange(S5_SGW)
    small = jnp.arange(q * c_n)[:, None]
    tok_rep = ((small // c_n == col[None, :] // LANES) & (small % c_n == col[None, :] % c_n)).astype(BF16)
    small = jnp.arange(2 * p_n)[:, None]
    st_rep = ((small // p_n == col[None, :] // S5_SB) & (small % p_n == col[None, :] % p_n)).astype(BF16)
    row = jnp.arange(rows)
    g_tok_r, g_tok_c = (row % LANES) // c_n, (col % LANES) // c_n
    g_st_r, g_st_c = (row % S5_SB) // p_n, (col % S5_SB) // p_n
    expand = lambda table, rep: jnp.dot(table.astype(BF16), rep, preferred_element_type=BF16)
    zero = jnp.zeros((), BF16)
    toep = jnp.where(g_tok_r[:, None] == g_tok_c[None, :], expand(kt, tok_rep), zero)
    w_state = jnp.where(g_tok_r[:, None] == g_st_c[None, :], expand(ws, st_rep), zero)
    w_out = jnp.where(g_st_r[:, None] == g_tok_c[None, :], expand(wy, tok_rep), zero)
    return toep, w_state, w_out, pw_re[q].reshape(1, S5_NSTATE), pw_im[q].reshape(1, S5_NSTATE)


def _s5_scan_body(ar_ref, ai_ref, s_ref, o_ref, c_ref):
    rows = BF16_ROWS
    sub = SUBLANES
    nb = s_ref.shape[0]
    half = ar_ref.shape[1]
    row_id = lax.broadcasted_iota(jnp.int32, (sub, half), 0)

    @pl.when(pl.program_id(1) == 0)
    def _():
        c_ref[...] = jnp.zeros_like(c_ref)

    def cmul(p, q):
        return p[0] * q[0] - p[1] * q[1], p[0] * q[1] + p[1] * q[0]

    def shifted(t, d, fill):
        return jnp.where(row_id >= d, pltpu.roll(t, d, axis=0), fill)

    steps = []
    a_pow = (ar_ref[...], ai_ref[...])
    d = 1
    while d < sub:
        steps.append((d, a_pow))
        a_pow = cmul(a_pow, a_pow)
        d *= 2
    p_row = (jnp.broadcast_to(ar_ref[...], (sub, half)), jnp.broadcast_to(ai_ref[...], (sub, half)))
    for d, _ in steps:
        p_row = cmul(p_row, (shifted(p_row[0], d, 1.0), shifted(p_row[1], d, 0.0)))

    def block(i, carry):
        base = pl.multiple_of(i * rows, rows)
        carry = list(carry)
        outs = [[] for _ in range(nb)]
        for part in range(rows // sub):
            for b in range(nb):
                x = s_ref[b, pl.ds(base + part * sub, sub), :]
                s = (x[:, :half], x[:, half:])
                for d, a_d in steps:
                    inc = cmul(a_d, (shifted(s[0], d, 0.0), shifted(s[1], d, 0.0)))
                    s = (s[0] + inc[0], s[1] + inc[1])
                inc = cmul(p_row, carry[b])
                s = (s[0] + inc[0], s[1] + inc[1])
                outs[b].append(
                    jnp.concatenate([shifted(s[0], 1, carry[b][0]), shifted(s[1], 1, carry[b][1])], axis=1))
                carry[b] = (s[0][sub - 1:sub, :], s[1][sub - 1:sub, :])
        for b in range(nb):
            o_ref[b, pl.ds(base, rows), :] = jnp.concatenate(outs[b], axis=0).astype(o_ref.dtype)
        return tuple(carry)

    init = tuple((c_ref[b, :, :half], c_ref[b, :, half:]) for b in range(nb))
    last = lax.fori_loop(0, s_ref.shape[1] // rows, block, init)
    for b in range(nb):
        c_ref[b] = jnp.concatenate(last[b], axis=1)


def _s5_fold_body(h_ref, w_ref, o_ref, tok_ref):
    u = _dot(h_ref[...], w_ref[...])
    tc = o_ref.shape[0]
    for c in range(S5_SG):
        tok_ref[c] = u[:, c * LANES:(c + 1) * LANES]
        for j in range(S5_Q):
            lo = c * S5_SGW + j * LANES
            o_ref[:, lo:lo + LANES] = tok_ref[c, pl.ds(j, tc, stride=S5_Q), :].astype(o_ref.dtype)


def _s5_glu_body(y_ref, w_ref, o_ref, tok_ref):
    tc = y_ref.shape[0]
    for c in range(S5_SG):
        for j in range(S5_Q):
            lo = c * S5_SGW + j * LANES
            tok_ref[c, pl.ds(j, tc, stride=S5_Q), :] = y_ref[:, lo:lo + LANES]
    y = jax.nn.gelu(jnp.concatenate([tok_ref[c] for c in range(S5_SG)], axis=1))
    o_ref[...] = (y * jax.nn.sigmoid(_dot(y.astype(BF16), w_ref[...]))).astype(o_ref.dtype)


def _s5_mixer(h, w_u, mats, w_glu, bsz, seq, layer=None):
    toep, w_state, w_out, aq_re, aq_im = mats
    m = bsz * seq
    nc = seq // S5_Q
    width = S5_Q * MIX_W
    tm = min(1024, m)
    tc = tm // S5_Q
    uc = pl.pallas_call(
        _s5_fold_body,
        name="s5_input_proj",
        grid=(m // tm,),
        in_specs=[pl.BlockSpec((tm, D_MODEL), lambda i: (i, 0)), _weight_spec(w_u, layer, MIX_W, lambda i: 0)],
        out_specs=pl.BlockSpec((tc, width), lambda i: (i, 0)),
        out_shape=jax.ShapeDtypeStruct((m // S5_Q, width), BF16),
        scratch_shapes=[pltpu.VMEM((MIX_W // LANES, tm, LANES), F32)],
        compiler_params=_params("parallel"),
    )(h, w_u)
    y1s = _matmul_blockdiag(uc, jnp.concatenate([toep, w_state], axis=1), S5_SG)
    rb = min(S5_SCAN_ROWS, nc)
    s_in = pl.pallas_call(
        _s5_scan_body,
        name="s5_chunk_scan",
        grid=(S5_SG, nc // rb),
        in_specs=[
            pl.BlockSpec((1, S5_SB), lambda j, r: (0, j)),
            pl.BlockSpec((1, S5_SB), lambda j, r: (0, j)),
            pl.BlockSpec((bsz, rb, S5_SGW), lambda j, r: (0, r, 2 * j + 1)),
        ],
        out_specs=pl.BlockSpec((bsz, rb, S5_SGW), lambda j, r: (0, r, j)),
        out_shape=jax.ShapeDtypeStruct((bsz, nc, width), BF16),
        scratch_shapes=[pltpu.VMEM((bsz, 1, S5_SGW), F32)],
        compiler_params=_params("parallel", "arbitrary"),
    )(aq_re, aq_im, y1s.reshape(bsz, nc, 2 * width))
    y = _matmul_blockdiag(s_in.reshape(bsz * nc, width), w_out, S5_SG, res=y1s, res_block=lambda s: 2 * s)
    return pl.pallas_call(
        _s5_glu_body,
        name="s5_glu",
        grid=(m // tm,),
        in_specs=[pl.BlockSpec((tc, width), lambda i: (i, 0)), pl.BlockSpec((MIX_W, MIX_W), lambda i: (0, 0))],
        out_specs=pl.BlockSpec((tm, MIX_W), lambda i: (i, 0)),
        out_shape=jax.ShapeDtypeStruct((m, MIX_W), BF16),
        scratch_shapes=[pltpu.VMEM((MIX_W // LANES, tm, LANES), F32)],
        compiler_params=_params("parallel"),
    )(y, w_glu.astype(BF16))


def _lru_body(x_ref, g_ref, cw_ref, cb_ref, wa_ref, ba_ref, wx_ref, bx_ref, lam_ref, o_ref,
              halo_ref, h_ref, a_s, b_s):
    nb, t, _ = x_ref.shape

    @pl.when(pl.program_id(0) == 0)
    def _():
        halo_ref[...] = jnp.zeros_like(halo_ref)
        h_ref[...] = jnp.zeros_like(h_ref)

    soft = jax.nn.softplus(-lam_ref[...])
    for b in range(nb):
        x_in = x_ref[b]
        x = _causal_conv(x_in, halo_ref[b], cw_ref, cb_ref)
        halo_ref[b] = x_in[t - HALO:, :]
        xb = x.astype(BF16)
        r = jax.nn.sigmoid(_dot(xb, wa_ref[...]) + ba_ref[...])
        i = jax.nn.sigmoid(_dot(xb, wx_ref[...]) + bx_ref[...])
        log_a = (-LRU_C * r) * soft
        a = jnp.exp(log_a)
        a_s[b] = a
        b_s[b] = x * i * jnp.sqrt(1.0 - a * a)
    rows = SUBLANES
    row_id = lax.broadcasted_iota(jnp.int32, (rows, MIX_W), 0)

    def block(k, hs):
        base = pl.multiple_of(k * rows, rows)
        av = [a_s[b, pl.ds(base, rows), :] for b in range(nb)]
        bv = [b_s[b, pl.ds(base, rows), :] for b in range(nb)]
        d = 1
        while d < rows:
            for b in range(nb):
                a_prev = jnp.where(row_id >= d, pltpu.roll(av[b], d, axis=0), 1.0)
                b_prev = jnp.where(row_id >= d, pltpu.roll(bv[b], d, axis=0), 0.0)
                bv[b] = av[b] * b_prev + bv[b]
                av[b] = av[b] * a_prev
            d *= 2
        hs = list(hs)
        for b in range(nb):
            h_blk = bv[b] + av[b] * hs[b]
            b_s[b, pl.ds(base, rows), :] = h_blk
            hs[b] = h_blk[rows - 1:rows, :]
        return tuple(hs)

    hs = lax.fori_loop(0, t // rows, block, tuple(h_ref[b] for b in range(nb)))
    for b in range(nb):
        h_ref[b] = hs[b]
        o_ref[b] = (b_s[b] * jax.nn.gelu(g_ref[b])).astype(o_ref.dtype)


def _block_diag(w):
    h_n, n, _ = w.shape
    eye = jnp.eye(h_n, dtype=w.dtype)
    return (w[:, :, None, :] * eye[:, None, :, None]).reshape(h_n * n, h_n * n)


def _lru_mixer(proj, conv_w, conv_b, w_a, b_a, w_x, b_x, lam, bsz, seq):
    t = min(256, seq)
    vec = lambda v: v.reshape(1, MIX_W)
    const = lambda shape: pl.BlockSpec(shape, lambda c: (0, 0))
    proj3 = proj.reshape(bsz, seq, proj.shape[1])
    out = pl.pallas_call(
        _lru_body,
        name="rglru_mixer",
        grid=(seq // t,),
        in_specs=[
            pl.BlockSpec((bsz, t, MIX_W), lambda c: (0, c, PC_LRU_X // MIX_W)),
            pl.BlockSpec((bsz, t, MIX_W), lambda c: (0, c, PC_LRU_G // MIX_W)),
            const(conv_w.shape), const((1, MIX_W)),
            const((MIX_W, MIX_W)), const((1, MIX_W)),
            const((MIX_W, MIX_W)), const((1, MIX_W)), const((1, MIX_W)),
        ],
        out_specs=pl.BlockSpec((bsz, t, MIX_W), lambda c: (0, c, 0)),
        out_shape=jax.ShapeDtypeStruct((bsz, seq, MIX_W), BF16),
        scratch_shapes=[
            pltpu.VMEM((bsz, HALO, MIX_W), F32), pltpu.VMEM((bsz, 1, MIX_W), F32),
            pltpu.VMEM((bsz, t, MIX_W), F32), pltpu.VMEM((bsz, t, MIX_W), F32),
        ],
        compiler_params=_params("arbitrary"),
    )(proj3, proj3, conv_w, vec(conv_b), _block_diag(w_a).astype(BF16), vec(b_a),
      _block_diag(w_x).astype(BF16), vec(b_x), vec(lam))
    return out.reshape(bsz * seq, MIX_W)


def _m2_body(z_ref, xbc_ref, dt_ref, rep_ref, cw_ref, cb_ref, dtb_ref, alog_ref, d_ref, ng_ref, o_ref,
             halo_ref, st_ref):
    nb, q, _ = z_ref.shape
    hg = N_HEADS // M2_GROUPS
    gw = hg * HEAD_W
    seqs = range(nb)

    @pl.when(pl.program_id(1) == 0)
    def _():
        halo_ref[...] = jnp.zeros_like(halo_ref)
        st_ref[...] = jnp.zeros_like(st_ref)

    causal = _tril_mask(q)
    causal_b = causal.astype(BF16)
    neg_a = -jnp.exp(alog_ref[...])
    dt_raw = _exact_right([dt_ref[b] for b in seqs], rep_ref[...])
    xc = []
    for b in seqs:
        xbc = xbc_ref[b]
        conv = _causal_conv(xbc, halo_ref[b], cw_ref, cb_ref)
        halo_ref[b] = xbc[q - HALO:, :]
        xc.append(conv * jax.nn.sigmoid(conv))
    dt = [jax.nn.softplus(dt_raw[b] + dtb_ref[...]) for b in seqs]
    a_cs = [_exact_left(causal_b, dt[b] * neg_a) for b in seqs]
    xs = [xc[b][:, :MIX_W] for b in seqs]
    xd = [xs[b] * dt[b] for b in seqs]
    a_cs_t = [a_cs[b].T for b in seqs]
    a_last = [a_cs[b][q - 1:q, :] for b in seqs]
    xd_st = [(xd[b] * jnp.exp(a_last[b] - a_cs[b])).astype(BF16) for b in seqs]
    e_cs = [jnp.exp(a_cs[b]) for b in seqs]
    bg, cg, cb, y_off = {}, {}, {}, {}
    for b in seqs:
        for g in range(M2_GROUPS):
            lo = MIX_W + g * M2_STATE
            bg[b, g] = xc[b][:, lo:lo + M2_STATE].astype(BF16)
            cg[b, g] = xc[b][:, lo + M2_GROUPS * M2_STATE:lo + (M2_GROUPS + 1) * M2_STATE].astype(BF16)
            cb[b, g] = _dot_nt(cg[b, g], bg[b, g])
            y_off[b, g] = _dot(cg[b, g], st_ref[b, g].astype(BF16)) * e_cs[b][:, g * gw:(g + 1) * gw]
    ys = [[] for _ in seqs]
    for g in range(M2_GROUPS):
        for hh in range(hg):
            for b in seqs:
                lo = (g * hg + hh) * HEAD_W
                col = a_cs[b][:, lo:lo + HEAD_W]
                col = jnp.concatenate([col] * (q // HEAD_W), axis=1)
                seg = col - a_cs_t[b][lo:lo + 1, :]
                dec = jnp.exp(jnp.where(causal, seg, -jnp.inf))
                y_d = _dot((cb[b, g] * dec).astype(BF16), xd[b][:, lo:lo + HEAD_W].astype(BF16))
                ys[b].append(y_d + y_off[b, g][:, hh * HEAD_W:(hh + 1) * HEAD_W])
    for b in seqs:
        for g in range(M2_GROUPS):
            upd = _dot_tn(bg[b, g], xd_st[b][:, g * gw:(g + 1) * gw])
            st_ref[b, g] = st_ref[b, g] * jnp.exp(a_last[b][:, g * gw:(g + 1) * gw]) + upd
    for b in seqs:
        y = jnp.concatenate(ys[b], axis=1) + d_ref[...] * xs[b]
        z = z_ref[b]
        y = y * (z * jax.nn.sigmoid(z))
        ms = jnp.mean(y * y, axis=-1, keepdims=True)
        o_ref[b] = (y * lax.rsqrt(ms + EPS) * ng_ref[...]).astype(o_ref.dtype)


def _per_head(v):
    return jnp.repeat(v, HEAD_W).reshape(1, -1)


def _m2_mixer(proj, conv_w, conv_b, dt_bias, a_log, d, norm_g, bsz, seq):
    q = min(M2_Q, seq)
    nc = seq // q
    nb = M2_BATCH if bsz % M2_BATCH == 0 else 1
    const = lambda shape: pl.BlockSpec(shape, lambda b, c: (0, 0))
    head_of_lane = jnp.arange(MIX_W)[None, :] // HEAD_W
    rep = (jnp.arange(LANES)[:, None] == head_of_lane).astype(BF16)
    proj3 = proj.reshape(bsz, seq, proj.shape[1])
    out = pl.pallas_call(
        _m2_body,
        name="mamba2_mixer",
        grid=(bsz // nb, nc),
        in_specs=[
            pl.BlockSpec((nb, q, MIX_W), lambda b, c: (b, c, PC_Z // MIX_W)),
            pl.BlockSpec((nb, q, M2_CONV_DIM), lambda b, c: (b, c, PC_XBC // M2_CONV_DIM)),
            pl.BlockSpec((nb, q, LANES), lambda b, c: (b, c, PC_DT // LANES)),
            const((LANES, MIX_W)), const(conv_w.shape), const((1, M2_CONV_DIM)),
            const((1, MIX_W)), const((1, MIX_W)), const((1, MIX_W)), const((1, MIX_W)),
        ],
        out_specs=pl.BlockSpec((nb, q, MIX_W), lambda b, c: (b, c, 0)),
        out_shape=jax.ShapeDtypeStruct((bsz, seq, MIX_W), BF16),
        scratch_shapes=[
            pltpu.VMEM((nb, HALO, M2_CONV_DIM), F32),
            pltpu.VMEM((nb, M2_GROUPS, M2_STATE, (N_HEADS // M2_GROUPS) * HEAD_W), F32),
        ],
        compiler_params=_params("parallel", "arbitrary"),
    )(proj3, proj3, proj3, rep, conv_w, conv_b.reshape(1, -1), _per_head(dt_bias), _per_head(a_log),
      _per_head(d), norm_g.reshape(1, -1))
    return out.reshape(bsz * seq, MIX_W)


def _rw_body(p_ref, pl_ref, mu_ref, mul_ref, w0_ref, w2_ref, a0_ref, a2_ref, g2_ref, kk_ref, ka_ref, rk_ref,
             lng_ref, lnb_ref, ones_ref, o_ref, prev_ref, prevl_ref, st_ref):
    nb, q, _ = p_ref.shape
    rows = nb * q

    @pl.when(pl.program_id(1) == 0)
    def _():
        prev_ref[...] = jnp.zeros_like(prev_ref)
        prevl_ref[...] = jnp.zeros_like(prevl_ref)
        st_ref[...] = jnp.zeros_like(st_ref)

    def token_mix(x_ref, last_ref, m_ref):
        x = x_ref[...].reshape(rows, x_ref.shape[2])
        row = lax.broadcasted_iota(jnp.int32, x.shape, 0)
        shifted = pltpu.roll(x, 1, axis=0)
        for b in range(nb):
            shifted = jnp.where(row == b * q, last_ref[b, 0:1, :], shifted)
            last_ref[b, 0:1, :] = x[(b + 1) * q - 1:(b + 1) * q, :]
        return x + (shifted - x) * m_ref[...]

    p = token_mix(p_ref, prev_ref, mu_ref)
    lora = token_mix(pl_ref, prevl_ref, mul_ref)
    r = p[:, 0:MIX_W]
    k = p[:, MIX_W:2 * MIX_W]
    v = p[:, 2 * MIX_W:3 * MIX_W]
    w = -jax.nn.softplus(-(w0_ref[...] + _dot(jnp.tanh(lora).astype(BF16), w2_ref[...]))) - 0.5
    log_w = -jnp.exp(w)
    a_sig = jax.nn.sigmoid(a0_ref[...] + _dot(lora.astype(BF16), a2_ref[...]))
    gate = _dot(jax.nn.sigmoid(lora).astype(BF16), g2_ref[...])
    ones = ones_ref[...]
    kk = k * kk_ref[...]
    k = k * (1.0 + (a_sig - 1.0) * ka_ref[...])
    kk_sq, rk_sum = _exact_right([kk * kk, r * k * rk_ref[...]], ones)
    kk = kk / jnp.maximum(jnp.sqrt(kk_sq), 1e-12)
    a_vec = -kk
    b_vec = kk * a_sig

    rr = lax.broadcasted_iota(jnp.int32, (rows, rows), 0)
    cc = lax.broadcasted_iota(jnp.int32, (rows, rows), 1)
    cum_mask = jnp.where((rr >= cc) & ((rr // q) == (cc // q)), 1.0, 0.0).astype(BF16)
    gam = _exact_left(cum_mask, log_w, terms=2)
    g_last = jnp.concatenate(
        [jnp.broadcast_to(gam[(b + 1) * q - 1:(b + 1) * q, :], (q, MIX_W)) for b in range(nb)], axis=0)
    e_neg = jnp.exp(-gam)
    e_last = jnp.exp(g_last)
    e_rem = e_last * e_neg
    a_t = a_vec * jnp.exp(gam - log_w)
    r_t = r * jnp.exp(gam)
    b_t = b_vec * e_neg
    k_t = k * e_neg
    b_h = b_vec * e_rem
    k_h = k * e_rem
    assert q == HEAD_W
    row2 = lax.broadcasted_iota(jnp.int32, (q, 2 * HEAD_W), 0)
    lane2 = lax.broadcasted_iota(jnp.int32, (q, 2 * HEAD_W), 1)
    low = lane2 < HEAD_W
    keep_n = lane2 < row2
    keep_t = jnp.where(low, lane2, lane2 - HEAD_W) <= row2
    strict = _tril_mask(q, strict=True)
    pairs = [(b, h) for b in range(nb) for h in range(N_HEADS)]
    zeros_x = jnp.zeros((q, HEAD_W), F32)

    def blk(t, key):
        b, h = key
        return t[b * q:(b + 1) * q, h * HEAD_W:(h + 1) * HEAD_W]

    p0, p1, s0, z, w_v = {}, {}, {}, {}, {}
    for key in pairs:
        ar = jnp.concatenate([blk(a_t, key), blk(r_t, key)], axis=0).astype(BF16)
        s0[key] = st_ref[key[0], key[1]]
        b_blk = blk(b_t, key)
        rhs = jnp.concatenate([blk(k_t, key), s0[key], b_blk, b_blk], axis=0).astype(BF16)
        prod = _dot_nt(ar, rhs)
        p0[key] = prod[:, :2 * HEAD_W]
        p1[key] = prod[:, 2 * HEAD_W:]
    for key in pairs:
        a_k = jnp.where(strict, p0[key][:q, :HEAD_W], 0.0).astype(BF16)
        w_v[key] = jnp.concatenate([zeros_x, blk(v, key)], axis=1).astype(BF16)
        z[key] = jnp.where(keep_n, p1[key][:q], jnp.where(low, 0.0, p0[key][:q])) + _dot(a_k, w_v[key])
    steps = max(1, (q - 1).bit_length())
    for _ in range(steps):
        for key in pairs:
            zb = z[key].astype(BF16)
            z[key] = jnp.where(low, 0.0, z[key]) + _dot(zb[:, :HEAD_W], zb)
    ys, w_vu = {}, {}
    for key in pairs:
        w_vu[key] = jnp.concatenate([w_v[key], jnp.where(low, 0.0, z[key]).astype(BF16)], axis=0)
        t_mat = jnp.where(keep_t, jnp.where(low, p0[key][q:], p1[key][q:]), 0.0)
        y_hi = jnp.where(low, 0.0, p0[key][q:]) + _dot(t_mat.astype(BF16), w_vu[key])
        ys[key] = y_hi[:, HEAD_W:]

    def update_states(keys):
        for key in keys:
            khb = jnp.concatenate([blk(k_h, key), blk(b_h, key)], axis=0).astype(BF16)
            st_ref[key[0], key[1]] = s0[key] * blk(e_last, key)[0:1, :] + _dot_tn(w_vu[key], khb)[HEAD_W:, :]

    y = jnp.concatenate(
        [jnp.concatenate([ys[(b, h)] for h in range(N_HEADS)], axis=1) for b in range(nb)], axis=0)
    inv_n = 1.0 / HEAD_W
    mean = _exact_right([y], ones)[0] * inv_n
    update_states(pairs[:len(pairs) // 2])
    yc = y - mean
    var = _exact_right([yc * yc], ones)[0] * inv_n
    update_states(pairs[len(pairs) // 2:])
    y = yc * lax.rsqrt(var + RW_LN_EPS) * lng_ref[...] + lnb_ref[...]
    bonus = rk_sum * v
    o_ref[...] = ((y + bonus) * gate).astype(o_ref.dtype).reshape(o_ref.shape)


def _pad_rows(w, rows, off):
    return jnp.zeros((rows, w.shape[1]), w.dtype).at[off:off + w.shape[0]].set(w)


def _rw_mixer(proj, mu, w0, w2, a0, a2, g2, k_k, k_a, r_k, ln_g, ln_b, bsz, seq):
    q = min(RW_Q, seq)
    nc = seq // q
    nb = RW_BATCH if bsz % RW_BATCH == 0 else 1
    vec = lambda t: t.reshape(1, MIX_W)
    const = lambda shape: pl.BlockSpec(shape, lambda b, c: (0, 0))
    offs = (0, RW_LORA_RANKS[0], RW_LORA_RANKS[0] + RW_LORA_RANKS[1])
    lora = lambda t, off: _pad_rows(t, RW_LORA_W, off).astype(BF16)
    mu_rkv, mu_lora = mu[:RW_RKV_W].reshape(1, RW_RKV_W), _pad_rows(mu[RW_RKV_W:, None], RW_LORA_W, 0).reshape(1, RW_LORA_W)
    ones = _block_diag(jnp.ones((N_HEADS, HEAD_W, HEAD_W), F32)).astype(BF16)
    proj3 = proj.reshape(bsz, seq, proj.shape[1])
    out = pl.pallas_call(
        _rw_body,
        name="rwkv7_mixer",
        grid=(bsz // nb, nc),
        in_specs=[
            pl.BlockSpec((nb, q, RW_RKV_W), lambda b, c: (b, c, PC_RKV // RW_RKV_W)),
            pl.BlockSpec((nb, q, RW_LORA_W), lambda b, c: (b, c, PC_LORA // RW_LORA_W)),
            const((1, RW_RKV_W)), const((1, RW_LORA_W)), const((1, MIX_W)), const((RW_LORA_W, MIX_W)),
            const((1, MIX_W)), const((RW_LORA_W, MIX_W)), const((RW_LORA_W, MIX_W)),
            const((1, MIX_W)), const((1, MIX_W)), const((1, MIX_W)),
            const((1, MIX_W)), const((1, MIX_W)), const((MIX_W, MIX_W)),
        ],
        out_specs=pl.BlockSpec((nb, q, MIX_W), lambda b, c: (b, c, 0)),
        out_shape=jax.ShapeDtypeStruct((bsz, seq, MIX_W), BF16),
        scratch_shapes=[pltpu.VMEM((nb, HALO, RW_RKV_W), F32), pltpu.VMEM((nb, HALO, RW_LORA_W), F32),
                        pltpu.VMEM((nb, N_HEADS, HEAD_W, HEAD_W), F32)],
        compiler_params=_params("parallel", "arbitrary"),
    )(proj3, proj3, mu_rkv, mu_lora, vec(w0), lora(w2, offs[0]), vec(a0), lora(a2, offs[1]), lora(g2, offs[2]),
      vec(k_k), vec(k_a), vec(r_k), vec(ln_g), vec(ln_b), ones)
    return out.reshape(bsz * seq, MIX_W)


def _merge_body(h_ref, *refs):
    nb = (len(refs) - 2) // 2
    y_refs, wg_refs, wp_ref, o_ref = refs[:nb], refs[nb:2 * nb], refs[2 * nb], refs[2 * nb + 1]
    h = h_ref[...]
    acc = None
    for kk in range(nb):
        term = jax.nn.sigmoid(_dot(h, wg_refs[kk][...])) * _dot(y_refs[kk][...], wp_ref[kk])
        acc = term if acc is None else acc + term
    o_ref[...] = acc.astype(o_ref.dtype)


def _merge(h, ys, w_gate, w_branch, layer=None, tm=1024, tn=256):
    m = h.shape[0]
    tm = min(tm, m)
    nb = len(ys)
    y_spec = pl.BlockSpec((tm, MIX_W), lambda j, i: (i, 0))
    gate_specs = [_weight_spec(w_gate, layer, tn, lambda j, i, k=k: k * (D_MODEL // tn) + j) for k in range(nb)]
    if w_branch.ndim == 3:
        branch_spec = pl.BlockSpec((nb, MIX_W, tn), lambda j, i: (0, 0, j))
    else:
        branch_spec = pl.BlockSpec((None, nb, MIX_W, tn), lambda j, i: (layer, 0, 0, j))
    return pl.pallas_call(
        _merge_body,
        name="gated_merge",
        grid=(D_MODEL // tn, m // tm),
        in_specs=[pl.BlockSpec((tm, D_MODEL), lambda j, i: (i, 0))] + [y_spec] * nb + gate_specs + [branch_spec],
        out_specs=pl.BlockSpec((tm, tn), lambda j, i: (i, j)),
        out_shape=jax.ShapeDtypeStruct((m, D_MODEL), BF16),
        compiler_params=_params("parallel", "parallel"),
    )(h, *ys, *([w_gate] * nb), w_branch)


def _ffn_body(h_ref, hh_ref, wg_ref, wu_ref, cw_ref, cb_ref, o_ref, wg_s, wu_s, *, tiles_per_seq):
    t = h_ref.shape[0]
    nh = hh_ref.shape[0]

    @pl.when(pl.program_id(1) == 0)
    def _():
        wg_s[...] = wg_ref[...].astype(BF16)
        wu_s[...] = wu_ref[...].astype(BF16)

    h = h_ref[...]
    seq_start = (pl.program_id(1) % tiles_per_seq) == 0
    halo = jnp.where(seq_start, jnp.zeros_like(hh_ref[...]), hh_ref[...])
    g_all = _dot(jnp.concatenate([halo, h], axis=0), wg_s[...])
    k_w = cw_ref.shape[0]
    u = cb_ref[...] + g_all[nh:, :] * cw_ref[k_w - 1:k_w, :]
    for j in range(k_w - 1):
        lag = k_w - 1 - j
        u = u + g_all[nh - lag:nh - lag + t, :] * cw_ref[j:j + 1, :]
    o_ref[...] = (jax.nn.gelu(u) * _dot(h, wu_s[...])).astype(o_ref.dtype)


def _ffn_act(h, w_gate, w_up, conv_w, conv_b, seq, layer=None, tm=1024, tf=512):
    m = h.shape[0]
    tm = min(tm, seq)
    f = w_gate.shape[-1]
    halo_blocks = tm // BF16_ROWS
    return pl.pallas_call(
        functools.partial(_ffn_body, tiles_per_seq=seq // tm),
        name="ffn_gate_up",
        grid=(f // tf, m // tm),
        in_specs=[
            pl.BlockSpec((tm, D_MODEL), lambda j, i: (i, 0)),
            pl.BlockSpec((BF16_ROWS, D_MODEL), lambda j, i: (jnp.maximum(i * halo_blocks - 1, 0), 0)),
            _weight_spec(w_gate, layer, tf, lambda j, i: j),
            _weight_spec(w_up, layer, tf, lambda j, i: j),
            pl.BlockSpec((conv_w.shape[0], tf), lambda j, i: (0, j)),
            pl.BlockSpec((1, tf), lambda j, i: (0, j)),
        ],
        out_specs=pl.BlockSpec((tm, tf), lambda j, i: (i, j)),
        out_shape=jax.ShapeDtypeStruct((m, f), BF16),
        scratch_shapes=[pltpu.VMEM((D_MODEL, tf), BF16), pltpu.VMEM((D_MODEL, tf), BF16)],
        compiler_params=_params("parallel", "arbitrary"),
    )(h, h, w_gate, w_up, conv_w, conv_b.reshape(1, f))


def _w_in_body(w_ref, s5_ref, proj_ref, gate_ref):
    w = w_ref[...]
    tk = w.shape[1]
    row = 0
    piece = {}
    for name, width in (("s5_u", MIX_W), ("lru_x", MIX_W), ("lru_g", MIX_W), ("z", MIX_W), ("xbc", M2_CONV_DIM),
                        ("dt", N_HEADS), ("rkv", RW_RKV_W), ("lora", sum(RW_LORA_RANKS))):
        piece[name] = w[row:row + width, :]
        row += width

    def put(ref, off, t, width):
        if t.shape[0] < width:
            t = jnp.concatenate([t, jnp.zeros((width - t.shape[0], tk), t.dtype)], axis=0)
        ref[:, off:off + width] = t.T.astype(ref.dtype)

    put(s5_ref, 0, piece["s5_u"], MIX_W)
    put(proj_ref, PC_XBC, piece["xbc"], M2_CONV_DIM)
    put(proj_ref, PC_Z, piece["z"], MIX_W)
    put(proj_ref, PC_RKV, piece["rkv"], RW_RKV_W)
    put(proj_ref, PC_LRU_X, piece["lru_x"], MIX_W)
    put(proj_ref, PC_LRU_G, piece["lru_g"], MIX_W)
    put(proj_ref, PC_LORA, piece["lora"], RW_LORA_W)
    put(proj_ref, PC_DT, piece["dt"], PROJ_W - PC_DT)
    put(gate_ref, 0, w[row:, :], w.shape[0] - row)


def _prepare_w_in(w_in, tk=128):
    depth, d, cols = w_in.shape
    gate_w = cols - (4 * MIX_W + M2_CONV_DIM + N_HEADS + RW_RKV_W + sum(RW_LORA_RANKS))
    out = lambda width: pl.BlockSpec((None, tk, width), lambda l, r: (l, r, 0))
    return pl.pallas_call(
        _w_in_body,
        name="w_in_layout",
        grid=(depth, d // tk),
        in_specs=[pl.BlockSpec((None, cols, tk), lambda l, r: (l, 0, r))],
        out_specs=[out(MIX_W), out(PROJ_W), out(gate_w)],
        out_shape=[jax.ShapeDtypeStruct((depth, d, MIX_W), BF16), jax.ShapeDtypeStruct((depth, d, PROJ_W), BF16),
                   jax.ShapeDtypeStruct((depth, d, gate_w), BF16)],
        compiler_params=_params("parallel", "parallel"),
    )(jnp.swapaxes(w_in, 1, 2))


def kernel(x, norm_mix_g, w_in, s5_lambda_re, s5_lambda_im, s5_b_re, s5_b_im, s5_c_re, s5_c_im, s5_d, s5_log_dt, s5_w_glu, lru_conv_w, lru_conv_b, lru_w_a, lru_b_a, lru_w_x, lru_b_x, lru_lambda, m2_conv_w, m2_conv_b, m2_dt_bias, m2_a_log, m2_d, m2_norm_g, rw_mu, rw_w0, rw_w2, rw_a0, rw_a2, rw_g2, rw_k_k, rw_k_a, rw_r_k, rw_ln_g, rw_ln_b, w_branch, w_out, norm_ffn_g, w_ffn_gate, w_ffn_up, ffn_conv_w, ffn_conv_b, w_ffn_down, final_norm_g):
    bsz, seq, d = x.shape
    depth = w_in.shape[0]
    xf = x.reshape(bsz * seq, d)
    wb_all, wo_all, wd_all = w_branch.astype(BF16), w_out.astype(BF16), w_ffn_down.astype(BF16)
    w_s5_all, w_proj_all, w_gate_all = _prepare_w_in(w_in)
    for l in range(depth):
        h = _rms_norm(xf, norm_mix_g[l], BF16)
        proj = _matmul(h, w_proj_all, tm=2048, layer=l)
        mats = _s5_matrices(s5_lambda_re[l], s5_lambda_im[l], s5_b_re[l], s5_b_im[l],
                            s5_c_re[l], s5_c_im[l], s5_d[l], s5_log_dt[l])
        y_a = _s5_mixer(h, w_s5_all, mats, s5_w_glu[l], bsz, seq, layer=l)
        y_b = _lru_mixer(proj, lru_conv_w[l], lru_conv_b[l], lru_w_a[l], lru_b_a[l],
                         lru_w_x[l], lru_b_x[l], lru_lambda[l], bsz, seq)
        y_c = _m2_mixer(proj, m2_conv_w[l], m2_conv_b[l], m2_dt_bias[l], m2_a_log[l],
                        m2_d[l], m2_norm_g[l], bsz, seq)
        y_d = _rw_mixer(proj, rw_mu[l], rw_w0[l], rw_w2[l], rw_a0[l], rw_a2[l], rw_g2[l],
                        rw_k_k[l], rw_k_a[l], rw_r_k[l].reshape(-1), rw_ln_g[l], rw_ln_b[l], bsz, seq)
        merged = _merge(h, (y_a, y_b, y_c, y_d), w_gate_all, wb_all, layer=l)
        xf, h = _matmul_res_norm(merged, wo_all, xf, norm_ffn_g[l], l)
        act = _ffn_act(h, w_ffn_gate, w_ffn_up, ffn_conv_w[l], ffn_conv_b[l], seq, layer=l)
        xf = _matmul(act, wd_all, res=xf, tm=1024, layer=l)
    return _rms_norm(xf, final_norm_g, F32).reshape(bsz, seq, d)
```
